```python
import jax, jax.numpy as jnp
from jax import lax
import numpy as np

D_MODEL = 2048
BATCH = 8
SEQ = 4096
DEPTH = 4

N_MEM = 256
N_MIXERS = 3
N_A_LAYERS = (DEPTH + 2) // 3
N_B_LAYERS = (DEPTH + 1) // 3
N_C_LAYERS = DEPTH // 3
SHORT_CONV = 3
CHUNK = 128
GMLP_GROUPS = 8
GMLP_HIDDEN = D_MODEL
GMLP_GROUP_DIM = GMLP_HIDDEN // GMLP_GROUPS
CONF_CONV = 31
XA_HEADS = 4
XA_HEAD_DIM = D_MODEL // XA_HEADS
D_FF = ((8 * D_MODEL + 3 * 256 - 1) // (3 * 256)) * 256
EPS = 1e-6

kernel_name = 'hybrid_interleaved_conv_gmlp_conformer_xattn'


def _rmsnorm(x, g):
    xf = x.astype(jnp.float32)
    y = xf * lax.rsqrt(jnp.mean(xf * xf, axis=-1, keepdims=True) + EPS)
    return (y * g.astype(jnp.float32)).astype(x.dtype)


def _layernorm(x, g, b):
    xf = x.astype(jnp.float32)
    mu = jnp.mean(xf, axis=-1, keepdims=True)
    var = jnp.mean(jnp.square(xf - mu), axis=-1, keepdims=True)
    y = (xf - mu) * lax.rsqrt(var + EPS)
    return (y * g.astype(jnp.float32) + b.astype(jnp.float32)).astype(x.dtype)


def _causal_dwconv(x, w):
    k = w.shape[0]
    return lax.conv_general_dilated(
        x, w[:, None, :].astype(x.dtype), window_strides=(1,),
        padding=[(k - 1, 0)], dimension_numbers=('NWC', 'WIO', 'NWC'),
        feature_group_count=x.shape[-1])


def _mixer_short_conv(h, w_in, conv_w, w_out):
    bcz = h @ w_in
    b_gate, c_gate, z = jnp.split(bcz, 3, axis=-1)
    y = _causal_dwconv(c_gate * z, conv_w)
    return (b_gate * y) @ w_out


def _mixer_chunked_gmlp(h, w_in, v_g, v_b, w_s, s_bias, w_out):
    bsz, seq, _ = h.shape
    uv = jax.nn.gelu(h @ w_in)
    u, v = jnp.split(uv, 2, axis=-1)
    v = _layernorm(v, v_g, v_b)
    v = v.reshape(bsz, seq // CHUNK, CHUNK, GMLP_GROUPS, GMLP_GROUP_DIM)
    mask = jnp.tril(jnp.ones((CHUNK, CHUNK), dtype=bool))
    ws = jnp.where(mask[None], w_s, jnp.zeros((), w_s.dtype))
    sv = jnp.einsum('gts,bnsgc->bntgc', ws, v)
    sv = sv + s_bias.T[None, None, :, :, None]
    gated = u * sv.reshape(bsz, seq, GMLP_HIDDEN)
    return gated @ w_out


def _mixer_conformer_conv(h, w_in, conv_w, conv_b, ln_g, ln_b, w_out):
    ag = h @ w_in
    a, g = jnp.split(ag, 2, axis=-1)
    y = a * jax.nn.sigmoid(g)
    y = _causal_dwconv(y, conv_w) + conv_b
    y = _layernorm(y, ln_g, ln_b)
    y = jax.nn.silu(y)
    return y @ w_out


def _cross_attention(h, mem_n, wq, wkv, wo):
    bsz, seq, _ = h.shape
    q = (h @ wq).reshape(bsz, seq, XA_HEADS, XA_HEAD_DIM)
    kv = mem_n @ wkv
    k, v = jnp.split(kv, 2, axis=-1)
    k = k.reshape(bsz, N_MEM, XA_HEADS, XA_HEAD_DIM)
    v = v.reshape(bsz, N_MEM, XA_HEADS, XA_HEAD_DIM)
    scale = XA_HEAD_DIM ** -0.5
    s = jnp.einsum('bshd,bmhd->bhsm', q, k).astype(jnp.float32) * scale
    p = jax.nn.softmax(s, axis=-1).astype(v.dtype)
    o = jnp.einsum('bhsm,bmhd->bshd', p, v).reshape(bsz, seq, D_MODEL)
    return o @ wo


def _swiglu(h, w_gu, w_down):
    gu = h @ w_gu
    gate, up = jnp.split(gu, 2, axis=-1)
    return (jax.nn.silu(gate) * up) @ w_down


def _fwd_setup_inputs(seed: int = 0) -> dict:
    key = jax.random.key(seed)
    ks = jax.random.split(key, 32)

    def nrm(k, shape, scale):
        return jax.random.normal(k, shape, jnp.float32) * scale

    def gain(k, shape):
        return 1.0 + 0.05 * jax.random.normal(k, shape, jnp.float32)

    d = D_MODEL
    return {
        'x': nrm(ks[0], (BATCH, SEQ, d), 1.0),
        'mem': nrm(ks[1], (BATCH, N_MEM, d), 1.0),
        'mix_norm': gain(ks[2], (DEPTH, 2, d)),
        'xa_norm': gain(ks[3], (DEPTH, 3, d)),
        'xa_wq': nrm(ks[4], (DEPTH, d, d), d ** -0.5),
        'xa_wkv': nrm(ks[5], (DEPTH, d, 2 * d), d ** -0.5),
        'xa_wo': nrm(ks[6], (DEPTH, d, d), d ** -0.5),
        'ffn_norm': gain(ks[7], (DEPTH, 2, d)),
        'ffn_w_gu': nrm(ks[8], (DEPTH, d, 2 * D_FF), d ** -0.5),
        'ffn_w_down': nrm(ks[9], (DEPTH, D_FF, d), D_FF ** -0.5),
        'a_w_in': nrm(ks[10], (N_A_LAYERS, d, 3 * d), d ** -0.5),
        'a_conv_w': nrm(ks[11], (N_A_LAYERS, SHORT_CONV, d), SHORT_CONV ** -0.5),
        'a_w_out': nrm(ks[12], (N_A_LAYERS, d, d), d ** -0.5),
        'b_w_in': nrm(ks[13], (N_B_LAYERS, d, 2 * GMLP_HIDDEN), d ** -0.5),
        'b_v_g': gain(ks[14], (N_B_LAYERS, GMLP_HIDDEN)),
        'b_v_b': nrm(ks[15], (N_B_LAYERS, GMLP_HIDDEN), 0.02),
        'b_w_s': nrm(ks[16], (N_B_LAYERS, GMLP_GROUPS, CHUNK, CHUNK), CHUNK ** -0.5),
        'b_s_bias': gain(ks[17], (N_B_LAYERS, GMLP_GROUPS, CHUNK)),
        'b_w_out': nrm(ks[18], (N_B_LAYERS, GMLP_HIDDEN, d), GMLP_HIDDEN ** -0.5),
        'c_w_in': nrm(ks[19], (N_C_LAYERS, d, 2 * d), d ** -0.5),
        'c_conv_w': nrm(ks[20], (N_C_LAYERS, CONF_CONV, d), CONF_CONV ** -0.5),
        'c_conv_b': nrm(ks[21], (N_C_LAYERS, d), 0.02),
        'c_ln_g': gain(ks[22], (N_C_LAYERS, d)),
        'c_ln_b': nrm(ks[23], (N_C_LAYERS, d), 0.02),
        'c_w_out': nrm(ks[24], (N_C_LAYERS, d, d), d ** -0.5),
    }


def _fwd_reference(x, mem, mix_norm, xa_norm, xa_wq, xa_wkv, xa_wo, ffn_norm,
              ffn_w_gu, ffn_w_down, a_w_in, a_conv_w, a_w_out,
              b_w_in, b_v_g, b_v_b, b_w_s, b_s_bias, b_w_out,
              c_w_in, c_conv_w, c_conv_b, c_ln_g, c_ln_b, c_w_out):
    for i in range(DEPTH):
        kind = i % N_MIXERS
        slot = i // N_MIXERS
        h = _rmsnorm(x, mix_norm[i, 0])
        if kind == 0:
            y = _mixer_short_conv(h, a_w_in[slot], a_conv_w[slot], a_w_out[slot])
        elif kind == 1:
            y = _mixer_chunked_gmlp(h, b_w_in[slot], b_v_g[slot], b_v_b[slot],
                                    b_w_s[slot], b_s_bias[slot], b_w_out[slot])
        else:
            y = _mixer_conformer_conv(h, c_w_in[slot], c_conv_w[slot], c_conv_b[slot],
                                      c_ln_g[slot], c_ln_b[slot], c_w_out[slot])
        x = x + _rmsnorm(y, mix_norm[i, 1])
        h = _rmsnorm(x, xa_norm[i, 0])
        mem_n = _rmsnorm(mem, xa_norm[i, 2])
        y = _cross_attention(h, mem_n, xa_wq[i], xa_wkv[i], xa_wo[i])
        x = x + _rmsnorm(y, xa_norm[i, 1])
        h = _rmsnorm(x, ffn_norm[i, 0])
        y = _swiglu(h, ffn_w_gu[i], ffn_w_down[i])
        x = x + _rmsnorm(y, ffn_norm[i, 1])
    return x


import jax as _jax
import jax.numpy as _jnp

TWIN_FORMAT = 'train_step'
FWD_PARAMS = ['x', 'mem', 'mix_norm', 'xa_norm', 'xa_wq', 'xa_wkv', 'xa_wo', 'ffn_norm', 'ffn_w_gu', 'ffn_w_down', 'a_w_in', 'a_conv_w', 'a_w_out', 'b_w_in', 'b_v_g', 'b_v_b', 'b_w_s', 'b_s_bias', 'b_w_out', 'c_w_in', 'c_conv_w', 'c_conv_b', 'c_ln_g', 'c_ln_b', 'c_w_out']
TWIN_WEIGHTS = ['mix_norm', 'xa_norm', 'xa_wq', 'xa_wkv', 'xa_wo', 'ffn_norm', 'ffn_w_gu', 'ffn_w_down', 'a_w_in', 'a_conv_w', 'a_w_out', 'b_w_in', 'b_v_g', 'b_v_b', 'b_w_s', 'b_s_bias', 'b_w_out', 'c_w_in', 'c_conv_w', 'c_conv_b', 'c_ln_g', 'c_ln_b', 'c_w_out']
TWIN_DIFF_INPUT = 'x'
TWIN_INPUTS = ['x', 'mem', 'mix_norm', 'xa_norm', 'xa_wq', 'xa_wkv', 'xa_wo', 'ffn_norm', 'ffn_w_gu', 'ffn_w_down', 'a_w_in', 'a_conv_w', 'a_w_out', 'b_w_in', 'b_v_g', 'b_v_b', 'b_w_s', 'b_s_bias', 'b_w_out', 'c_w_in', 'c_conv_w', 'c_conv_b', 'c_ln_g', 'c_ln_b', 'c_w_out', 'loss_target', 'm_mix_norm', 'm_xa_norm', 'm_xa_wq', 'm_xa_wkv', 'm_xa_wo', 'm_ffn_norm', 'm_ffn_w_gu', 'm_ffn_w_down', 'm_a_w_in', 'm_a_conv_w', 'm_a_w_out', 'm_b_w_in', 'm_b_v_g', 'm_b_v_b', 'm_b_w_s', 'm_b_s_bias', 'm_b_w_out', 'm_c_w_in', 'm_c_conv_w', 'm_c_conv_b', 'm_c_ln_g', 'm_c_ln_b', 'm_c_w_out', 'v_mix_norm', 'v_xa_norm', 'v_xa_wq', 'v_xa_wkv', 'v_xa_wo', 'v_ffn_norm', 'v_ffn_w_gu', 'v_ffn_w_down', 'v_a_w_in', 'v_a_conv_w', 'v_a_w_out', 'v_b_w_in', 'v_b_v_g', 'v_b_v_b', 'v_b_w_s', 'v_b_s_bias', 'v_b_w_out', 'v_c_w_in', 'v_c_conv_w', 'v_c_conv_b', 'v_c_ln_g', 'v_c_ln_b', 'v_c_w_out']
TWIN_OUTPUTS = ['loss', 'grad_x', 'grad_mix_norm', 'grad_xa_norm', 'grad_xa_wq', 'grad_xa_wkv', 'grad_xa_wo', 'grad_ffn_norm', 'grad_ffn_w_gu', 'grad_ffn_w_down', 'grad_a_w_in', 'grad_a_conv_w', 'grad_a_w_out', 'grad_b_w_in', 'grad_b_v_g', 'grad_b_v_b', 'grad_b_w_s', 'grad_b_s_bias', 'grad_b_w_out', 'grad_c_w_in', 'grad_c_conv_w', 'grad_c_conv_b', 'grad_c_ln_g', 'grad_c_ln_b', 'grad_c_w_out', 'delta_mix_norm', 'delta_xa_norm', 'delta_xa_wq', 'delta_xa_wkv', 'delta_xa_wo', 'delta_ffn_norm', 'delta_ffn_w_gu', 'delta_ffn_w_down', 'delta_a_w_in', 'delta_a_conv_w', 'delta_a_w_out', 'delta_b_w_in', 'delta_b_v_g', 'delta_b_v_b', 'delta_b_w_s', 'delta_b_s_bias', 'delta_b_w_out', 'delta_c_w_in', 'delta_c_conv_w', 'delta_c_conv_b', 'delta_c_ln_g', 'delta_c_ln_b', 'delta_c_w_out', 'new_m_mix_norm', 'new_m_xa_norm', 'new_m_xa_wq', 'new_m_xa_wkv', 'new_m_xa_wo', 'new_m_ffn_norm', 'new_m_ffn_w_gu', 'new_m_ffn_w_down', 'new_m_a_w_in', 'new_m_a_conv_w', 'new_m_a_w_out', 'new_m_b_w_in', 'new_m_b_v_g', 'new_m_b_v_b', 'new_m_b_w_s', 'new_m_b_s_bias', 'new_m_b_w_out', 'new_m_c_w_in', 'new_m_c_conv_w', 'new_m_c_conv_b', 'new_m_c_ln_g', 'new_m_c_ln_b', 'new_m_c_w_out', 'new_v_mix_norm', 'new_v_xa_norm', 'new_v_xa_wq', 'new_v_xa_wkv', 'new_v_xa_wo', 'new_v_ffn_norm', 'new_v_ffn_w_gu', 'new_v_ffn_w_down', 'new_v_a_w_in', 'new_v_a_conv_w', 'new_v_a_w_out', 'new_v_b_w_in', 'new_v_b_v_g', 'new_v_b_v_b', 'new_v_b_w_s', 'new_v_b_s_bias', 'new_v_b_w_out', 'new_v_c_w_in', 'new_v_c_conv_w', 'new_v_c_conv_b', 'new_v_c_ln_g', 'new_v_c_ln_b', 'new_v_c_w_out']
TWIN_LEAF_KINDS = {'loss': 'loss', 'grad_x': 'grad_x', 'grad_mix_norm': 'grad_w', 'grad_xa_norm': 'grad_w', 'grad_xa_wq': 'grad_w', 'grad_xa_wkv': 'grad_w', 'grad_xa_wo': 'grad_w', 'grad_ffn_norm': 'grad_w', 'grad_ffn_w_gu': 'grad_w', 'grad_ffn_w_down': 'grad_w', 'grad_a_w_in': 'grad_w', 'grad_a_conv_w': 'grad_w', 'grad_a_w_out': 'grad_w', 'grad_b_w_in': 'grad_w', 'grad_b_v_g': 'grad_w', 'grad_b_v_b': 'grad_w', 'grad_b_w_s': 'grad_w', 'grad_b_s_bias': 'grad_w', 'grad_b_w_out': 'grad_w', 'grad_c_w_in': 'grad_w', 'grad_c_conv_w': 'grad_w', 'grad_c_conv_b': 'grad_w', 'grad_c_ln_g': 'grad_w', 'grad_c_ln_b': 'grad_w', 'grad_c_w_out': 'grad_w', 'delta_mix_norm': 'delta_w', 'delta_xa_norm': 'delta_w', 'delta_xa_wq': 'delta_w', 'delta_xa_wkv': 'delta_w', 'delta_xa_wo': 'delta_w', 'delta_ffn_norm': 'delta_w', 'delta_ffn_w_gu': 'delta_w', 'delta_ffn_w_down': 'delta_w', 'delta_a_w_in': 'delta_w', 'delta_a_conv_w': 'delta_w', 'delta_a_w_out': 'delta_w', 'delta_b_w_in': 'delta_w', 'delta_b_v_g': 'delta_w', 'delta_b_v_b': 'delta_w', 'delta_b_w_s': 'delta_w', 'delta_b_s_bias': 'delta_w', 'delta_b_w_out': 'delta_w', 'delta_c_w_in': 'delta_w', 'delta_c_conv_w': 'delta_w', 'delta_c_conv_b': 'delta_w', 'delta_c_ln_g': 'delta_w', 'delta_c_ln_b': 'delta_w', 'delta_c_w_out': 'delta_w', 'new_m_mix_norm': 'new_m', 'new_m_xa_norm': 'new_m', 'new_m_xa_wq': 'new_m', 'new_m_xa_wkv': 'new_m', 'new_m_xa_wo': 'new_m', 'new_m_ffn_norm': 'new_m', 'new_m_ffn_w_gu': 'new_m', 'new_m_ffn_w_down': 'new_m', 'new_m_a_w_in': 'new_m', 'new_m_a_conv_w': 'new_m', 'new_m_a_w_out': 'new_m', 'new_m_b_w_in': 'new_m', 'new_m_b_v_g': 'new_m', 'new_m_b_v_b': 'new_m', 'new_m_b_w_s': 'new_m', 'new_m_b_s_bias': 'new_m', 'new_m_b_w_out': 'new_m', 'new_m_c_w_in': 'new_m', 'new_m_c_conv_w': 'new_m', 'new_m_c_conv_b': 'new_m', 'new_m_c_ln_g': 'new_m', 'new_m_c_ln_b': 'new_m', 'new_m_c_w_out': 'new_m', 'new_v_mix_norm': 'new_v', 'new_v_xa_norm': 'new_v', 'new_v_xa_wq': 'new_v', 'new_v_xa_wkv': 'new_v', 'new_v_xa_wo': 'new_v', 'new_v_ffn_norm': 'new_v', 'new_v_ffn_w_gu': 'new_v', 'new_v_ffn_w_down': 'new_v', 'new_v_a_w_in': 'new_v', 'new_v_a_conv_w': 'new_v', 'new_v_a_w_out': 'new_v', 'new_v_b_w_in': 'new_v', 'new_v_b_v_g': 'new_v', 'new_v_b_v_b': 'new_v', 'new_v_b_w_s': 'new_v', 'new_v_b_s_bias': 'new_v', 'new_v_b_w_out': 'new_v', 'new_v_c_w_in': 'new_v', 'new_v_c_conv_w': 'new_v', 'new_v_c_conv_b': 'new_v', 'new_v_c_ln_g': 'new_v', 'new_v_c_ln_b': 'new_v', 'new_v_c_w_out': 'new_v'}


def _forward(args):
    return _fwd_reference(*[args[k] for k in FWD_PARAMS])


def _output_shape():
    out = _jax.eval_shape(lambda: _forward(_fwd_setup_inputs(0)))
    return out.shape, out.dtype

N_MICROBATCH = 1
ADAM_LR = 0.001
ADAM_B1 = 0.9
ADAM_B2 = 0.999
ADAM_EPS = 1e-08
ADAM_WD = 0.01
ADAM_STEP = 10
PER_EXAMPLE_BATCH_AXIS = {'x': 0, 'mem': 0, 'loss_target': 0}
SHARED_INPUTS = []
_WEIGHT_DTYPES = {'mix_norm': _jnp.float32, 'xa_norm': _jnp.float32, 'xa_wq': _jnp.float32, 'xa_wkv': _jnp.float32, 'xa_wo': _jnp.float32, 'ffn_norm': _jnp.float32, 'ffn_w_gu': _jnp.float32, 'ffn_w_down': _jnp.float32, 'a_w_in': _jnp.float32, 'a_conv_w': _jnp.float32, 'a_w_out': _jnp.float32, 'b_w_in': _jnp.float32, 'b_v_g': _jnp.float32, 'b_v_b': _jnp.float32, 'b_w_s': _jnp.float32, 'b_s_bias': _jnp.float32, 'b_w_out': _jnp.float32, 'c_w_in': _jnp.float32, 'c_conv_w': _jnp.float32, 'c_conv_b': _jnp.float32, 'c_ln_g': _jnp.float32, 'c_ln_b': _jnp.float32, 'c_w_out': _jnp.float32}
MOMENT_SCALE = {'mix_norm': 1.102229e+01, 'xa_norm': 1.132460e+01, 'xa_wq': 1.913140e+00, 'xa_wkv': 5.343001e+00, 'xa_wo': 7.539460e+00, 'ffn_norm': 1.132494e+01, 'ffn_w_gu': 1.051124e+00, 'ffn_w_down': 2.063246e+00, 'a_w_in': 9.790998e-01, 'a_conv_w': 9.966304e-01, 'a_w_out': 1.000201e+00, 'b_w_in': 1.511203e+00, 'b_v_g': 4.166164e-01, 'b_v_b': 4.490557e-01, 'b_w_s': 5.374097e-01, 'b_s_bias': 7.643488e-01, 'b_w_out': 5.442653e+00, 'c_w_in': 2.498645e+00, 'c_conv_w': 3.746022e+00, 'c_conv_b': 1.931632e+01, 'c_ln_g': 8.458372e+00, 'c_ln_b': 1.134570e+01, 'c_w_out': 5.932939e+00}


def _to_microbatches(a, axis):
    t = _jnp.moveaxis(a, axis, 0)
    t = t.reshape((N_MICROBATCH, t.shape[0] // N_MICROBATCH) + t.shape[1:])
    return _jnp.moveaxis(t, 1, axis + 1)


def setup_inputs(seed: int = 0) -> dict:
    inp = _fwd_setup_inputs(seed)
    key = _jax.random.fold_in(_jax.random.key(seed), 7919)
    shape, _ = _output_shape()
    out = dict(inp)
    out["loss_target"] = _jax.random.normal(_jax.random.fold_in(key, 0), shape, _jnp.float32)
    for i, name in enumerate(TWIN_WEIGHTS):
        w = inp[name].astype(_jnp.float32)
        if MOMENT_SCALE is None:
            s = _jnp.sqrt(_jnp.mean(_jnp.square(w)) + 1e-30)
        else:
            s = MOMENT_SCALE[name]
        km, kv = _jax.random.split(_jax.random.fold_in(key, i + 1))
        out[name] = w
        out["m_" + name] = s * _jax.random.normal(km, w.shape, _jnp.float32)
        out["v_" + name] = (s * s) * _jax.random.uniform(kv, w.shape, _jnp.float32, 0.5, 1.5)
    if N_MICROBATCH > 1:
        for name, axis in PER_EXAMPLE_BATCH_AXIS.items():
            out[name] = _to_microbatches(out[name], axis)
    return {'x': out['x'], 'mem': out['mem'], 'mix_norm': out['mix_norm'], 'xa_norm': out['xa_norm'], 'xa_wq': out['xa_wq'], 'xa_wkv': out['xa_wkv'], 'xa_wo': out['xa_wo'], 'ffn_norm': out['ffn_norm'], 'ffn_w_gu': out['ffn_w_gu'], 'ffn_w_down': out['ffn_w_down'], 'a_w_in': out['a_w_in'], 'a_conv_w': out['a_conv_w'], 'a_w_out': out['a_w_out'], 'b_w_in': out['b_w_in'], 'b_v_g': out['b_v_g'], 'b_v_b': out['b_v_b'], 'b_w_s': out['b_w_s'], 'b_s_bias': out['b_s_bias'], 'b_w_out': out['b_w_out'], 'c_w_in': out['c_w_in'], 'c_conv_w': out['c_conv_w'], 'c_conv_b': out['c_conv_b'], 'c_ln_g': out['c_ln_g'], 'c_ln_b': out['c_ln_b'], 'c_w_out': out['c_w_out'], 'loss_target': out['loss_target'], 'm_mix_norm': out['m_mix_norm'], 'm_xa_norm': out['m_xa_norm'], 'm_xa_wq': out['m_xa_wq'], 'm_xa_wkv': out['m_xa_wkv'], 'm_xa_wo': out['m_xa_wo'], 'm_ffn_norm': out['m_ffn_norm'], 'm_ffn_w_gu': out['m_ffn_w_gu'], 'm_ffn_w_down': out['m_ffn_w_down'], 'm_a_w_in': out['m_a_w_in'], 'm_a_conv_w': out['m_a_conv_w'], 'm_a_w_out': out['m_a_w_out'], 'm_b_w_in': out['m_b_w_in'], 'm_b_v_g': out['m_b_v_g'], 'm_b_v_b': out['m_b_v_b'], 'm_b_w_s': out['m_b_w_s'], 'm_b_s_bias': out['m_b_s_bias'], 'm_b_w_out': out['m_b_w_out'], 'm_c_w_in': out['m_c_w_in'], 'm_c_conv_w': out['m_c_conv_w'], 'm_c_conv_b': out['m_c_conv_b'], 'm_c_ln_g': out['m_c_ln_g'], 'm_c_ln_b': out['m_c_ln_b'], 'm_c_w_out': out['m_c_w_out'], 'v_mix_norm': out['v_mix_norm'], 'v_xa_norm': out['v_xa_norm'], 'v_xa_wq': out['v_xa_wq'], 'v_xa_wkv': out['v_xa_wkv'], 'v_xa_wo': out['v_xa_wo'], 'v_ffn_norm': out['v_ffn_norm'], 'v_ffn_w_gu': out['v_ffn_w_gu'], 'v_ffn_w_down': out['v_ffn_w_down'], 'v_a_w_in': out['v_a_w_in'], 'v_a_conv_w': out['v_a_conv_w'], 'v_a_w_out': out['v_a_w_out'], 'v_b_w_in': out['v_b_w_in'], 'v_b_v_g': out['v_b_v_g'], 'v_b_v_b': out['v_b_v_b'], 'v_b_w_s': out['v_b_w_s'], 'v_b_s_bias': out['v_b_s_bias'], 'v_b_w_out': out['v_b_w_out'], 'v_c_w_in': out['v_c_w_in'], 'v_c_conv_w': out['v_c_conv_w'], 'v_c_conv_b': out['v_c_conv_b'], 'v_c_ln_g': out['v_c_ln_g'], 'v_c_ln_b': out['v_c_ln_b'], 'v_c_w_out': out['v_c_w_out']}


def _loss(weights, diff, rest, loss_target):
    with _jax.named_scope("forward"):
        args = {**rest, TWIN_DIFF_INPUT: diff, **{k: w.astype(_WEIGHT_DTYPES[k]) for k, w in weights.items()}}
        y = _forward(args)
    with _jax.named_scope("loss_head"):
        err = _jnp.square(y.astype(_jnp.float32) - loss_target)
        return 0.5 * _jnp.sum(_jnp.mean(err, axis=-1)) if err.ndim else 0.5 * err


def _adamw(w, g, m, v):
    m = ADAM_B1 * m + (1.0 - ADAM_B1) * g
    v = ADAM_B2 * v + (1.0 - ADAM_B2) * _jnp.square(g)
    m_hat = m / (1.0 - ADAM_B1 ** ADAM_STEP)
    v_hat = v / (1.0 - ADAM_B2 ** ADAM_STEP)
    delta = -ADAM_LR * (m_hat / (_jnp.sqrt(v_hat) + ADAM_EPS) + ADAM_WD * w)
    return delta, m, v


def reference(x, mem, mix_norm, xa_norm, xa_wq, xa_wkv, xa_wo, ffn_norm, ffn_w_gu, ffn_w_down, a_w_in, a_conv_w, a_w_out, b_w_in, b_v_g, b_v_b, b_w_s, b_s_bias, b_w_out, c_w_in, c_conv_w, c_conv_b, c_ln_g, c_ln_b, c_w_out, loss_target, m_mix_norm, m_xa_norm, m_xa_wq, m_xa_wkv, m_xa_wo, m_ffn_norm, m_ffn_w_gu, m_ffn_w_down, m_a_w_in, m_a_conv_w, m_a_w_out, m_b_w_in, m_b_v_g, m_b_v_b, m_b_w_s, m_b_s_bias, m_b_w_out, m_c_w_in, m_c_conv_w, m_c_conv_b, m_c_ln_g, m_c_ln_b, m_c_w_out, v_mix_norm, v_xa_norm, v_xa_wq, v_xa_wkv, v_xa_wo, v_ffn_norm, v_ffn_w_gu, v_ffn_w_down, v_a_w_in, v_a_conv_w, v_a_w_out, v_b_w_in, v_b_v_g, v_b_v_b, v_b_w_s, v_b_s_bias, v_b_w_out, v_c_w_in, v_c_conv_w, v_c_conv_b, v_c_ln_g, v_c_ln_b, v_c_w_out):
    given = dict(x=x, mem=mem, mix_norm=mix_norm, xa_norm=xa_norm, xa_wq=xa_wq, xa_wkv=xa_wkv, xa_wo=xa_wo, ffn_norm=ffn_norm, ffn_w_gu=ffn_w_gu, ffn_w_down=ffn_w_down, a_w_in=a_w_in, a_conv_w=a_conv_w, a_w_out=a_w_out, b_w_in=b_w_in, b_v_g=b_v_g, b_v_b=b_v_b, b_w_s=b_w_s, b_s_bias=b_s_bias, b_w_out=b_w_out, c_w_in=c_w_in, c_conv_w=c_conv_w, c_conv_b=c_conv_b, c_ln_g=c_ln_g, c_ln_b=c_ln_b, c_w_out=c_w_out, loss_target=loss_target, m_mix_norm=m_mix_norm, m_xa_norm=m_xa_norm, m_xa_wq=m_xa_wq, m_xa_wkv=m_xa_wkv, m_xa_wo=m_xa_wo, m_ffn_norm=m_ffn_norm, m_ffn_w_gu=m_ffn_w_gu, m_ffn_w_down=m_ffn_w_down, m_a_w_in=m_a_w_in, m_a_conv_w=m_a_conv_w, m_a_w_out=m_a_w_out, m_b_w_in=m_b_w_in, m_b_v_g=m_b_v_g, m_b_v_b=m_b_v_b, m_b_w_s=m_b_w_s, m_b_s_bias=m_b_s_bias, m_b_w_out=m_b_w_out, m_c_w_in=m_c_w_in, m_c_conv_w=m_c_conv_w, m_c_conv_b=m_c_conv_b, m_c_ln_g=m_c_ln_g, m_c_ln_b=m_c_ln_b, m_c_w_out=m_c_w_out, v_mix_norm=v_mix_norm, v_xa_norm=v_xa_norm, v_xa_wq=v_xa_wq, v_xa_wkv=v_xa_wkv, v_xa_wo=v_xa_wo, v_ffn_norm=v_ffn_norm, v_ffn_w_gu=v_ffn_w_gu, v_ffn_w_down=v_ffn_w_down, v_a_w_in=v_a_w_in, v_a_conv_w=v_a_conv_w, v_a_w_out=v_a_w_out, v_b_w_in=v_b_w_in, v_b_v_g=v_b_v_g, v_b_v_b=v_b_v_b, v_b_w_s=v_b_w_s, v_b_s_bias=v_b_s_bias, v_b_w_out=v_b_w_out, v_c_w_in=v_c_w_in, v_c_conv_w=v_c_conv_w, v_c_conv_b=v_c_conv_b, v_c_ln_g=v_c_ln_g, v_c_ln_b=v_c_ln_b, v_c_w_out=v_c_w_out)
    weights = {n: given[n] for n in TWIN_WEIGHTS}
    shared = {n: given[n] for n in SHARED_INPUTS}
    per_example = {n: given[n] for n in ['x', 'mem']}
    grad_fn = _jax.value_and_grad(_loss, argnums=(0, 1))

    def one_microbatch(ex, loss_target):
        ex = dict(ex)
        diff = ex.pop(TWIN_DIFF_INPUT)
        return grad_fn(weights, diff, {**shared, **ex}, loss_target)

    if N_MICROBATCH == 1:
        loss, (grad_w, grad_x) = one_microbatch(per_example, given["loss_target"])
    else:
        def body(carry, xs):
            loss_sum, grad_sum = carry
            l_k, (gw_k, gx_k) = one_microbatch(xs[0], xs[1])
            with _jax.named_scope("update"):
                return (loss_sum + l_k, _jax.tree.map(_jnp.add, grad_sum, gw_k)), gx_k

        init = (_jnp.zeros((), _jnp.float32), _jax.tree.map(_jnp.zeros_like, weights))
        (loss, grad_w), grad_x = _jax.lax.scan(body, init, (per_example, given["loss_target"]))
    with _jax.named_scope("update"):
        delta_w, new_m, new_v = {}, {}, {}
        for n in TWIN_WEIGHTS:
            delta_w[n], new_m[n], new_v[n] = _adamw(weights[n], grad_w[n], given["m_" + n], given["v_" + n])
    return (loss, grad_x, *[grad_w[n] for n in TWIN_WEIGHTS], *[delta_w[n] for n in TWIN_WEIGHTS],
            *[new_m[n] for n in TWIN_WEIGHTS], *[new_v[n] for n in TWIN_WEIGHTS])
```

```python
import functools

import jax
import jax.numpy as jnp
from jax import lax
from jax.experimental import pallas as pl
from jax.experimental.pallas import tpu as pltpu

F32 = jnp.float32
BF16 = jnp.bfloat16
MESH = pl.DeviceIdType.MESH
ANY = pl.BlockSpec(memory_space=pl.ANY)

N_DEV = 8
N_MIXERS = 3
XA_HEADS = 4
GMLP_GROUPS = 8
CHUNK = 128
NORM_EPS = 1e-6
HALO = 32
ROW_TILE = 256
CONV_LANES = 512
V7X_VMEM_LIMIT = 56 * 1024 * 1024

ADAM_LR = 0.001
ADAM_B1 = 0.9
ADAM_B2 = 0.999
ADAM_EPS = 1e-08
ADAM_WD = 0.01
ADAM_STEP = 10


def _pcall(body, **kw):
    return pl.pallas_call(body, **kw)


def _params(*sem):
    return pltpu.CompilerParams(dimension_semantics=sem, vmem_limit_bytes=V7X_VMEM_LIMIT)


def _fit(n, pref, mult=128):
    if n <= pref:
        return n
    t = (pref // mult) * mult
    while t >= mult:
        if n % t == 0:
            return t
        t -= mult
    return n


def _sds(shape, dtype):
    return jax.ShapeDtypeStruct(shape, dtype)


_DIMS = {"nn": (((1,), (0,)), ((), ())), "nt": (((1,), (1,)), ((), ())), "tn": (((0,), (0,)), ((), ()))}


def _mm(a, b, mode, out_dtype, name, *, b_blocked=False, out_blocks=None, tm=1024, tn=1536, tk=512):
    if mode == "tn":
        K, M = a.shape
    else:
        M, K = a.shape
    if b_blocked:
        S, d1, c = b.shape
        N = S * c if mode == "nn" else d1
    else:
        N = b.shape[1] if mode in ("nn", "tn") else b.shape[0]
    tm = _fit(M, tm)
    if b_blocked and mode == "nn":
        tn = _fit(c, tn)
    elif out_blocks:
        tn = _fit(N // out_blocks, tn)
    else:
        tn = _fit(N, min(tn, 1024))
    if b_blocked and mode == "nt":
        tk = _fit(c, 1536)
    else:
        tk = _fit(K, tk)
    nk = K // tk

    if mode == "tn":
        a_spec = pl.BlockSpec((tk, tm), lambda i, j, k: (k, i))
    else:
        a_spec = pl.BlockSpec((tm, tk), lambda i, j, k: (i, k))
    if b_blocked and mode == "nn":
        nb = c // tn
        b_spec = pl.BlockSpec((None, tk, tn), lambda i, j, k: (j // nb, k, j % nb))
    elif b_blocked and mode == "nt":
        kb = c // tk
        b_spec = pl.BlockSpec((None, tn, tk), lambda i, j, k: (k // kb, j, k % kb))
    elif mode == "nt":
        b_spec = pl.BlockSpec((tn, tk), lambda i, j, k: (j, k))
    else:
        b_spec = pl.BlockSpec((tk, tn), lambda i, j, k: (k, j))
    if out_blocks:
        ob = (N // out_blocks) // tn
        out_shape = _sds((out_blocks, M, N // out_blocks), out_dtype)
        o_spec = pl.BlockSpec((None, tm, tn), lambda i, j, k: (j // ob, i, j % ob))
    else:
        out_shape = _sds((M, N), out_dtype)
        o_spec = pl.BlockSpec((tm, tn), lambda i, j, k: (i, j))
    dims = _DIMS[mode]

    def body(a_ref, b_ref, o_ref, acc_ref):
        p = lax.dot_general(a_ref[...], b_ref[...], dims, preferred_element_type=F32)
        if nk == 1:
            o_ref[...] = p.astype(o_ref.dtype)
            return
        k = pl.program_id(2)

        @pl.when(k == 0)
        def _():
            acc_ref[...] = p

        @pl.when(k > 0)
        def _():
            acc_ref[...] += p

        @pl.when(k == nk - 1)
        def _():
            o_ref[...] = acc_ref[...].astype(o_ref.dtype)

    return _pcall(
        body, name=name, out_shape=out_shape, grid=(M // tm, N // tn, nk),
        in_specs=[a_spec, b_spec], out_specs=o_spec,
        scratch_shapes=[pltpu.VMEM((tm, tn), F32)],
        compiler_params=_params("parallel", "parallel", "arbitrary"),
    )(a, b)


def _rstd(v):
    return lax.rsqrt(jnp.mean(v * v, axis=-1, keepdims=True) + NORM_EPS)


def _rms_bwd_rows(v, g, dout):
    r = _rstd(v)
    vh = v * r
    dvh = dout * g
    dv = r * (dvh - vh * jnp.mean(dvh * vh, axis=-1, keepdims=True))
    return dv, jnp.sum(dout * vh, axis=0, keepdims=True)


def _row(tt, d, col=0):
    return pl.BlockSpec((tt, d), lambda i: (i, col))


def _const(shape):
    return pl.BlockSpec(shape, lambda i: (0,) * len(shape))


def _prev_halo(tt, d, col=0):
    return pl.BlockSpec((HALO, d), lambda i: (jnp.maximum(i * (tt // HALO) - 1, 0), col))


def _next_halo(tt, d, rows, col=0):
    last = rows // HALO - 1
    return pl.BlockSpec((HALO, d), lambda i: (jnp.minimum((i + 1) * (tt // HALO), last), col))


def _rms_fwd(x, g, name):
    T, D = x.shape
    tt = _fit(T, ROW_TILE, 8)

    def body(x_ref, g_ref, h_ref):
        v = x_ref[...]
        h_ref[...] = (v * _rstd(v) * g_ref[...]).astype(BF16)

    return _pcall(body, name=name, out_shape=_sds((T, D), BF16), grid=(T // tt,),
                  in_specs=[_row(tt, D), _const((1, D))], out_specs=_row(tt, D),
                  compiler_params=_params("parallel"))(x, g)


def _post_pre_fwd(x, y, g_post, g_pre, name):
    T, D = x.shape
    tt = _fit(T, ROW_TILE, 8)

    def body(x_ref, y_ref, gp_ref, gn_ref, xo_ref, h_ref):
        y = y_ref[...]
        xn = x_ref[...] + y * _rstd(y) * gp_ref[...]
        xo_ref[...] = xn
        h_ref[...] = (xn * _rstd(xn) * gn_ref[...]).astype(BF16)

    return _pcall(body, name=name, out_shape=(_sds((T, D), F32), _sds((T, D), BF16)), grid=(T // tt,),
                  in_specs=[_row(tt, D), _row(tt, D), _const((1, D)), _const((1, D))],
                  out_specs=(_row(tt, D), _row(tt, D)),
                  compiler_params=_params("parallel"))(x, y, g_post, g_pre)


def _final_fwd_loss(x, y, g_post, target, name):
    T, D = x.shape
    tt = _fit(T, ROW_TILE, 8)

    def body(x_ref, y_ref, g_ref, t_ref, loss_ref, dx_ref, dy_ref, dg_ref):
        i = pl.program_id(0)
        y = y_ref[...]
        g = g_ref[...]
        err = x_ref[...] + y * _rstd(y) * g - t_ref[...]
        part = 0.5 * jnp.sum(jnp.mean(err * err, axis=-1, keepdims=True))
        dx = err / D
        dx_ref[...] = dx
        dy, dg = _rms_bwd_rows(y, g, dx)
        dy_ref[...] = dy.astype(BF16)

        @pl.when(i == 0)
        def _():
            loss_ref[...] = jnp.zeros_like(loss_ref)
            dg_ref[...] = jnp.zeros_like(dg_ref)

        loss_ref[...] += part
        dg_ref[...] += dg

    return _pcall(body, name=name,
                  out_shape=(_sds((1, 128), F32), _sds((T, D), F32), _sds((T, D), BF16), _sds((1, D), F32)),
                  grid=(T // tt,),
                  in_specs=[_row(tt, D), _row(tt, D), _const((1, D)), _row(tt, D)],
                  out_specs=(_const((1, 128)), _row(tt, D), _row(tt, D), _const((1, D))),
                  compiler_params=_params("arbitrary"))(x, y, g_post, target)


def _pre_post_bwd(dx_out, dh, x_in, g_pre, y_prev, g_post_prev, name):
    T, D = x_in.shape
    tt = _fit(T, ROW_TILE, 8)
    with_prev = y_prev is not None

    def body(*refs):
        if with_prev:
            dxo_ref, dh_ref, x_ref, g_ref, y_ref, gp_ref, dxi_ref, dg_ref, dy_ref, dgp_ref = refs
        else:
            dxo_ref, dh_ref, x_ref, g_ref, dxi_ref, dg_ref = refs
        i = pl.program_id(0)
        dv, dg = _rms_bwd_rows(x_ref[...], g_ref[...], dh_ref[...].astype(F32))
        dxi = dxo_ref[...] + dv
        dxi_ref[...] = dxi

        @pl.when(i == 0)
        def _():
            dg_ref[...] = jnp.zeros_like(dg_ref)

        dg_ref[...] += dg
        if with_prev:
            dy, dgp = _rms_bwd_rows(y_ref[...], gp_ref[...], dxi)
            dy_ref[...] = dy.astype(BF16)

            @pl.when(i == 0)
            def _():
                dgp_ref[...] = jnp.zeros_like(dgp_ref)

            dgp_ref[...] += dgp

    ins = [dx_out, dh, x_in, g_pre]
    in_specs = [_row(tt, D), _row(tt, D), _row(tt, D), _const((1, D))]
    out_shape = [_sds((T, D), F32), _sds((1, D), F32)]
    out_specs = [_row(tt, D), _const((1, D))]
    if with_prev:
        ins += [y_prev, g_post_prev]
        in_specs += [_row(tt, D), _const((1, D))]
        out_shape += [_sds((T, D), BF16), _sds((1, D), F32)]
        out_specs += [_row(tt, D), _const((1, D))]
    return _pcall(body, name=name, out_shape=tuple(out_shape), grid=(T // tt,),
                  in_specs=in_specs, out_specs=tuple(out_specs),
                  compiler_params=_params("arbitrary"))(*ins)


def _rms_gain_grad(dout, v, name):
    T, D = v.shape
    tt = _fit(T, ROW_TILE, 8)

    def body(d_ref, v_ref, dg_ref):
        @pl.when(pl.program_id(0) == 0)
        def _():
            dg_ref[...] = jnp.zeros_like(dg_ref)

        v = v_ref[...]
        dg_ref[...] += jnp.sum(d_ref[...] * (v * _rstd(v)), axis=0, keepdims=True)

    return _pcall(body, name=name, out_shape=_sds((1, D), F32), grid=(T // tt,),
                  in_specs=[_row(tt, D), _row(tt, D)], out_specs=_const((1, D)),
                  compiler_params=_params("arbitrary"))(dout, v)


def _softmax_rows(s):
    e = jnp.exp(s - jnp.max(s, axis=-1, keepdims=True))
    return e / jnp.sum(e, axis=-1, keepdims=True)


def _attn_fwd(q, kv, name):
    T, D = q.shape
    nm = kv.shape[0]
    hd = D // XA_HEADS
    scale = hd ** -0.5
    tq = _fit(T, ROW_TILE, 8)

    def body(q_ref, k_ref, v_ref, o_ref):
        for h in range(XA_HEADS):
            sl = slice(h * hd, (h + 1) * hd)
            s = lax.dot_general(q_ref[:, sl], k_ref[:, sl], _DIMS["nt"], preferred_element_type=F32) * scale
            p = _softmax_rows(s)
            o_ref[:, sl] = jnp.dot(p.astype(BF16), v_ref[:, sl], preferred_element_type=F32).astype(BF16)

    return _pcall(body, name=name, out_shape=_sds((T, D), BF16), grid=(T // tq,),
                  in_specs=[_row(tq, D), pl.BlockSpec((nm, D), lambda i: (0, 0)), pl.BlockSpec((nm, D), lambda i: (0, 1))],
                  out_specs=_row(tq, D), compiler_params=_params("parallel"))(q, kv, kv)


def _attn_bwd(q, kv, do, name):
    T, D = q.shape
    nm = kv.shape[0]
    hd = D // XA_HEADS
    scale = hd ** -0.5
    tq = _fit(T, ROW_TILE, 8)

    def body(q_ref, k_ref, v_ref, do_ref, dq_ref, dkv_ref):
        @pl.when(pl.program_id(0) == 0)
        def _():
            dkv_ref[...] = jnp.zeros_like(dkv_ref)

        for h in range(XA_HEADS):
            sl = slice(h * hd, (h + 1) * hd)
            qh, kh, vh, doh = q_ref[:, sl], k_ref[:, sl], v_ref[:, sl], do_ref[:, sl]
            s = lax.dot_general(qh, kh, _DIMS["nt"], preferred_element_type=F32) * scale
            p = _softmax_rows(s)
            dp = lax.dot_general(doh, vh, _DIMS["nt"], preferred_element_type=F32)
            ds = (p * (dp - jnp.sum(dp * p, axis=-1, keepdims=True)) * scale).astype(BF16)
            dq_ref[:, sl] = jnp.dot(ds, kh, preferred_element_type=F32).astype(BF16)
            dkv_ref[:, sl] += lax.dot_general(ds, qh, _DIMS["tn"], preferred_element_type=F32)
            dkv_ref[:, D + h * hd:D + (h + 1) * hd] += lax.dot_general(
                p.astype(BF16), doh, _DIMS["tn"], preferred_element_type=F32)

    return _pcall(body, name=name, out_shape=(_sds((T, D), BF16), _sds((nm, 2 * D), F32)), grid=(T // tq,),
                  in_specs=[_row(tq, D), pl.BlockSpec((nm, D), lambda i: (0, 0)), pl.BlockSpec((nm, D), lambda i: (0, 1)),
                            _row(tq, D)],
                  out_specs=(_row(tq, D), _const((nm, 2 * D))),
                  compiler_params=_params("arbitrary"))(q, kv, kv, do)


def _ffn_gu_fwd(h, w_gu, name, tm=512, tk=1024):
    T, D = h.shape
    S, _, c = w_gu.shape
    F = S * c // 2
    tm = _fit(T, tm)
    tn = _fit(c, 1536)
    tk = _fit(D, tk)
    nk = D // tk
    nb = c // tn
    nj = F // tn

    def w_spec(off):
        return pl.BlockSpec((None, tk, tn), lambda i, j, k: ((j + off) // nb, k, (j + off) % nb))

    def body(h_ref, wg_ref, wu_ref, g_ref, u_ref, a_ref, accg_ref, accu_ref):
        k = pl.program_id(2)
        hv = h_ref[...]
        pg = jnp.dot(hv, wg_ref[...], preferred_element_type=F32)
        pu = jnp.dot(hv, wu_ref[...], preferred_element_type=F32)

        @pl.when(k == 0)
        def _():
            accg_ref[...] = pg
            accu_ref[...] = pu

        @pl.when(k > 0)
        def _():
            accg_ref[...] += pg
            accu_ref[...] += pu

        @pl.when(k == nk - 1)
        def _():
            g = accg_ref[...]
            u = accu_ref[...]
            g_ref[...] = g.astype(BF16)
            u_ref[...] = u.astype(BF16)
            a_ref[...] = (g * jax.nn.sigmoid(g) * u).astype(BF16)

    o_spec = pl.BlockSpec((tm, tn), lambda i, j, k: (i, j))
    return _pcall(body, name=name, out_shape=(_sds((T, F), BF16),) * 3, grid=(T // tm, nj, nk),
                  in_specs=[pl.BlockSpec((tm, tk), lambda i, j, k: (i, k)), w_spec(0), w_spec(nj)],
                  out_specs=(o_spec,) * 3,
                  scratch_shapes=[pltpu.VMEM((tm, tn), F32), pltpu.VMEM((tm, tn), F32)],
                  compiler_params=_params("parallel", "parallel", "arbitrary"))(h, w_gu, w_gu)


def _ffn_dgu_bwd(dy, w_down, gate, up, name, tm=512, tn=512):
    T, D = dy.shape
    F = w_down.shape[0]
    tm = _fit(T, tm)
    tn = _fit(F, tn)
    nj = F // tn

    def body(dy_ref, wd_ref, g_ref, u_ref, o_ref, da_ref):
        s = pl.program_id(2)
        g = g_ref[...].astype(F32)
        sg = jax.nn.sigmoid(g)

        @pl.when(s == 0)
        def _():
            da = lax.dot_general(dy_ref[...], wd_ref[...], _DIMS["nt"], preferred_element_type=F32)
            da_ref[...] = da
            o_ref[...] = (da * u_ref[...].astype(F32) * (sg * (1.0 + g * (1.0 - sg)))).astype(BF16)

        @pl.when(s == 1)
        def _():
            o_ref[...] = (da_ref[...] * (g * sg)).astype(BF16)

    gu_spec = pl.BlockSpec((tm, tn), lambda i, j, s: (i, j))
    return _pcall(body, name=name, out_shape=_sds((T, 2 * F), BF16), grid=(T // tm, nj, 2),
                  in_specs=[pl.BlockSpec((tm, D), lambda i, j, s: (i, 0)), pl.BlockSpec((tn, D), lambda i, j, s: (j, 0)),
                            gu_spec, gu_spec],
                  out_specs=pl.BlockSpec((tm, tn), lambda i, j, s: (i, j + s * nj)),
                  scratch_shapes=[pltpu.VMEM((tm, tn), F32)],
                  compiler_params=_params("parallel", "parallel", "arbitrary"))(dy, w_down, gate, up)


def _causal_taps(win, width, tt):
    for b in range(min(8, width)):
        wb = win if b == 0 else pltpu.roll(win, b, 0)
        a = 0
        while 8 * a + b <= width - 1:
            yield width - 1 - (8 * a + b), wb[HALO - 8 * a:HALO - 8 * a + tt]
            a += 1


def _anticausal_taps(win, width, tt):
    rows = tt + HALO
    for b in range(min(8, width)):
        wb = win if b == 0 else pltpu.roll(win, rows - b, 0)
        a = 0
        while 8 * a + b <= width - 1:
            yield width - 1 - (8 * a + b), wb[8 * a:8 * a + tt]
            a += 1


def _lanes(d):
    cw = _fit(d, CONV_LANES)
    return [slice(s, s + cw) for s in range(0, d, cw)], cw


def _a_mid_fwd(bcz, conv_w, name):
    T, D3 = bcz.shape
    D = D3 // 3
    width = conv_w.shape[0]
    tt = _fit(T, ROW_TILE, HALO)
    chunks, cw = _lanes(D)

    def body(b_ref, c_ref, z_ref, ch_ref, zh_ref, w_ref, o_ref, win_ref):
        i = pl.program_id(0)
        for sl in chunks:
            uh = ch_ref[:, sl].astype(F32) * zh_ref[:, sl].astype(F32)
            win_ref[0:HALO, :] = jnp.where(i > 0, uh, 0.0)
            win_ref[HALO:, :] = c_ref[:, sl].astype(F32) * z_ref[:, sl].astype(F32)
            acc = jnp.zeros((tt, cw), F32)
            for k, xs in _causal_taps(win_ref[...], width, tt):
                acc = acc + w_ref[k:k + 1, sl] * xs
            o_ref[:, sl] = (b_ref[:, sl].astype(F32) * acc).astype(BF16)

    return _pcall(body, name=name, out_shape=_sds((T, D), BF16), grid=(T // tt,),
                  in_specs=[_row(tt, D, 0), _row(tt, D, 1), _row(tt, D, 2), _prev_halo(tt, D, 1), _prev_halo(tt, D, 2),
                            _const((width, D))],
                  out_specs=_row(tt, D), scratch_shapes=[pltpu.VMEM((HALO + tt, cw), F32)],
                  compiler_params=_params("parallel"))(bcz, bcz, bcz, bcz, bcz, conv_w)


def _a_mid_bwd(bcz, dy2, conv_w, name):
    T, D3 = bcz.shape
    D = D3 // 3
    width = conv_w.shape[0]
    tt = _fit(T, ROW_TILE, HALO)
    chunks, cw = _lanes(D)
    n_tiles = T // tt

    def body(b_ref, c_ref, z_ref, ch_ref, zh_ref, bn_ref, d_ref, dn_ref, w_ref, o_ref, dw_ref, win_ref, dwin_ref):
        i = pl.program_id(0)

        @pl.when(i == 0)
        def _():
            dw_ref[...] = jnp.zeros_like(dw_ref)

        for ci, sl in enumerate(chunks):
            c = c_ref[:, sl].astype(F32)
            z = z_ref[:, sl].astype(F32)
            b = b_ref[:, sl].astype(F32)
            d2 = d_ref[:, sl].astype(F32)
            uh = ch_ref[:, sl].astype(F32) * zh_ref[:, sl].astype(F32)
            win_ref[0:HALO, :] = jnp.where(i > 0, uh, 0.0)
            win_ref[HALO:, :] = c * z
            d1 = d2 * b
            d1n = dn_ref[:, sl].astype(F32) * bn_ref[:, sl].astype(F32)
            dwin_ref[0:tt, :] = d1
            dwin_ref[tt:, :] = jnp.where(i < n_tiles - 1, d1n, 0.0)
            y1 = jnp.zeros((tt, cw), F32)
            for k, xs in _causal_taps(win_ref[...], width, tt):
                y1 = y1 + w_ref[k:k + 1, sl] * xs
                dw_ref[k:k + 1, sl] += jnp.sum(d1 * xs, axis=0, keepdims=True)
            du = jnp.zeros((tt, cw), F32)
            for k, xs in _anticausal_taps(dwin_ref[...], width, tt):
                du = du + w_ref[k:k + 1, sl] * xs
            o_ref[:, ci * cw:(ci + 1) * cw] = (d2 * y1).astype(BF16)
            o_ref[:, D + ci * cw:D + (ci + 1) * cw] = (du * z).astype(BF16)
            o_ref[:, 2 * D + ci * cw:2 * D + (ci + 1) * cw] = (du * c).astype(BF16)

    return _pcall(body, name=name, out_shape=(_sds((T, 3 * D), BF16), _sds((width, D), F32)), grid=(n_tiles,),
                  in_specs=[_row(tt, D, 0), _row(tt, D, 1), _row(tt, D, 2), _prev_halo(tt, D, 1), _prev_halo(tt, D, 2),
                            _next_halo(tt, D, T, 0), _row(tt, D), _next_halo(tt, D, T), _const((width, D))],
                  out_specs=(_row(tt, 3 * D), _const((width, D))),
                  scratch_shapes=[pltpu.VMEM((HALO + tt, cw), F32), pltpu.VMEM((tt + HALO, cw), F32)],
                  compiler_params=_params("arbitrary"))(bcz, bcz, bcz, bcz, bcz, bcz, dy2, dy2, conv_w)


_GELU_C = 0.7978845608028654
_GELU_A = 0.044715


def _gelu(v):
    return 0.5 * v * (1.0 + jnp.tanh(_GELU_C * (v + _GELU_A * v * v * v)))


def _gelu_grad(v):
    t = jnp.tanh(_GELU_C * (v + _GELU_A * v * v * v))
    return 0.5 * (1.0 + t) + 0.5 * v * (1.0 - t * t) * (_GELU_C * (1.0 + 3.0 * _GELU_A * v * v))


def _ln_stats(v):
    mu = jnp.mean(v, axis=-1, keepdims=True)
    vc = v - mu
    return vc * lax.rsqrt(jnp.mean(vc * vc, axis=-1, keepdims=True) + NORM_EPS)


def _tril(n):
    return lax.broadcasted_iota(jnp.int32, (n, n), 0) >= lax.broadcasted_iota(jnp.int32, (n, n), 1)


def _b_mid_fwd(uv, v_g, v_b, w_s, bias_b, name):
    T, D2 = uv.shape
    D = D2 // 2
    G, C, _ = w_s.shape
    gd = D // G
    tt = _fit(T, ROW_TILE, C)

    def body(u_ref, v_ref, g_ref, b_ref, ws_ref, bias_ref, o_ref, vln_ref):
        vln_ref[...] = (_ln_stats(_gelu(v_ref[...].astype(F32))) * g_ref[...] + b_ref[...]).astype(BF16)
        mask = _tril(C)
        for g in range(G):
            wsm = jnp.where(mask, ws_ref[g], 0.0).astype(BF16)
            cs = slice(g * gd, (g + 1) * gd)
            for n in range(tt // C):
                rs = slice(n * C, (n + 1) * C)
                sv = jnp.dot(wsm, vln_ref[rs, cs], preferred_element_type=F32) + bias_ref[g]
                o_ref[rs, cs] = (_gelu(u_ref[rs, cs].astype(F32)) * sv).astype(BF16)

    return _pcall(body, name=name, out_shape=_sds((T, D), BF16), grid=(T // tt,),
                  in_specs=[_row(tt, D, 0), _row(tt, D, 1), _const((1, D)), _const((1, D)), _const((G, C, C)),
                            _const((G, C, gd))],
                  out_specs=_row(tt, D), scratch_shapes=[pltpu.VMEM((tt, D), BF16)],
                  compiler_params=_params("parallel"))(uv, uv, v_g, v_b, w_s, bias_b)


def _b_mid_bwd(uv, dgated, v_g, v_b, w_s, bias_b, name):
    T, D2 = uv.shape
    D = D2 // 2
    G, C, _ = w_s.shape
    gd = D // G
    tt = _fit(T, ROW_TILE, C)

    def body(u_ref, v_ref, d_ref, g_ref, b_ref, ws_ref, bias_ref, o_ref, dws_ref, dsb_ref, dvg_ref, dvb_ref,
             vln_ref, dvln_ref):
        @pl.when(pl.program_id(0) == 0)
        def _():
            dws_ref[...] = jnp.zeros_like(dws_ref)
            dsb_ref[...] = jnp.zeros_like(dsb_ref)
            dvg_ref[...] = jnp.zeros_like(dvg_ref)
            dvb_ref[...] = jnp.zeros_like(dvb_ref)

        vpre = v_ref[...].astype(F32)
        vhat = _ln_stats(_gelu(vpre))
        vln_ref[...] = (vhat * g_ref[...] + b_ref[...]).astype(BF16)
        mask = _tril(C)
        lane = lax.broadcasted_iota(jnp.int32, (C, 128), 1)
        for g in range(G):
            wsm = jnp.where(mask, ws_ref[g], 0.0).astype(BF16)
            cs = slice(g * gd, (g + 1) * gd)
            for n in range(tt // C):
                rs = slice(n * C, (n + 1) * C)
                vt = vln_ref[rs, cs]
                sv = jnp.dot(wsm, vt, preferred_element_type=F32) + bias_ref[g]
                dg = d_ref[rs, cs].astype(F32)
                upre = u_ref[rs, cs].astype(F32)
                o_ref[rs, cs] = (dg * sv * _gelu_grad(upre)).astype(BF16)
                dsv = dg * _gelu(upre)
                dsb_ref[...] += jnp.where(lane == g, jnp.sum(dsv, axis=-1, keepdims=True), 0.0)
                dsv16 = dsv.astype(BF16)
                dws_ref[g] += jnp.where(mask, lax.dot_general(dsv16, vt, _DIMS["nt"], preferred_element_type=F32), 0.0)
                dvln_ref[rs, cs] = lax.dot_general(wsm, dsv16, _DIMS["tn"], preferred_element_type=F32)
        dvln = dvln_ref[...]
        dvg_ref[...] += jnp.sum(dvln * vhat, axis=0, keepdims=True)
        dvb_ref[...] += jnp.sum(dvln, axis=0, keepdims=True)
        dvh = dvln * g_ref[...]
        vc = _gelu(vpre)
        vc = vc - jnp.mean(vc, axis=-1, keepdims=True)
        rstd = lax.rsqrt(jnp.mean(vc * vc, axis=-1, keepdims=True) + NORM_EPS)
        dv = rstd * (dvh - jnp.mean(dvh, axis=-1, keepdims=True) - vhat * jnp.mean(dvh * vhat, axis=-1, keepdims=True))
        o_ref[:, D:] = (dv * _gelu_grad(vpre)).astype(BF16)

    return _pcall(body, name=name,
                  out_shape=(_sds((T, 2 * D), BF16), _sds((G, C, C), F32), _sds((C, 128), F32), _sds((1, D), F32),
                             _sds((1, D), F32)),
                  grid=(T // tt,),
                  in_specs=[_row(tt, D, 0), _row(tt, D, 1), _row(tt, D), _const((1, D)), _const((1, D)),
                            _const((G, C, C)), _const((G, C, gd))],
                  out_specs=(_row(tt, 2 * D), _const((G, C, C)), _const((C, 128)), _const((1, D)), _const((1, D))),
                  scratch_shapes=[pltpu.VMEM((tt, D), BF16), pltpu.VMEM((tt, D), F32)],
                  compiler_params=_params("arbitrary"))(uv, uv, dgated, v_g, v_b, w_s, bias_b)


def _c_mid_fwd(ag, conv_w, conv_b, ln_g, ln_b, name):
    T, D2 = ag.shape
    D = D2 // 2
    width = conv_w.shape[0]
    tt = _fit(T, ROW_TILE, HALO)
    chunks, cw = _lanes(D)

    def body(a_ref, g_ref, ah_ref, gh_ref, w_ref, cb_ref, lg_ref, lb_ref, y2_ref, o_ref, win_ref):
        i = pl.program_id(0)
        for sl in chunks:
            yh = ah_ref[:, sl].astype(F32) * jax.nn.sigmoid(gh_ref[:, sl].astype(F32))
            win_ref[0:HALO, :] = jnp.where(i > 0, yh, 0.0)
            win_ref[HALO:, :] = a_ref[:, sl].astype(F32) * jax.nn.sigmoid(g_ref[:, sl].astype(F32))
            acc = jnp.zeros((tt, cw), F32)
            for k, xs in _causal_taps(win_ref[...], width, tt):
                acc = acc + w_ref[k:k + 1, sl] * xs
            y2_ref[:, sl] = acc + cb_ref[:, sl]
        y3 = _ln_stats(y2_ref[...]) * lg_ref[...] + lb_ref[...]
        o_ref[...] = (y3 * jax.nn.sigmoid(y3)).astype(BF16)

    return _pcall(body, name=name, out_shape=(_sds((T, D), F32), _sds((T, D), BF16)), grid=(T // tt,),
                  in_specs=[_row(tt, D, 0), _row(tt, D, 1), _prev_halo(tt, D, 0), _prev_halo(tt, D, 1),
                            _const((width, D)), _const((1, D)), _const((1, D)), _const((1, D))],
                  out_specs=(_row(tt, D), _row(tt, D)), scratch_shapes=[pltpu.VMEM((HALO + tt, cw), F32)],
                  compiler_params=_params("parallel"))(ag, ag, ag, ag, conv_w, conv_b, ln_g, ln_b)


def _c_mid_bwd(ag, y2, dy4, conv_w, ln_g, ln_b, name):
    T, D2 = ag.shape
    D = D2 // 2
    width = conv_w.shape[0]
    tt = _fit(T, ROW_TILE, HALO)
    chunks, cw = _lanes(D)
    n_tiles = T // tt

    def ln_silu_bwd(y2v, dy4v, lg, lb):
        mu = jnp.mean(y2v, axis=-1, keepdims=True)
        yc = y2v - mu
        rstd = lax.rsqrt(jnp.mean(yc * yc, axis=-1, keepdims=True) + NORM_EPS)
        yh = yc * rstd
        y3 = yh * lg + lb
        sg = jax.nn.sigmoid(y3)
        dy3 = dy4v * (sg * (1.0 + y3 * (1.0 - sg)))
        dyh = dy3 * lg
        dy2 = rstd * (dyh - jnp.mean(dyh, axis=-1, keepdims=True) - yh * jnp.mean(dyh * yh, axis=-1, keepdims=True))
        return dy2, dy3, yh

    def body(a_ref, g_ref, ah_ref, gh_ref, y2_ref, y2n_ref, d_ref, dn_ref, w_ref, lg_ref, lb_ref,
             o_ref, dw_ref, dcb_ref, dlg_ref, dlb_ref, win_ref, dwin_ref):
        i = pl.program_id(0)

        @pl.when(i == 0)
        def _():
            dw_ref[...] = jnp.zeros_like(dw_ref)
            dcb_ref[...] = jnp.zeros_like(dcb_ref)
            dlg_ref[...] = jnp.zeros_like(dlg_ref)
            dlb_ref[...] = jnp.zeros_like(dlb_ref)

        lg = lg_ref[...]
        lb = lb_ref[...]
        dy2, dy3, yh = ln_silu_bwd(y2_ref[...], d_ref[...].astype(F32), lg, lb)
        dlg_ref[...] += jnp.sum(dy3 * yh, axis=0, keepdims=True)
        dlb_ref[...] += jnp.sum(dy3, axis=0, keepdims=True)
        dcb_ref[...] += jnp.sum(dy2, axis=0, keepdims=True)
        dwin_ref[0:tt, :] = dy2
        dy2n, _, _ = ln_silu_bwd(y2n_ref[...], dn_ref[...].astype(F32), lg, lb)
        dwin_ref[tt:, :] = jnp.where(i < n_tiles - 1, dy2n, 0.0)
        for ci, sl in enumerate(chunks):
            a = a_ref[:, sl].astype(F32)
            sg = jax.nn.sigmoid(g_ref[:, sl].astype(F32))
            yh1 = ah_ref[:, sl].astype(F32) * jax.nn.sigmoid(gh_ref[:, sl].astype(F32))
            win_ref[0:HALO, :] = jnp.where(i > 0, yh1, 0.0)
            win_ref[HALO:, :] = a * sg
            d2 = dwin_ref[0:tt, sl]
            for k, xs in _causal_taps(win_ref[...], width, tt):
                dw_ref[k:k + 1, sl] += jnp.sum(d2 * xs, axis=0, keepdims=True)
            d1 = jnp.zeros((tt, cw), F32)
            for k, xs in _anticausal_taps(dwin_ref[:, sl], width, tt):
                d1 = d1 + w_ref[k:k + 1, sl] * xs
            o_ref[:, ci * cw:(ci + 1) * cw] = (d1 * sg).astype(BF16)
            o_ref[:, D + ci * cw:D + (ci + 1) * cw] = (d1 * a * sg * (1.0 - sg)).astype(BF16)

    return _pcall(body, name=name,
                  out_shape=(_sds((T, 2 * D), BF16), _sds((width, D), F32), _sds((1, D), F32), _sds((1, D), F32),
                             _sds((1, D), F32)),
                  grid=(n_tiles,),
                  in_specs=[_row(tt, D, 0), _row(tt, D, 1), _prev_halo(tt, D, 0), _prev_halo(tt, D, 1),
                            _row(tt, D), _next_halo(tt, D, T), _row(tt, D), _next_halo(tt, D, T),
                            _const((width, D)), _const((1, D)), _const((1, D))],
                  out_specs=(_row(tt, 2 * D), _const((width, D)), _const((1, D)), _const((1, D)), _const((1, D))),
                  scratch_shapes=[pltpu.VMEM((HALO + tt, cw), F32), pltpu.VMEM((tt + HALO, D), F32)],
                  compiler_params=_params("arbitrary"))(ag, ag, ag, ag, y2, y2, dy4, dy4, conv_w, ln_g, ln_b)


def _place():
    x, y, c = lax.axis_index("x"), lax.axis_index("y"), lax.axis_index("c")
    return x, y, c


def _slot(px, py, pc):
    return 4 * px + 2 * py + pc


def _all_gather(shards, name):
    n = len(shards)

    def body(*refs):
        ins, outs = refs[:n], refs[n:2 * n]
        send_sems, recv_sems, local_sems = refs[2 * n:]
        x, y, c = _place()
        me, sibling = (x, y, c), (x, y, 1 - c)
        chips = [(1 - x, y), (x, 1 - y), (1 - x, 1 - y)]

        def copy(t, k, block, to, src=None):
            dst = outs[t].at[_slot(*block)]
            return pltpu.make_async_remote_copy(
                src_ref=dst if src is None else src, dst_ref=dst, send_sem=send_sems.at[t, k],
                recv_sem=recv_sems.at[t, k], device_id=to, device_id_type=MESH)

        mine = [pltpu.make_async_copy(ins[t], outs[t].at[_slot(*me)], local_sems.at[t]) for t in range(n)]
        for cp in mine:
            cp.start()
        first = []
        for j, chip in enumerate(chips):
            first += [copy(t, 1 + j, me, (*chip, c), src=ins[t]) for t in range(n)]
        first += [copy(t, 0, me, sibling, src=ins[t]) for t in range(n)]
        for cp in first:
            cp.start()
        passed = []
        for j, chip in enumerate(chips):
            for t in range(n):
                copy(t, 1 + j, (*chip, c), me).wait_recv()
                cp = copy(t, 4 + j, (*chip, c), sibling)
                cp.start()
                passed.append(cp)
        for t in range(n):
            copy(t, 0, sibling, me).wait_recv()
            for j, chip in enumerate(chips):
                copy(t, 4 + j, (*chip, 1 - c), me).wait_recv()
        for cp in first + passed:
            cp.wait_send()
        for cp in mine:
            cp.wait()

    outs = _pcall(
        body, name=name, out_shape=tuple(_sds((N_DEV,) + s.shape, s.dtype) for s in shards),
        in_specs=[ANY] * n, out_specs=(ANY,) * n,
        scratch_shapes=[pltpu.SemaphoreType.DMA((n, 7)), pltpu.SemaphoreType.DMA((n, 7)), pltpu.SemaphoreType.DMA((n,))],
    )(*shards)
    return list(outs)


def _reduce_scatter_send(parts, name):
    n = len(parts)

    def body(*refs):
        ins, outs = refs[:n], refs[n:2 * n]
        send_sems, recv_sems, local_sems = refs[2 * n:]
        x, y, c = _place()
        my_slot = _slot(x, y, c)
        mine = [pltpu.make_async_copy(ins[t].at[my_slot], outs[t].at[my_slot], local_sems.at[t]) for t in range(n)]
        for cp in mine:
            cp.start()
        peers = []
        for j in range(1, N_DEV):
            fx, fy, fc = (j >> 2) & 1, (j >> 1) & 1, j & 1
            peers.append((1 - x if fx else x, 1 - y if fy else y, 1 - c if fc else c))

        def copy(t, k, peer):
            return pltpu.make_async_remote_copy(
                src_ref=ins[t].at[_slot(*peer)], dst_ref=outs[t].at[my_slot], send_sem=send_sems.at[t, k],
                recv_sem=recv_sems.at[t, k], device_id=peer, device_id_type=MESH)

        def landing(t, k, peer):
            dst = outs[t].at[_slot(*peer)]
            return pltpu.make_async_remote_copy(
                src_ref=dst, dst_ref=dst, send_sem=send_sems.at[t, k], recv_sem=recv_sems.at[t, k],
                device_id=peer, device_id_type=MESH)

        sent = [copy(t, k, peer) for k, peer in enumerate(peers) for t in range(n)]
        for cp in sent:
            cp.start()
        for k, peer in enumerate(peers):
            for t in range(n):
                landing(t, k, peer).wait_recv()
        for cp in sent:
            cp.wait_send()
        for cp in mine:
            cp.wait()

    outs = _pcall(
        body, name=name, out_shape=tuple(_sds(p.shape, p.dtype) for p in parts),
        in_specs=[ANY] * n, out_specs=(ANY,) * n,
        scratch_shapes=[pltpu.SemaphoreType.DMA((n, 7)), pltpu.SemaphoreType.DMA((n, 7)), pltpu.SemaphoreType.DMA((n,))],
    )(*parts)
    return list(outs)


def _reduce_adam(recvs, w, m, v, name):
    L, r, c = w.shape
    tr = _fit(r, max(16, (256 * 1024) // c), 16)
    ni = r // tr

    def recv_spec(l0):
        def index(l, i):
            return 0, jnp.where(l == l0, i, jnp.where(l < l0, 0, ni - 1)), 0
        return pl.BlockSpec((N_DEV, tr, c), index)

    lay = pl.BlockSpec((None, tr, c), lambda l, i: (l, i, 0))

    def body(*refs):
        recv_refs = refs[:L]
        w_ref, m_ref, v_ref, g_out, d_out, m_out, v_out = refs[L:]
        l = pl.program_id(0)
        for l0 in range(L):
            @pl.when(l == l0)
            def _(l0=l0):
                g = recv_refs[l0][0].astype(F32)
                for s in range(1, N_DEV):
                    g = g + recv_refs[l0][s].astype(F32)
                mn = ADAM_B1 * m_ref[...] + (1.0 - ADAM_B1) * g
                vn = ADAM_B2 * v_ref[...] + (1.0 - ADAM_B2) * (g * g)
                m_hat = mn / (1.0 - ADAM_B1 ** ADAM_STEP)
                v_hat = vn / (1.0 - ADAM_B2 ** ADAM_STEP)
                g_out[...] = g
                d_out[...] = -ADAM_LR * (m_hat / (jnp.sqrt(v_hat) + ADAM_EPS) + ADAM_WD * w_ref[...])
                m_out[...] = mn
                v_out[...] = vn

    return _pcall(body, name=name, out_shape=(_sds((L, r, c), F32),) * 4, grid=(L, ni),
                  in_specs=[recv_spec(l0) for l0 in range(L)] + [lay, lay, lay], out_specs=(lay,) * 4,
                  compiler_params=_params("arbitrary", "arbitrary"))(*recvs, w, m, v)


_SMALL_SHARDED = ["mix_norm", "xa_norm", "ffn_norm", "a_conv_w", "c_conv_w", "c_conv_b", "c_ln_g", "c_ln_b"]
_SMALL_REPLICATED = ["b_v_g", "b_v_b", "b_w_s", "b_s_bias"]
_BIG = ["xa_wq", "xa_wkv", "xa_wo", "ffn_w_gu", "ffn_w_down", "a_w_in", "a_w_out", "b_w_in", "b_w_out", "c_w_in",
        "c_w_out"]
_COL_SHARDED = {"xa_wkv", "ffn_w_gu", "a_w_in", "b_w_in", "c_w_in"}
_WEIGHTS = ["mix_norm", "xa_norm", "xa_wq", "xa_wkv", "xa_wo", "ffn_norm", "ffn_w_gu", "ffn_w_down", "a_w_in",
            "a_conv_w", "a_w_out", "b_w_in", "b_v_g", "b_v_b", "b_w_s", "b_s_bias", "b_w_out", "c_w_in", "c_conv_w",
            "c_conv_b", "c_ln_g", "c_ln_b", "c_w_out"]
_MIXER = "abc"


def _pack_rows(arrays, width, fills):
    rows = [a.reshape(-1, width) if a.size % width == 0 else
            jnp.pad(a.reshape(1, -1), ((0, 0), (0, width - a.size)), constant_values=f)
            for a, f in zip(arrays, fills)]
    n = sum(r.shape[0] for r in rows)
    pad = (-n) % 8
    if pad:
        rows.append(jnp.full((pad, width), fills[-1], F32))
    return jnp.concatenate(rows, axis=0)


def _unpack_rows(packed, like):
    out, r = [], 0
    width = packed.shape[-1]
    for shape in like:
        size = 1
        for s in shape:
            size *= s
        nr = -(-size // width)
        out.append(packed[r:r + nr].reshape(-1)[:size].reshape(shape))
        r += nr
    return out


def kernel(x, mem, mix_norm, xa_norm, xa_wq, xa_wkv, xa_wo, ffn_norm, ffn_w_gu, ffn_w_down, a_w_in, a_conv_w, a_w_out, b_w_in, b_v_g, b_v_b, b_w_s, b_s_bias, b_w_out, c_w_in, c_conv_w, c_conv_b, c_ln_g, c_ln_b, c_w_out, loss_target, m_mix_norm, m_xa_norm, m_xa_wq, m_xa_wkv, m_xa_wo, m_ffn_norm, m_ffn_w_gu, m_ffn_w_down, m_a_w_in, m_a_conv_w, m_a_w_out, m_b_w_in, m_b_v_g, m_b_v_b, m_b_w_s, m_b_s_bias, m_b_w_out, m_c_w_in, m_c_conv_w, m_c_conv_b, m_c_ln_g, m_c_ln_b, m_c_w_out, v_mix_norm, v_xa_norm, v_xa_wq, v_xa_wkv, v_xa_wo, v_ffn_norm, v_ffn_w_gu, v_ffn_w_down, v_a_w_in, v_a_conv_w, v_a_w_out, v_b_w_in, v_b_v_g, v_b_v_b, v_b_w_s, v_b_s_bias, v_b_w_out, v_c_w_in, v_c_conv_w, v_c_conv_b, v_c_ln_g, v_c_ln_b, v_c_w_out):
    P = dict(locals())
    T, D = x.shape[1], x.shape[2]
    dl = D // N_DEV
    depth = mix_norm.shape[0]
    x0, mem0, target = x[0], mem[0], loss_target[0]
    my_slot = _slot(*_place())

    sh_shapes = [P[n].shape for n in _SMALL_SHARDED]
    packed = _pack_rows([P[n] for n in _SMALL_SHARDED], dl, [0.0] * len(_SMALL_SHARDED))
    n_sh = packed.shape[0]
    gathered = _all_gather([packed], "ag_small")[0]
    full_rows = jnp.transpose(gathered, (1, 0, 2)).reshape(n_sh, D)
    small = dict(zip(_SMALL_SHARDED, _unpack_rows(full_rows, [s[:-1] + (D,) for s in sh_shapes])))
    G, C = b_w_s.shape[1], b_w_s.shape[2]
    gd = D // G
    bias_b = jnp.broadcast_to(b_s_bias[0][:, :, None], (G, C, gd))

    w16 = {n: P[n].astype(BF16) for n in _BIG}

    def gather_layer(i):
        kind, slot = i % N_MIXERS, i // N_MIXERS
        mx = _MIXER[kind]
        names = [(mx + "_w_in", slot), (mx + "_w_out", slot), ("xa_wq", i), ("xa_wkv", i), ("xa_wo", i),
                 ("ffn_w_gu", i), ("ffn_w_down", i)]
        fulls = _all_gather([w16[n][j] for n, j in names], "ag_layer_" + ("a" if kind == 0 else "bc"))
        out = {}
        for (n, _), f in zip(names, fulls):
            key = n[2:] if n[1] == "_" and n[0] in _MIXER else n
            out[key] = f if n in _COL_SHARDED else f.reshape(-1, f.shape[-1])
        return out

    saved = []
    xin = x0
    h = _rms_fwd(x0, small["mix_norm"][0, 0][None], "rms_first")
    for i in range(depth):
        kind, slot = i % N_MIXERS, i // N_MIXERS
        W = gather_layer(i)
        S = {"W": W, "x0": xin, "h0": h}
        pre = _mm(h, W["w_in"], "nn", BF16, "mm_in_%s" % _MIXER[kind], b_blocked=True)
        S["pre"] = pre
        if kind == 0:
            mid = _a_mid_fwd(pre, small["a_conv_w"][slot], "a_mid_fwd")
        elif kind == 1:
            mid = _b_mid_fwd(pre, b_v_g, b_v_b, b_w_s[0], bias_b, "b_mid_fwd")
        else:
            y2c, mid = _c_mid_fwd(pre, small["c_conv_w"][slot], small["c_conv_b"], small["c_ln_g"], small["c_ln_b"],
                                  "c_mid_fwd")
            S["y2c"] = y2c
        S["mid"] = mid
        S["y0"] = _mm(mid, W["w_out"], "nn", F32, "mm_out")
        xin, h = _post_pre_fwd(xin, S["y0"], small["mix_norm"][i, 1][None], small["xa_norm"][i, 0][None], "post_pre")
        S["x1"], S["h1"] = xin, h
        S["q"] = _mm(h, W["xa_wq"], "nn", BF16, "mm_q")
        S["memn"] = _rms_fwd(mem0, small["xa_norm"][i, 2][None], "rms_mem")
        S["kv"] = _mm(S["memn"], W["xa_wkv"], "nn", BF16, "mm_kv", b_blocked=True)
        S["o"] = _attn_fwd(S["q"], S["kv"], "attn_fwd")
        S["y1"] = _mm(S["o"], W["xa_wo"], "nn", F32, "mm_out")
        xin, h = _post_pre_fwd(xin, S["y1"], small["xa_norm"][i, 1][None], small["ffn_norm"][i, 0][None], "post_pre")
        S["x2"], S["h2"] = xin, h
        S["gate"], S["up"], S["act"] = _ffn_gu_fwd(h, W["ffn_w_gu"], "ffn_gu_fwd")
        S["y2"] = _mm(S["act"], W["ffn_w_down"], "nn", F32, "mm_down")
        if i + 1 < depth:
            xin, h = _post_pre_fwd(xin, S["y2"], small["ffn_norm"][i, 1][None], small["mix_norm"][i + 1, 0][None],
                                   "post_pre")
        saved.append(S)

    last = saved[-1]
    loss_part, dx, dy, dg = _final_fwd_loss(xin, last["y2"], small["ffn_norm"][depth - 1, 1][None], target, "final_loss")
    loss = lax.psum(loss_part[0, 0], ("x", "y", "c"))

    zero_row = jnp.zeros((1, D), F32)
    g_mix = [[zero_row, zero_row] for _ in range(depth)]
    g_xa = [[zero_row, zero_row, zero_row] for _ in range(depth)]
    g_ffn = [[zero_row, zero_row] for _ in range(depth)]
    g_small = {}
    recv = {n: [None] * P[n].shape[0] for n in _BIG}
    g_ffn[depth - 1][1] = dg

    for i in reversed(range(depth)):
        kind, slot = i % N_MIXERS, i // N_MIXERS
        mx = _MIXER[kind]
        S = saved[i]
        W = S["W"]
        dgu = _ffn_dgu_bwd(dy, W["ffn_w_down"], S["gate"], S["up"], "ffn_dgu_bwd")
        dw_down = _mm(S["act"], dy, "tn", BF16, "mm_dw_down")
        dh = _mm(dgu, W["ffn_w_gu"], "nt", F32, "mm_dh_gu", b_blocked=True)
        dw_gu = _mm(S["h2"], dgu, "tn", BF16, "mm_dw_gu", out_blocks=N_DEV)
        dx, g_ffn[i][0], dy, g_xa[i][1] = _pre_post_bwd(dx, dh, S["x2"], small["ffn_norm"][i, 0][None], S["y1"],
                                                         small["xa_norm"][i, 1][None], "pre_post_bwd")
        do = _mm(dy, W["xa_wo"], "nt", BF16, "mm_nt_dd16")
        dw_o = _mm(S["o"], dy, "tn", BF16, "mm_dw_dd")
        dq, dkv = _attn_bwd(S["q"], S["kv"], do, "attn_bwd")
        dkv16 = dkv.astype(BF16)
        dh = _mm(dq, W["xa_wq"], "nt", F32, "mm_nt_dd32")
        dw_q = _mm(S["h1"], dq, "tn", BF16, "mm_dw_dd")
        dw_kv = _mm(S["memn"], dkv16, "tn", BF16, "mm_dw_kv", out_blocks=N_DEV)
        dmemn = _mm(dkv16, W["xa_wkv"], "nt", F32, "mm_dmem", b_blocked=True)
        g_xa[i][2] = _rms_gain_grad(dmemn, mem0, "rms_gain_grad")
        dx, g_xa[i][0], dy, g_mix[i][1] = _pre_post_bwd(dx, dh, S["x1"], small["xa_norm"][i, 0][None], S["y0"],
                                                         small["mix_norm"][i, 1][None], "pre_post_bwd")
        dmid = _mm(dy, W["w_out"], "nt", BF16, "mm_nt_dd16")
        dw_out = _mm(S["mid"], dy, "tn", BF16, "mm_dw_dd")
        if kind == 0:
            dpre, dcw = _a_mid_bwd(S["pre"], dmid, small["a_conv_w"][slot], "a_mid_bwd")
            g_small.setdefault("a_conv_w", {})[slot] = dcw
        elif kind == 1:
            dpre, dws, dsb, dvg, dvb = _b_mid_bwd(S["pre"], dmid, b_v_g, b_v_b, b_w_s[0], bias_b, "b_mid_bwd")
            g_small.update(b_w_s=dws[None], b_s_bias=jnp.transpose(dsb[:, :G])[None], b_v_g=dvg, b_v_b=dvb)
        else:
            dpre, dcw, dcb, dlg, dlb = _c_mid_bwd(S["pre"], S["y2c"], dmid, small["c_conv_w"][slot], small["c_ln_g"],
                                                  small["c_ln_b"], "c_mid_bwd")
            g_small.update(c_conv_w=dcw[None], c_conv_b=dcb, c_ln_g=dlg, c_ln_b=dlb)
        dh = _mm(dpre, W["w_in"], "nt", F32, "mm_dh_in_%s" % mx, b_blocked=True)
        dw_in = _mm(S["h0"], dpre, "tn", BF16, "mm_dw_in_%s" % mx, out_blocks=N_DEV)
        if i > 0:
            dx, g_mix[i][0], dy, g_ffn[i - 1][1] = _pre_post_bwd(
                dx, dh, S["x0"], small["mix_norm"][i, 0][None], saved[i - 1]["y2"],
                small["ffn_norm"][i - 1, 1][None], "pre_post_bwd")
        else:
            dx, g_mix[i][0] = _pre_post_bwd(dx, dh, S["x0"], small["mix_norm"][i, 0][None], None, None, "pre_bwd")
        names = [(mx + "_w_in", slot, dw_in), (mx + "_w_out", slot, dw_out), ("xa_wq", i, dw_q), ("xa_wkv", i, dw_kv),
                 ("xa_wo", i, dw_o), ("ffn_w_gu", i, dw_gu), ("ffn_w_down", i, dw_down)]
        parts = [g if n in _COL_SHARDED else g.reshape(N_DEV, -1, g.shape[-1]) for n, _, g in names]
        landed = _reduce_scatter_send(parts, "rs_layer_" + ("a" if kind == 0 else "bc"))
        for (n, j, _), r in zip(names, landed):
            recv[n][j] = r
        S.clear()
    grad_x = dx[None]

    g_small["mix_norm"] = jnp.stack([jnp.concatenate(r, axis=0) for r in g_mix])
    g_small["xa_norm"] = jnp.stack([jnp.concatenate(r, axis=0) for r in g_xa])
    g_small["ffn_norm"] = jnp.stack([jnp.concatenate(r, axis=0) for r in g_ffn])
    g_small["a_conv_w"] = jnp.stack([g_small["a_conv_w"][s] for s in range(a_conv_w.shape[0])])
    part_sh = _pack_rows([g_small[n] for n in _SMALL_SHARDED], D, [0.0] * len(_SMALL_SHARDED))
    part_rep = _pack_rows([g_small[n] for n in _SMALL_REPLICATED], D, [0.0] * len(_SMALL_REPLICATED))
    n_rep = part_rep.shape[0]
    parts_all = _all_gather([jnp.concatenate([part_sh, part_rep], axis=0)], "ag_small_grads")[0]
    recv_sh = lax.dynamic_slice_in_dim(parts_all[:, :n_sh], my_slot * dl, dl, axis=2)
    recv_rep = parts_all[:, n_sh:]

    out = {}

    def adam_small(names, recv_s, width, name):
        shapes = [P[n].shape for n in names]
        pw = _pack_rows([P[n] for n in names], width, [0.0] * len(names))
        pm = _pack_rows([P["m_" + n] for n in names], width, [0.0] * len(names))
        pv = _pack_rows([P["v_" + n] for n in names], width, [1.0] * len(names))
        res = _reduce_adam([recv_s], pw[None], pm[None], pv[None], name)
        for kind, r in zip(("grad", "delta", "new_m", "new_v"), res):
            for n, a in zip(names, _unpack_rows(r[0], shapes)):
                out[kind + "_" + n] = a

    adam_small(_SMALL_SHARDED, recv_sh, dl, "adam_small_sharded")
    adam_small(_SMALL_REPLICATED, recv_rep, D, "adam_small_replicated")
    for n in _BIG:
        res = _reduce_adam(recv[n], P[n], P["m_" + n], P["v_" + n], "adam_" + n)
        for kind, r in zip(("grad", "delta", "new_m", "new_v"), res):
            out[kind + "_" + n] = r

    return (loss, grad_x, *[out[k + "_" + n] for k in ("grad", "delta", "new_m", "new_v") for n in _WEIGHTS])
```

```python
import functools

import jax
import jax.numpy as jnp
from jax import lax
from jax.experimental import pallas as pl
from jax.experimental.pallas import tpu as pltpu

F32 = jnp.float32
BF16 = jnp.bfloat16
MESH = pl.DeviceIdType.MESH
ANY = pl.BlockSpec(memory_space=pl.ANY)

N_DEV = 8
N_MIXERS = 3
XA_HEADS = 4
GMLP_GROUPS = 8
CHUNK = 128
NORM_EPS = 1e-6
HALO = 32
ROW_TILE = 256
CONV_LANES = 512
V7X_VMEM_LIMIT = 56 * 1024 * 1024

ADAM_LR = 0.001
ADAM_B1 = 0.9
ADAM_B2 = 0.999
ADAM_EPS = 1e-08
ADAM_WD = 0.01
ADAM_STEP = 10


def _pcall(body, **kw):
    return pl.pallas_call(body, **kw)


def _params(*sem):
    return pltpu.CompilerParams(dimension_semantics=sem, vmem_limit_bytes=V7X_VMEM_LIMIT)


def _fit(n, pref, mult=128):
    if n <= pref:
        return n
    t = (pref // mult) * mult
    while t >= mult:
        if n % t == 0:
            return t
        t -= mult
    return n


def _sds(shape, dtype):
    return jax.ShapeDtypeStruct(shape, dtype)


_DIMS = {"nn": (((1,), (0,)), ((), ())), "nt": (((1,), (1,)), ((), ())), "tn": (((0,), (0,)), ((), ()))}
MM_VMEM_BUDGET = 44 * 1024 * 1024


def _gcd(a, b):
    while b:
        a, b = b, a % b
    return a


def _mm(a, b, mode, out_dtype, name, *, a_blocked=False, b_blocked=False, out_blocks=None, dep=None):
    if mode == "tn":
        K, M = a.shape
    elif a_blocked:
        sa, M, ca = a.shape
        K = sa * ca
    else:
        M, K = a.shape
    n_unit = k_unit = None
    if b_blocked:
        _, d1, cb = b.shape
        if mode == "nt":
            N, k_unit = d1, cb
        else:
            N, n_unit = b.shape[0] * cb, cb
    else:
        N = b.shape[0] if mode == "nt" else b.shape[1]
    n_unit = n_unit or N
    k_unit = k_unit or K
    if a_blocked:
        k_unit = _gcd(k_unit, ca)
    if out_blocks:
        n_unit = _gcd(n_unit, N // out_blocks)
    tn = _fit(n_unit, 1536)
    tk = k_unit
    out_bytes = jnp.dtype(out_dtype).itemsize

    def need(tm_):
        nk_ = K // tk
        return (4 * (tm_ * tk + tk * tn) + 2 * tm_ * tn * out_bytes + 4 * tm_ * tn * (2 if nk_ > 1 else 1))

    tm = _fit(M, 1024)
    while need(tm) > MM_VMEM_BUDGET and tm % 256 == 0:
        tm //= 2
    nk = K // tk

    if mode == "tn":
        a_spec = pl.BlockSpec((tk, tm), lambda i, j, k: (k, i))
    elif a_blocked:
        ka = ca // tk
        a_spec = pl.BlockSpec((None, tm, tk), lambda i, j, k: (k // ka, i, k % ka))
    else:
        a_spec = pl.BlockSpec((tm, tk), lambda i, j, k: (i, k))
    if b_blocked and mode == "nt":
        kb = cb // tk
        b_spec = pl.BlockSpec((None, tn, tk), lambda i, j, k: (k // kb, j, k % kb))
    elif b_blocked:
        nb = cb // tn
        b_spec = pl.BlockSpec((None, tk, tn), lambda i, j, k: (j // nb, k, j % nb))
    elif mode == "nt":
        b_spec = pl.BlockSpec((tn, tk), lambda i, j, k: (j, k))
    else:
        b_spec = pl.BlockSpec((tk, tn), lambda i, j, k: (k, j))
    if out_blocks:
        ob = (N // out_blocks) // tn
        out_shape = _sds((out_blocks, M, N // out_blocks), out_dtype)
        o_spec = pl.BlockSpec((None, tm, tn), lambda i, j, k: (j // ob, i, j % ob))
    else:
        out_shape = _sds((M, N), out_dtype)
        o_spec = pl.BlockSpec((tm, tn), lambda i, j, k: (i, j))
    dims = _DIMS[mode]
    n_in = 2 if dep is None else 3

    def body(*refs):
        a_ref, b_ref = refs[0], refs[1]
        o_ref = refs[n_in]
        p = lax.dot_general(a_ref[...], b_ref[...], dims, preferred_element_type=F32)
        if nk == 1:
            o_ref[...] = p.astype(o_ref.dtype)
            return
        acc_ref = refs[n_in + 1]
        k = pl.program_id(2)

        @pl.when(k == 0)
        def _():
            acc_ref[...] = p

        @pl.when(k > 0)
        def _():
            acc_ref[...] += p

        @pl.when(k == nk - 1)
        def _():
            o_ref[...] = acc_ref[...].astype(o_ref.dtype)

    ins, in_specs = [a, b], [a_spec, b_spec]
    if dep is not None:
        ins.append(dep)
        in_specs.append(ANY)
    return _pcall(
        body, name=name, out_shape=out_shape, grid=(M // tm, N // tn, nk),
        in_specs=in_specs, out_specs=o_spec,
        scratch_shapes=[pltpu.VMEM((tm, tn), F32)] if nk > 1 else [],
        compiler_params=_params("parallel", "parallel", "arbitrary"),
    )(*ins)


def _rstd(v):
    return lax.rsqrt(jnp.mean(v * v, axis=-1, keepdims=True) + NORM_EPS)


def _rms_bwd_rows(v, g, dout):
    r = _rstd(v)
    vh = v * r
    dvh = dout * g
    dv = r * (dvh - vh * jnp.mean(dvh * vh, axis=-1, keepdims=True))
    return dv, jnp.sum(dout * vh, axis=0, keepdims=True)


def _row(tt, d, col=0):
    return pl.BlockSpec((tt, d), lambda i: (i, col))


def _const(shape):
    return pl.BlockSpec(shape, lambda i: (0,) * len(shape))


def _prev_halo(tt, d, col=0):
    return pl.BlockSpec((HALO, d), lambda i: (jnp.maximum(i * (tt // HALO) - 1, 0), col))


def _next_halo(tt, d, rows, col=0):
    last = rows // HALO - 1
    return pl.BlockSpec((HALO, d), lambda i: (jnp.minimum((i + 1) * (tt // HALO), last), col))


def _rms_fwd(x, g, name):
    T, D = x.shape
    tt = _fit(T, ROW_TILE, 8)

    def body(x_ref, g_ref, h_ref):
        v = x_ref[...]
        h_ref[...] = (v * _rstd(v) * g_ref[...]).astype(BF16)

    return _pcall(body, name=name, out_shape=_sds((T, D), BF16), grid=(T // tt,),
                  in_specs=[_row(tt, D), _const((1, D))], out_specs=_row(tt, D),
                  compiler_params=_params("parallel"))(x, g)


def _post_pre_fwd(x, y, g_post, g_pre, name):
    T, D = x.shape
    tt = _fit(T, ROW_TILE, 8)

    def body(x_ref, y_ref, gp_ref, gn_ref, xo_ref, h_ref):
        y = y_ref[...]
        xn = x_ref[...] + y * _rstd(y) * gp_ref[...]
        xo_ref[...] = xn
        h_ref[...] = (xn * _rstd(xn) * gn_ref[...]).astype(BF16)

    return _pcall(body, name=name, out_shape=(_sds((T, D), F32), _sds((T, D), BF16)), grid=(T // tt,),
                  in_specs=[_row(tt, D), _row(tt, D), _const((1, D)), _const((1, D))],
                  out_specs=(_row(tt, D), _row(tt, D)),
                  compiler_params=_params("parallel"))(x, y, g_post, g_pre)


def _final_fwd_loss(x, y, g_post, target, name):
    T, D = x.shape
    tt = _fit(T, ROW_TILE, 8)

    def body(x_ref, y_ref, g_ref, t_ref, loss_ref, dx_ref, dy_ref, dg_ref):
        i = pl.program_id(0)
        y = y_ref[...]
        g = g_ref[...]
        err = x_ref[...] + y * _rstd(y) * g - t_ref[...]
        part = 0.5 * jnp.sum(jnp.mean(err * err, axis=-1, keepdims=True))
        dx = err / D
        dx_ref[...] = dx
        dy, dg = _rms_bwd_rows(y, g, dx)
        dy_ref[...] = dy.astype(BF16)

        @pl.when(i == 0)
        def _():
            loss_ref[...] = jnp.zeros_like(loss_ref)
            dg_ref[...] = jnp.zeros_like(dg_ref)

        loss_ref[...] += part
        dg_ref[...] += dg

    return _pcall(body, name=name,
                  out_shape=(_sds((1, 128), F32), _sds((T, D), F32), _sds((T, D), BF16), _sds((1, D), F32)),
                  grid=(T // tt,),
                  in_specs=[_row(tt, D), _row(tt, D), _const((1, D)), _row(tt, D)],
                  out_specs=(_const((1, 128)), _row(tt, D), _row(tt, D), _const((1, D))),
                  compiler_params=_params("arbitrary"))(x, y, g_post, target)


def _pre_post_bwd(dx_out, dh, x_in, g_pre, y_prev, g_post_prev, name):
    T, D = x_in.shape
    tt = _fit(T, ROW_TILE, 8)
    with_prev = y_prev is not None

    def body(*refs):
        if with_prev:
            dxo_ref, dh_ref, x_ref, g_ref, y_ref, gp_ref, dxi_ref, dg_ref, dy_ref, dgp_ref = refs
        else:
            dxo_ref, dh_ref, x_ref, g_ref, dxi_ref, dg_ref = refs
        i = pl.program_id(0)
        dv, dg = _rms_bwd_rows(x_ref[...], g_ref[...], dh_ref[...].astype(F32))
        dxi = dxo_ref[...] + dv
        dxi_ref[...] = dxi

        @pl.when(i == 0)
        def _():
            dg_ref[...] = jnp.zeros_like(dg_ref)

        dg_ref[...] += dg
        if with_prev:
            dy, dgp = _rms_bwd_rows(y_ref[...], gp_ref[...], dxi)
            dy_ref[...] = dy.astype(BF16)

            @pl.when(i == 0)
            def _():
                dgp_ref[...] = jnp.zeros_like(dgp_ref)

            dgp_ref[...] += dgp

    ins = [dx_out, dh, x_in, g_pre]
    in_specs = [_row(tt, D), _row(tt, D), _row(tt, D), _const((1, D))]
    out_shape = [_sds((T, D), F32), _sds((1, D), F32)]
    out_specs = [_row(tt, D), _const((1, D))]
    if with_prev:
        ins += [y_prev, g_post_prev]
        in_specs += [_row(tt, D), _const((1, D))]
        out_shape += [_sds((T, D), BF16), _sds((1, D), F32)]
        out_specs += [_row(tt, D), _const((1, D))]
    return _pcall(body, name=name, out_shape=tuple(out_shape), grid=(T // tt,),
                  in_specs=in_specs, out_specs=tuple(out_specs),
                  compiler_params=_params("arbitrary"))(*ins)


def _rms_gain_grad(dout, v, name):
    T, D = v.shape
    tt = _fit(T, ROW_TILE, 8)

    def body(d_ref, v_ref, dg_ref):
        @pl.when(pl.program_id(0) == 0)
        def _():
            dg_ref[...] = jnp.zeros_like(dg_ref)

        v = v_ref[...]
        dg_ref[...] += jnp.sum(d_ref[...] * (v * _rstd(v)), axis=0, keepdims=True)

    return _pcall(body, name=name, out_shape=_sds((1, D), F32), grid=(T // tt,),
                  in_specs=[_row(tt, D), _row(tt, D)], out_specs=_const((1, D)),
                  compiler_params=_params("arbitrary"))(dout, v)


def _softmax_rows(s):
    e = jnp.exp(s - jnp.max(s, axis=-1, keepdims=True))
    return e / jnp.sum(e, axis=-1, keepdims=True)


def _attn_fwd(q, kv, name):
    T, D = q.shape
    nm = kv.shape[0]
    hd = D // XA_HEADS
    scale = hd ** -0.5
    tq = _fit(T, ROW_TILE, 8)

    def body(q_ref, k_ref, v_ref, o_ref):
        for h in range(XA_HEADS):
            sl = slice(h * hd, (h + 1) * hd)
            s = lax.dot_general(q_ref[:, sl], k_ref[:, sl], _DIMS["nt"], preferred_element_type=F32) * scale
            p = _softmax_rows(s)
            o_ref[:, sl] = jnp.dot(p.astype(BF16), v_ref[:, sl], preferred_element_type=F32).astype(BF16)

    return _pcall(body, name=name, out_shape=_sds((T, D), BF16), grid=(T // tq,),
                  in_specs=[_row(tq, D), pl.BlockSpec((nm, D), lambda i: (0, 0)), pl.BlockSpec((nm, D), lambda i: (0, 1))],
                  out_specs=_row(tq, D), compiler_params=_params("parallel"))(q, kv, kv)


def _attn_bwd(q, kv, do, name):
    T, D = q.shape
    nm = kv.shape[0]
    hd = D // XA_HEADS
    scale = hd ** -0.5
    tq = _fit(T, ROW_TILE, 8)

    def body(q_ref, k_ref, v_ref, do_ref, dq_ref, dkv_ref):
        @pl.when(pl.program_id(0) == 0)
        def _():
            dkv_ref[...] = jnp.zeros_like(dkv_ref)

        for h in range(XA_HEADS):
            sl = slice(h * hd, (h + 1) * hd)
            qh, kh, vh, doh = q_ref[:, sl], k_ref[:, sl], v_ref[:, sl], do_ref[:, sl]
            s = lax.dot_general(qh, kh, _DIMS["nt"], preferred_element_type=F32) * scale
            p = _softmax_rows(s)
            dp = lax.dot_general(doh, vh, _DIMS["nt"], preferred_element_type=F32)
            ds = (p * (dp - jnp.sum(dp * p, axis=-1, keepdims=True)) * scale).astype(BF16)
            dq_ref[:, sl] = jnp.dot(ds, kh, preferred_element_type=F32).astype(BF16)
            dkv_ref[:, sl] += lax.dot_general(ds, qh, _DIMS["tn"], preferred_element_type=F32)
            dkv_ref[:, D + h * hd:D + (h + 1) * hd] += lax.dot_general(
                p.astype(BF16), doh, _DIMS["tn"], preferred_element_type=F32)

    return _pcall(body, name=name, out_shape=(_sds((T, D), BF16), _sds((nm, 2 * D), F32)), grid=(T // tq,),
                  in_specs=[_row(tq, D), pl.BlockSpec((nm, D), lambda i: (0, 0)), pl.BlockSpec((nm, D), lambda i: (0, 1)),
                            _row(tq, D)],
                  out_specs=(_row(tq, D), _const((nm, 2 * D))),
                  compiler_params=_params("arbitrary"))(q, kv, kv, do)


def _ffn_gu_fwd(h, w_gu, name, dep=None, tm=512):
    T, D = h.shape
    S, _, c = w_gu.shape
    F = S * c // 2
    tm = _fit(T, tm)
    tn = _fit(c, 1536)
    nb = c // tn
    nj = F // tn
    n_in = 3 if dep is None else 4

    def w_spec(off):
        return pl.BlockSpec((None, D, tn), lambda i, j: ((j + off) // nb, 0, (j + off) % nb))

    def body(*refs):
        h_ref, wg_ref, wu_ref = refs[:3]
        g_ref, u_ref, a_ref = refs[n_in:]
        hv = h_ref[...]
        g = jnp.dot(hv, wg_ref[...], preferred_element_type=F32)
        g_ref[...] = g.astype(BF16)
        u = jnp.dot(hv, wu_ref[...], preferred_element_type=F32)
        u_ref[...] = u.astype(BF16)
        a_ref[...] = (g * jax.nn.sigmoid(g) * u).astype(BF16)

    ins = [h, w_gu, w_gu]
    in_specs = [pl.BlockSpec((tm, D), lambda i, j: (i, 0)), w_spec(0), w_spec(nj)]
    if dep is not None:
        ins.append(dep)
        in_specs.append(ANY)
    o_spec = pl.BlockSpec((tm, tn), lambda i, j: (i, j))
    return _pcall(body, name=name, out_shape=(_sds((T, F), BF16),) * 3, grid=(T // tm, nj),
                  in_specs=in_specs, out_specs=(o_spec,) * 3,
                  compiler_params=_params("parallel", "parallel"))(*ins)


def _ffn_dgu_bwd(dy, w_down, gate, up, name, dep=None, tm=512):
    T, D = dy.shape
    F = w_down.shape[0]
    tm = _fit(T, tm)
    tn = _fit(F, 1536)
    n_in = 4 if dep is None else 5

    def body(*refs):
        dy_ref, wd_ref, g_ref, u_ref = refs[:4]
        o_ref = refs[n_in]
        da = lax.dot_general(dy_ref[...], wd_ref[...], _DIMS["nt"], preferred_element_type=F32)
        g = g_ref[...].astype(F32)
        sg = jax.nn.sigmoid(g)
        o_ref[0] = (da * u_ref[...].astype(F32) * (sg * (1.0 + g * (1.0 - sg)))).astype(BF16)
        o_ref[1] = (da * (g * sg)).astype(BF16)

    ins = [dy, w_down, gate, up]
    gu_spec = pl.BlockSpec((tm, tn), lambda i, j: (i, j))
    in_specs = [pl.BlockSpec((tm, D), lambda i, j: (i, 0)), pl.BlockSpec((tn, D), lambda i, j: (j, 0)), gu_spec, gu_spec]
    if dep is not None:
        ins.append(dep)
        in_specs.append(ANY)
    return _pcall(body, name=name, out_shape=_sds((2, T, F), BF16), grid=(T // tm, F // tn),
                  in_specs=in_specs, out_specs=pl.BlockSpec((2, tm, tn), lambda i, j: (0, i, j)),
                  compiler_params=_params("parallel", "parallel"))(*ins)


def _causal_taps(win, width, tt):
    for b in range(min(8, width)):
        wb = win if b == 0 else pltpu.roll(win, b, 0)
        a = 0
        while 8 * a + b <= width - 1:
            yield width - 1 - (8 * a + b), wb[HALO - 8 * a:HALO - 8 * a + tt]
            a += 1


def _anticausal_taps(win, width, tt):
    rows = tt + HALO
    for b in range(min(8, width)):
        wb = win if b == 0 else pltpu.roll(win, rows - b, 0)
        a = 0
        while 8 * a + b <= width - 1:
            yield width - 1 - (8 * a + b), wb[8 * a:8 * a + tt]
            a += 1


def _lanes(d):
    cw = _fit(d, CONV_LANES)
    return [slice(s, s + cw) for s in range(0, d, cw)], cw


def _a_mid_fwd(bcz, conv_w, name):
    T, D3 = bcz.shape
    D = D3 // 3
    width = conv_w.shape[0]
    tt = _fit(T, ROW_TILE, HALO)
    chunks, cw = _lanes(D)

    def body(b_ref, c_ref, z_ref, ch_ref, zh_ref, w_ref, o_ref, win_ref):
        i = pl.program_id(0)
        for sl in chunks:
            uh = ch_ref[:, sl].astype(F32) * zh_ref[:, sl].astype(F32)
            win_ref[0:HALO, :] = jnp.where(i > 0, uh, 0.0)
            win_ref[HALO:, :] = c_ref[:, sl].astype(F32) * z_ref[:, sl].astype(F32)
            acc = jnp.zeros((tt, cw), F32)
            for k, xs in _causal_taps(win_ref[...], width, tt):
                acc = acc + w_ref[k:k + 1, sl] * xs
            o_ref[:, sl] = (b_ref[:, sl].astype(F32) * acc).astype(BF16)

    return _pcall(body, name=name, out_shape=_sds((T, D), BF16), grid=(T // tt,),
                  in_specs=[_row(tt, D, 0), _row(tt, D, 1), _row(tt, D, 2), _prev_halo(tt, D, 1), _prev_halo(tt, D, 2),
                            _const((width, D))],
                  out_specs=_row(tt, D), scratch_shapes=[pltpu.VMEM((HALO + tt, cw), F32)],
                  compiler_params=_params("parallel"))(bcz, bcz, bcz, bcz, bcz, conv_w)


def _a_mid_bwd(bcz, dy2, conv_w, name):
    T, D3 = bcz.shape
    D = D3 // 3
    width = conv_w.shape[0]
    tt = _fit(T, ROW_TILE, HALO)
    chunks, cw = _lanes(D)
    n_tiles = T // tt

    def body(b_ref, c_ref, z_ref, ch_ref, zh_ref, bn_ref, d_ref, dn_ref, w_ref, o_ref, dw_ref, win_ref, dwin_ref):
        i = pl.program_id(0)

        @pl.when(i == 0)
        def _():
            dw_ref[...] = jnp.zeros_like(dw_ref)

        for ci, sl in enumerate(chunks):
            c = c_ref[:, sl].astype(F32)
            z = z_ref[:, sl].astype(F32)
            b = b_ref[:, sl].astype(F32)
            d2 = d_ref[:, sl].astype(F32)
            uh = ch_ref[:, sl].astype(F32) * zh_ref[:, sl].astype(F32)
            win_ref[0:HALO, :] = jnp.where(i > 0, uh, 0.0)
            win_ref[HALO:, :] = c * z
            d1 = d2 * b
            d1n = dn_ref[:, sl].astype(F32) * bn_ref[:, sl].astype(F32)
            dwin_ref[0:tt, :] = d1
            dwin_ref[tt:, :] = jnp.where(i < n_tiles - 1, d1n, 0.0)
            y1 = jnp.zeros((tt, cw), F32)
            for k, xs in _causal_taps(win_ref[...], width, tt):
                y1 = y1 + w_ref[k:k + 1, sl] * xs
                dw_ref[k:k + 1, sl] += jnp.sum(d1 * xs, axis=0, keepdims=True)
            du = jnp.zeros((tt, cw), F32)
            for k, xs in _anticausal_taps(dwin_ref[...], width, tt):
                du = du + w_ref[k:k + 1, sl] * xs
            o_ref[:, ci * cw:(ci + 1) * cw] = (d2 * y1).astype(BF16)
            o_ref[:, D + ci * cw:D + (ci + 1) * cw] = (du * z).astype(BF16)
            o_ref[:, 2 * D + ci * cw:2 * D + (ci + 1) * cw] = (du * c).astype(BF16)

    return _pcall(body, name=name, out_shape=(_sds((T, 3 * D), BF16), _sds((width, D), F32)), grid=(n_tiles,),
                  in_specs=[_row(tt, D, 0), _row(tt, D, 1), _row(tt, D, 2), _prev_halo(tt, D, 1), _prev_halo(tt, D, 2),
                            _next_halo(tt, D, T, 0), _row(tt, D), _next_halo(tt, D, T), _const((width, D))],
                  out_specs=(_row(tt, 3 * D), _const((width, D))),
                  scratch_shapes=[pltpu.VMEM((HALO + tt, cw), F32), pltpu.VMEM((tt + HALO, cw), F32)],
                  compiler_params=_params("arbitrary"))(bcz, bcz, bcz, bcz, bcz, bcz, dy2, dy2, conv_w)


_GELU_C = 0.7978845608028654
_GELU_A = 0.044715


def _gelu(v):
    return 0.5 * v * (1.0 + jnp.tanh(_GELU_C * (v + _GELU_A * v * v * v)))


def _gelu_grad(v):
    t = jnp.tanh(_GELU_C * (v + _GELU_A * v * v * v))
    return 0.5 * (1.0 + t) + 0.5 * v * (1.0 - t * t) * (_GELU_C * (1.0 + 3.0 * _GELU_A * v * v))


def _ln_stats(v):
    mu = jnp.mean(v, axis=-1, keepdims=True)
    vc = v - mu
    return vc * lax.rsqrt(jnp.mean(vc * vc, axis=-1, keepdims=True) + NORM_EPS)


def _tril(n):
    return lax.broadcasted_iota(jnp.int32, (n, n), 0) >= lax.broadcasted_iota(jnp.int32, (n, n), 1)


def _b_mid_fwd(uv, v_g, v_b, w_s, bias_b, name):
    T, D2 = uv.shape
    D = D2 // 2
    G, C, _ = w_s.shape
    gd = D // G
    tt = _fit(T, ROW_TILE, C)

    def body(u_ref, v_ref, g_ref, b_ref, ws_ref, bias_ref, o_ref, vln_ref):
        vln_ref[...] = (_ln_stats(_gelu(v_ref[...].astype(F32))) * g_ref[...] + b_ref[...]).astype(BF16)
        mask = _tril(C)
        for g in range(G):
            wsm = jnp.where(mask, ws_ref[g], 0.0).astype(BF16)
            cs = slice(g * gd, (g + 1) * gd)
            for n in range(tt // C):
                rs = slice(n * C, (n + 1) * C)
                sv = jnp.dot(wsm, vln_ref[rs, cs], preferred_element_type=F32) + bias_ref[g]
                o_ref[rs, cs] = (_gelu(u_ref[rs, cs].astype(F32)) * sv).astype(BF16)

    return _pcall(body, name=name, out_shape=_sds((T, D), BF16), grid=(T // tt,),
                  in_specs=[_row(tt, D, 0), _row(tt, D, 1), _const((1, D)), _const((1, D)), _const((G, C, C)),
                            _const((G, C, gd))],
                  out_specs=_row(tt, D), scratch_shapes=[pltpu.VMEM((tt, D), BF16)],
                  compiler_params=_params("parallel"))(uv, uv, v_g, v_b, w_s, bias_b)


def _b_mid_bwd(uv, dgated, v_g, v_b, w_s, bias_b, name):
    T, D2 = uv.shape
    D = D2 // 2
    G, C, _ = w_s.shape
    gd = D // G
    tt = _fit(T, ROW_TILE, C)

    def body(u_ref, v_ref, d_ref, g_ref, b_ref, ws_ref, bias_ref, o_ref, dws_ref, dsb_ref, dvg_ref, dvb_ref,
             vln_ref, dvln_ref):
        @pl.when(pl.program_id(0) == 0)
        def _():
            dws_ref[...] = jnp.zeros_like(dws_ref)
            dsb_ref[...] = jnp.zeros_like(dsb_ref)
            dvg_ref[...] = jnp.zeros_like(dvg_ref)
            dvb_ref[...] = jnp.zeros_like(dvb_ref)

        vpre = v_ref[...].astype(F32)
        vhat = _ln_stats(_gelu(vpre))
        vln_ref[...] = (vhat * g_ref[...] + b_ref[...]).astype(BF16)
        mask = _tril(C)
        lane = lax.broadcasted_iota(jnp.int32, (C, 128), 1)
        for g in range(G):
            wsm = jnp.where(mask, ws_ref[g], 0.0).astype(BF16)
            cs = slice(g * gd, (g + 1) * gd)
            for n in range(tt // C):
                rs = slice(n * C, (n + 1) * C)
                vt = vln_ref[rs, cs]
                sv = jnp.dot(wsm, vt, preferred_element_type=F32) + bias_ref[g]
                dg = d_ref[rs, cs].astype(F32)
                upre = u_ref[rs, cs].astype(F32)
                o_ref[rs, cs] = (dg * sv * _gelu_grad(upre)).astype(BF16)
                dsv = dg * _gelu(upre)
                dsb_ref[...] += jnp.where(lane == g, jnp.sum(dsv, axis=-1, keepdims=True), 0.0)
                dsv16 = dsv.astype(BF16)
                dws_ref[g] += jnp.where(mask, lax.dot_general(dsv16, vt, _DIMS["nt"], preferred_element_type=F32), 0.0)
                dvln_ref[rs, cs] = lax.dot_general(wsm, dsv16, _DIMS["tn"], preferred_element_type=F32)
        dvln = dvln_ref[...]
        dvg_ref[...] += jnp.sum(dvln * vhat, axis=0, keepdims=True)
        dvb_ref[...] += jnp.sum(dvln, axis=0, keepdims=True)
        dvh = dvln * g_ref[...]
        vc = _gelu(vpre)
        vc = vc - jnp.mean(vc, axis=-1, keepdims=True)
        rstd = lax.rsqrt(jnp.mean(vc * vc, axis=-1, keepdims=True) + NORM_EPS)
        dv = rstd * (dvh - jnp.mean(dvh, axis=-1, keepdims=True) - vhat * jnp.mean(dvh * vhat, axis=-1, keepdims=True))
        o_ref[:, D:] = (dv * _gelu_grad(vpre)).astype(BF16)

    return _pcall(body, name=name,
                  out_shape=(_sds((T, 2 * D), BF16), _sds((G, C, C), F32), _sds((C, 128), F32), _sds((1, D), F32),
                             _sds((1, D), F32)),
                  grid=(T // tt,),
                  in_specs=[_row(tt, D, 0), _row(tt, D, 1), _row(tt, D), _const((1, D)), _const((1, D)),
                            _const((G, C, C)), _const((G, C, gd))],
                  out_specs=(_row(tt, 2 * D), _const((G, C, C)), _const((C, 128)), _const((1, D)), _const((1, D))),
                  scratch_shapes=[pltpu.VMEM((tt, D), BF16), pltpu.VMEM((tt, D), F32)],
                  compiler_params=_params("arbitrary"))(uv, uv, dgated, v_g, v_b, w_s, bias_b)


def _c_mid_fwd(ag, conv_w, conv_b, ln_g, ln_b, name):
    T, D2 = ag.shape
    D = D2 // 2
    width = conv_w.shape[0]
    tt = _fit(T, ROW_TILE, HALO)
    chunks, cw = _lanes(D)

    def body(a_ref, g_ref, ah_ref, gh_ref, w_ref, cb_ref, lg_ref, lb_ref, y2_ref, o_ref, win_ref):
        i = pl.program_id(0)
        for sl in chunks:
            yh = ah_ref[:, sl].astype(F32) * jax.nn.sigmoid(gh_ref[:, sl].astype(F32))
            win_ref[0:HALO, :] = jnp.where(i > 0, yh, 0.0)
            win_ref[HALO:, :] = a_ref[:, sl].astype(F32) * jax.nn.sigmoid(g_ref[:, sl].astype(F32))
            acc = jnp.zeros((tt, cw), F32)
            for k, xs in _causal_taps(win_ref[...], width, tt):
                acc = acc + w_ref[k:k + 1, sl] * xs
            y2_ref[:, sl] = acc + cb_ref[:, sl]
        y3 = _ln_stats(y2_ref[...]) * lg_ref[...] + lb_ref[...]
        o_ref[...] = (y3 * jax.nn.sigmoid(y3)).astype(BF16)

    return _pcall(body, name=name, out_shape=(_sds((T, D), F32), _sds((T, D), BF16)), grid=(T // tt,),
                  in_specs=[_row(tt, D, 0), _row(tt, D, 1), _prev_halo(tt, D, 0), _prev_halo(tt, D, 1),
                            _const((width, D)), _const((1, D)), _const((1, D)), _const((1, D))],
                  out_specs=(_row(tt, D), _row(tt, D)), scratch_shapes=[pltpu.VMEM((HALO + tt, cw), F32)],
                  compiler_params=_params("parallel"))(ag, ag, ag, ag, conv_w, conv_b, ln_g, ln_b)


def _c_mid_bwd(ag, y2, dy4, conv_w, ln_g, ln_b, name):
    T, D2 = ag.shape
    D = D2 // 2
    width = conv_w.shape[0]
    tt = _fit(T, ROW_TILE, HALO)
    chunks, cw = _lanes(D)
    n_tiles = T // tt

    def ln_silu_bwd(y2v, dy4v, lg, lb):
        mu = jnp.mean(y2v, axis=-1, keepdims=True)
        yc = y2v - mu
        rstd = lax.rsqrt(jnp.mean(yc * yc, axis=-1, keepdims=True) + NORM_EPS)
        yh = yc * rstd
        y3 = yh * lg + lb
        sg = jax.nn.sigmoid(y3)
        dy3 = dy4v * (sg * (1.0 + y3 * (1.0 - sg)))
        dyh = dy3 * lg
        dy2 = rstd * (dyh - jnp.mean(dyh, axis=-1, keepdims=True) - yh * jnp.mean(dyh * yh, axis=-1, keepdims=True))
        return dy2, dy3, yh

    def body(a_ref, g_ref, ah_ref, gh_ref, y2_ref, y2n_ref, d_ref, dn_ref, w_ref, lg_ref, lb_ref,
             o_ref, dw_ref, dcb_ref, dlg_ref, dlb_ref, win_ref, dwin_ref):
        i = pl.program_id(0)

        @pl.when(i == 0)
        def _():
            dw_ref[...] = jnp.zeros_like(dw_ref)
            dcb_ref[...] = jnp.zeros_like(dcb_ref)
            dlg_ref[...] = jnp.zeros_like(dlg_ref)
            dlb_ref[...] = jnp.zeros_like(dlb_ref)

        lg = lg_ref[...]
        lb = lb_ref[...]
        dy2, dy3, yh = ln_silu_bwd(y2_ref[...], d_ref[...].astype(F32), lg, lb)
        dlg_ref[...] += jnp.sum(dy3 * yh, axis=0, keepdims=True)
        dlb_ref[...] += jnp.sum(dy3, axis=0, keepdims=True)
        dcb_ref[...] += jnp.sum(dy2, axis=0, keepdims=True)
        dwin_ref[0:tt, :] = dy2
        dy2n, _, _ = ln_silu_bwd(y2n_ref[...], dn_ref[...].astype(F32), lg, lb)
        dwin_ref[tt:, :] = jnp.where(i < n_tiles - 1, dy2n, 0.0)
        for ci, sl in enumerate(chunks):
            a = a_ref[:, sl].astype(F32)
            sg = jax.nn.sigmoid(g_ref[:, sl].astype(F32))
            yh1 = ah_ref[:, sl].astype(F32) * jax.nn.sigmoid(gh_ref[:, sl].astype(F32))
            win_ref[0:HALO, :] = jnp.where(i > 0, yh1, 0.0)
            win_ref[HALO:, :] = a * sg
            d2 = dwin_ref[0:tt, sl]
            for k, xs in _causal_taps(win_ref[...], width, tt):
                dw_ref[k:k + 1, sl] += jnp.sum(d2 * xs, axis=0, keepdims=True)
            d1 = jnp.zeros((tt, cw), F32)
            for k, xs in _anticausal_taps(dwin_ref[:, sl], width, tt):
                d1 = d1 + w_ref[k:k + 1, sl] * xs
            o_ref[:, ci * cw:(ci + 1) * cw] = (d1 * sg).astype(BF16)
            o_ref[:, D + ci * cw:D + (ci + 1) * cw] = (d1 * a * sg * (1.0 - sg)).astype(BF16)

    return _pcall(body, name=name,
                  out_shape=(_sds((T, 2 * D), BF16), _sds((width, D), F32), _sds((1, D), F32), _sds((1, D), F32),
                             _sds((1, D), F32)),
                  grid=(n_tiles,),
                  in_specs=[_row(tt, D, 0), _row(tt, D, 1), _prev_halo(tt, D, 0), _prev_halo(tt, D, 1),
                            _row(tt, D), _next_halo(tt, D, T), _row(tt, D), _next_halo(tt, D, T),
                            _const((width, D)), _const((1, D)), _const((1, D))],
                  out_specs=(_row(tt, 2 * D), _const((width, D)), _const((1, D)), _const((1, D)), _const((1, D))),
                  scratch_shapes=[pltpu.VMEM((HALO + tt, cw), F32), pltpu.VMEM((tt + HALO, D), F32)],
                  compiler_params=_params("arbitrary"))(ag, ag, ag, ag, y2, y2, dy4, dy4, conv_w, ln_g, ln_b)


def _place():
    x, y, c = lax.axis_index("x"), lax.axis_index("y"), lax.axis_index("c")
    return x, y, c


def _slot(px, py, pc):
    return 4 * px + 2 * py + pc


def _all_gather(shards, name):
    n = len(shards)

    def body(*refs):
        ins, outs = refs[:n], refs[n:2 * n]
        send_sems, recv_sems, local_sems = refs[2 * n:]
        x, y, c = _place()
        me, sibling = (x, y, c), (x, y, 1 - c)
        chips = [(1 - x, y), (x, 1 - y), (1 - x, 1 - y)]

        def copy(t, k, block, to, src=None):
            dst = outs[t].at[_slot(*block)]
            return pltpu.make_async_remote_copy(
                src_ref=dst if src is None else src, dst_ref=dst, send_sem=send_sems.at[t, k],
                recv_sem=recv_sems.at[t, k], device_id=to, device_id_type=MESH)

        mine = [pltpu.make_async_copy(ins[t], outs[t].at[_slot(*me)], local_sems.at[t]) for t in range(n)]
        for cp in mine:
            cp.start()
        first = []
        for j, chip in enumerate(chips):
            first += [copy(t, 1 + j, me, (*chip, c), src=ins[t]) for t in range(n)]
        first += [copy(t, 0, me, sibling, src=ins[t]) for t in range(n)]
        for cp in first:
            cp.start()
        passed = []
        for j, chip in enumerate(chips):
            for t in range(n):
                copy(t, 1 + j, (*chip, c), me).wait_recv()
                cp = copy(t, 4 + j, (*chip, c), sibling)
                cp.start()
                passed.append(cp)
        for t in range(n):
            copy(t, 0, sibling, me).wait_recv()
            for j, chip in enumerate(chips):
                copy(t, 4 + j, (*chip, 1 - c), me).wait_recv()
        for cp in first + passed:
            cp.wait_send()
        for cp in mine:
            cp.wait()

    outs = _pcall(
        body, name=name, out_shape=tuple(_sds((N_DEV,) + s.shape, s.dtype) for s in shards),
        in_specs=[ANY] * n, out_specs=(ANY,) * n,
        scratch_shapes=[pltpu.SemaphoreType.DMA((n, 7)), pltpu.SemaphoreType.DMA((n, 7)), pltpu.SemaphoreType.DMA((n,))],
    )(*shards)
    return list(outs)


_HBM = pl.BlockSpec(memory_space=pltpu.HBM)
_SEM = pl.BlockSpec(memory_space=pltpu.SEMAPHORE)
_DATAFLOW = pltpu.SideEffectType.DATAFLOW_SIDE_EFFECTING


def _peers(x, y, c):
    out = []
    for j in range(1, N_DEV):
        fx, fy, fc = (j >> 2) & 1, (j >> 1) & 1, j & 1
        out.append((1 - x if fx else x, 1 - y if fy else y, 1 - c if fc else c))
    return out


def _exchange_copies(ins, lands, scatter, sems):
    send_sem, recv_sem, local_sem = sems
    x, y, c = _place()
    me = _slot(x, y, c)
    local = [pltpu.make_async_copy(ins[t].at[me] if scatter else ins[t], lands[t].at[me], local_sem)
             for t in range(len(ins))]
    remote = []
    for peer in _peers(x, y, c):
        for t in range(len(ins)):
            src = ins[t].at[_slot(*peer)] if scatter else ins[t]
            remote.append((pltpu.make_async_remote_copy(
                src_ref=src, dst_ref=lands[t].at[me], send_sem=send_sem, recv_sem=recv_sem,
                device_id=peer, device_id_type=MESH), lands[t].at[_slot(*peer)]))
    return local, remote


def _exchange_start(srcs, scatter, after, name):
    n = len(srcs)
    lands = [lax.empty(s.shape if scatter else (N_DEV,) + s.shape, s.dtype) for s in srcs]

    def body(*refs):
        ins, zones = refs[:n], refs[n:2 * n]
        sems = refs[2 * n + 1:2 * n + 4]
        token = refs[-1]
        local, remote = _exchange_copies(ins, zones, scatter, sems)
        for cp in local:
            cp.start()
        for cp, _ in remote:
            cp.start()
        token[...] = jnp.zeros_like(token)

    hbm = lambda a: pltpu.with_memory_space_constraint(a, pltpu.HBM)
    outs = _pcall(
        body, name=name,
        out_shape=(pltpu.SemaphoreType.DMA(()),) * 3
        + tuple(pltpu.HBM(a.shape, a.dtype) for a in list(srcs) + lands) + (_sds((8, 128), F32),),
        in_specs=[_HBM] * (2 * n) + [ANY],
        out_specs=(_SEM,) * 3 + (_HBM,) * (2 * n) + (pl.BlockSpec(memory_space=pltpu.VMEM),),
        input_output_aliases={t: 3 + t for t in range(2 * n)},
        compiler_params=pltpu.CompilerParams(has_side_effects=_DATAFLOW),
    )(*[hbm(a) for a in list(srcs) + lands], after)
    return outs[:3], list(outs[3:3 + n]), list(outs[3 + n:3 + 2 * n]), outs[-1]


def _exchange_wait(sems, srcs, lands, scatter, after, name):
    n = len(srcs)

    def body(*refs):
        ins, zones = refs[:n], refs[n:2 * n]
        local, remote = _exchange_copies(ins, zones, scatter, refs[2 * n:2 * n + 3])
        for cp in local:
            cp.wait()
        for cp, landed in remote:
            cp.wait_send()
            pltpu.make_async_remote_copy(
                src_ref=landed, dst_ref=landed, send_sem=refs[2 * n], recv_sem=refs[2 * n + 1],
                device_id=_place(), device_id_type=MESH).wait_recv()

    outs = _pcall(
        body, name=name, out_shape=tuple(pltpu.HBM(a.shape, a.dtype) for a in list(srcs) + list(lands)),
        in_specs=[_HBM] * (2 * n) + [_SEM] * 3 + [ANY], out_specs=(_HBM,) * (2 * n),
        input_output_aliases={t: t for t in range(2 * n)},
        compiler_params=pltpu.CompilerParams(has_side_effects=_DATAFLOW),
    )(*srcs, *lands, *sems, after)
    return list(outs[n:])


def _reduce_adam(recvs, w, m, v, name):
    L, r, c = w.shape
    tr = _fit(r, max(16, (128 * 1024) // c), 16)
    ni = r // tr

    def recv_spec(l0):
        def index(l, i):
            return 0, jnp.where(l == l0, i, jnp.where(l < l0, 0, ni - 1)), 0
        return pl.BlockSpec((N_DEV, tr, c), index)

    lay = pl.BlockSpec((None, tr, c), lambda l, i: (l, i, 0))

    def body(*refs):
        recv_refs = refs[:L]
        w_ref, m_ref, v_ref, g_out, d_out, m_out, v_out = refs[L:]
        l = pl.program_id(0)
        for l0 in range(L):
            @pl.when(l == l0)
            def _(l0=l0):
                g = recv_refs[l0][0].astype(F32)
                for s in range(1, N_DEV):
                    g = g + recv_refs[l0][s].astype(F32)
                mn = ADAM_B1 * m_ref[...] + (1.0 - ADAM_B1) * g
                vn = ADAM_B2 * v_ref[...] + (1.0 - ADAM_B2) * (g * g)
                m_hat = mn / (1.0 - ADAM_B1 ** ADAM_STEP)
                v_hat = vn / (1.0 - ADAM_B2 ** ADAM_STEP)
                g_out[...] = g
                d_out[...] = -ADAM_LR * (m_hat / (jnp.sqrt(v_hat) + ADAM_EPS) + ADAM_WD * w_ref[...])
                m_out[...] = mn
                v_out[...] = vn

    return _pcall(body, name=name, out_shape=(_sds((L, r, c), F32),) * 4, grid=(L, ni),
                  in_specs=[recv_spec(l0) for l0 in range(L)] + [lay, lay, lay], out_specs=(lay,) * 4,
                  compiler_params=_params("arbitrary", "arbitrary"))(*recvs, w, m, v)


_SMALL_SHARDED = ["mix_norm", "xa_norm", "ffn_norm", "a_conv_w", "c_conv_w", "c_conv_b", "c_ln_g", "c_ln_b"]
_SMALL_REPLICATED = ["b_v_g", "b_v_b", "b_w_s", "b_s_bias"]
_BIG = ["xa_wq", "xa_wkv", "xa_wo", "ffn_w_gu", "ffn_w_down", "a_w_in", "a_w_out", "b_w_in", "b_w_out", "c_w_in",
        "c_w_out"]
_COL_SHARDED = {"xa_wkv", "ffn_w_gu", "a_w_in", "b_w_in", "c_w_in"}
_WEIGHTS = ["mix_norm", "xa_norm", "xa_wq", "xa_wkv", "xa_wo", "ffn_norm", "ffn_w_gu", "ffn_w_down", "a_w_in",
            "a_conv_w", "a_w_out", "b_w_in", "b_v_g", "b_v_b", "b_w_s", "b_s_bias", "b_w_out", "c_w_in", "c_conv_w",
            "c_conv_b", "c_ln_g", "c_ln_b", "c_w_out"]
_MIXER = "abc"


def _size(shape):
    size = 1
    for s in shape:
        size *= s
    return size


def _row_layout(shapes, width):
    offs, r = [], 0
    for shape in shapes:
        offs.append(r)
        r += -(-(-(-_size(shape) // width)) // 8) * 8
    return offs, r


def _pack_rows(arrays, width, fill):
    offs, total = _row_layout([a.shape for a in arrays], width)
    ends = offs[1:] + [total]
    rows = [jnp.pad(a.reshape(-1), (0, (e - o) * width - a.size), constant_values=fill).reshape(e - o, width)
            for a, o, e in zip(arrays, offs, ends)]
    return jnp.concatenate(rows, axis=0)


def _unpack_rows(packed, like):
    width = packed.shape[-1]
    offs, _ = _row_layout(like, width)
    return [packed[o:o + -(-_size(s) // width)].reshape(-1)[:_size(s)].reshape(s) for o, s in zip(offs, like)]


def _assemble_rows(pieces, rows, width, name):
    n = len(pieces)

    def body(*refs):
        o_ref = refs[n]
        o_ref[...] = jnp.zeros_like(o_ref)
        for r, (a, off) in zip(refs[:n], pieces):
            o_ref[off:off + a.shape[0], :] = r[...]

    return _pcall(body, name=name, out_shape=_sds((rows, width), F32),
                  compiler_params=pltpu.CompilerParams(vmem_limit_bytes=V7X_VMEM_LIMIT))(*[a for a, _ in pieces])


def kernel(x, mem, mix_norm, xa_norm, xa_wq, xa_wkv, xa_wo, ffn_norm, ffn_w_gu, ffn_w_down, a_w_in, a_conv_w, a_w_out, b_w_in, b_v_g, b_v_b, b_w_s, b_s_bias, b_w_out, c_w_in, c_conv_w, c_conv_b, c_ln_g, c_ln_b, c_w_out, loss_target, m_mix_norm, m_xa_norm, m_xa_wq, m_xa_wkv, m_xa_wo, m_ffn_norm, m_ffn_w_gu, m_ffn_w_down, m_a_w_in, m_a_conv_w, m_a_w_out, m_b_w_in, m_b_v_g, m_b_v_b, m_b_w_s, m_b_s_bias, m_b_w_out, m_c_w_in, m_c_conv_w, m_c_conv_b, m_c_ln_g, m_c_ln_b, m_c_w_out, v_mix_norm, v_xa_norm, v_xa_wq, v_xa_wkv, v_xa_wo, v_ffn_norm, v_ffn_w_gu, v_ffn_w_down, v_a_w_in, v_a_conv_w, v_a_w_out, v_b_w_in, v_b_v_g, v_b_v_b, v_b_w_s, v_b_s_bias, v_b_w_out, v_c_w_in, v_c_conv_w, v_c_conv_b, v_c_ln_g, v_c_ln_b, v_c_w_out):
    P = dict(locals())
    T, D = x.shape[1], x.shape[2]
    dl = D // N_DEV
    depth = mix_norm.shape[0]
    x0, mem0, target = x[0], mem[0], loss_target[0]
    my_slot = _slot(*_place())

    sh_shapes = [P[n].shape for n in _SMALL_SHARDED]
    packed = _pack_rows([P[n] for n in _SMALL_SHARDED], dl, 0.0)
    n_sh = packed.shape[0]
    gathered = _all_gather([packed], "ag_small")[0]
    full_rows = jnp.transpose(gathered, (1, 0, 2)).reshape(n_sh, D)
    small = dict(zip(_SMALL_SHARDED, _unpack_rows(full_rows, [s[:-1] + (D,) for s in sh_shapes])))
    G, C = b_w_s.shape[1], b_w_s.shape[2]
    gd = D // G
    bias_b = jnp.broadcast_to(b_s_bias[0][:, :, None], (G, C, gd))
    zero_row = jnp.zeros((1, D), F32)

    w16 = {n: P[n].astype(BF16) for n in _BIG}

    groups = [(i, part) for i in range(depth) for part in range(3)]

    def group_names(i, part):
        mx, slot = _MIXER[i % N_MIXERS], i // N_MIXERS
        if part == 0:
            return [(mx + "_w_in", slot), (mx + "_w_out", slot)]
        if part == 1:
            return [("xa_wq", i), ("xa_wkv", i), ("xa_wo", i)]
        return [("ffn_w_gu", i), ("ffn_w_down", i)]

    def start_gather(g, after):
        i, part = groups[g]
        names = group_names(i, part)
        sems, srcs, lands, token = _exchange_start([w16[n][j] for n, j in names], False, after,
                                                   "ag_start_%d_%d" % (i, part))
        return names, sems, srcs, lands, token

    def finish_gather(g, pending, after):
        names, sems, srcs, lands, _ = pending
        fulls = _exchange_wait(sems, srcs, lands, False, after, "ag_wait_%d_%d" % groups[g])
        out = {}
        for (n, _), f in zip(names, fulls):
            key = n[2:] if n[1] == "_" and n[0] in _MIXER else n
            out[key] = f if n in _COL_SHARDED else f.reshape(-1, f.shape[-1])
        return out, fulls[0]

    fwd = {"g": 0, "cur": finish_gather(0, start_gather(0, x0), x0), "pend": None}

    def begin_group():
        Wg, landed = fwd["cur"]
        if fwd["g"] + 1 < len(groups):
            fwd["pend"] = start_gather(fwd["g"] + 1, landed)
            return Wg, fwd["pend"][4]
        fwd["pend"] = None
        return Wg, zero_row

    def end_group(y):
        if fwd["pend"] is not None:
            fwd["cur"] = finish_gather(fwd["g"] + 1, fwd["pend"], y)
        fwd["g"] += 1

    saved = []
    xin = x0
    h = _rms_fwd(x0, small["mix_norm"][0, 0][None], "rms_first")
    for i in range(depth):
        kind, slot = i % N_MIXERS, i // N_MIXERS
        W, dep = begin_group()
        S = {"W": W, "x0": xin, "h0": h}
        pre = _mm(h, W["w_in"], "nn", BF16, "mm_in_%s" % _MIXER[kind], b_blocked=True, dep=dep)
        S["pre"] = pre
        if kind == 0:
            mid = _a_mid_fwd(pre, small["a_conv_w"][slot], "a_mid_fwd")
        elif kind == 1:
            mid = _b_mid_fwd(pre, b_v_g, b_v_b, b_w_s[0], bias_b, "b_mid_fwd")
        else:
            y2c, mid = _c_mid_fwd(pre, small["c_conv_w"][slot], small["c_conv_b"], small["c_ln_g"], small["c_ln_b"],
                                  "c_mid_fwd")
            S["y2c"] = y2c
        S["mid"] = mid
        S["y0"] = _mm(mid, W["w_out"], "nn", F32, "mm_out")
        end_group(S["y0"])
        Wx, dep = begin_group()
        W.update(Wx)
        xin, h = _post_pre_fwd(xin, S["y0"], small["mix_norm"][i, 1][None], small["xa_norm"][i, 0][None], "post_pre")
        S["x1"], S["h1"] = xin, h
        S["q"] = _mm(h, W["xa_wq"], "nn", BF16, "mm_q", dep=dep)
        S["memn"] = _rms_fwd(mem0, small["xa_norm"][i, 2][None], "rms_mem")
        S["kv"] = _mm(S["memn"], W["xa_wkv"], "nn", BF16, "mm_kv", b_blocked=True)
        S["o"] = _attn_fwd(S["q"], S["kv"], "attn_fwd")
        S["y1"] = _mm(S["o"], W["xa_wo"], "nn", F32, "mm_out")
        end_group(S["y1"])
        Wf, dep = begin_group()
        W.update(Wf)
        xin, h = _post_pre_fwd(xin, S["y1"], small["xa_norm"][i, 1][None], small["ffn_norm"][i, 0][None], "post_pre")
        S["x2"], S["h2"] = xin, h
        S["gate"], S["up"], S["act"] = _ffn_gu_fwd(h, W["ffn_w_gu"], "ffn_gu_fwd", dep=dep)
        S["y2"] = _mm(S["act"], W["ffn_w_down"], "nn", F32, "mm_down")
        end_group(S["y2"])
        if i + 1 < depth:
            xin, h = _post_pre_fwd(xin, S["y2"], small["ffn_norm"][i, 1][None], small["mix_norm"][i + 1, 0][None],
                                   "post_pre")
        saved.append(S)

    last = saved[-1]
    loss_part, dx, dy, dg = _final_fwd_loss(xin, last["y2"], small["ffn_norm"][depth - 1, 1][None], target, "final_loss")
    loss = lax.psum(loss_part[0, 0], ("x", "y", "c"))

    g_mix = [[zero_row, zero_row] for _ in range(depth)]
    g_xa = [[zero_row, zero_row, zero_row] for _ in range(depth)]
    g_ffn = [[zero_row, zero_row] for _ in range(depth)]
    g_small = {}
    recv = {n: [None] * P[n].shape[0] for n in _BIG}
    g_ffn[depth - 1][1] = dg

    bwd = {"pend": None}

    def finish_scatter(after):
        if bwd["pend"] is not None:
            names, tag, sems, srcs, lands, _ = bwd["pend"]
            for (n, j, _), r in zip(names, _exchange_wait(sems, srcs, lands, True, after, "rs_wait_" + tag)):
                recv[n][j] = r
            bwd["pend"] = None

    def scatter_group(names, tag, after):
        finish_scatter(after)
        parts = [g if n in _COL_SHARDED else g.reshape(N_DEV, -1, g.shape[-1]) for n, _, g in names]
        bwd["pend"] = (names, tag) + _exchange_start(parts, True, after, "rs_start_" + tag)

    def scatter_token():
        return zero_row if bwd["pend"] is None else bwd["pend"][5]

    for i in reversed(range(depth)):
        kind, slot = i % N_MIXERS, i // N_MIXERS
        mx = _MIXER[kind]
        S = saved[i]
        W = S["W"]
        dgu = _ffn_dgu_bwd(dy, W["ffn_w_down"], S["gate"], S["up"], "ffn_dgu_bwd", dep=scatter_token())
        dw_down = _mm(S["act"], dy, "tn", BF16, "mm_dw_down")
        dh = _mm(dgu, W["ffn_w_gu"], "nt", F32, "mm_dh_gu", a_blocked=True, b_blocked=True)
        dw_gu = _mm(S["h2"], dgu, "tn", BF16, "mm_dw_gu", b_blocked=True, out_blocks=N_DEV)
        dx, g_ffn[i][0], dy, g_xa[i][1] = _pre_post_bwd(dx, dh, S["x2"], small["ffn_norm"][i, 0][None], S["y1"],
                                                         small["xa_norm"][i, 1][None], "pre_post_bwd")
        scatter_group([("ffn_w_gu", i, dw_gu), ("ffn_w_down", i, dw_down)], "%d_2" % i, dx)
        do = _mm(dy, W["xa_wo"], "nt", BF16, "mm_nt_dd16", dep=scatter_token())
        dw_o = _mm(S["o"], dy, "tn", BF16, "mm_dw_dd")
        dq, dkv = _attn_bwd(S["q"], S["kv"], do, "attn_bwd")
        dkv16 = dkv.astype(BF16)
        dh = _mm(dq, W["xa_wq"], "nt", F32, "mm_nt_dd32")
        dw_q = _mm(S["h1"], dq, "tn", BF16, "mm_dw_dd")
        dw_kv = _mm(S["memn"], dkv16, "tn", BF16, "mm_dw_kv", out_blocks=N_DEV)
        dmemn = _mm(dkv16, W["xa_wkv"], "nt", F32, "mm_dmem", b_blocked=True)
        g_xa[i][2] = _rms_gain_grad(dmemn, mem0, "rms_gain_grad")
        dx, g_xa[i][0], dy, g_mix[i][1] = _pre_post_bwd(dx, dh, S["x1"], small["xa_norm"][i, 0][None], S["y0"],
                                                         small["mix_norm"][i, 1][None], "pre_post_bwd")
        scatter_group([("xa_wq", i, dw_q), ("xa_wkv", i, dw_kv), ("xa_wo", i, dw_o)], "%d_1" % i, dx)
        dmid = _mm(dy, W["w_out"], "nt", BF16, "mm_nt_dd16", dep=scatter_token())
        dw_out = _mm(S["mid"], dy, "tn", BF16, "mm_dw_dd")
        if kind == 0:
            dpre, dcw = _a_mid_bwd(S["pre"], dmid, small["a_conv_w"][slot], "a_mid_bwd")
            g_small.setdefault("a_conv_w", {})[slot] = dcw
        elif kind == 1:
            dpre, dws, dsb, dvg, dvb = _b_mid_bwd(S["pre"], dmid, b_v_g, b_v_b, b_w_s[0], bias_b, "b_mid_bwd")
            dsb_row = jnp.pad(jnp.transpose(dsb[:, :G]).reshape(1, G * C), ((0, 0), (0, (-G * C) % D)))
            g_small.update(b_w_s=dws.reshape(-1, D), b_s_bias=dsb_row.reshape(-1, D), b_v_g=dvg, b_v_b=dvb)
        else:
            dpre, dcw, dcb, dlg, dlb = _c_mid_bwd(S["pre"], S["y2c"], dmid, small["c_conv_w"][slot], small["c_ln_g"],
                                                  small["c_ln_b"], "c_mid_bwd")
            g_small.update(c_conv_w=dcw, c_conv_b=dcb, c_ln_g=dlg, c_ln_b=dlb)
        dh = _mm(dpre, W["w_in"], "nt", F32, "mm_dh_in_%s" % mx, b_blocked=True)
        dw_in = _mm(S["h0"], dpre, "tn", BF16, "mm_dw_in_%s" % mx, out_blocks=N_DEV)
        if i > 0:
            dx, g_mix[i][0], dy, g_ffn[i - 1][1] = _pre_post_bwd(
                dx, dh, S["x0"], small["mix_norm"][i, 0][None], saved[i - 1]["y2"],
                small["ffn_norm"][i - 1, 1][None], "pre_post_bwd")
        else:
            dx, g_mix[i][0] = _pre_post_bwd(dx, dh, S["x0"], small["mix_norm"][i, 0][None], None, None, "pre_bwd")
        scatter_group([(mx + "_w_in", slot, dw_in), (mx + "_w_out", slot, dw_out)], "%d_0" % i, dx)
        S.clear()
    finish_scatter(dx)
    grad_x = dx[None]

    sh_off = dict(zip(_SMALL_SHARDED, _row_layout(sh_shapes, dl)[0]))
    rep_offs, n_rep = _row_layout([P[n].shape for n in _SMALL_REPLICATED], D)
    rep_off = {n: n_sh + o for n, o in zip(_SMALL_REPLICATED, rep_offs)}
    pieces = []
    for i in range(depth):
        pieces += [(g, sh_off["mix_norm"] + 2 * i + j) for j, g in enumerate(g_mix[i])]
        pieces += [(g, sh_off["xa_norm"] + 3 * i + j) for j, g in enumerate(g_xa[i])]
        pieces += [(g, sh_off["ffn_norm"] + 2 * i + j) for j, g in enumerate(g_ffn[i])]
    pieces += [(g, sh_off["a_conv_w"] + a_conv_w.shape[1] * s) for s, g in g_small["a_conv_w"].items()]
    pieces += [(g_small[n], sh_off[n]) for n in ("c_conv_w", "c_conv_b", "c_ln_g", "c_ln_b")]
    pieces += [(g_small[n], rep_off[n]) for n in _SMALL_REPLICATED]
    part_all = _assemble_rows(pieces, n_sh + n_rep, D, "pack_small_grads")
    parts_all = _all_gather([part_all], "ag_small_grads")[0]
    recv_sh = lax.dynamic_slice_in_dim(parts_all[:, :n_sh], my_slot * dl, dl, axis=2)
    recv_rep = parts_all[:, n_sh:]

    out = {}

    def adam_small(names, recv_s, width, name):
        shapes = [P[n].shape for n in names]
        pw = _pack_rows([P[n] for n in names], width, 0.0)
        pm = _pack_rows([P["m_" + n] for n in names], width, 0.0)
        pv = _pack_rows([P["v_" + n] for n in names], width, 1.0)
        res = _reduce_adam([recv_s], pw[None], pm[None], pv[None], name)
        for kind, r in zip(("grad", "delta", "new_m", "new_v"), res):
            for n, a in zip(names, _unpack_rows(r[0], shapes)):
                out[kind + "_" + n] = a

    adam_small(_SMALL_SHARDED, recv_sh, dl, "adam_small_sharded")
    adam_small(_SMALL_REPLICATED, recv_rep, D, "adam_small_replicated")
    for n in _BIG:
        res = _reduce_adam(recv[n], P[n], P["m_" + n], P["v_" + n], "adam_" + n)
        for kind, r in zip(("grad", "delta", "new_m", "new_v"), res):
            out[kind + "_" + n] = r

    return (loss, grad_x, *[out[k + "_" + n] for k in ("grad", "delta", "new_m", "new_v") for n in _WEIGHTS])
```

```python
import functools

import jax
import jax.numpy as jnp
from jax import lax
from jax.experimental import pallas as pl
from jax.experimental.pallas import tpu as pltpu

F32 = jnp.float32
BF16 = jnp.bfloat16
MESH = pl.DeviceIdType.MESH
ANY = pl.BlockSpec(memory_space=pl.ANY)

N_DEV = 8
N_MIXERS = 3
XA_HEADS = 4
GMLP_GROUPS = 8
CHUNK = 128
NORM_EPS = 1e-6
HALO = 32
ROW_TILE = 256
CONV_LANES = 512
V7X_VMEM_LIMIT = 56 * 1024 * 1024

ADAM_LR = 0.001
ADAM_B1 = 0.9
ADAM_B2 = 0.999
ADAM_EPS = 1e-08
ADAM_WD = 0.01
ADAM_STEP = 10


def _pcall(body, **kw):
    return pl.pallas_call(body, **kw)


def _params(*sem):
    return pltpu.CompilerParams(dimension_semantics=sem, vmem_limit_bytes=V7X_VMEM_LIMIT)


def _fit(n, pref, mult=128):
    if n <= pref:
        return n
    t = (pref // mult) * mult
    while t >= mult:
        if n % t == 0:
            return t
        t -= mult
    return n


def _sds(shape, dtype):
    return jax.ShapeDtypeStruct(shape, dtype)


_DIMS = {"nn": (((1,), (0,)), ((), ())), "nt": (((1,), (1,)), ((), ())), "tn": (((0,), (0,)), ((), ()))}
MM_VMEM_BUDGET = 44 * 1024 * 1024


def _gcd(a, b):
    while b:
        a, b = b, a % b
    return a


def _mm(a, b, mode, out_dtype, name, *, a_blocked=False, b_blocked=False, out_blocks=None, dep=None):
    if mode == "tn":
        K, M = a.shape
    elif a_blocked:
        sa, M, ca = a.shape
        K = sa * ca
    else:
        M, K = a.shape
    n_unit = k_unit = None
    if b_blocked:
        _, d1, cb = b.shape
        if mode == "nt":
            N, k_unit = d1, cb
        else:
            N, n_unit = b.shape[0] * cb, cb
    else:
        N = b.shape[0] if mode == "nt" else b.shape[1]
    n_unit = n_unit or N
    k_unit = k_unit or K
    if a_blocked:
        k_unit = _gcd(k_unit, ca)
    if out_blocks:
        n_unit = _gcd(n_unit, N // out_blocks)
    tn = _fit(n_unit, 1536)
    tk = k_unit
    out_bytes = jnp.dtype(out_dtype).itemsize

    def need(tm_):
        nk_ = K // tk
        return (4 * (tm_ * tk + tk * tn) + 2 * tm_ * tn * out_bytes + 4 * tm_ * tn * (2 if nk_ > 1 else 1))

    tm = _fit(M, 1024)
    while need(tm) > MM_VMEM_BUDGET and tm % 256 == 0:
        tm //= 2
    nk = K // tk

    if mode == "tn":
        a_spec = pl.BlockSpec((tk, tm), lambda i, j, k: (k, i))
    elif a_blocked:
        ka = ca // tk
        a_spec = pl.BlockSpec((None, tm, tk), lambda i, j, k: (k // ka, i, k % ka))
    else:
        a_spec = pl.BlockSpec((tm, tk), lambda i, j, k: (i, k))
    if b_blocked and mode == "nt":
        kb = cb // tk
        b_spec = pl.BlockSpec((None, tn, tk), lambda i, j, k: (k // kb, j, k % kb))
    elif b_blocked:
        nb = cb // tn
        b_spec = pl.BlockSpec((None, tk, tn), lambda i, j, k: (j // nb, k, j % nb))
    elif mode == "nt":
        b_spec = pl.BlockSpec((tn, tk), lambda i, j, k: (j, k))
    else:
        b_spec = pl.BlockSpec((tk, tn), lambda i, j, k: (k, j))
    if out_blocks:
        ob = (N // out_blocks) // tn
        out_shape = _sds((out_blocks, M, N // out_blocks), out_dtype)
        o_spec = pl.BlockSpec((None, tm, tn), lambda i, j, k: (j // ob, i, j % ob))
    else:
        out_shape = _sds((M, N), out_dtype)
        o_spec = pl.BlockSpec((tm, tn), lambda i, j, k: (i, j))
    dims = _DIMS[mode]
    n_in = 2 if dep is None else 3

    def body(*refs):
        a_ref, b_ref = refs[0], refs[1]
        o_ref = refs[n_in]
        p = lax.dot_general(a_ref[...], b_ref[...], dims, preferred_element_type=F32)
        if nk == 1:
            o_ref[...] = p.astype(o_ref.dtype)
            return
        acc_ref = refs[n_in + 1]
        k = pl.program_id(2)

        @pl.when(k == 0)
        def _():
            acc_ref[...] = p

        @pl.when(k > 0)
        def _():
            acc_ref[...] += p

        @pl.when(k == nk - 1)
        def _():
            o_ref[...] = acc_ref[...].astype(o_ref.dtype)

    ins, in_specs = [a, b], [a_spec, b_spec]
    if dep is not None:
        ins.append(dep)
        in_specs.append(ANY)
    return _pcall(
        body, name=name, out_shape=out_shape, grid=(M // tm, N // tn, nk),
        in_specs=in_specs, out_specs=o_spec,
        scratch_shapes=[pltpu.VMEM((tm, tn), F32)] if nk > 1 else [],
        compiler_params=_params("parallel", "parallel", "arbitrary"),
    )(*ins)


def _rstd(v):
    return lax.rsqrt(jnp.mean(v * v, axis=-1, keepdims=True) + NORM_EPS)


def _rms_bwd_rows(v, g, dout):
    r = _rstd(v)
    vh = v * r
    dvh = dout * g
    dv = r * (dvh - vh * jnp.mean(dvh * vh, axis=-1, keepdims=True))
    return dv, jnp.sum(dout * vh, axis=0, keepdims=True)


def _row(tt, d, col=0):
    return pl.BlockSpec((tt, d), lambda i: (i, col))


def _const(shape):
    return pl.BlockSpec(shape, lambda i: (0,) * len(shape))


def _prev_halo(tt, d, col=0):
    return pl.BlockSpec((HALO, d), lambda i: (jnp.maximum(i * (tt // HALO) - 1, 0), col))


def _next_halo(tt, d, rows, col=0):
    last = rows // HALO - 1
    return pl.BlockSpec((HALO, d), lambda i: (jnp.minimum((i + 1) * (tt // HALO), last), col))


def _rms_fwd(x, g, name):
    T, D = x.shape
    tt = _fit(T, ROW_TILE, 8)

    def body(x_ref, g_ref, h_ref):
        v = x_ref[...]
        h_ref[...] = (v * _rstd(v) * g_ref[...]).astype(BF16)

    return _pcall(body, name=name, out_shape=_sds((T, D), BF16), grid=(T // tt,),
                  in_specs=[_row(tt, D), _const((1, D))], out_specs=_row(tt, D),
                  compiler_params=_params("parallel"))(x, g)


def _post_pre_fwd(x, y, g_post, g_pre, name):
    T, D = x.shape
    tt = _fit(T, ROW_TILE, 8)

    def body(x_ref, y_ref, gp_ref, gn_ref, xo_ref, h_ref):
        y = y_ref[...]
        xn = x_ref[...] + y * _rstd(y) * gp_ref[...]
        xo_ref[...] = xn
        h_ref[...] = (xn * _rstd(xn) * gn_ref[...]).astype(BF16)

    return _pcall(body, name=name, out_shape=(_sds((T, D), F32), _sds((T, D), BF16)), grid=(T // tt,),
                  in_specs=[_row(tt, D), _row(tt, D), _const((1, D)), _const((1, D))],
                  out_specs=(_row(tt, D), _row(tt, D)),
                  compiler_params=_params("parallel"))(x, y, g_post, g_pre)


def _final_fwd_loss(x, y, g_post, target, name):
    T, D = x.shape
    tt = _fit(T, ROW_TILE, 8)

    def body(x_ref, y_ref, g_ref, t_ref, loss_ref, dx_ref, dy_ref, dg_ref):
        i = pl.program_id(0)
        y = y_ref[...]
        g = g_ref[...]
        err = x_ref[...] + y * _rstd(y) * g - t_ref[...]
        part = 0.5 * jnp.sum(jnp.mean(err * err, axis=-1, keepdims=True))
        dx = err / D
        dx_ref[...] = dx
        dy, dg = _rms_bwd_rows(y, g, dx)
        dy_ref[...] = dy.astype(BF16)

        @pl.when(i == 0)
        def _():
            loss_ref[...] = jnp.zeros_like(loss_ref)
            dg_ref[...] = jnp.zeros_like(dg_ref)

        loss_ref[...] += part
        dg_ref[...] += dg

    return _pcall(body, name=name,
                  out_shape=(_sds((1, 128), F32), _sds((T, D), F32), _sds((T, D), BF16), _sds((1, D), F32)),
                  grid=(T // tt,),
                  in_specs=[_row(tt, D), _row(tt, D), _const((1, D)), _row(tt, D)],
                  out_specs=(_const((1, 128)), _row(tt, D), _row(tt, D), _const((1, D))),
                  compiler_params=_params("arbitrary"))(x, y, g_post, target)


def _pre_post_bwd(dx_out, dh, x_in, g_pre, y_prev, g_post_prev, name):
    T, D = x_in.shape
    tt = _fit(T, ROW_TILE, 8)
    with_prev = y_prev is not None

    def body(*refs):
        if with_prev:
            dxo_ref, dh_ref, x_ref, g_ref, y_ref, gp_ref, dxi_ref, dg_ref, dy_ref, dgp_ref = refs
        else:
            dxo_ref, dh_ref, x_ref, g_ref, dxi_ref, dg_ref = refs
        i = pl.program_id(0)
        dv, dg = _rms_bwd_rows(x_ref[...], g_ref[...], dh_ref[...].astype(F32))
        dxi = dxo_ref[...] + dv
        dxi_ref[...] = dxi

        @pl.when(i == 0)
        def _():
            dg_ref[...] = jnp.zeros_like(dg_ref)

        dg_ref[...] += dg
        if with_prev:
            dy, dgp = _rms_bwd_rows(y_ref[...], gp_ref[...], dxi)
            dy_ref[...] = dy.astype(BF16)

            @pl.when(i == 0)
            def _():
                dgp_ref[...] = jnp.zeros_like(dgp_ref)

            dgp_ref[...] += dgp

    ins = [dx_out, dh, x_in, g_pre]
    in_specs = [_row(tt, D), _row(tt, D), _row(tt, D), _const((1, D))]
    out_shape = [_sds((T, D), F32), _sds((1, D), F32)]
    out_specs = [_row(tt, D), _const((1, D))]
    if with_prev:
        ins += [y_prev, g_post_prev]
        in_specs += [_row(tt, D), _const((1, D))]
        out_shape += [_sds((T, D), BF16), _sds((1, D), F32)]
        out_specs += [_row(tt, D), _const((1, D))]
    return _pcall(body, name=name, out_shape=tuple(out_shape), grid=(T // tt,),
                  in_specs=in_specs, out_specs=tuple(out_specs),
                  compiler_params=_params("arbitrary"))(*ins)


def _rms_gain_grad(dout, v, name):
    T, D = v.shape
    tt = _fit(T, ROW_TILE, 8)

    def body(d_ref, v_ref, dg_ref):
        @pl.when(pl.program_id(0) == 0)
        def _():
            dg_ref[...] = jnp.zeros_like(dg_ref)

        v = v_ref[...]
        dg_ref[...] += jnp.sum(d_ref[...] * (v * _rstd(v)), axis=0, keepdims=True)

    return _pcall(body, name=name, out_shape=_sds((1, D), F32), grid=(T // tt,),
                  in_specs=[_row(tt, D), _row(tt, D)], out_specs=_const((1, D)),
                  compiler_params=_params("arbitrary"))(dout, v)


def _softmax_rows(s):
    e = jnp.exp(s - jnp.max(s, axis=-1, keepdims=True))
    return e / jnp.sum(e, axis=-1, keepdims=True)


def _attn_fwd(q, kv, name):
    T, D = q.shape
    nm = kv.shape[0]
    hd = D // XA_HEADS
    scale = hd ** -0.5
    tq = _fit(T, ROW_TILE, 8)

    def body(q_ref, k_ref, v_ref, o_ref):
        for h in range(XA_HEADS):
            sl = slice(h * hd, (h + 1) * hd)
            s = lax.dot_general(q_ref[:, sl], k_ref[:, sl], _DIMS["nt"], preferred_element_type=F32) * scale
            p = _softmax_rows(s)
            o_ref[:, sl] = jnp.dot(p.astype(BF16), v_ref[:, sl], preferred_element_type=F32).astype(BF16)

    return _pcall(body, name=name, out_shape=_sds((T, D), BF16), grid=(T // tq,),
                  in_specs=[_row(tq, D), pl.BlockSpec((nm, D), lambda i: (0, 0)), pl.BlockSpec((nm, D), lambda i: (0, 1))],
                  out_specs=_row(tq, D), compiler_params=_params("parallel"))(q, kv, kv)


def _attn_bwd(q, kv, do, name):
    T, D = q.shape
    nm = kv.shape[0]
    hd = D // XA_HEADS
    scale = hd ** -0.5
    tq = _fit(T, ROW_TILE, 8)

    def body(q_ref, k_ref, v_ref, do_ref, dq_ref, dkv_ref):
        @pl.when(pl.program_id(0) == 0)
        def _():
            dkv_ref[...] = jnp.zeros_like(dkv_ref)

        for h in range(XA_HEADS):
            sl = slice(h * hd, (h + 1) * hd)
            qh, kh, vh, doh = q_ref[:, sl], k_ref[:, sl], v_ref[:, sl], do_ref[:, sl]
            s = lax.dot_general(qh, kh, _DIMS["nt"], preferred_element_type=F32) * scale
            p = _softmax_rows(s)
            dp = lax.dot_general(doh, vh, _DIMS["nt"], preferred_element_type=F32)
            ds = (p * (dp - jnp.sum(dp * p, axis=-1, keepdims=True)) * scale).astype(BF16)
            dq_ref[:, sl] = jnp.dot(ds, kh, preferred_element_type=F32).astype(BF16)
            dkv_ref[:, sl] += lax.dot_general(ds, qh, _DIMS["tn"], preferred_element_type=F32)
            dkv_ref[:, D + h * hd:D + (h + 1) * hd] += lax.dot_general(
                p.astype(BF16), doh, _DIMS["tn"], preferred_element_type=F32)

    return _pcall(body, name=name, out_shape=(_sds((T, D), BF16), _sds((nm, 2 * D), F32)), grid=(T // tq,),
                  in_specs=[_row(tq, D), pl.BlockSpec((nm, D), lambda i: (0, 0)), pl.BlockSpec((nm, D), lambda i: (0, 1)),
                            _row(tq, D)],
                  out_specs=(_row(tq, D), _const((nm, 2 * D))),
                  compiler_params=_params("arbitrary"))(q, kv, kv, do)


def _ffn_gu_fwd(h, w_gu, name, dep=None, tm=512):
    T, D = h.shape
    S, _, c = w_gu.shape
    F = S * c // 2
    tm = _fit(T, tm)
    tn = _fit(c, 1536)
    nb = c // tn
    nj = F // tn
    n_in = 3 if dep is None else 4

    def w_spec(off):
        return pl.BlockSpec((None, D, tn), lambda i, j: ((j + off) // nb, 0, (j + off) % nb))

    def body(*refs):
        h_ref, wg_ref, wu_ref = refs[:3]
        g_ref, u_ref, a_ref = refs[n_in:]
        hv = h_ref[...]
        g = jnp.dot(hv, wg_ref[...], preferred_element_type=F32)
        g_ref[...] = g.astype(BF16)
        u = jnp.dot(hv, wu_ref[...], preferred_element_type=F32)
        u_ref[...] = u.astype(BF16)
        a_ref[...] = (g * jax.nn.sigmoid(g) * u).astype(BF16)

    ins = [h, w_gu, w_gu]
    in_specs = [pl.BlockSpec((tm, D), lambda i, j: (i, 0)), w_spec(0), w_spec(nj)]
    if dep is not None:
        ins.append(dep)
        in_specs.append(ANY)
    o_spec = pl.BlockSpec((tm, tn), lambda i, j: (i, j))
    return _pcall(body, name=name, out_shape=(_sds((T, F), BF16),) * 3, grid=(T // tm, nj),
                  in_specs=in_specs, out_specs=(o_spec,) * 3,
                  compiler_params=_params("parallel", "parallel"))(*ins)


def _ffn_dgu_bwd(dy, w_down, gate, up, name, dep=None, tm=512):
    T, D = dy.shape
    F = w_down.shape[0]
    tm = _fit(T, tm)
    tn = _fit(F, 1536)
    n_in = 4 if dep is None else 5

    def body(*refs):
        dy_ref, wd_ref, g_ref, u_ref = refs[:4]
        o_ref = refs[n_in]
        da = lax.dot_general(dy_ref[...], wd_ref[...], _DIMS["nt"], preferred_element_type=F32)
        g = g_ref[...].astype(F32)
        sg = jax.nn.sigmoid(g)
        o_ref[0] = (da * u_ref[...].astype(F32) * (sg * (1.0 + g * (1.0 - sg)))).astype(BF16)
        o_ref[1] = (da * (g * sg)).astype(BF16)

    ins = [dy, w_down, gate, up]
    gu_spec = pl.BlockSpec((tm, tn), lambda i, j: (i, j))
    in_specs = [pl.BlockSpec((tm, D), lambda i, j: (i, 0)), pl.BlockSpec((tn, D), lambda i, j: (j, 0)), gu_spec, gu_spec]
    if dep is not None:
        ins.append(dep)
        in_specs.append(ANY)
    return _pcall(body, name=name, out_shape=_sds((2, T, F), BF16), grid=(T // tm, F // tn),
                  in_specs=in_specs, out_specs=pl.BlockSpec((2, tm, tn), lambda i, j: (0, i, j)),
                  compiler_params=_params("parallel", "parallel"))(*ins)


def _causal_taps(win, width, tt):
    for b in range(min(8, width)):
        wb = win if b == 0 else pltpu.roll(win, b, 0)
        a = 0
        while 8 * a + b <= width - 1:
            yield width - 1 - (8 * a + b), wb[HALO - 8 * a:HALO - 8 * a + tt]
            a += 1


def _anticausal_taps(win, width, tt):
    rows = tt + HALO
    for b in range(min(8, width)):
        wb = win if b == 0 else pltpu.roll(win, rows - b, 0)
        a = 0
        while 8 * a + b <= width - 1:
            yield width - 1 - (8 * a + b), wb[8 * a:8 * a + tt]
            a += 1


def _lanes(d):
    cw = _fit(d, CONV_LANES)
    return [slice(s, s + cw) for s in range(0, d, cw)], cw


def _a_mid_fwd(bcz, conv_w, name):
    T, D3 = bcz.shape
    D = D3 // 3
    width = conv_w.shape[0]
    tt = _fit(T, ROW_TILE, HALO)
    chunks, cw = _lanes(D)

    def body(b_ref, c_ref, z_ref, ch_ref, zh_ref, w_ref, o_ref, win_ref):
        i = pl.program_id(0)
        for sl in chunks:
            uh = ch_ref[:, sl].astype(F32) * zh_ref[:, sl].astype(F32)
            win_ref[0:HALO, :] = jnp.where(i > 0, uh, 0.0)
            win_ref[HALO:, :] = c_ref[:, sl].astype(F32) * z_ref[:, sl].astype(F32)
            acc = jnp.zeros((tt, cw), F32)
            for k, xs in _causal_taps(win_ref[...], width, tt):
                acc = acc + w_ref[k:k + 1, sl] * xs
            o_ref[:, sl] = (b_ref[:, sl].astype(F32) * acc).astype(BF16)

    return _pcall(body, name=name, out_shape=_sds((T, D), BF16), grid=(T // tt,),
                  in_specs=[_row(tt, D, 0), _row(tt, D, 1), _row(tt, D, 2), _prev_halo(tt, D, 1), _prev_halo(tt, D, 2),
                            _const((width, D))],
                  out_specs=_row(tt, D), scratch_shapes=[pltpu.VMEM((HALO + tt, cw), F32)],
                  compiler_params=_params("parallel"))(bcz, bcz, bcz, bcz, bcz, conv_w)


def _a_mid_bwd(bcz, dy2, conv_w, name):
    T, D3 = bcz.shape
    D = D3 // 3
    width = conv_w.shape[0]
    tt = _fit(T, ROW_TILE, HALO)
    chunks, cw = _lanes(D)
    n_tiles = T // tt

    def body(b_ref, c_ref, z_ref, ch_ref, zh_ref, bn_ref, d_ref, dn_ref, w_ref, o_ref, dw_ref, win_ref, dwin_ref):
        i = pl.program_id(0)

        @pl.when(i == 0)
        def _():
            dw_ref[...] = jnp.zeros_like(dw_ref)

        for ci, sl in enumerate(chunks):
            c = c_ref[:, sl].astype(F32)
            z = z_ref[:, sl].astype(F32)
            b = b_ref[:, sl].astype(F32)
            d2 = d_ref[:, sl].astype(F32)
            uh = ch_ref[:, sl].astype(F32) * zh_ref[:, sl].astype(F32)
            win_ref[0:HALO, :] = jnp.where(i > 0, uh, 0.0)
            win_ref[HALO:, :] = c * z
            d1 = d2 * b
            d1n = dn_ref[:, sl].astype(F32) * bn_ref[:, sl].astype(F32)
            dwin_ref[0:tt, :] = d1
            dwin_ref[tt:, :] = jnp.where(i < n_tiles - 1, d1n, 0.0)
            y1 = jnp.zeros((tt, cw), F32)
            for k, xs in _causal_taps(win_ref[...], width, tt):
                y1 = y1 + w_ref[k:k + 1, sl] * xs
                dw_ref[k:k + 1, sl] += jnp.sum(d1 * xs, axis=0, keepdims=True)
            du = jnp.zeros((tt, cw), F32)
            for k, xs in _anticausal_taps(dwin_ref[...], width, tt):
                du = du + w_ref[k:k + 1, sl] * xs
            o_ref[:, ci * cw:(ci + 1) * cw] = (d2 * y1).astype(BF16)
            o_ref[:, D + ci * cw:D + (ci + 1) * cw] = (du * z).astype(BF16)
            o_ref[:, 2 * D + ci * cw:2 * D + (ci + 1) * cw] = (du * c).astype(BF16)

    return _pcall(body, name=name, out_shape=(_sds((T, 3 * D), BF16), _sds((width, D), F32)), grid=(n_tiles,),
                  in_specs=[_row(tt, D, 0), _row(tt, D, 1), _row(tt, D, 2), _prev_halo(tt, D, 1), _prev_halo(tt, D, 2),
                            _next_halo(tt, D, T, 0), _row(tt, D), _next_halo(tt, D, T), _const((width, D))],
                  out_specs=(_row(tt, 3 * D), _const((width, D))),
                  scratch_shapes=[pltpu.VMEM((HALO + tt, cw), F32), pltpu.VMEM((tt + HALO, cw), F32)],
                  compiler_params=_params("arbitrary"))(bcz, bcz, bcz, bcz, bcz, bcz, dy2, dy2, conv_w)


_GELU_C = 0.7978845608028654
_GELU_A = 0.044715


def _gelu(v):
    return 0.5 * v * (1.0 + jnp.tanh(_GELU_C * (v + _GELU_A * v * v * v)))


def _gelu_grad(v):
    t = jnp.tanh(_GELU_C * (v + _GELU_A * v * v * v))
    return 0.5 * (1.0 + t) + 0.5 * v * (1.0 - t * t) * (_GELU_C * (1.0 + 3.0 * _GELU_A * v * v))


def _ln_stats(v):
    mu = jnp.mean(v, axis=-1, keepdims=True)
    vc = v - mu
    return vc * lax.rsqrt(jnp.mean(vc * vc, axis=-1, keepdims=True) + NORM_EPS)


def _tril(n):
    return lax.broadcasted_iota(jnp.int32, (n, n), 0) >= lax.broadcasted_iota(jnp.int32, (n, n), 1)


def _b_mid_fwd(uv, v_g, v_b, w_s, bias_b, name):
    T, D2 = uv.shape
    D = D2 // 2
    G, C, _ = w_s.shape
    gd = D // G
    tt = _fit(T, ROW_TILE, C)

    def body(u_ref, v_ref, g_ref, b_ref, ws_ref, bias_ref, o_ref, vln_ref):
        vln_ref[...] = (_ln_stats(_gelu(v_ref[...].astype(F32))) * g_ref[...] + b_ref[...]).astype(BF16)
        mask = _tril(C)
        for g in range(G):
            wsm = jnp.where(mask, ws_ref[g], 0.0).astype(BF16)
            cs = slice(g * gd, (g + 1) * gd)
            for n in range(tt // C):
                rs = slice(n * C, (n + 1) * C)
                sv = jnp.dot(wsm, vln_ref[rs, cs], preferred_element_type=F32) + bias_ref[g]
                o_ref[rs, cs] = (_gelu(u_ref[rs, cs].astype(F32)) * sv).astype(BF16)

    return _pcall(body, name=name, out_shape=_sds((T, D), BF16), grid=(T // tt,),
                  in_specs=[_row(tt, D, 0), _row(tt, D, 1), _const((1, D)), _const((1, D)), _const((G, C, C)),
                            _const((G, C, gd))],
                  out_specs=_row(tt, D), scratch_shapes=[pltpu.VMEM((tt, D), BF16)],
                  compiler_params=_params("parallel"))(uv, uv, v_g, v_b, w_s, bias_b)


def _b_mid_bwd(uv, dgated, v_g, v_b, w_s, bias_b, name):
    T, D2 = uv.shape
    D = D2 // 2
    G, C, _ = w_s.shape
    gd = D // G
    tt = _fit(T, ROW_TILE, C)

    def body(u_ref, v_ref, d_ref, g_ref, b_ref, ws_ref, bias_ref, o_ref, dws_ref, dsb_ref, dvg_ref, dvb_ref,
             vln_ref, dvln_ref):
        @pl.when(pl.program_id(0) == 0)
        def _():
            dws_ref[...] = jnp.zeros_like(dws_ref)
            dsb_ref[...] = jnp.zeros_like(dsb_ref)
            dvg_ref[...] = jnp.zeros_like(dvg_ref)
            dvb_ref[...] = jnp.zeros_like(dvb_ref)

        vpre = v_ref[...].astype(F32)
        vhat = _ln_stats(_gelu(vpre))
        vln_ref[...] = (vhat * g_ref[...] + b_ref[...]).astype(BF16)
        mask = _tril(C)
        lane = lax.broadcasted_iota(jnp.int32, (C, 128), 1)
        for g in range(G):
            wsm = jnp.where(mask, ws_ref[g], 0.0).astype(BF16)
            cs = slice(g * gd, (g + 1) * gd)
            for n in range(tt // C):
                rs = slice(n * C, (n + 1) * C)
                vt = vln_ref[rs, cs]
                sv = jnp.dot(wsm, vt, preferred_element_type=F32) + bias_ref[g]
                dg = d_ref[rs, cs].astype(F32)
                upre = u_ref[rs, cs].astype(F32)
                o_ref[rs, cs] = (dg * sv * _gelu_grad(upre)).astype(BF16)
                dsv = dg * _gelu(upre)
                dsb_ref[...] += jnp.where(lane == g, jnp.sum(dsv, axis=-1, keepdims=True), 0.0)
                dsv16 = dsv.astype(BF16)
                dws_ref[g] += jnp.where(mask, lax.dot_general(dsv16, vt, _DIMS["nt"], preferred_element_type=F32), 0.0)
                dvln_ref[rs, cs] = lax.dot_general(wsm, dsv16, _DIMS["tn"], preferred_element_type=F32)
        dvln = dvln_ref[...]
        dvg_ref[...] += jnp.sum(dvln * vhat, axis=0, keepdims=True)
        dvb_ref[...] += jnp.sum(dvln, axis=0, keepdims=True)
        dvh = dvln * g_ref[...]
        vc = _gelu(vpre)
        vc = vc - jnp.mean(vc, axis=-1, keepdims=True)
        rstd = lax.rsqrt(jnp.mean(vc * vc, axis=-1, keepdims=True) + NORM_EPS)
        dv = rstd * (dvh - jnp.mean(dvh, axis=-1, keepdims=True) - vhat * jnp.mean(dvh * vhat, axis=-1, keepdims=True))
        o_ref[:, D:] = (dv * _gelu_grad(vpre)).astype(BF16)

    return _pcall(body, name=name,
                  out_shape=(_sds((T, 2 * D), BF16), _sds((G, C, C), F32), _sds((C, 128), F32), _sds((1, D), F32),
                             _sds((1, D), F32)),
                  grid=(T // tt,),
                  in_specs=[_row(tt, D, 0), _row(tt, D, 1), _row(tt, D), _const((1, D)), _const((1, D)),
                            _const((G, C, C)), _const((G, C, gd))],
                  out_specs=(_row(tt, 2 * D), _const((G, C, C)), _const((C, 128)), _const((1, D)), _const((1, D))),
                  scratch_shapes=[pltpu.VMEM((tt, D), BF16), pltpu.VMEM((tt, D), F32)],
                  compiler_params=_params("arbitrary"))(uv, uv, dgated, v_g, v_b, w_s, bias_b)


def _c_mid_fwd(ag, conv_w, conv_b, ln_g, ln_b, name):
    T, D2 = ag.shape
    D = D2 // 2
    width = conv_w.shape[0]
    tt = _fit(T, ROW_TILE, HALO)
    chunks, cw = _lanes(D)

    def body(a_ref, g_ref, ah_ref, gh_ref, w_ref, cb_ref, lg_ref, lb_ref, y2_ref, o_ref, win_ref):
        i = pl.program_id(0)
        for sl in chunks:
            yh = ah_ref[:, sl].astype(F32) * jax.nn.sigmoid(gh_ref[:, sl].astype(F32))
            win_ref[0:HALO, :] = jnp.where(i > 0, yh, 0.0)
            win_ref[HALO:, :] = a_ref[:, sl].astype(F32) * jax.nn.sigmoid(g_ref[:, sl].astype(F32))
            acc = jnp.zeros((tt, cw), F32)
            for k, xs in _causal_taps(win_ref[...], width, tt):
                acc = acc + w_ref[k:k + 1, sl] * xs
            y2_ref[:, sl] = acc + cb_ref[:, sl]
        y3 = _ln_stats(y2_ref[...]) * lg_ref[...] + lb_ref[...]
        o_ref[...] = (y3 * jax.nn.sigmoid(y3)).astype(BF16)

    return _pcall(body, name=name, out_shape=(_sds((T, D), F32), _sds((T, D), BF16)), grid=(T // tt,),
                  in_specs=[_row(tt, D, 0), _row(tt, D, 1), _prev_halo(tt, D, 0), _prev_halo(tt, D, 1),
                            _const((width, D)), _const((1, D)), _const((1, D)), _const((1, D))],
                  out_specs=(_row(tt, D), _row(tt, D)), scratch_shapes=[pltpu.VMEM((HALO + tt, cw), F32)],
                  compiler_params=_params("parallel"))(ag, ag, ag, ag, conv_w, conv_b, ln_g, ln_b)


def _c_mid_bwd(ag, y2, dy4, conv_w, ln_g, ln_b, name):
    T, D2 = ag.shape
    D = D2 // 2
    width = conv_w.shape[0]
    tt = _fit(T, ROW_TILE, HALO)
    chunks, cw = _lanes(D)
    n_tiles = T // tt

    def ln_silu_bwd(y2v, dy4v, lg, lb):
        mu = jnp.mean(y2v, axis=-1, keepdims=True)
        yc = y2v - mu
        rstd = lax.rsqrt(jnp.mean(yc * yc, axis=-1, keepdims=True) + NORM_EPS)
        yh = yc * rstd
        y3 = yh * lg + lb
        sg = jax.nn.sigmoid(y3)
        dy3 = dy4v * (sg * (1.0 + y3 * (1.0 - sg)))
        dyh = dy3 * lg
        dy2 = rstd * (dyh - jnp.mean(dyh, axis=-1, keepdims=True) - yh * jnp.mean(dyh * yh, axis=-1, keepdims=True))
        return dy2, dy3, yh

    def body(a_ref, g_ref, ah_ref, gh_ref, y2_ref, y2n_ref, d_ref, dn_ref, w_ref, lg_ref, lb_ref,
             o_ref, dw_ref, dcb_ref, dlg_ref, dlb_ref, win_ref, dwin_ref):
        i = pl.program_id(0)

        @pl.when(i == 0)
        def _():
            dw_ref[...] = jnp.zeros_like(dw_ref)
            dcb_ref[...] = jnp.zeros_like(dcb_ref)
            dlg_ref[...] = jnp.zeros_like(dlg_ref)
            dlb_ref[...] = jnp.zeros_like(dlb_ref)

        lg = lg_ref[...]
        lb = lb_ref[...]
        dy2, dy3, yh = ln_silu_bwd(y2_ref[...], d_ref[...].astype(F32), lg, lb)
        dlg_ref[...] += jnp.sum(dy3 * yh, axis=0, keepdims=True)
        dlb_ref[...] += jnp.sum(dy3, axis=0, keepdims=True)
        dcb_ref[...] += jnp.sum(dy2, axis=0, keepdims=True)
        dwin_ref[0:tt, :] = dy2
        dy2n, _, _ = ln_silu_bwd(y2n_ref[...], dn_ref[...].astype(F32), lg, lb)
        dwin_ref[tt:, :] = jnp.where(i < n_tiles - 1, dy2n, 0.0)
        for ci, sl in enumerate(chunks):
            a = a_ref[:, sl].astype(F32)
            sg = jax.nn.sigmoid(g_ref[:, sl].astype(F32))
            yh1 = ah_ref[:, sl].astype(F32) * jax.nn.sigmoid(gh_ref[:, sl].astype(F32))
            win_ref[0:HALO, :] = jnp.where(i > 0, yh1, 0.0)
            win_ref[HALO:, :] = a * sg
            d2 = dwin_ref[0:tt, sl]
            for k, xs in _causal_taps(win_ref[...], width, tt):
                dw_ref[k:k + 1, sl] += jnp.sum(d2 * xs, axis=0, keepdims=True)
            d1 = jnp.zeros((tt, cw), F32)
            for k, xs in _anticausal_taps(dwin_ref[:, sl], width, tt):
                d1 = d1 + w_ref[k:k + 1, sl] * xs
            o_ref[:, ci * cw:(ci + 1) * cw] = (d1 * sg).astype(BF16)
            o_ref[:, D + ci * cw:D + (ci + 1) * cw] = (d1 * a * sg * (1.0 - sg)).astype(BF16)

    return _pcall(body, name=name,
                  out_shape=(_sds((T, 2 * D), BF16), _sds((width, D), F32), _sds((1, D), F32), _sds((1, D), F32),
                             _sds((1, D), F32)),
                  grid=(n_tiles,),
                  in_specs=[_row(tt, D, 0), _row(tt, D, 1), _prev_halo(tt, D, 0), _prev_halo(tt, D, 1),
                            _row(tt, D), _next_halo(tt, D, T), _row(tt, D), _next_halo(tt, D, T),
                            _const((width, D)), _const((1, D)), _const((1, D))],
                  out_specs=(_row(tt, 2 * D), _const((width, D)), _const((1, D)), _const((1, D)), _const((1, D))),
                  scratch_shapes=[pltpu.VMEM((HALO + tt, cw), F32), pltpu.VMEM((tt + HALO, D), F32)],
                  compiler_params=_params("arbitrary"))(ag, ag, ag, ag, y2, y2, dy4, dy4, conv_w, ln_g, ln_b)


def _place():
    x, y, c = lax.axis_index("x"), lax.axis_index("y"), lax.axis_index("c")
    return x, y, c


def _slot(px, py, pc):
    return 4 * px + 2 * py + pc


def _all_gather(shards, name):
    n = len(shards)

    def body(*refs):
        ins, outs = refs[:n], refs[n:2 * n]
        send_sems, recv_sems, local_sems = refs[2 * n:]
        x, y, c = _place()
        me, sibling = (x, y, c), (x, y, 1 - c)
        chips = [(1 - x, y), (x, 1 - y), (1 - x, 1 - y)]

        def copy(t, k, block, to, src=None):
            dst = outs[t].at[_slot(*block)]
            return pltpu.make_async_remote_copy(
                src_ref=dst if src is None else src, dst_ref=dst, send_sem=send_sems.at[t, k],
                recv_sem=recv_sems.at[t, k], device_id=to, device_id_type=MESH)

        mine = [pltpu.make_async_copy(ins[t], outs[t].at[_slot(*me)], local_sems.at[t]) for t in range(n)]
        for cp in mine:
            cp.start()
        first = []
        for j, chip in enumerate(chips):
            first += [copy(t, 1 + j, me, (*chip, c), src=ins[t]) for t in range(n)]
        first += [copy(t, 0, me, sibling, src=ins[t]) for t in range(n)]
        for cp in first:
            cp.start()
        passed = []
        for j, chip in enumerate(chips):
            for t in range(n):
                copy(t, 1 + j, (*chip, c), me).wait_recv()
                cp = copy(t, 4 + j, (*chip, c), sibling)
                cp.start()
                passed.append(cp)
        for t in range(n):
            copy(t, 0, sibling, me).wait_recv()
            for j, chip in enumerate(chips):
                copy(t, 4 + j, (*chip, 1 - c), me).wait_recv()
        for cp in first + passed:
            cp.wait_send()
        for cp in mine:
            cp.wait()

    outs = _pcall(
        body, name=name, out_shape=tuple(_sds((N_DEV,) + s.shape, s.dtype) for s in shards),
        in_specs=[ANY] * n, out_specs=(ANY,) * n,
        scratch_shapes=[pltpu.SemaphoreType.DMA((n, 7)), pltpu.SemaphoreType.DMA((n, 7)), pltpu.SemaphoreType.DMA((n,))],
    )(*shards)
    return list(outs)


_HBM = pl.BlockSpec(memory_space=pltpu.HBM)
_SEM = pl.BlockSpec(memory_space=pltpu.SEMAPHORE)
_DATAFLOW = pltpu.SideEffectType.DATAFLOW_SIDE_EFFECTING


def _peers(x, y, c):
    out = []
    for j in range(1, N_DEV):
        fx, fy, fc = (j >> 2) & 1, (j >> 1) & 1, j & 1
        out.append((1 - x if fx else x, 1 - y if fy else y, 1 - c if fc else c))
    return out


def _exchange_copies(ins, zones, mode, sems):
    send_sem, recv_sem, local_sem = sems
    x, y, c = _place()
    me = _slot(x, y, c)
    sibling = (x, y, 1 - c)
    chips = [(1 - x, y), (x, 1 - y), (1 - x, 1 - y)]
    local, remote = [], []

    def add(src, dst, to, landed):
        remote.append((pltpu.make_async_remote_copy(src_ref=src, dst_ref=dst, send_sem=send_sem, recv_sem=recv_sem,
                                                    device_id=to, device_id_type=MESH), landed))

    for t, zone in enumerate(zones):
        if mode == "scatter":
            local.append(pltpu.make_async_copy(ins[t].at[me], zone.at[me], local_sem))
            for peer in _peers(x, y, c):
                add(ins[t].at[_slot(*peer)], zone.at[me], peer, zone.at[_slot(*peer)])
        elif mode == "gather_chips":
            local.append(pltpu.make_async_copy(ins[t], zone.at[me], local_sem))
            for peer in [(*chip, c) for chip in chips] + [sibling]:
                add(ins[t], zone.at[me], peer, zone.at[_slot(*peer)])
        else:
            for chip in chips:
                block = zone.at[_slot(*chip, c)]
                add(block, block, sibling, zone.at[_slot(*chip, 1 - c)])
    return local, remote


def _exchange_start(srcs, lands, mode, after, name):
    if lands is None:
        lands = [lax.empty(s.shape if mode == "scatter" else (N_DEV,) + s.shape, s.dtype) for s in srcs]
    ns, na = len(srcs), len(srcs) + len(lands)

    def body(*refs):
        local, remote = _exchange_copies(refs[:ns], refs[ns:na], mode, refs[na + 1:na + 4])
        for cp in local:
            cp.start()
        for cp, _ in remote:
            cp.start()
        refs[-1][...] = jnp.zeros_like(refs[-1])

    hbm = lambda a: pltpu.with_memory_space_constraint(a, pltpu.HBM)
    arrays = list(srcs) + list(lands)
    outs = _pcall(
        body, name=name,
        out_shape=(pltpu.SemaphoreType.DMA(()),) * 3
        + tuple(pltpu.HBM(a.shape, a.dtype) for a in arrays) + (_sds((8, 128), F32),),
        in_specs=[_HBM] * na + [ANY],
        out_specs=(_SEM,) * 3 + (_HBM,) * na + (pl.BlockSpec(memory_space=pltpu.VMEM),),
        input_output_aliases={t: 3 + t for t in range(na)},
        compiler_params=pltpu.CompilerParams(has_side_effects=_DATAFLOW),
    )(*[hbm(a) for a in arrays], after)
    return outs[:3], list(outs[3:3 + ns]), list(outs[3 + ns:3 + na]), outs[-1]


def _exchange_wait(sems, srcs, lands, mode, after, name):
    ns, na = len(srcs), len(srcs) + len(lands)

    def body(*refs):
        local, remote = _exchange_copies(refs[:ns], refs[ns:na], mode, refs[na:na + 3])
        for cp in local:
            cp.wait()
        for cp, landed in remote:
            cp.wait_send()
            pltpu.make_async_remote_copy(
                src_ref=landed, dst_ref=landed, send_sem=refs[na], recv_sem=refs[na + 1],
                device_id=_place(), device_id_type=MESH).wait_recv()

    outs = _pcall(
        body, name=name, out_shape=tuple(pltpu.HBM(a.shape, a.dtype) for a in list(srcs) + list(lands)),
        in_specs=[_HBM] * na + [_SEM] * 3 + [ANY], out_specs=(_HBM,) * na,
        input_output_aliases={t: t for t in range(na)},
        compiler_params=pltpu.CompilerParams(has_side_effects=_DATAFLOW),
    )(*srcs, *lands, *sems, after)
    return list(outs[ns:])


def _reduce_adam(recvs, w, m, v, name):
    L, r, c = w.shape
    tr = _fit(r, max(16, (128 * 1024) // c), 16)
    ni = r // tr

    def recv_spec(l0):
        def index(l, i):
            return 0, jnp.where(l == l0, i, jnp.where(l < l0, 0, ni - 1)), 0
        return pl.BlockSpec((N_DEV, tr, c), index)

    lay = pl.BlockSpec((None, tr, c), lambda l, i: (l, i, 0))

    def body(*refs):
        recv_refs = refs[:L]
        w_ref, m_ref, v_ref, g_out, d_out, m_out, v_out = refs[L:]
        l = pl.program_id(0)
        for l0 in range(L):
            @pl.when(l == l0)
            def _(l0=l0):
                g = recv_refs[l0][0].astype(F32)
                for s in range(1, N_DEV):
                    g = g + recv_refs[l0][s].astype(F32)
                mn = ADAM_B1 * m_ref[...] + (1.0 - ADAM_B1) * g
                vn = ADAM_B2 * v_ref[...] + (1.0 - ADAM_B2) * (g * g)
                m_hat = mn / (1.0 - ADAM_B1 ** ADAM_STEP)
                v_hat = vn / (1.0 - ADAM_B2 ** ADAM_STEP)
                g_out[...] = g
                d_out[...] = -ADAM_LR * (m_hat / (jnp.sqrt(v_hat) + ADAM_EPS) + ADAM_WD * w_ref[...])
                m_out[...] = mn
                v_out[...] = vn

    return _pcall(body, name=name, out_shape=(_sds((L, r, c), F32),) * 4, grid=(L, ni),
                  in_specs=[recv_spec(l0) for l0 in range(L)] + [lay, lay, lay], out_specs=(lay,) * 4,
                  compiler_params=_params("arbitrary", "arbitrary"))(*recvs, w, m, v)


_SMALL_SHARDED = ["mix_norm", "xa_norm", "ffn_norm", "a_conv_w", "c_conv_w", "c_conv_b", "c_ln_g", "c_ln_b"]
_SMALL_REPLICATED = ["b_v_g", "b_v_b", "b_w_s", "b_s_bias"]
_BIG = ["xa_wq", "xa_wkv", "xa_wo", "ffn_w_gu", "ffn_w_down", "a_w_in", "a_w_out", "b_w_in", "b_w_out", "c_w_in",
        "c_w_out"]
_COL_SHARDED = {"xa_wkv", "ffn_w_gu", "a_w_in", "b_w_in", "c_w_in"}
_WEIGHTS = ["mix_norm", "xa_norm", "xa_wq", "xa_wkv", "xa_wo", "ffn_norm", "ffn_w_gu", "ffn_w_down", "a_w_in",
            "a_conv_w", "a_w_out", "b_w_in", "b_v_g", "b_v_b", "b_w_s", "b_s_bias", "b_w_out", "c_w_in", "c_conv_w",
            "c_conv_b", "c_ln_g", "c_ln_b", "c_w_out"]
_MIXER = "abc"


def _size(shape):
    size = 1
    for s in shape:
        size *= s
    return size


def _row_layout(shapes, width):
    offs, r = [], 0
    for shape in shapes:
        offs.append(r)
        r += -(-(-(-_size(shape) // width)) // 8) * 8
    return offs, r


def _pack_rows(arrays, width, fill):
    offs, total = _row_layout([a.shape for a in arrays], width)
    ends = offs[1:] + [total]
    rows = [jnp.pad(a.reshape(-1), (0, (e - o) * width - a.size), constant_values=fill).reshape(e - o, width)
            for a, o, e in zip(arrays, offs, ends)]
    return jnp.concatenate(rows, axis=0)


def _unpack_rows(packed, like):
    width = packed.shape[-1]
    offs, _ = _row_layout(like, width)
    return [packed[o:o + -(-_size(s) // width)].reshape(-1)[:_size(s)].reshape(s) for o, s in zip(offs, like)]


def _assemble_rows(pieces, rows, width, name):
    n = len(pieces)

    def body(*refs):
        o_ref = refs[n]
        o_ref[...] = jnp.zeros_like(o_ref)
        for r, (a, off) in zip(refs[:n], pieces):
            o_ref[off:off + a.shape[0], :] = r[...]

    return _pcall(body, name=name, out_shape=_sds((rows, width), F32),
                  compiler_params=pltpu.CompilerParams(vmem_limit_bytes=V7X_VMEM_LIMIT))(*[a for a, _ in pieces])


def kernel(x, mem, mix_norm, xa_norm, xa_wq, xa_wkv, xa_wo, ffn_norm, ffn_w_gu, ffn_w_down, a_w_in, a_conv_w, a_w_out, b_w_in, b_v_g, b_v_b, b_w_s, b_s_bias, b_w_out, c_w_in, c_conv_w, c_conv_b, c_ln_g, c_ln_b, c_w_out, loss_target, m_mix_norm, m_xa_norm, m_xa_wq, m_xa_wkv, m_xa_wo, m_ffn_norm, m_ffn_w_gu, m_ffn_w_down, m_a_w_in, m_a_conv_w, m_a_w_out, m_b_w_in, m_b_v_g, m_b_v_b, m_b_w_s, m_b_s_bias, m_b_w_out, m_c_w_in, m_c_conv_w, m_c_conv_b, m_c_ln_g, m_c_ln_b, m_c_w_out, v_mix_norm, v_xa_norm, v_xa_wq, v_xa_wkv, v_xa_wo, v_ffn_norm, v_ffn_w_gu, v_ffn_w_down, v_a_w_in, v_a_conv_w, v_a_w_out, v_b_w_in, v_b_v_g, v_b_v_b, v_b_w_s, v_b_s_bias, v_b_w_out, v_c_w_in, v_c_conv_w, v_c_conv_b, v_c_ln_g, v_c_ln_b, v_c_w_out):
    P = dict(locals())
    T, D = x.shape[1], x.shape[2]
    dl = D // N_DEV
    depth = mix_norm.shape[0]
    x0, mem0, target = x[0], mem[0], loss_target[0]
    my_slot = _slot(*_place())

    sh_shapes = [P[n].shape for n in _SMALL_SHARDED]
    packed = _pack_rows([P[n] for n in _SMALL_SHARDED], dl, 0.0)
    n_sh = packed.shape[0]
    gathered = _all_gather([packed], "ag_small")[0]
    full_rows = jnp.transpose(gathered, (1, 0, 2)).reshape(n_sh, D)
    small = dict(zip(_SMALL_SHARDED, _unpack_rows(full_rows, [s[:-1] + (D,) for s in sh_shapes])))
    G, C = b_w_s.shape[1], b_w_s.shape[2]
    gd = D // G
    bias_b = jnp.broadcast_to(b_s_bias[0][:, :, None], (G, C, gd))
    zero_row = jnp.zeros((1, D), F32)

    w16 = {n: P[n].astype(BF16) for n in _BIG}

    groups = [(i, part) for i in range(depth) for part in range(3)]

    def group_names(i, part):
        mx, slot = _MIXER[i % N_MIXERS], i // N_MIXERS
        if part == 0:
            return [(mx + "_w_in", slot), (mx + "_w_out", slot)]
        if part == 1:
            return [("xa_wq", i), ("xa_wkv", i), ("xa_wo", i)]
        return [("ffn_w_gu", i), ("ffn_w_down", i)]

    no_token = jnp.zeros((8, 128), F32)
    fwd = {"g": 0, "last": x0, "token": no_token, "stage1": {}, "stage2": {}}

    def tag(g):
        return "%d_%d" % groups[g]

    def start_stage1(g):
        if g < len(groups):
            names = group_names(*groups[g])
            sems, srcs, lands, token = _exchange_start([w16[n][j] for n, j in names], None, "gather_chips",
                                                       fwd["last"], "ag_start_" + tag(g))
            fwd["stage1"][g] = (sems, srcs, lands)
            fwd["last"] = fwd["token"] = token

    def start_stage2(g, after):
        if g < len(groups):
            sems, srcs, lands = fwd["stage1"].pop(g)
            lands = _exchange_wait(sems, srcs, lands, "gather_chips", after, "ag_wait_" + tag(g))
            sems, _, lands, token = _exchange_start([], lands, "gather_sibling", lands[0], "ag_pass_" + tag(g))
            fwd["stage2"][g] = (sems, lands)
            fwd["last"] = fwd["token"] = token

    def begin_group():
        g, y = fwd["g"], fwd["last"]
        sems, lands = fwd["stage2"].pop(g)
        fulls = _exchange_wait(sems, [], lands, "gather_sibling", y, "ag_done_" + tag(g))
        fwd["last"] = fulls[0]
        start_stage2(g + 1, y)
        start_stage1(g + 3)
        out = {}
        for (n, _), f in zip(group_names(*groups[g]), fulls):
            key = n[2:] if n[1] == "_" and n[0] in _MIXER else n
            out[key] = f if n in _COL_SHARDED else f.reshape(-1, f.shape[-1])
        fwd["g"] += 1
        return out, fwd["token"]

    def end_group(y):
        fwd["last"] = y

    for g0 in range(3):
        start_stage1(g0)
    start_stage2(0, fwd["last"])

    saved = []
    xin = x0
    h = _rms_fwd(x0, small["mix_norm"][0, 0][None], "rms_first")
    for i in range(depth):
        kind, slot = i % N_MIXERS, i // N_MIXERS
        W, dep = begin_group()
        S = {"W": W, "x0": xin, "h0": h}
        pre = _mm(h, W["w_in"], "nn", BF16, "mm_in_%s" % _MIXER[kind], b_blocked=True, dep=dep)
        S["pre"] = pre
        if kind == 0:
            mid = _a_mid_fwd(pre, small["a_conv_w"][slot], "a_mid_fwd")
        elif kind == 1:
            mid = _b_mid_fwd(pre, b_v_g, b_v_b, b_w_s[0], bias_b, "b_mid_fwd")
        else:
            y2c, mid = _c_mid_fwd(pre, small["c_conv_w"][slot], small["c_conv_b"], small["c_ln_g"], small["c_ln_b"],
                                  "c_mid_fwd")
            S["y2c"] = y2c
        S["mid"] = mid
        S["y0"] = _mm(mid, W["w_out"], "nn", F32, "mm_out")
        end_group(S["y0"])
        Wx, dep = begin_group()
        W.update(Wx)
        xin, h = _post_pre_fwd(xin, S["y0"], small["mix_norm"][i, 1][None], small["xa_norm"][i, 0][None], "post_pre")
        S["x1"], S["h1"] = xin, h
        S["q"] = _mm(h, W["xa_wq"], "nn", BF16, "mm_q", dep=dep)
        S["memn"] = _rms_fwd(mem0, small["xa_norm"][i, 2][None], "rms_mem")
        S["kv"] = _mm(S["memn"], W["xa_wkv"], "nn", BF16, "mm_kv", b_blocked=True)
        S["o"] = _attn_fwd(S["q"], S["kv"], "attn_fwd")
        S["y1"] = _mm(S["o"], W["xa_wo"], "nn", F32, "mm_out")
        end_group(S["y1"])
        Wf, dep = begin_group()
        W.update(Wf)
        xin, h = _post_pre_fwd(xin, S["y1"], small["xa_norm"][i, 1][None], small["ffn_norm"][i, 0][None], "post_pre")
        S["x2"], S["h2"] = xin, h
        S["gate"], S["up"], S["act"] = _ffn_gu_fwd(h, W["ffn_w_gu"], "ffn_gu_fwd", dep=dep)
        S["y2"] = _mm(S["act"], W["ffn_w_down"], "nn", F32, "mm_down")
        end_group(S["y2"])
        if i + 1 < depth:
            xin, h = _post_pre_fwd(xin, S["y2"], small["ffn_norm"][i, 1][None], small["mix_norm"][i + 1, 0][None],
                                   "post_pre")
        saved.append(S)

    last = saved[-1]
    loss_part, dx, dy, dg = _final_fwd_loss(xin, last["y2"], small["ffn_norm"][depth - 1, 1][None], target, "final_loss")
    loss = lax.psum(loss_part[0, 0], ("x", "y", "c"))

    g_mix = [[zero_row, zero_row] for _ in range(depth)]
    g_xa = [[zero_row, zero_row, zero_row] for _ in range(depth)]
    g_ffn = [[zero_row, zero_row] for _ in range(depth)]
    g_small = {}
    recv = {n: [None] * P[n].shape[0] for n in _BIG}
    g_ffn[depth - 1][1] = dg

    bwd = {"pend": None}

    def finish_scatter(after):
        if bwd["pend"] is not None:
            names, tag, sems, srcs, lands, _ = bwd["pend"]
            for (n, j, _), r in zip(names, _exchange_wait(sems, srcs, lands, "scatter", after, "rs_wait_" + tag)):
                recv[n][j] = r
            bwd["pend"] = None

    def scatter_group(names, tag, after):
        finish_scatter(after)
        parts = [g if n in _COL_SHARDED else g.reshape(N_DEV, -1, g.shape[-1]) for n, _, g in names]
        bwd["pend"] = (names, tag) + _exchange_start(parts, None, "scatter", after, "rs_start_" + tag)

    def scatter_token():
        return no_token if bwd["pend"] is None else bwd["pend"][5]

    for i in reversed(range(depth)):
        kind, slot = i % N_MIXERS, i // N_MIXERS
        mx = _MIXER[kind]
        S = saved[i]
        W = S["W"]
        dgu = _ffn_dgu_bwd(dy, W["ffn_w_down"], S["gate"], S["up"], "ffn_dgu_bwd", dep=scatter_token())
        dw_down = _mm(S["act"], dy, "tn", BF16, "mm_dw_down")
        dh = _mm(dgu, W["ffn_w_gu"], "nt", F32, "mm_dh_gu", a_blocked=True, b_blocked=True)
        dw_gu = _mm(S["h2"], dgu, "tn", BF16, "mm_dw_gu", b_blocked=True, out_blocks=N_DEV)
        dx, g_ffn[i][0], dy, g_xa[i][1] = _pre_post_bwd(dx, dh, S["x2"], small["ffn_norm"][i, 0][None], S["y1"],
                                                         small["xa_norm"][i, 1][None], "pre_post_bwd")
        scatter_group([("ffn_w_gu", i, dw_gu), ("ffn_w_down", i, dw_down)], "%d_2" % i, dx)
        do = _mm(dy, W["xa_wo"], "nt", BF16, "mm_nt_dd16", dep=scatter_token())
        dw_o = _mm(S["o"], dy, "tn", BF16, "mm_dw_dd")
        dq, dkv = _attn_bwd(S["q"], S["kv"], do, "attn_bwd")
        dkv16 = dkv.astype(BF16)
        dh = _mm(dq, W["xa_wq"], "nt", F32, "mm_nt_dd32")
        dw_q = _mm(S["h1"], dq, "tn", BF16, "mm_dw_dd")
        dw_kv = _mm(S["memn"], dkv16, "tn", BF16, "mm_dw_kv", out_blocks=N_DEV)
        dmemn = _mm(dkv16, W["xa_wkv"], "nt", F32, "mm_dmem", b_blocked=True)
        g_xa[i][2] = _rms_gain_grad(dmemn, mem0, "rms_gain_grad")
        dx, g_xa[i][0], dy, g_mix[i][1] = _pre_post_bwd(dx, dh, S["x1"], small["xa_norm"][i, 0][None], S["y0"],
                                                         small["mix_norm"][i, 1][None], "pre_post_bwd")
        scatter_group([("xa_wq", i, dw_q), ("xa_wkv", i, dw_kv), ("xa_wo", i, dw_o)], "%d_1" % i, dx)
        dmid = _mm(dy, W["w_out"], "nt", BF16, "mm_nt_dd16", dep=scatter_token())
        dw_out = _mm(S["mid"], dy, "tn", BF16, "mm_dw_dd")
        if kind == 0:
            dpre, dcw = _a_mid_bwd(S["pre"], dmid, small["a_conv_w"][slot], "a_mid_bwd")
            g_small.setdefault("a_conv_w", {})[slot] = dcw
        elif kind == 1:
            dpre, dws, dsb, dvg, dvb = _b_mid_bwd(S["pre"], dmid, b_v_g, b_v_b, b_w_s[0], bias_b, "b_mid_bwd")
            dsb_row = jnp.pad(jnp.transpose(dsb[:, :G]).reshape(1, G * C), ((0, 0), (0, (-G * C) % D)))
            g_small.update(b_w_s=dws.reshape(-1, D), b_s_bias=dsb_row.reshape(-1, D), b_v_g=dvg, b_v_b=dvb)
        else:
            dpre, dcw, dcb, dlg, dlb = _c_mid_bwd(S["pre"], S["y2c"], dmid, small["c_conv_w"][slot], small["c_ln_g"],
                                                  small["c_ln_b"], "c_mid_bwd")
            g_small.update(c_conv_w=dcw, c_conv_b=dcb, c_ln_g=dlg, c_ln_b=dlb)
        dh = _mm(dpre, W["w_in"], "nt", F32, "mm_dh_in_%s" % mx, b_blocked=True)
        dw_in = _mm(S["h0"], dpre, "tn", BF16, "mm_dw_in_%s" % mx, out_blocks=N_DEV)
        if i > 0:
            dx, g_mix[i][0], dy, g_ffn[i - 1][1] = _pre_post_bwd(
                dx, dh, S["x0"], small["mix_norm"][i, 0][None], saved[i - 1]["y2"],
                small["ffn_norm"][i - 1, 1][None], "pre_post_bwd")
        else:
            dx, g_mix[i][0] = _pre_post_bwd(dx, dh, S["x0"], small["mix_norm"][i, 0][None], None, None, "pre_bwd")
        scatter_group([(mx + "_w_in", slot, dw_in), (mx + "_w_out", slot, dw_out)], "%d_0" % i, dx)
        S.clear()
    grad_x = dx[None]

    sh_off = dict(zip(_SMALL_SHARDED, _row_layout(sh_shapes, dl)[0]))
    rep_offs, n_rep = _row_layout([P[n].shape for n in _SMALL_REPLICATED], D)
    rep_off = {n: n_sh + o for n, o in zip(_SMALL_REPLICATED, rep_offs)}
    pieces = []
    for i in range(depth):
        pieces += [(g, sh_off["mix_norm"] + 2 * i + j) for j, g in enumerate(g_mix[i])]
        pieces += [(g, sh_off["xa_norm"] + 3 * i + j) for j, g in enumerate(g_xa[i])]
        pieces += [(g, sh_off["ffn_norm"] + 2 * i + j) for j, g in enumerate(g_ffn[i])]
    pieces += [(g, sh_off["a_conv_w"] + a_conv_w.shape[1] * s) for s, g in g_small["a_conv_w"].items()]
    pieces += [(g_small[n], sh_off[n]) for n in ("c_conv_w", "c_conv_b", "c_ln_g", "c_ln_b")]
    pieces += [(g_small[n], rep_off[n]) for n in _SMALL_REPLICATED]
    part_all = _assemble_rows(pieces, n_sh + n_rep, D, "pack_small_grads")
    parts_all = _all_gather([part_all], "ag_small_grads")[0]
    recv_sh = lax.dynamic_slice_in_dim(parts_all[:, :n_sh], my_slot * dl, dl, axis=2)
    recv_rep = parts_all[:, n_sh:]

    out = {}

    def adam_small(names, recv_s, width, name):
        shapes = [P[n].shape for n in names]
        pw = _pack_rows([P[n] for n in names], width, 0.0)
        pm = _pack_rows([P["m_" + n] for n in names], width, 0.0)
        pv = _pack_rows([P["v_" + n] for n in names], width, 1.0)
        res = _reduce_adam([recv_s], pw[None], pm[None], pv[None], name)
        for kind, r in zip(("grad", "delta", "new_m", "new_v"), res):
            for n, a in zip(names, _unpack_rows(r[0], shapes)):
                out[kind + "_" + n] = a

    adam_small(_SMALL_SHARDED, recv_sh, dl, "adam_small_sharded")
    adam_small(_SMALL_REPLICATED, recv_rep, D, "adam_small_replicated")
    late = [n for n in _BIG if any(r is None for r in recv[n])]
    res = None
    for n in [n for n in _BIG if n not in late] + late:
        if n == late[0]:
            finish_scatter(dx if res is None else res[0])
        res = _reduce_adam(recv[n], P[n], P["m_" + n], P["v_" + n], "adam_" + n)
        for kind, r in zip(("grad", "delta", "new_m", "new_v"), res):
            out[kind + "_" + n] = r

    return (loss, grad_x, *[out[k + "_" + n] for k in ("grad", "delta", "new_m", "new_v") for n in _WEIGHTS])
```

```python
import functools

import jax
import jax.numpy as jnp
from jax import lax
from jax.experimental import pallas as pl
from jax.experimental.pallas import tpu as pltpu

F32 = jnp.float32
BF16 = jnp.bfloat16
MESH = pl.DeviceIdType.MESH
ANY = pl.BlockSpec(memory_space=pl.ANY)

N_DEV = 8
N_MIXERS = 3
XA_HEADS = 4
GMLP_GROUPS = 8
CHUNK = 128
NORM_EPS = 1e-6
HALO = 32
ROW_TILE = 256
CONV_LANES = 512
V7X_VMEM_LIMIT = 56 * 1024 * 1024

ADAM_LR = 0.001
ADAM_B1 = 0.9
ADAM_B2 = 0.999
ADAM_EPS = 1e-08
ADAM_WD = 0.01
ADAM_STEP = 10


def _pcall(body, **kw):
    return pl.pallas_call(body, **kw)


def _params(*sem):
    return pltpu.CompilerParams(dimension_semantics=sem, vmem_limit_bytes=V7X_VMEM_LIMIT)


def _fit(n, pref, mult=128):
    if n <= pref:
        return n
    t = (pref // mult) * mult
    while t >= mult:
        if n % t == 0:
            return t
        t -= mult
    return n


def _sds(shape, dtype):
    return jax.ShapeDtypeStruct(shape, dtype)


_DIMS = {"nn": (((1,), (0,)), ((), ())), "nt": (((1,), (1,)), ((), ())), "tn": (((0,), (0,)), ((), ()))}
MM_VMEM_BUDGET = 44 * 1024 * 1024


def _gcd(a, b):
    while b:
        a, b = b, a % b
    return a


def _mm(a, b, mode, out_dtype, name, *, a_blocked=False, b_blocked=False, out_blocks=None, dep=None):
    if mode == "tn":
        K, M = a.shape
    elif a_blocked:
        sa, M, ca = a.shape
        K = sa * ca
    else:
        M, K = a.shape
    n_unit = k_unit = None
    if b_blocked:
        _, d1, cb = b.shape
        if mode == "nt":
            N, k_unit = d1, cb
        else:
            N, n_unit = b.shape[0] * cb, cb
    else:
        N = b.shape[0] if mode == "nt" else b.shape[1]
    n_unit = n_unit or N
    k_unit = k_unit or K
    if a_blocked:
        k_unit = _gcd(k_unit, ca)
    if out_blocks:
        n_unit = _gcd(n_unit, N // out_blocks)
    tn = _fit(n_unit, 1536)
    tk = k_unit
    out_bytes = jnp.dtype(out_dtype).itemsize

    def need(tm_):
        nk_ = K // tk
        return (4 * (tm_ * tk + tk * tn) + 2 * tm_ * tn * out_bytes + 4 * tm_ * tn * (2 if nk_ > 1 else 1))

    tm = _fit(M, 1024)
    while need(tm) > MM_VMEM_BUDGET and tm % 256 == 0:
        tm //= 2
    nk = K // tk

    if mode == "tn":
        a_spec = pl.BlockSpec((tk, tm), lambda i, j, k: (k, i))
    elif a_blocked:
        ka = ca // tk
        a_spec = pl.BlockSpec((None, tm, tk), lambda i, j, k: (k // ka, i, k % ka))
    else:
        a_spec = pl.BlockSpec((tm, tk), lambda i, j, k: (i, k))
    if b_blocked and mode == "nt":
        kb = cb // tk
        b_spec = pl.BlockSpec((None, tn, tk), lambda i, j, k: (k // kb, j, k % kb))
    elif b_blocked:
        nb = cb // tn
        b_spec = pl.BlockSpec((None, tk, tn), lambda i, j, k: (j // nb, k, j % nb))
    elif mode == "nt":
        b_spec = pl.BlockSpec((tn, tk), lambda i, j, k: (j, k))
    else:
        b_spec = pl.BlockSpec((tk, tn), lambda i, j, k: (k, j))
    if out_blocks:
        ob = (N // out_blocks) // tn
        out_shape = _sds((out_blocks, M, N // out_blocks), out_dtype)
        o_spec = pl.BlockSpec((None, tm, tn), lambda i, j, k: (j // ob, i, j % ob))
    else:
        out_shape = _sds((M, N), out_dtype)
        o_spec = pl.BlockSpec((tm, tn), lambda i, j, k: (i, j))
    dims = _DIMS[mode]
    n_in = 2 if dep is None else 3

    def body(*refs):
        a_ref, b_ref = refs[0], refs[1]
        o_ref = refs[n_in]
        p = lax.dot_general(a_ref[...], b_ref[...], dims, preferred_element_type=F32)
        if nk == 1:
            o_ref[...] = p.astype(o_ref.dtype)
            return
        acc_ref = refs[n_in + 1]
        k = pl.program_id(2)

        @pl.when(k == 0)
        def _():
            acc_ref[...] = p

        @pl.when(k > 0)
        def _():
            acc_ref[...] += p

        @pl.when(k == nk - 1)
        def _():
            o_ref[...] = acc_ref[...].astype(o_ref.dtype)

    ins, in_specs = [a, b], [a_spec, b_spec]
    if dep is not None:
        ins.append(dep)
        in_specs.append(ANY)
    return _pcall(
        body, name=name, out_shape=out_shape, grid=(M // tm, N // tn, nk),
        in_specs=in_specs, out_specs=o_spec,
        scratch_shapes=[pltpu.VMEM((tm, tn), F32)] if nk > 1 else [],
        compiler_params=_params("parallel", "parallel", "arbitrary"),
    )(*ins)


def _rstd(v):
    return lax.rsqrt(jnp.mean(v * v, axis=-1, keepdims=True) + NORM_EPS)


def _rms_bwd_rows(v, g, dout):
    r = _rstd(v)
    vh = v * r
    dvh = dout * g
    dv = r * (dvh - vh * jnp.mean(dvh * vh, axis=-1, keepdims=True))
    return dv, jnp.sum(dout * vh, axis=0, keepdims=True)


def _row(tt, d, col=0):
    return pl.BlockSpec((tt, d), lambda i: (i, col))


def _const(shape):
    return pl.BlockSpec(shape, lambda i: (0,) * len(shape))


def _prev_halo(tt, d, col=0):
    return pl.BlockSpec((HALO, d), lambda i: (jnp.maximum(i * (tt // HALO) - 1, 0), col))


def _next_halo(tt, d, rows, col=0):
    last = rows // HALO - 1
    return pl.BlockSpec((HALO, d), lambda i: (jnp.minimum((i + 1) * (tt // HALO), last), col))


def _rms_fwd(x, g, name):
    T, D = x.shape
    tt = _fit(T, ROW_TILE, 8)

    def body(x_ref, g_ref, h_ref):
        v = x_ref[...]
        h_ref[...] = (v * _rstd(v) * g_ref[...]).astype(BF16)

    return _pcall(body, name=name, out_shape=_sds((T, D), BF16), grid=(T // tt,),
                  in_specs=[_row(tt, D), _const((1, D))], out_specs=_row(tt, D),
                  compiler_params=_params("parallel"))(x, g)


def _post_pre_fwd(x, y, g_post, g_pre, name):
    T, D = x.shape
    tt = _fit(T, ROW_TILE, 8)

    def body(x_ref, y_ref, gp_ref, gn_ref, xo_ref, h_ref):
        y = y_ref[...]
        xn = x_ref[...] + y * _rstd(y) * gp_ref[...]
        xo_ref[...] = xn
        h_ref[...] = (xn * _rstd(xn) * gn_ref[...]).astype(BF16)

    return _pcall(body, name=name, out_shape=(_sds((T, D), F32), _sds((T, D), BF16)), grid=(T // tt,),
                  in_specs=[_row(tt, D), _row(tt, D), _const((1, D)), _const((1, D))],
                  out_specs=(_row(tt, D), _row(tt, D)),
                  compiler_params=_params("parallel"))(x, y, g_post, g_pre)


def _final_fwd_loss(x, y, g_post, target, name):
    T, D = x.shape
    tt = _fit(T, ROW_TILE, 8)

    def body(x_ref, y_ref, g_ref, t_ref, loss_ref, dx_ref, dy_ref, dg_ref):
        i = pl.program_id(0)
        y = y_ref[...]
        g = g_ref[...]
        err = x_ref[...] + y * _rstd(y) * g - t_ref[...]
        part = 0.5 * jnp.sum(jnp.mean(err * err, axis=-1, keepdims=True))
        dx = err / D
        dx_ref[...] = dx
        dy, dg = _rms_bwd_rows(y, g, dx)
        dy_ref[...] = dy.astype(BF16)

        @pl.when(i == 0)
        def _():
            loss_ref[...] = jnp.zeros_like(loss_ref)
            dg_ref[...] = jnp.zeros_like(dg_ref)

        loss_ref[...] += part
        dg_ref[...] += dg

    return _pcall(body, name=name,
                  out_shape=(_sds((1, 128), F32), _sds((T, D), F32), _sds((T, D), BF16), _sds((1, D), F32)),
                  grid=(T // tt,),
                  in_specs=[_row(tt, D), _row(tt, D), _const((1, D)), _row(tt, D)],
                  out_specs=(_const((1, 128)), _row(tt, D), _row(tt, D), _const((1, D))),
                  compiler_params=_params("arbitrary"))(x, y, g_post, target)


def _pre_post_bwd(dx_out, dh, x_in, g_pre, y_prev, g_post_prev, name):
    T, D = x_in.shape
    tt = _fit(T, ROW_TILE, 8)
    with_prev = y_prev is not None

    def body(*refs):
        if with_prev:
            dxo_ref, dh_ref, x_ref, g_ref, y_ref, gp_ref, dxi_ref, dg_ref, dy_ref, dgp_ref = refs
        else:
            dxo_ref, dh_ref, x_ref, g_ref, dxi_ref, dg_ref = refs
        i = pl.program_id(0)
        dv, dg = _rms_bwd_rows(x_ref[...], g_ref[...], dh_ref[...].astype(F32))
        dxi = dxo_ref[...] + dv
        dxi_ref[...] = dxi

        @pl.when(i == 0)
        def _():
            dg_ref[...] = jnp.zeros_like(dg_ref)

        dg_ref[...] += dg
        if with_prev:
            dy, dgp = _rms_bwd_rows(y_ref[...], gp_ref[...], dxi)
            dy_ref[...] = dy.astype(BF16)

            @pl.when(i == 0)
            def _():
                dgp_ref[...] = jnp.zeros_like(dgp_ref)

            dgp_ref[...] += dgp

    ins = [dx_out, dh, x_in, g_pre]
    in_specs = [_row(tt, D), _row(tt, D), _row(tt, D), _const((1, D))]
    out_shape = [_sds((T, D), F32), _sds((1, D), F32)]
    out_specs = [_row(tt, D), _const((1, D))]
    if with_prev:
        ins += [y_prev, g_post_prev]
        in_specs += [_row(tt, D), _const((1, D))]
        out_shape += [_sds((T, D), BF16), _sds((1, D), F32)]
        out_specs += [_row(tt, D), _const((1, D))]
    return _pcall(body, name=name, out_shape=tuple(out_shape), grid=(T // tt,),
                  in_specs=in_specs, out_specs=tuple(out_specs),
                  compiler_params=_params("arbitrary"))(*ins)


def _rms_gain_grad(dout, v, name):
    T, D = v.shape
    tt = _fit(T, ROW_TILE, 8)

    def body(d_ref, v_ref, dg_ref):
        @pl.when(pl.program_id(0) == 0)
        def _():
            dg_ref[...] = jnp.zeros_like(dg_ref)

        v = v_ref[...]
        dg_ref[...] += jnp.sum(d_ref[...] * (v * _rstd(v)), axis=0, keepdims=True)

    return _pcall(body, name=name, out_shape=_sds((1, D), F32), grid=(T // tt,),
                  in_specs=[_row(tt, D), _row(tt, D)], out_specs=_const((1, D)),
                  compiler_params=_params("arbitrary"))(dout, v)


def _softmax_rows(s):
    e = jnp.exp(s - jnp.max(s, axis=-1, keepdims=True))
    return e / jnp.sum(e, axis=-1, keepdims=True)


def _attn_fwd(q, kv, name):
    T, D = q.shape
    nm = kv.shape[0]
    hd = D // XA_HEADS
    scale = hd ** -0.5
    tq = _fit(T, ROW_TILE, 8)

    def body(q_ref, k_ref, v_ref, o_ref):
        for h in range(XA_HEADS):
            sl = slice(h * hd, (h + 1) * hd)
            s = lax.dot_general(q_ref[:, sl], k_ref[:, sl], _DIMS["nt"], preferred_element_type=F32) * scale
            p = _softmax_rows(s)
            o_ref[:, sl] = jnp.dot(p.astype(BF16), v_ref[:, sl], preferred_element_type=F32).astype(BF16)

    return _pcall(body, name=name, out_shape=_sds((T, D), BF16), grid=(T // tq,),
                  in_specs=[_row(tq, D), pl.BlockSpec((nm, D), lambda i: (0, 0)), pl.BlockSpec((nm, D), lambda i: (0, 1))],
                  out_specs=_row(tq, D), compiler_params=_params("parallel"))(q, kv, kv)


def _attn_bwd(q, kv, do, name):
    T, D = q.shape
    nm = kv.shape[0]
    hd = D // XA_HEADS
    scale = hd ** -0.5
    tq = _fit(T, ROW_TILE, 8)

    def body(q_ref, k_ref, v_ref, do_ref, dq_ref, dkv_ref):
        @pl.when(pl.program_id(0) == 0)
        def _():
            dkv_ref[...] = jnp.zeros_like(dkv_ref)

        for h in range(XA_HEADS):
            sl = slice(h * hd, (h + 1) * hd)
            qh, kh, vh, doh = q_ref[:, sl], k_ref[:, sl], v_ref[:, sl], do_ref[:, sl]
            s = lax.dot_general(qh, kh, _DIMS["nt"], preferred_element_type=F32) * scale
            p = _softmax_rows(s)
            dp = lax.dot_general(doh, vh, _DIMS["nt"], preferred_element_type=F32)
            ds = (p * (dp - jnp.sum(dp * p, axis=-1, keepdims=True)) * scale).astype(BF16)
            dq_ref[:, sl] = jnp.dot(ds, kh, preferred_element_type=F32).astype(BF16)
            dkv_ref[:, sl] += lax.dot_general(ds, qh, _DIMS["tn"], preferred_element_type=F32)
            dkv_ref[:, D + h * hd:D + (h + 1) * hd] += lax.dot_general(
                p.astype(BF16), doh, _DIMS["tn"], preferred_element_type=F32)

    return _pcall(body, name=name, out_shape=(_sds((T, D), BF16), _sds((nm, 2 * D), F32)), grid=(T // tq,),
                  in_specs=[_row(tq, D), pl.BlockSpec((nm, D), lambda i: (0, 0)), pl.BlockSpec((nm, D), lambda i: (0, 1)),
                            _row(tq, D)],
                  out_specs=(_row(tq, D), _const((nm, 2 * D))),
                  compiler_params=_params("arbitrary"))(q, kv, kv, do)


def _ffn_gu_fwd(h, w_gu, name, dep=None, tm=512):
    T, D = h.shape
    S, _, c = w_gu.shape
    F = S * c // 2
    tm = _fit(T, tm)
    tn = _fit(c, 1536)
    nb = c // tn
    nj = F // tn
    n_in = 3 if dep is None else 4

    def w_spec(off):
        return pl.BlockSpec((None, D, tn), lambda i, j: ((j + off) // nb, 0, (j + off) % nb))

    def body(*refs):
        h_ref, wg_ref, wu_ref = refs[:3]
        g_ref, u_ref, a_ref = refs[n_in:]
        hv = h_ref[...]
        g = jnp.dot(hv, wg_ref[...], preferred_element_type=F32)
        g_ref[...] = g.astype(BF16)
        u = jnp.dot(hv, wu_ref[...], preferred_element_type=F32)
        u_ref[...] = u.astype(BF16)
        a_ref[...] = (g * jax.nn.sigmoid(g) * u).astype(BF16)

    ins = [h, w_gu, w_gu]
    in_specs = [pl.BlockSpec((tm, D), lambda i, j: (i, 0)), w_spec(0), w_spec(nj)]
    if dep is not None:
        ins.append(dep)
        in_specs.append(ANY)
    o_spec = pl.BlockSpec((tm, tn), lambda i, j: (i, j))
    return _pcall(body, name=name, out_shape=(_sds((T, F), BF16),) * 3, grid=(T // tm, nj),
                  in_specs=in_specs, out_specs=(o_spec,) * 3,
                  compiler_params=_params("parallel", "parallel"))(*ins)


def _ffn_dgu_bwd(dy, w_down, gate, up, name, dep=None, tm=512):
    T, D = dy.shape
    F = w_down.shape[0]
    tm = _fit(T, tm)
    tn = _fit(F, 1536)
    n_in = 4 if dep is None else 5

    def body(*refs):
        dy_ref, wd_ref, g_ref, u_ref = refs[:4]
        o_ref = refs[n_in]
        da = lax.dot_general(dy_ref[...], wd_ref[...], _DIMS["nt"], preferred_element_type=F32)
        g = g_ref[...].astype(F32)
        sg = jax.nn.sigmoid(g)
        o_ref[0] = (da * u_ref[...].astype(F32) * (sg * (1.0 + g * (1.0 - sg)))).astype(BF16)
        o_ref[1] = (da * (g * sg)).astype(BF16)

    ins = [dy, w_down, gate, up]
    gu_spec = pl.BlockSpec((tm, tn), lambda i, j: (i, j))
    in_specs = [pl.BlockSpec((tm, D), lambda i, j: (i, 0)), pl.BlockSpec((tn, D), lambda i, j: (j, 0)), gu_spec, gu_spec]
    if dep is not None:
        ins.append(dep)
        in_specs.append(ANY)
    return _pcall(body, name=name, out_shape=_sds((2, T, F), BF16), grid=(T // tm, F // tn),
                  in_specs=in_specs, out_specs=pl.BlockSpec((2, tm, tn), lambda i, j: (0, i, j)),
                  compiler_params=_params("parallel", "parallel"))(*ins)


def _causal_taps(win, width, tt):
    for b in range(min(8, width)):
        wb = win if b == 0 else pltpu.roll(win, b, 0)
        a = 0
        while 8 * a + b <= width - 1:
            yield width - 1 - (8 * a + b), wb[HALO - 8 * a:HALO - 8 * a + tt]
            a += 1


def _anticausal_taps(win, width, tt):
    rows = tt + HALO
    for b in range(min(8, width)):
        wb = win if b == 0 else pltpu.roll(win, rows - b, 0)
        a = 0
        while 8 * a + b <= width - 1:
            yield width - 1 - (8 * a + b), wb[8 * a:8 * a + tt]
            a += 1


def _lanes(d):
    cw = _fit(d, CONV_LANES)
    return [slice(s, s + cw) for s in range(0, d, cw)], cw


def _a_mid_fwd(bcz, conv_w, name):
    T, D3 = bcz.shape
    D = D3 // 3
    width = conv_w.shape[0]
    tt = _fit(T, ROW_TILE, HALO)
    chunks, cw = _lanes(D)

    def body(b_ref, c_ref, z_ref, ch_ref, zh_ref, w_ref, o_ref, win_ref):
        i = pl.program_id(0)
        for sl in chunks:
            uh = ch_ref[:, sl].astype(F32) * zh_ref[:, sl].astype(F32)
            win_ref[0:HALO, :] = jnp.where(i > 0, uh, 0.0)
            win_ref[HALO:, :] = c_ref[:, sl].astype(F32) * z_ref[:, sl].astype(F32)
            acc = jnp.zeros((tt, cw), F32)
            for k, xs in _causal_taps(win_ref[...], width, tt):
                acc = acc + w_ref[k:k + 1, sl] * xs
            o_ref[:, sl] = (b_ref[:, sl].astype(F32) * acc).astype(BF16)

    return _pcall(body, name=name, out_shape=_sds((T, D), BF16), grid=(T // tt,),
                  in_specs=[_row(tt, D, 0), _row(tt, D, 1), _row(tt, D, 2), _prev_halo(tt, D, 1), _prev_halo(tt, D, 2),
                            _const((width, D))],
                  out_specs=_row(tt, D), scratch_shapes=[pltpu.VMEM((HALO + tt, cw), F32)],
                  compiler_params=_params("parallel"))(bcz, bcz, bcz, bcz, bcz, conv_w)


def _a_mid_bwd(bcz, dy2, conv_w, name):
    T, D3 = bcz.shape
    D = D3 // 3
    width = conv_w.shape[0]
    tt = _fit(T, ROW_TILE, HALO)
    chunks, cw = _lanes(D)
    n_tiles = T // tt

    def body(b_ref, c_ref, z_ref, ch_ref, zh_ref, bn_ref, d_ref, dn_ref, w_ref, o_ref, dw_ref, win_ref, dwin_ref):
        i = pl.program_id(0)

        @pl.when(i == 0)
        def _():
            dw_ref[...] = jnp.zeros_like(dw_ref)

        for ci, sl in enumerate(chunks):
            c = c_ref[:, sl].astype(F32)
            z = z_ref[:, sl].astype(F32)
            b = b_ref[:, sl].astype(F32)
            d2 = d_ref[:, sl].astype(F32)
            uh = ch_ref[:, sl].astype(F32) * zh_ref[:, sl].astype(F32)
            win_ref[0:HALO, :] = jnp.where(i > 0, uh, 0.0)
            win_ref[HALO:, :] = c * z
            d1 = d2 * b
            d1n = dn_ref[:, sl].astype(F32) * bn_ref[:, sl].astype(F32)
            dwin_ref[0:tt, :] = d1
            dwin_ref[tt:, :] = jnp.where(i < n_tiles - 1, d1n, 0.0)
            y1 = jnp.zeros((tt, cw), F32)
            for k, xs in _causal_taps(win_ref[...], width, tt):
                y1 = y1 + w_ref[k:k + 1, sl] * xs
                dw_ref[k:k + 1, sl] += jnp.sum(d1 * xs, axis=0, keepdims=True)
            du = jnp.zeros((tt, cw), F32)
            for k, xs in _anticausal_taps(dwin_ref[...], width, tt):
                du = du + w_ref[k:k + 1, sl] * xs
            o_ref[:, ci * cw:(ci + 1) * cw] = (d2 * y1).astype(BF16)
            o_ref[:, D + ci * cw:D + (ci + 1) * cw] = (du * z).astype(BF16)
            o_ref[:, 2 * D + ci * cw:2 * D + (ci + 1) * cw] = (du * c).astype(BF16)

    return _pcall(body, name=name, out_shape=(_sds((T, 3 * D), BF16), _sds((width, D), F32)), grid=(n_tiles,),
                  in_specs=[_row(tt, D, 0), _row(tt, D, 1), _row(tt, D, 2), _prev_halo(tt, D, 1), _prev_halo(tt, D, 2),
                            _next_halo(tt, D, T, 0), _row(tt, D), _next_halo(tt, D, T), _const((width, D))],
                  out_specs=(_row(tt, 3 * D), _const((width, D))),
                  scratch_shapes=[pltpu.VMEM((HALO + tt, cw), F32), pltpu.VMEM((tt + HALO, cw), F32)],
                  compiler_params=_params("arbitrary"))(bcz, bcz, bcz, bcz, bcz, bcz, dy2, dy2, conv_w)


_GELU_C = 0.7978845608028654
_GELU_A = 0.044715


def _gelu(v):
    return 0.5 * v * (1.0 + jnp.tanh(_GELU_C * (v + _GELU_A * v * v * v)))


def _gelu_grad(v):
    t = jnp.tanh(_GELU_C * (v + _GELU_A * v * v * v))
    return 0.5 * (1.0 + t) + 0.5 * v * (1.0 - t * t) * (_GELU_C * (1.0 + 3.0 * _GELU_A * v * v))


def _ln_stats(v):
    mu = jnp.mean(v, axis=-1, keepdims=True)
    vc = v - mu
    return vc * lax.rsqrt(jnp.mean(vc * vc, axis=-1, keepdims=True) + NORM_EPS)


def _tril(n):
    return lax.broadcasted_iota(jnp.int32, (n, n), 0) >= lax.broadcasted_iota(jnp.int32, (n, n), 1)


def _b_mid_fwd(uv, v_g, v_b, w_s, bias_b, name):
    T, D2 = uv.shape
    D = D2 // 2
    G, C, _ = w_s.shape
    gd = D // G
    tt = _fit(T, ROW_TILE, C)

    def body(u_ref, v_ref, g_ref, b_ref, ws_ref, bias_ref, o_ref, vln_ref):
        vln_ref[...] = (_ln_stats(_gelu(v_ref[...].astype(F32))) * g_ref[...] + b_ref[...]).astype(BF16)
        mask = _tril(C)
        for g in range(G):
            wsm = jnp.where(mask, ws_ref[g], 0.0).astype(BF16)
            cs = slice(g * gd, (g + 1) * gd)
            for n in range(tt // C):
                rs = slice(n * C, (n + 1) * C)
                sv = jnp.dot(wsm, vln_ref[rs, cs], preferred_element_type=F32) + bias_ref[g]
                o_ref[rs, cs] = (_gelu(u_ref[rs, cs].astype(F32)) * sv).astype(BF16)

    return _pcall(body, name=name, out_shape=_sds((T, D), BF16), grid=(T // tt,),
                  in_specs=[_row(tt, D, 0), _row(tt, D, 1), _const((1, D)), _const((1, D)), _const((G, C, C)),
                            _const((G, C, gd))],
                  out_specs=_row(tt, D), scratch_shapes=[pltpu.VMEM((tt, D), BF16)],
                  compiler_params=_params("parallel"))(uv, uv, v_g, v_b, w_s, bias_b)


def _b_mid_bwd(uv, dgated, v_g, v_b, w_s, bias_b, name):
    T, D2 = uv.shape
    D = D2 // 2
    G, C, _ = w_s.shape
    gd = D // G
    tt = _fit(T, ROW_TILE, C)

    def body(u_ref, v_ref, d_ref, g_ref, b_ref, ws_ref, bias_ref, o_ref, dws_ref, dsb_ref, dvg_ref, dvb_ref,
             vln_ref, dvln_ref):
        @pl.when(pl.program_id(0) == 0)
        def _():
            dws_ref[...] = jnp.zeros_like(dws_ref)
            dsb_ref[...] = jnp.zeros_like(dsb_ref)
            dvg_ref[...] = jnp.zeros_like(dvg_ref)
            dvb_ref[...] = jnp.zeros_like(dvb_ref)

        vpre = v_ref[...].astype(F32)
        vhat = _ln_stats(_gelu(vpre))
        vln_ref[...] = (vhat * g_ref[...] + b_ref[...]).astype(BF16)
        mask = _tril(C)
        lane = lax.broadcasted_iota(jnp.int32, (C, 128), 1)
        for g in range(G):
            wsm = jnp.where(mask, ws_ref[g], 0.0).astype(BF16)
            cs = slice(g * gd, (g + 1) * gd)
            for n in range(tt // C):
                rs = slice(n * C, (n + 1) * C)
                vt = vln_ref[rs, cs]
                sv = jnp.dot(wsm, vt, preferred_element_type=F32) + bias_ref[g]
                dg = d_ref[rs, cs].astype(F32)
                upre = u_ref[rs, cs].astype(F32)
                o_ref[rs, cs] = (dg * sv * _gelu_grad(upre)).astype(BF16)
                dsv = dg * _gelu(upre)
                dsb_ref[...] += jnp.where(lane == g, jnp.sum(dsv, axis=-1, keepdims=True), 0.0)
                dsv16 = dsv.astype(BF16)
                dws_ref[g] += jnp.where(mask, lax.dot_general(dsv16, vt, _DIMS["nt"], preferred_element_type=F32), 0.0)
                dvln_ref[rs, cs] = lax.dot_general(wsm, dsv16, _DIMS["tn"], preferred_element_type=F32)
        dvln = dvln_ref[...]
        dvg_ref[...] += jnp.sum(dvln * vhat, axis=0, keepdims=True)
        dvb_ref[...] += jnp.sum(dvln, axis=0, keepdims=True)
        dvh = dvln * g_ref[...]
        vc = _gelu(vpre)
        vc = vc - jnp.mean(vc, axis=-1, keepdims=True)
        rstd = lax.rsqrt(jnp.mean(vc * vc, axis=-1, keepdims=True) + NORM_EPS)
        dv = rstd * (dvh - jnp.mean(dvh, axis=-1, keepdims=True) - vhat * jnp.mean(dvh * vhat, axis=-1, keepdims=True))
        o_ref[:, D:] = (dv * _gelu_grad(vpre)).astype(BF16)

    return _pcall(body, name=name,
                  out_shape=(_sds((T, 2 * D), BF16), _sds((G, C, C), F32), _sds((C, 128), F32), _sds((1, D), F32),
                             _sds((1, D), F32)),
                  grid=(T // tt,),
                  in_specs=[_row(tt, D, 0), _row(tt, D, 1), _row(tt, D), _const((1, D)), _const((1, D)),
                            _const((G, C, C)), _const((G, C, gd))],
                  out_specs=(_row(tt, 2 * D), _const((G, C, C)), _const((C, 128)), _const((1, D)), _const((1, D))),
                  scratch_shapes=[pltpu.VMEM((tt, D), BF16), pltpu.VMEM((tt, D), F32)],
                  compiler_params=_params("arbitrary"))(uv, uv, dgated, v_g, v_b, w_s, bias_b)


def _c_mid_fwd(ag, conv_w, conv_b, ln_g, ln_b, name):
    T, D2 = ag.shape
    D = D2 // 2
    width = conv_w.shape[0]
    tt = _fit(T, ROW_TILE, HALO)
    chunks, cw = _lanes(D)

    def body(a_ref, g_ref, ah_ref, gh_ref, w_ref, cb_ref, lg_ref, lb_ref, y2_ref, o_ref, win_ref):
        i = pl.program_id(0)
        for sl in chunks:
            yh = ah_ref[:, sl].astype(F32) * jax.nn.sigmoid(gh_ref[:, sl].astype(F32))
            win_ref[0:HALO, :] = jnp.where(i > 0, yh, 0.0)
            win_ref[HALO:, :] = a_ref[:, sl].astype(F32) * jax.nn.sigmoid(g_ref[:, sl].astype(F32))
            acc = jnp.zeros((tt, cw), F32)
            for k, xs in _causal_taps(win_ref[...], width, tt):
                acc = acc + w_ref[k:k + 1, sl] * xs
            y2_ref[:, sl] = acc + cb_ref[:, sl]
        y3 = _ln_stats(y2_ref[...]) * lg_ref[...] + lb_ref[...]
        o_ref[...] = (y3 * jax.nn.sigmoid(y3)).astype(BF16)

    return _pcall(body, name=name, out_shape=(_sds((T, D), F32), _sds((T, D), BF16)), grid=(T // tt,),
                  in_specs=[_row(tt, D, 0), _row(tt, D, 1), _prev_halo(tt, D, 0), _prev_halo(tt, D, 1),
                            _const((width, D)), _const((1, D)), _const((1, D)), _const((1, D))],
                  out_specs=(_row(tt, D), _row(tt, D)), scratch_shapes=[pltpu.VMEM((HALO + tt, cw), F32)],
                  compiler_params=_params("parallel"))(ag, ag, ag, ag, conv_w, conv_b, ln_g, ln_b)


def _c_mid_bwd(ag, y2, dy4, conv_w, ln_g, ln_b, name):
    T, D2 = ag.shape
    D = D2 // 2
    width = conv_w.shape[0]
    tt = _fit(T, ROW_TILE, HALO)
    chunks, cw = _lanes(D)
    n_tiles = T // tt

    def ln_silu_bwd(y2v, dy4v, lg, lb):
        mu = jnp.mean(y2v, axis=-1, keepdims=True)
        yc = y2v - mu
        rstd = lax.rsqrt(jnp.mean(yc * yc, axis=-1, keepdims=True) + NORM_EPS)
        yh = yc * rstd
        y3 = yh * lg + lb
        sg = jax.nn.sigmoid(y3)
        dy3 = dy4v * (sg * (1.0 + y3 * (1.0 - sg)))
        dyh = dy3 * lg
        dy2 = rstd * (dyh - jnp.mean(dyh, axis=-1, keepdims=True) - yh * jnp.mean(dyh * yh, axis=-1, keepdims=True))
        return dy2, dy3, yh

    def body(a_ref, g_ref, ah_ref, gh_ref, y2_ref, y2n_ref, d_ref, dn_ref, w_ref, lg_ref, lb_ref,
             o_ref, dw_ref, dcb_ref, dlg_ref, dlb_ref, win_ref, dwin_ref):
        i = pl.program_id(0)

        @pl.when(i == 0)
        def _():
            dw_ref[...] = jnp.zeros_like(dw_ref)
            dcb_ref[...] = jnp.zeros_like(dcb_ref)
            dlg_ref[...] = jnp.zeros_like(dlg_ref)
            dlb_ref[...] = jnp.zeros_like(dlb_ref)

        lg = lg_ref[...]
        lb = lb_ref[...]
        dy2, dy3, yh = ln_silu_bwd(y2_ref[...], d_ref[...].astype(F32), lg, lb)
        dlg_ref[...] += jnp.sum(dy3 * yh, axis=0, keepdims=True)
        dlb_ref[...] += jnp.sum(dy3, axis=0, keepdims=True)
        dcb_ref[...] += jnp.sum(dy2, axis=0, keepdims=True)
        dwin_ref[0:tt, :] = dy2
        dy2n, _, _ = ln_silu_bwd(y2n_ref[...], dn_ref[...].astype(F32), lg, lb)
        dwin_ref[tt:, :] = jnp.where(i < n_tiles - 1, dy2n, 0.0)
        for ci, sl in enumerate(chunks):
            a = a_ref[:, sl].astype(F32)
            sg = jax.nn.sigmoid(g_ref[:, sl].astype(F32))
            yh1 = ah_ref[:, sl].astype(F32) * jax.nn.sigmoid(gh_ref[:, sl].astype(F32))
            win_ref[0:HALO, :] = jnp.where(i > 0, yh1, 0.0)
            win_ref[HALO:, :] = a * sg
            d2 = dwin_ref[0:tt, sl]
            for k, xs in _causal_taps(win_ref[...], width, tt):
                dw_ref[k:k + 1, sl] += jnp.sum(d2 * xs, axis=0, keepdims=True)
            d1 = jnp.zeros((tt, cw), F32)
            for k, xs in _anticausal_taps(dwin_ref[:, sl], width, tt):
                d1 = d1 + w_ref[k:k + 1, sl] * xs
            o_ref[:, ci * cw:(ci + 1) * cw] = (d1 * sg).astype(BF16)
            o_ref[:, D + ci * cw:D + (ci + 1) * cw] = (d1 * a * sg * (1.0 - sg)).astype(BF16)

    return _pcall(body, name=name,
                  out_shape=(_sds((T, 2 * D), BF16), _sds((width, D), F32), _sds((1, D), F32), _sds((1, D), F32),
                             _sds((1, D), F32)),
                  grid=(n_tiles,),
                  in_specs=[_row(tt, D, 0), _row(tt, D, 1), _prev_halo(tt, D, 0), _prev_halo(tt, D, 1),
                            _row(tt, D), _next_halo(tt, D, T), _row(tt, D), _next_halo(tt, D, T),
                            _const((width, D)), _const((1, D)), _const((1, D))],
                  out_specs=(_row(tt, 2 * D), _const((width, D)), _const((1, D)), _const((1, D)), _const((1, D))),
                  scratch_shapes=[pltpu.VMEM((HALO + tt, cw), F32), pltpu.VMEM((tt + HALO, D), F32)],
                  compiler_params=_params("arbitrary"))(ag, ag, ag, ag, y2, y2, dy4, dy4, conv_w, ln_g, ln_b)


def _place():
    x, y, c = lax.axis_index("x"), lax.axis_index("y"), lax.axis_index("c")
    return x, y, c


def _slot(px, py, pc):
    return 4 * px + 2 * py + pc


def _all_gather(shards, name):
    n = len(shards)

    def body(*refs):
        ins, outs = refs[:n], refs[n:2 * n]
        send_sems, recv_sems, local_sems = refs[2 * n:]
        x, y, c = _place()
        me, sibling = (x, y, c), (x, y, 1 - c)
        chips = [(1 - x, y), (x, 1 - y), (1 - x, 1 - y)]

        def copy(t, k, block, to, src=None):
            dst = outs[t].at[_slot(*block)]
            return pltpu.make_async_remote_copy(
                src_ref=dst if src is None else src, dst_ref=dst, send_sem=send_sems.at[t, k],
                recv_sem=recv_sems.at[t, k], device_id=to, device_id_type=MESH)

        mine = [pltpu.make_async_copy(ins[t], outs[t].at[_slot(*me)], local_sems.at[t]) for t in range(n)]
        for cp in mine:
            cp.start()
        first = []
        for j, chip in enumerate(chips):
            first += [copy(t, 1 + j, me, (*chip, c), src=ins[t]) for t in range(n)]
        first += [copy(t, 0, me, sibling, src=ins[t]) for t in range(n)]
        for cp in first:
            cp.start()
        passed = []
        for j, chip in enumerate(chips):
            for t in range(n):
                copy(t, 1 + j, (*chip, c), me).wait_recv()
                cp = copy(t, 4 + j, (*chip, c), sibling)
                cp.start()
                passed.append(cp)
        for t in range(n):
            copy(t, 0, sibling, me).wait_recv()
            for j, chip in enumerate(chips):
                copy(t, 4 + j, (*chip, 1 - c), me).wait_recv()
        for cp in first + passed:
            cp.wait_send()
        for cp in mine:
            cp.wait()

    outs = _pcall(
        body, name=name, out_shape=tuple(_sds((N_DEV,) + s.shape, s.dtype) for s in shards),
        in_specs=[ANY] * n, out_specs=(ANY,) * n,
        scratch_shapes=[pltpu.SemaphoreType.DMA((n, 7)), pltpu.SemaphoreType.DMA((n, 7)), pltpu.SemaphoreType.DMA((n,))],
    )(*shards)
    return list(outs)


_HBM = pl.BlockSpec(memory_space=pltpu.HBM)
_SEM = pl.BlockSpec(memory_space=pltpu.SEMAPHORE)
_DATAFLOW = pltpu.SideEffectType.DATAFLOW_SIDE_EFFECTING


def _peers(x, y, c):
    out = []
    for j in range(1, N_DEV):
        fx, fy, fc = (j >> 2) & 1, (j >> 1) & 1, j & 1
        out.append((1 - x if fx else x, 1 - y if fy else y, 1 - c if fc else c))
    return out


def _exchange_copies(ins, zones, mode, sems):
    send_sem, recv_sem, local_sem = sems
    x, y, c = _place()
    me = _slot(x, y, c)
    sibling = (x, y, 1 - c)
    chips = [(1 - x, y), (x, 1 - y), (1 - x, 1 - y)]
    local, remote = [], []

    def add(src, dst, to, landed):
        remote.append((pltpu.make_async_remote_copy(src_ref=src, dst_ref=dst, send_sem=send_sem, recv_sem=recv_sem,
                                                    device_id=to, device_id_type=MESH), landed))

    for t, zone in enumerate(zones):
        if mode == "scatter":
            local.append(pltpu.make_async_copy(ins[t].at[me], zone.at[me], local_sem))
            for peer in _peers(x, y, c):
                add(ins[t].at[_slot(*peer)], zone.at[me], peer, zone.at[_slot(*peer)])
        elif mode == "gather_chips":
            local.append(pltpu.make_async_copy(ins[t], zone.at[me], local_sem))
            for peer in [(*chip, c) for chip in chips] + [sibling]:
                add(ins[t], zone.at[me], peer, zone.at[_slot(*peer)])
        else:
            for chip in chips:
                block = zone.at[_slot(*chip, c)]
                add(block, block, sibling, zone.at[_slot(*chip, 1 - c)])
    return local, remote


def _exchange_start(srcs, lands, mode, after, name):
    if lands is None:
        lands = [lax.empty(s.shape if mode == "scatter" else (N_DEV,) + s.shape, s.dtype) for s in srcs]
    ns, na = len(srcs), len(srcs) + len(lands)

    def body(*refs):
        local, remote = _exchange_copies(refs[:ns], refs[ns:na], mode, refs[na + 1:na + 4])
        for cp in local:
            cp.start()
        for cp, _ in remote:
            cp.start()
        refs[-1][...] = jnp.zeros_like(refs[-1])

    hbm = lambda a: pltpu.with_memory_space_constraint(a, pltpu.HBM)
    arrays = list(srcs) + list(lands)
    outs = _pcall(
        body, name=name,
        out_shape=(pltpu.SemaphoreType.DMA(()),) * 3
        + tuple(pltpu.HBM(a.shape, a.dtype) for a in arrays) + (_sds((8, 128), F32),),
        in_specs=[_HBM] * na + [ANY],
        out_specs=(_SEM,) * 3 + (_HBM,) * na + (pl.BlockSpec(memory_space=pltpu.VMEM),),
        input_output_aliases={t: 3 + t for t in range(na)},
        compiler_params=pltpu.CompilerParams(has_side_effects=_DATAFLOW),
    )(*[hbm(a) for a in arrays], after)
    return outs[:3], list(outs[3:3 + ns]), list(outs[3 + ns:3 + na]), outs[-1]


def _exchange_wait(sems, srcs, lands, mode, after, name):
    ns, na = len(srcs), len(srcs) + len(lands)
    afters = list(after) if isinstance(after, (list, tuple)) else [after]

    def body(*refs):
        local, remote = _exchange_copies(refs[:ns], refs[ns:na], mode, refs[na:na + 3])
        for cp in local:
            cp.wait()
        for cp, landed in remote:
            cp.wait_send()
            pltpu.make_async_remote_copy(
                src_ref=landed, dst_ref=landed, send_sem=refs[na], recv_sem=refs[na + 1],
                device_id=_place(), device_id_type=MESH).wait_recv()

    outs = _pcall(
        body, name=name, out_shape=tuple(pltpu.HBM(a.shape, a.dtype) for a in list(srcs) + list(lands)),
        in_specs=[_HBM] * na + [_SEM] * 3 + [ANY] * len(afters), out_specs=(_HBM,) * na,
        input_output_aliases={t: t for t in range(na)},
        compiler_params=pltpu.CompilerParams(has_side_effects=_DATAFLOW),
    )(*srcs, *lands, *sems, *afters)
    return list(outs[ns:])


def _reduce_adam(recvs, w, m, v, name, dep=None):
    L, r, c = w.shape
    tr = _fit(r, max(16, (128 * 1024) // c), 16)
    ni = r // tr

    def recv_spec(l0):
        def index(l, i):
            return 0, jnp.where(l == l0, i, jnp.where(l < l0, 0, ni - 1)), 0
        return pl.BlockSpec((N_DEV, tr, c), index)

    lay = pl.BlockSpec((None, tr, c), lambda l, i: (l, i, 0))

    n_dep = 0 if dep is None else 1

    def body(*refs):
        recv_refs = refs[:L]
        w_ref, m_ref, v_ref = refs[L:L + 3]
        g_out, d_out, m_out, v_out = refs[L + 3 + n_dep:]
        l = pl.program_id(0)
        for l0 in range(L):
            @pl.when(l == l0)
            def _(l0=l0):
                g = recv_refs[l0][0].astype(F32)
                for s in range(1, N_DEV):
                    g = g + recv_refs[l0][s].astype(F32)
                mn = ADAM_B1 * m_ref[...] + (1.0 - ADAM_B1) * g
                vn = ADAM_B2 * v_ref[...] + (1.0 - ADAM_B2) * (g * g)
                m_hat = mn / (1.0 - ADAM_B1 ** ADAM_STEP)
                v_hat = vn / (1.0 - ADAM_B2 ** ADAM_STEP)
                g_out[...] = g
                d_out[...] = -ADAM_LR * (m_hat / (jnp.sqrt(v_hat) + ADAM_EPS) + ADAM_WD * w_ref[...])
                m_out[...] = mn
                v_out[...] = vn

    return _pcall(body, name=name, out_shape=(_sds((L, r, c), F32),) * 4, grid=(L, ni),
                  in_specs=[recv_spec(l0) for l0 in range(L)] + [lay, lay, lay] + [ANY] * n_dep, out_specs=(lay,) * 4,
                  compiler_params=_params("arbitrary", "arbitrary"))(*recvs, w, m, v, *([dep] if n_dep else []))


_SMALL_SHARDED = ["mix_norm", "xa_norm", "ffn_norm", "a_conv_w", "c_conv_w", "c_conv_b", "c_ln_g", "c_ln_b"]
_SMALL_REPLICATED = ["b_v_g", "b_v_b", "b_w_s", "b_s_bias"]
_BIG = ["xa_wq", "xa_wkv", "xa_wo", "ffn_w_gu", "ffn_w_down", "a_w_in", "a_w_out", "b_w_in", "b_w_out", "c_w_in",
        "c_w_out"]
_COL_SHARDED = {"xa_wkv", "ffn_w_gu", "a_w_in", "b_w_in", "c_w_in"}
_WEIGHTS = ["mix_norm", "xa_norm", "xa_wq", "xa_wkv", "xa_wo", "ffn_norm", "ffn_w_gu", "ffn_w_down", "a_w_in",
            "a_conv_w", "a_w_out", "b_w_in", "b_v_g", "b_v_b", "b_w_s", "b_s_bias", "b_w_out", "c_w_in", "c_conv_w",
            "c_conv_b", "c_ln_g", "c_ln_b", "c_w_out"]
_MIXER = "abc"


def _size(shape):
    size = 1
    for s in shape:
        size *= s
    return size


def _row_layout(shapes, width):
    offs, r = [], 0
    for shape in shapes:
        offs.append(r)
        r += -(-(-(-_size(shape) // width)) // 8) * 8
    return offs, r


def _pack_rows(arrays, width, fill):
    offs, total = _row_layout([a.shape for a in arrays], width)
    ends = offs[1:] + [total]
    rows = [jnp.pad(a.reshape(-1), (0, (e - o) * width - a.size), constant_values=fill).reshape(e - o, width)
            for a, o, e in zip(arrays, offs, ends)]
    return jnp.concatenate(rows, axis=0)


def _unpack_rows(packed, like):
    width = packed.shape[-1]
    offs, _ = _row_layout(like, width)
    return [packed[o:o + -(-_size(s) // width)].reshape(-1)[:_size(s)].reshape(s) for o, s in zip(offs, like)]


def _assemble_rows(pieces, rows, width, name):
    n = len(pieces)

    def body(*refs):
        o_ref = refs[n]
        o_ref[...] = jnp.zeros_like(o_ref)
        for r, (a, off) in zip(refs[:n], pieces):
            o_ref[off:off + a.shape[0], :] = r[...]

    return _pcall(body, name=name, out_shape=_sds((rows, width), F32),
                  compiler_params=pltpu.CompilerParams(vmem_limit_bytes=V7X_VMEM_LIMIT))(*[a for a, _ in pieces])


def kernel(x, mem, mix_norm, xa_norm, xa_wq, xa_wkv, xa_wo, ffn_norm, ffn_w_gu, ffn_w_down, a_w_in, a_conv_w, a_w_out, b_w_in, b_v_g, b_v_b, b_w_s, b_s_bias, b_w_out, c_w_in, c_conv_w, c_conv_b, c_ln_g, c_ln_b, c_w_out, loss_target, m_mix_norm, m_xa_norm, m_xa_wq, m_xa_wkv, m_xa_wo, m_ffn_norm, m_ffn_w_gu, m_ffn_w_down, m_a_w_in, m_a_conv_w, m_a_w_out, m_b_w_in, m_b_v_g, m_b_v_b, m_b_w_s, m_b_s_bias, m_b_w_out, m_c_w_in, m_c_conv_w, m_c_conv_b, m_c_ln_g, m_c_ln_b, m_c_w_out, v_mix_norm, v_xa_norm, v_xa_wq, v_xa_wkv, v_xa_wo, v_ffn_norm, v_ffn_w_gu, v_ffn_w_down, v_a_w_in, v_a_conv_w, v_a_w_out, v_b_w_in, v_b_v_g, v_b_v_b, v_b_w_s, v_b_s_bias, v_b_w_out, v_c_w_in, v_c_conv_w, v_c_conv_b, v_c_ln_g, v_c_ln_b, v_c_w_out):
    P = dict(locals())
    T, D = x.shape[1], x.shape[2]
    dl = D // N_DEV
    depth = mix_norm.shape[0]
    x0, mem0, target = x[0], mem[0], loss_target[0]
    my_slot = _slot(*_place())

    sh_shapes = [P[n].shape for n in _SMALL_SHARDED]
    packed = _pack_rows([P[n] for n in _SMALL_SHARDED], dl, 0.0)
    n_sh = packed.shape[0]
    gathered = _all_gather([packed], "ag_small")[0]
    full_rows = jnp.transpose(gathered, (1, 0, 2)).reshape(n_sh, D)
    small = dict(zip(_SMALL_SHARDED, _unpack_rows(full_rows, [s[:-1] + (D,) for s in sh_shapes])))
    G, C = b_w_s.shape[1], b_w_s.shape[2]
    gd = D // G
    bias_b = jnp.broadcast_to(b_s_bias[0][:, :, None], (G, C, gd))
    zero_row = jnp.zeros((1, D), F32)

    w16 = {n: P[n].astype(BF16) for n in _BIG}

    groups = [(i, part) for i in range(depth) for part in range(3)]

    def group_names(i, part):
        mx, slot = _MIXER[i % N_MIXERS], i // N_MIXERS
        if part == 0:
            return [(mx + "_w_in", slot), (mx + "_w_out", slot)]
        if part == 1:
            return [("xa_wq", i), ("xa_wkv", i), ("xa_wo", i)]
        return [("ffn_w_gu", i), ("ffn_w_down", i)]

    no_token = jnp.zeros((8, 128), F32)
    fwd = {"g": 0, "last": full_rows, "token": no_token, "stage1": {}, "stage2": {}}

    def tag(g):
        return "%d_%d" % groups[g]

    def start_stage1(g):
        if g < len(groups):
            names = group_names(*groups[g])
            sems, srcs, lands, token = _exchange_start([w16[n][j] for n, j in names], None, "gather_chips",
                                                       fwd["last"], "ag_start_" + tag(g))
            fwd["stage1"][g] = (sems, srcs, lands)
            fwd["last"] = fwd["token"] = token

    def start_stage2(g, after):
        if g < len(groups):
            sems, srcs, lands = fwd["stage1"].pop(g)
            lands = _exchange_wait(sems, srcs, lands, "gather_chips", after, "ag_wait_" + tag(g))
            sems, _, lands, token = _exchange_start([], lands, "gather_sibling", after, "ag_pass_" + tag(g))
            fwd["stage2"][g] = (sems, lands)
            fwd["last"] = fwd["token"] = token

    def begin_group():
        g, y = fwd["g"], fwd["last"]
        sems, lands = fwd["stage2"].pop(g)
        fulls = _exchange_wait(sems, [], lands, "gather_sibling", y, "ag_done_" + tag(g))
        fwd["last"] = fulls[0]
        start_stage1(g + 3)
        out = {}
        for (n, _), f in zip(group_names(*groups[g]), fulls):
            key = n[2:] if n[1] == "_" and n[0] in _MIXER else n
            out[key] = f if n in _COL_SHARDED else f.reshape(-1, f.shape[-1])
        fwd["g"] += 1
        return out, fwd["token"]

    def mid_group(y):
        start_stage2(fwd["g"], y)
        return fwd["token"]

    def end_group(y):
        fwd["last"] = y

    for g0 in range(3):
        start_stage1(g0)
    start_stage2(0, fwd["last"])

    saved = []
    xin = x0
    h = _rms_fwd(x0, small["mix_norm"][0, 0][None], "rms_first")
    for i in range(depth):
        kind, slot = i % N_MIXERS, i // N_MIXERS
        W, dep = begin_group()
        S = {"W": W, "x0": xin, "h0": h}
        pre = _mm(h, W["w_in"], "nn", BF16, "mm_in_%s" % _MIXER[kind], b_blocked=True, dep=dep)
        S["pre"] = pre
        dep = mid_group(pre)
        if kind == 0:
            mid = _a_mid_fwd(pre, small["a_conv_w"][slot], "a_mid_fwd")
        elif kind == 1:
            mid = _b_mid_fwd(pre, b_v_g, b_v_b, b_w_s[0], bias_b, "b_mid_fwd")
        else:
            y2c, mid = _c_mid_fwd(pre, small["c_conv_w"][slot], small["c_conv_b"], small["c_ln_g"], small["c_ln_b"],
                                  "c_mid_fwd")
            S["y2c"] = y2c
        S["mid"] = mid
        S["y0"] = _mm(mid, W["w_out"], "nn", F32, "mm_out", dep=dep)
        end_group(S["y0"])
        Wx, dep = begin_group()
        W.update(Wx)
        xin, h = _post_pre_fwd(xin, S["y0"], small["mix_norm"][i, 1][None], small["xa_norm"][i, 0][None], "post_pre")
        S["x1"], S["h1"] = xin, h
        S["q"] = _mm(h, W["xa_wq"], "nn", BF16, "mm_q", dep=dep)
        dep = mid_group(S["q"])
        S["memn"] = _rms_fwd(mem0, small["xa_norm"][i, 2][None], "rms_mem")
        S["kv"] = _mm(S["memn"], W["xa_wkv"], "nn", BF16, "mm_kv", b_blocked=True)
        S["o"] = _attn_fwd(S["q"], S["kv"], "attn_fwd")
        S["y1"] = _mm(S["o"], W["xa_wo"], "nn", F32, "mm_out", dep=dep)
        end_group(S["y1"])
        Wf, dep = begin_group()
        W.update(Wf)
        xin, h = _post_pre_fwd(xin, S["y1"], small["xa_norm"][i, 1][None], small["ffn_norm"][i, 0][None], "post_pre")
        S["x2"], S["h2"] = xin, h
        S["gate"], S["up"], S["act"] = _ffn_gu_fwd(h, W["ffn_w_gu"], "ffn_gu_fwd", dep=dep)
        S["y2"] = _mm(S["act"], W["ffn_w_down"], "nn", F32, "mm_down", dep=mid_group(S["act"]))
        end_group(S["y2"])
        if i + 1 < depth:
            xin, h = _post_pre_fwd(xin, S["y2"], small["ffn_norm"][i, 1][None], small["mix_norm"][i + 1, 0][None],
                                   "post_pre")
        saved.append(S)

    last = saved[-1]
    loss_part, dx, dy, dg = _final_fwd_loss(xin, last["y2"], small["ffn_norm"][depth - 1, 1][None], target, "final_loss")
    loss = lax.psum(loss_part[0, 0], ("x", "y", "c"))

    g_mix = [[zero_row, zero_row] for _ in range(depth)]
    g_xa = [[zero_row, zero_row, zero_row] for _ in range(depth)]
    g_ffn = [[zero_row, zero_row] for _ in range(depth)]
    g_small = {}
    recv = {n: [None] * P[n].shape[0] for n in _BIG}
    g_ffn[depth - 1][1] = dg

    bwd = {"pend": None}

    def finish_scatter(after):
        if bwd["pend"] is not None:
            names, tag, sems, srcs, lands, _ = bwd["pend"]
            for (n, j, _), r in zip(names, _exchange_wait(sems, srcs, lands, "scatter", after, "rs_wait_" + tag)):
                recv[n][j] = r
            bwd["pend"] = None

    def scatter_group(names, tag, after):
        finish_scatter(after)
        parts = [g if n in _COL_SHARDED else g.reshape(N_DEV, -1, g.shape[-1]) for n, _, g in names]
        bwd["pend"] = (names, tag) + _exchange_start(parts, None, "scatter", after, "rs_start_" + tag)

    def scatter_token():
        return no_token if bwd["pend"] is None else bwd["pend"][5]

    for i in reversed(range(depth)):
        kind, slot = i % N_MIXERS, i // N_MIXERS
        mx = _MIXER[kind]
        S = saved[i]
        W = S["W"]
        dgu = _ffn_dgu_bwd(dy, W["ffn_w_down"], S["gate"], S["up"], "ffn_dgu_bwd", dep=scatter_token())
        dw_down = _mm(S["act"], dy, "tn", BF16, "mm_dw_down")
        dh = _mm(dgu, W["ffn_w_gu"], "nt", F32, "mm_dh_gu", a_blocked=True, b_blocked=True)
        dw_gu = _mm(S["h2"], dgu, "tn", BF16, "mm_dw_gu", b_blocked=True, out_blocks=N_DEV)
        dx, g_ffn[i][0], dy, g_xa[i][1] = _pre_post_bwd(dx, dh, S["x2"], small["ffn_norm"][i, 0][None], S["y1"],
                                                         small["xa_norm"][i, 1][None], "pre_post_bwd")
        scatter_group([("ffn_w_gu", i, dw_gu), ("ffn_w_down", i, dw_down)], "%d_2" % i, dx)
        do = _mm(dy, W["xa_wo"], "nt", BF16, "mm_nt_dd16", dep=scatter_token())
        dw_o = _mm(S["o"], dy, "tn", BF16, "mm_dw_dd")
        dq, dkv = _attn_bwd(S["q"], S["kv"], do, "attn_bwd")
        dkv16 = dkv.astype(BF16)
        dh = _mm(dq, W["xa_wq"], "nt", F32, "mm_nt_dd32")
        dw_q = _mm(S["h1"], dq, "tn", BF16, "mm_dw_dd")
        dw_kv = _mm(S["memn"], dkv16, "tn", BF16, "mm_dw_kv", out_blocks=N_DEV)
        dmemn = _mm(dkv16, W["xa_wkv"], "nt", F32, "mm_dmem", b_blocked=True)
        g_xa[i][2] = _rms_gain_grad(dmemn, mem0, "rms_gain_grad")
        dx, g_xa[i][0], dy, g_mix[i][1] = _pre_post_bwd(dx, dh, S["x1"], small["xa_norm"][i, 0][None], S["y0"],
                                                         small["mix_norm"][i, 1][None], "pre_post_bwd")
        scatter_group([("xa_wq", i, dw_q), ("xa_wkv", i, dw_kv), ("xa_wo", i, dw_o)], "%d_1" % i, dx)
        dmid = _mm(dy, W["w_out"], "nt", BF16, "mm_nt_dd16", dep=scatter_token())
        dw_out = _mm(S["mid"], dy, "tn", BF16, "mm_dw_dd")
        if kind == 0:
            dpre, dcw = _a_mid_bwd(S["pre"], dmid, small["a_conv_w"][slot], "a_mid_bwd")
            g_small.setdefault("a_conv_w", {})[slot] = dcw
        elif kind == 1:
            dpre, dws, dsb, dvg, dvb = _b_mid_bwd(S["pre"], dmid, b_v_g, b_v_b, b_w_s[0], bias_b, "b_mid_bwd")
            dsb_row = jnp.pad(jnp.transpose(dsb[:, :G]).reshape(1, G * C), ((0, 0), (0, (-G * C) % D)))
            g_small.update(b_w_s=dws.reshape(-1, D), b_s_bias=dsb_row.reshape(-1, D), b_v_g=dvg, b_v_b=dvb)
        else:
            dpre, dcw, dcb, dlg, dlb = _c_mid_bwd(S["pre"], S["y2c"], dmid, small["c_conv_w"][slot], small["c_ln_g"],
                                                  small["c_ln_b"], "c_mid_bwd")
            g_small.update(c_conv_w=dcw, c_conv_b=dcb, c_ln_g=dlg, c_ln_b=dlb)
        dh = _mm(dpre, W["w_in"], "nt", F32, "mm_dh_in_%s" % mx, b_blocked=True)
        dw_in = _mm(S["h0"], dpre, "tn", BF16, "mm_dw_in_%s" % mx, out_blocks=N_DEV)
        if i > 0:
            dx, g_mix[i][0], dy, g_ffn[i - 1][1] = _pre_post_bwd(
                dx, dh, S["x0"], small["mix_norm"][i, 0][None], saved[i - 1]["y2"],
                small["ffn_norm"][i - 1, 1][None], "pre_post_bwd")
        else:
            dx, g_mix[i][0] = _pre_post_bwd(dx, dh, S["x0"], small["mix_norm"][i, 0][None], None, None, "pre_bwd")
        scatter_group([(mx + "_w_in", slot, dw_in), (mx + "_w_out", slot, dw_out)], "%d_0" % i, dx)
        S.clear()
    grad_x = dx[None]

    sh_off = dict(zip(_SMALL_SHARDED, _row_layout(sh_shapes, dl)[0]))
    rep_offs, n_rep = _row_layout([P[n].shape for n in _SMALL_REPLICATED], D)
    rep_off = {n: n_sh + o for n, o in zip(_SMALL_REPLICATED, rep_offs)}
    pieces = []
    for i in range(depth):
        pieces += [(g, sh_off["mix_norm"] + 2 * i + j) for j, g in enumerate(g_mix[i])]
        pieces += [(g, sh_off["xa_norm"] + 3 * i + j) for j, g in enumerate(g_xa[i])]
        pieces += [(g, sh_off["ffn_norm"] + 2 * i + j) for j, g in enumerate(g_ffn[i])]
    pieces += [(g, sh_off["a_conv_w"] + a_conv_w.shape[1] * s) for s, g in g_small["a_conv_w"].items()]
    pieces += [(g_small[n], sh_off[n]) for n in ("c_conv_w", "c_conv_b", "c_ln_g", "c_ln_b")]
    pieces += [(g_small[n], rep_off[n]) for n in _SMALL_REPLICATED]
    part_all = _assemble_rows(pieces, n_sh + n_rep, D, "pack_small_grads")
    parts_all = _all_gather([part_all], "ag_small_grads")[0]
    recv_sh = lax.dynamic_slice_in_dim(parts_all[:, :n_sh], my_slot * dl, dl, axis=2)
    recv_rep = parts_all[:, n_sh:]

    out = {}

    def adam_small(names, recv_s, width, name):
        shapes = [P[n].shape for n in names]
        pw = _pack_rows([P[n] for n in names], width, 0.0)
        pm = _pack_rows([P["m_" + n] for n in names], width, 0.0)
        pv = _pack_rows([P["v_" + n] for n in names], width, 1.0)
        res = _reduce_adam([recv_s], pw[None], pm[None], pv[None], name)
        for kind, r in zip(("grad", "delta", "new_m", "new_v"), res):
            for n, a in zip(names, _unpack_rows(r[0], shapes)):
                out[kind + "_" + n] = a

    adam_small(_SMALL_SHARDED, recv_sh, dl, "adam_small_sharded")
    adam_small(_SMALL_REPLICATED, recv_rep, D, "adam_small_replicated")
    late = [n for n in _BIG if any(r is None for r in recv[n])]
    early_done = [g_xa[i][2] for i in range(depth)]
    for n in [n for n in _BIG if n not in late] + late:
        if n == late[0]:
            finish_scatter(early_done)
        res = _reduce_adam(recv[n], P[n], P["m_" + n], P["v_" + n], "adam_" + n, dep=scatter_token())
        early_done.append(res[0])
        for kind, r in zip(("grad", "delta", "new_m", "new_v"), res):
            out[kind + "_" + n] = r

    return (loss, grad_x, *[out[k + "_" + n] for k in ("grad", "delta", "new_m", "new_v") for n in _WEIGHTS])
```

```python
import functools

import jax
import jax.numpy as jnp
from jax import lax
from jax.experimental import pallas as pl
from jax.experimental.pallas import tpu as pltpu

F32 = jnp.float32
BF16 = jnp.bfloat16
MESH = pl.DeviceIdType.MESH
ANY = pl.BlockSpec(memory_space=pl.ANY)

N_DEV = 8
N_MIXERS = 3
XA_HEADS = 4
GMLP_GROUPS = 8
CHUNK = 128
NORM_EPS = 1e-6
HALO = 32
ROW_TILE = 256
CONV_LANES = 512
V7X_VMEM_LIMIT = 56 * 1024 * 1024

ADAM_LR = 0.001
ADAM_B1 = 0.9
ADAM_B2 = 0.999
ADAM_EPS = 1e-08
ADAM_WD = 0.01
ADAM_STEP = 10


def _pcall(body, **kw):
    return pl.pallas_call(body, **kw)


def _params(*sem):
    return pltpu.CompilerParams(dimension_semantics=sem, vmem_limit_bytes=V7X_VMEM_LIMIT)


def _fit(n, pref, mult=128):
    if n <= pref:
        return n
    t = (pref // mult) * mult
    while t >= mult:
        if n % t == 0:
            return t
        t -= mult
    return n


def _sds(shape, dtype):
    return jax.ShapeDtypeStruct(shape, dtype)


_DIMS = {"nn": (((1,), (0,)), ((), ())), "nt": (((1,), (1,)), ((), ())), "tn": (((0,), (0,)), ((), ()))}
MM_VMEM_BUDGET = 44 * 1024 * 1024


def _gcd(a, b):
    while b:
        a, b = b, a % b
    return a


def _mm(a, b, mode, out_dtype, name, *, a_blocked=False, b_blocked=False, out_blocks=None, dep=None):
    if mode == "tn":
        K, M = a.shape
    elif a_blocked:
        sa, M, ca = a.shape
        K = sa * ca
    else:
        M, K = a.shape
    n_unit = k_unit = None
    if b_blocked:
        _, d1, cb = b.shape
        if mode == "nt":
            N, k_unit = d1, cb
        else:
            N, n_unit = b.shape[0] * cb, cb
    else:
        N = b.shape[0] if mode == "nt" else b.shape[1]
    n_unit = n_unit or N
    k_unit = k_unit or K
    if a_blocked:
        k_unit = _gcd(k_unit, ca)
    if out_blocks:
        n_unit = _gcd(n_unit, N // out_blocks)
    tn = _fit(n_unit, 1536)
    tk = k_unit
    out_bytes = jnp.dtype(out_dtype).itemsize

    def need(tm_, gk_=1):
        nk_ = K // (tk * gk_)
        return (4 * gk_ * (tm_ * tk + tk * tn) + 2 * tm_ * tn * out_bytes
                + 4 * tm_ * tn * ((2 if nk_ > 1 else 1) + (1 if gk_ > 1 else 0)))

    tm = _fit(M, 1024)
    gk = 1
    if b_blocked and mode == "nt" and tk == cb:
        for cand in (8, 4, 2):
            if b.shape[0] % cand == 0 and (not a_blocked or (ca // tk) % cand == 0) and need(tm, cand) <= MM_VMEM_BUDGET:
                gk = cand
                break
    while need(tm, gk) > MM_VMEM_BUDGET and tm % 256 == 0:
        tm //= 2
    nk = K // (tk * gk)

    if mode == "tn":
        a_spec = pl.BlockSpec((tk, tm), lambda i, j, k: (k, i))
    elif a_blocked:
        ka = ca // (tk * gk)
        a_spec = pl.BlockSpec((None, tm, tk * gk), lambda i, j, k: (k // ka, i, k % ka))
    else:
        a_spec = pl.BlockSpec((tm, tk * gk), lambda i, j, k: (i, k))
    if b_blocked and mode == "nt" and gk > 1:
        b_spec = pl.BlockSpec((gk, tn, tk), lambda i, j, k: (k, j, 0))
    elif b_blocked and mode == "nt":
        kb = cb // tk
        b_spec = pl.BlockSpec((None, tn, tk), lambda i, j, k: (k // kb, j, k % kb))
    elif b_blocked:
        nb = cb // tn
        b_spec = pl.BlockSpec((None, tk, tn), lambda i, j, k: (j // nb, k, j % nb))
    elif mode == "nt":
        b_spec = pl.BlockSpec((tn, tk), lambda i, j, k: (j, k))
    else:
        b_spec = pl.BlockSpec((tk, tn), lambda i, j, k: (k, j))
    if out_blocks:
        ob = (N // out_blocks) // tn
        out_shape = _sds((out_blocks, M, N // out_blocks), out_dtype)
        o_spec = pl.BlockSpec((None, tm, tn), lambda i, j, k: (j // ob, i, j % ob))
    else:
        out_shape = _sds((M, N), out_dtype)
        o_spec = pl.BlockSpec((tm, tn), lambda i, j, k: (i, j))
    dims = _DIMS[mode]
    n_in = 2 if dep is None else 3

    def body(*refs):
        a_ref, b_ref = refs[0], refs[1]
        o_ref = refs[n_in]
        if gk == 1:
            p = lax.dot_general(a_ref[...], b_ref[...], dims, preferred_element_type=F32)
        else:
            p = lax.dot_general(a_ref[:, 0:tk], b_ref[0], dims, preferred_element_type=F32)
            for s in range(1, gk):
                p = p + lax.dot_general(a_ref[:, s * tk:(s + 1) * tk], b_ref[s], dims, preferred_element_type=F32)
        if nk == 1:
            o_ref[...] = p.astype(o_ref.dtype)
            return
        acc_ref = refs[n_in + 1]
        k = pl.program_id(2)

        @pl.when(k == 0)
        def _():
            acc_ref[...] = p

        @pl.when(k > 0)
        def _():
            acc_ref[...] += p

        @pl.when(k == nk - 1)
        def _():
            o_ref[...] = acc_ref[...].astype(o_ref.dtype)

    ins, in_specs = [a, b], [a_spec, b_spec]
    if dep is not None:
        ins.append(dep)
        in_specs.append(ANY)
    return _pcall(
        body, name=name, out_shape=out_shape, grid=(M // tm, N // tn, nk),
        in_specs=in_specs, out_specs=o_spec,
        scratch_shapes=[pltpu.VMEM((tm, tn), F32)] if nk > 1 else [],
        compiler_params=_params("parallel", "parallel", "arbitrary"),
    )(*ins)


def _rstd(v):
    return lax.rsqrt(jnp.mean(v * v, axis=-1, keepdims=True) + NORM_EPS)


def _rms_bwd_rows(v, g, dout):
    r = _rstd(v)
    vh = v * r
    dvh = dout * g
    dv = r * (dvh - vh * jnp.mean(dvh * vh, axis=-1, keepdims=True))
    return dv, jnp.sum(dout * vh, axis=0, keepdims=True)


def _row(tt, d, col=0):
    return pl.BlockSpec((tt, d), lambda i: (i, col))


def _const(shape):
    return pl.BlockSpec(shape, lambda i: (0,) * len(shape))


def _prev_halo(tt, d, col=0):
    return pl.BlockSpec((HALO, d), lambda i: (jnp.maximum(i * (tt // HALO) - 1, 0), col))


def _next_halo(tt, d, rows, col=0):
    last = rows // HALO - 1
    return pl.BlockSpec((HALO, d), lambda i: (jnp.minimum((i + 1) * (tt // HALO), last), col))


def _rms_fwd(x, g, name):
    T, D = x.shape
    tt = _fit(T, ROW_TILE, 8)

    def body(x_ref, g_ref, h_ref):
        v = x_ref[...]
        h_ref[...] = (v * _rstd(v) * g_ref[...]).astype(BF16)

    return _pcall(body, name=name, out_shape=_sds((T, D), BF16), grid=(T // tt,),
                  in_specs=[_row(tt, D), _const((1, D))], out_specs=_row(tt, D),
                  compiler_params=_params("parallel"))(x, g)


def _post_pre_fwd(x, y, g_post, g_pre, name):
    T, D = x.shape
    tt = _fit(T, ROW_TILE, 8)

    def body(x_ref, y_ref, gp_ref, gn_ref, xo_ref, h_ref):
        y = y_ref[...].astype(F32)
        xn = x_ref[...] + y * _rstd(y) * gp_ref[...]
        xo_ref[...] = xn
        h_ref[...] = (xn * _rstd(xn) * gn_ref[...]).astype(BF16)

    return _pcall(body, name=name, out_shape=(_sds((T, D), F32), _sds((T, D), BF16)), grid=(T // tt,),
                  in_specs=[_row(tt, D), _row(tt, D), _const((1, D)), _const((1, D))],
                  out_specs=(_row(tt, D), _row(tt, D)),
                  compiler_params=_params("parallel"))(x, y, g_post, g_pre)


def _final_fwd_loss(x, y, g_post, target, name):
    T, D = x.shape
    tt = _fit(T, ROW_TILE, 8)

    def body(x_ref, y_ref, g_ref, t_ref, loss_ref, dx_ref, dy_ref, dg_ref):
        i = pl.program_id(0)
        y = y_ref[...].astype(F32)
        g = g_ref[...]
        err = x_ref[...] + y * _rstd(y) * g - t_ref[...]
        part = 0.5 * jnp.sum(jnp.mean(err * err, axis=-1, keepdims=True))
        dx = err / D
        dx_ref[...] = dx
        dy, dg = _rms_bwd_rows(y, g, dx)
        dy_ref[...] = dy.astype(BF16)

        @pl.when(i == 0)
        def _():
            loss_ref[...] = jnp.zeros_like(loss_ref)
            dg_ref[...] = jnp.zeros_like(dg_ref)

        loss_ref[...] += part
        dg_ref[...] += dg

    return _pcall(body, name=name,
                  out_shape=(_sds((1, 128), F32), _sds((T, D), F32), _sds((T, D), BF16), _sds((1, D), F32)),
                  grid=(T // tt,),
                  in_specs=[_row(tt, D), _row(tt, D), _const((1, D)), _row(tt, D)],
                  out_specs=(_const((1, 128)), _row(tt, D), _row(tt, D), _const((1, D))),
                  compiler_params=_params("arbitrary"))(x, y, g_post, target)


def _pre_post_bwd(dx_out, dh, x_in, g_pre, y_prev, g_post_prev, name):
    T, D = x_in.shape
    tt = _fit(T, ROW_TILE, 8)
    with_prev = y_prev is not None

    def body(*refs):
        if with_prev:
            dxo_ref, dh_ref, x_ref, g_ref, y_ref, gp_ref, dxi_ref, dg_ref, dy_ref, dgp_ref = refs
        else:
            dxo_ref, dh_ref, x_ref, g_ref, dxi_ref, dg_ref = refs
        i = pl.program_id(0)
        dv, dg = _rms_bwd_rows(x_ref[...], g_ref[...], dh_ref[...].astype(F32))
        dxi = dxo_ref[...] + dv
        dxi_ref[...] = dxi

        @pl.when(i == 0)
        def _():
            dg_ref[...] = jnp.zeros_like(dg_ref)

        dg_ref[...] += dg
        if with_prev:
            dy, dgp = _rms_bwd_rows(y_ref[...].astype(F32), gp_ref[...], dxi)
            dy_ref[...] = dy.astype(BF16)

            @pl.when(i == 0)
            def _():
                dgp_ref[...] = jnp.zeros_like(dgp_ref)

            dgp_ref[...] += dgp

    ins = [dx_out, dh, x_in, g_pre]
    in_specs = [_row(tt, D), _row(tt, D), _row(tt, D), _const((1, D))]
    out_shape = [_sds((T, D), F32), _sds((1, D), F32)]
    out_specs = [_row(tt, D), _const((1, D))]
    if with_prev:
        ins += [y_prev, g_post_prev]
        in_specs += [_row(tt, D), _const((1, D))]
        out_shape += [_sds((T, D), BF16), _sds((1, D), F32)]
        out_specs += [_row(tt, D), _const((1, D))]
    return _pcall(body, name=name, out_shape=tuple(out_shape), grid=(T // tt,),
                  in_specs=in_specs, out_specs=tuple(out_specs),
                  compiler_params=_params("arbitrary"))(*ins)


def _rms_gain_grad(dout, v, name):
    T, D = v.shape
    tt = _fit(T, ROW_TILE, 8)

    def body(d_ref, v_ref, dg_ref):
        @pl.when(pl.program_id(0) == 0)
        def _():
            dg_ref[...] = jnp.zeros_like(dg_ref)

        v = v_ref[...]
        dg_ref[...] += jnp.sum(d_ref[...] * (v * _rstd(v)), axis=0, keepdims=True)

    return _pcall(body, name=name, out_shape=_sds((1, D), F32), grid=(T // tt,),
                  in_specs=[_row(tt, D), _row(tt, D)], out_specs=_const((1, D)),
                  compiler_params=_params("arbitrary"))(dout, v)


def _softmax_rows(s):
    e = jnp.exp(s - jnp.max(s, axis=-1, keepdims=True))
    return e / jnp.sum(e, axis=-1, keepdims=True)


def _attn_fwd(q, kv, name):
    T, D = q.shape
    nm = kv.shape[0]
    hd = D // XA_HEADS
    scale = hd ** -0.5
    tq = _fit(T, ROW_TILE, 8)

    def body(q_ref, k_ref, v_ref, o_ref):
        for h in range(XA_HEADS):
            sl = slice(h * hd, (h + 1) * hd)
            s = lax.dot_general(q_ref[:, sl], k_ref[:, sl], _DIMS["nt"], preferred_element_type=F32) * scale
            p = _softmax_rows(s)
            o_ref[:, sl] = jnp.dot(p.astype(BF16), v_ref[:, sl], preferred_element_type=F32).astype(BF16)

    return _pcall(body, name=name, out_shape=_sds((T, D), BF16), grid=(T // tq,),
                  in_specs=[_row(tq, D), pl.BlockSpec((nm, D), lambda i: (0, 0)), pl.BlockSpec((nm, D), lambda i: (0, 1))],
                  out_specs=_row(tq, D), compiler_params=_params("parallel"))(q, kv, kv)


def _attn_bwd(q, kv, do, name):
    T, D = q.shape
    nm = kv.shape[0]
    hd = D // XA_HEADS
    scale = hd ** -0.5
    tq = _fit(T, ROW_TILE, 8)

    def body(q_ref, k_ref, v_ref, do_ref, dq_ref, dkv_ref):
        @pl.when(pl.program_id(0) == 0)
        def _():
            dkv_ref[...] = jnp.zeros_like(dkv_ref)

        for h in range(XA_HEADS):
            sl = slice(h * hd, (h + 1) * hd)
            qh, kh, vh, doh = q_ref[:, sl], k_ref[:, sl], v_ref[:, sl], do_ref[:, sl]
            s = lax.dot_general(qh, kh, _DIMS["nt"], preferred_element_type=F32) * scale
            p = _softmax_rows(s)
            dp = lax.dot_general(doh, vh, _DIMS["nt"], preferred_element_type=F32)
            ds = (p * (dp - jnp.sum(dp * p, axis=-1, keepdims=True)) * scale).astype(BF16)
            dq_ref[:, sl] = jnp.dot(ds, kh, preferred_element_type=F32).astype(BF16)
            dkv_ref[:, sl] += lax.dot_general(ds, qh, _DIMS["tn"], preferred_element_type=F32)
            dkv_ref[:, D + h * hd:D + (h + 1) * hd] += lax.dot_general(
                p.astype(BF16), doh, _DIMS["tn"], preferred_element_type=F32)

    return _pcall(body, name=name, out_shape=(_sds((T, D), BF16), _sds((nm, 2 * D), F32)), grid=(T // tq,),
                  in_specs=[_row(tq, D), pl.BlockSpec((nm, D), lambda i: (0, 0)), pl.BlockSpec((nm, D), lambda i: (0, 1)),
                            _row(tq, D)],
                  out_specs=(_row(tq, D), _const((nm, 2 * D))),
                  compiler_params=_params("arbitrary"))(q, kv, kv, do)


def _ffn_gu_fwd(h, w_gu, name, dep=None, tm=512):
    T, D = h.shape
    S, _, c = w_gu.shape
    F = S * c // 2
    tm = _fit(T, tm)
    tn = _fit(c, 1536)
    nb = c // tn
    nj = F // tn
    n_in = 3 if dep is None else 4

    def w_spec(off):
        return pl.BlockSpec((None, D, tn), lambda i, j: ((j + off) // nb, 0, (j + off) % nb))

    def body(*refs):
        h_ref, wg_ref, wu_ref = refs[:3]
        g_ref, u_ref, a_ref = refs[n_in:]
        hv = h_ref[...]
        g = jnp.dot(hv, wg_ref[...], preferred_element_type=F32)
        g_ref[...] = g.astype(BF16)
        u = jnp.dot(hv, wu_ref[...], preferred_element_type=F32)
        u_ref[...] = u.astype(BF16)
        a_ref[...] = (g * jax.nn.sigmoid(g) * u).astype(BF16)

    ins = [h, w_gu, w_gu]
    in_specs = [pl.BlockSpec((tm, D), lambda i, j: (i, 0)), w_spec(0), w_spec(nj)]
    if dep is not None:
        ins.append(dep)
        in_specs.append(ANY)
    o_spec = pl.BlockSpec((tm, tn), lambda i, j: (i, j))
    return _pcall(body, name=name, out_shape=(_sds((T, F), BF16),) * 3, grid=(T // tm, nj),
                  in_specs=in_specs, out_specs=(o_spec,) * 3,
                  compiler_params=_params("parallel", "parallel"))(*ins)


def _ffn_dgu_bwd(dy, w_down, gate, up, name, dep=None, tm=1024):
    T, D = dy.shape
    F = w_down.shape[0]
    tm = _fit(T, tm)
    tn = _fit(F, 512)
    cn = _fit(tn, 256)
    n_in = 4 if dep is None else 5

    def body(*refs):
        dy_ref, wd_ref, g_ref, u_ref = refs[:4]
        o_ref = refs[n_in]
        dyv = dy_ref[...]
        for n0 in range(0, tn, cn):
            da = lax.dot_general(dyv, wd_ref[n0:n0 + cn, :], _DIMS["nt"], preferred_element_type=F32)
            g = g_ref[:, n0:n0 + cn].astype(F32)
            sg = jax.nn.sigmoid(g)
            o_ref[0, :, n0:n0 + cn] = (da * u_ref[:, n0:n0 + cn].astype(F32) * (sg * (1.0 + g * (1.0 - sg)))).astype(BF16)
            o_ref[1, :, n0:n0 + cn] = (da * (g * sg)).astype(BF16)

    ins = [dy, w_down, gate, up]
    gu_spec = pl.BlockSpec((tm, tn), lambda i, j: (i, j))
    in_specs = [pl.BlockSpec((tm, D), lambda i, j: (i, 0)), pl.BlockSpec((tn, D), lambda i, j: (j, 0)), gu_spec, gu_spec]
    if dep is not None:
        ins.append(dep)
        in_specs.append(ANY)
    return _pcall(body, name=name, out_shape=_sds((2, T, F), BF16), grid=(T // tm, F // tn),
                  in_specs=in_specs, out_specs=pl.BlockSpec((2, tm, tn), lambda i, j: (0, i, j)),
                  compiler_params=_params("parallel", "parallel"))(*ins)


def _causal_taps(win, width, tt):
    for b in range(min(8, width)):
        wb = win if b == 0 else pltpu.roll(win, b, 0)
        a = 0
        while 8 * a + b <= width - 1:
            yield width - 1 - (8 * a + b), wb[HALO - 8 * a:HALO - 8 * a + tt]
            a += 1


def _anticausal_taps(win, width, tt):
    rows = tt + HALO
    for b in range(min(8, width)):
        wb = win if b == 0 else pltpu.roll(win, rows - b, 0)
        a = 0
        while 8 * a + b <= width - 1:
            yield width - 1 - (8 * a + b), wb[8 * a:8 * a + tt]
            a += 1


def _lanes(d):
    cw = _fit(d, CONV_LANES)
    return [slice(s, s + cw) for s in range(0, d, cw)], cw


def _a_mid_fwd(bcz, conv_w, name):
    T, D3 = bcz.shape
    D = D3 // 3
    width = conv_w.shape[0]
    tt = _fit(T, ROW_TILE, HALO)
    chunks, cw = _lanes(D)

    def body(b_ref, c_ref, z_ref, ch_ref, zh_ref, w_ref, o_ref, win_ref):
        i = pl.program_id(0)
        for sl in chunks:
            uh = ch_ref[:, sl].astype(F32) * zh_ref[:, sl].astype(F32)
            win_ref[0:HALO, :] = jnp.where(i > 0, uh, 0.0)
            win_ref[HALO:, :] = c_ref[:, sl].astype(F32) * z_ref[:, sl].astype(F32)
            acc = jnp.zeros((tt, cw), F32)
            for k, xs in _causal_taps(win_ref[...], width, tt):
                acc = acc + w_ref[k:k + 1, sl] * xs
            o_ref[:, sl] = (b_ref[:, sl].astype(F32) * acc).astype(BF16)

    return _pcall(body, name=name, out_shape=_sds((T, D), BF16), grid=(T // tt,),
                  in_specs=[_row(tt, D, 0), _row(tt, D, 1), _row(tt, D, 2), _prev_halo(tt, D, 1), _prev_halo(tt, D, 2),
                            _const((width, D))],
                  out_specs=_row(tt, D), scratch_shapes=[pltpu.VMEM((HALO + tt, cw), F32)],
                  compiler_params=_params("parallel"))(bcz, bcz, bcz, bcz, bcz, conv_w)


def _a_mid_bwd(bcz, dy2, conv_w, name):
    T, D3 = bcz.shape
    D = D3 // 3
    width = conv_w.shape[0]
    tt = _fit(T, ROW_TILE, HALO)
    chunks, cw = _lanes(D)
    n_tiles = T // tt

    def body(b_ref, c_ref, z_ref, ch_ref, zh_ref, bn_ref, d_ref, dn_ref, w_ref, o_ref, dw_ref, win_ref, dwin_ref):
        i = pl.program_id(0)

        @pl.when(i == 0)
        def _():
            dw_ref[...] = jnp.zeros_like(dw_ref)

        for ci, sl in enumerate(chunks):
            c = c_ref[:, sl].astype(F32)
            z = z_ref[:, sl].astype(F32)
            b = b_ref[:, sl].astype(F32)
            d2 = d_ref[:, sl].astype(F32)
            uh = ch_ref[:, sl].astype(F32) * zh_ref[:, sl].astype(F32)
            win_ref[0:HALO, :] = jnp.where(i > 0, uh, 0.0)
            win_ref[HALO:, :] = c * z
            d1 = d2 * b
            d1n = dn_ref[:, sl].astype(F32) * bn_ref[:, sl].astype(F32)
            dwin_ref[0:tt, :] = d1
            dwin_ref[tt:, :] = jnp.where(i < n_tiles - 1, d1n, 0.0)
            y1 = jnp.zeros((tt, cw), F32)
            for k, xs in _causal_taps(win_ref[...], width, tt):
                y1 = y1 + w_ref[k:k + 1, sl] * xs
                dw_ref[k:k + 1, sl] += jnp.sum(d1 * xs, axis=0, keepdims=True)
            du = jnp.zeros((tt, cw), F32)
            for k, xs in _anticausal_taps(dwin_ref[...], width, tt):
                du = du + w_ref[k:k + 1, sl] * xs
            o_ref[:, ci * cw:(ci + 1) * cw] = (d2 * y1).astype(BF16)
            o_ref[:, D + ci * cw:D + (ci + 1) * cw] = (du * z).astype(BF16)
            o_ref[:, 2 * D + ci * cw:2 * D + (ci + 1) * cw] = (du * c).astype(BF16)

    return _pcall(body, name=name, out_shape=(_sds((T, 3 * D), BF16), _sds((width, D), F32)), grid=(n_tiles,),
                  in_specs=[_row(tt, D, 0), _row(tt, D, 1), _row(tt, D, 2), _prev_halo(tt, D, 1), _prev_halo(tt, D, 2),
                            _next_halo(tt, D, T, 0), _row(tt, D), _next_halo(tt, D, T), _const((width, D))],
                  out_specs=(_row(tt, 3 * D), _const((width, D))),
                  scratch_shapes=[pltpu.VMEM((HALO + tt, cw), F32), pltpu.VMEM((tt + HALO, cw), F32)],
                  compiler_params=_params("arbitrary"))(bcz, bcz, bcz, bcz, bcz, bcz, dy2, dy2, conv_w)


_GELU_C = 0.7978845608028654
_GELU_A = 0.044715


def _gelu(v):
    return 0.5 * v * (1.0 + jnp.tanh(_GELU_C * (v + _GELU_A * v * v * v)))


def _gelu_grad(v):
    t = jnp.tanh(_GELU_C * (v + _GELU_A * v * v * v))
    return 0.5 * (1.0 + t) + 0.5 * v * (1.0 - t * t) * (_GELU_C * (1.0 + 3.0 * _GELU_A * v * v))


def _ln_stats(v):
    mu = jnp.mean(v, axis=-1, keepdims=True)
    vc = v - mu
    return vc * lax.rsqrt(jnp.mean(vc * vc, axis=-1, keepdims=True) + NORM_EPS)


def _tril(n):
    return lax.broadcasted_iota(jnp.int32, (n, n), 0) >= lax.broadcasted_iota(jnp.int32, (n, n), 1)


def _b_mid_fwd(uv, v_g, v_b, w_s, bias_b, name):
    T, D2 = uv.shape
    D = D2 // 2
    G, C, _ = w_s.shape
    gd = D // G
    tt = _fit(T, ROW_TILE, C)

    def body(u_ref, v_ref, g_ref, b_ref, ws_ref, bias_ref, o_ref, vln_ref):
        vln_ref[...] = (_ln_stats(_gelu(v_ref[...].astype(F32))) * g_ref[...] + b_ref[...]).astype(BF16)
        mask = _tril(C)
        for g in range(G):
            wsm = jnp.where(mask, ws_ref[g], 0.0).astype(BF16)
            cs = slice(g * gd, (g + 1) * gd)
            for n in range(tt // C):
                rs = slice(n * C, (n + 1) * C)
                sv = jnp.dot(wsm, vln_ref[rs, cs], preferred_element_type=F32) + bias_ref[g]
                o_ref[rs, cs] = (_gelu(u_ref[rs, cs].astype(F32)) * sv).astype(BF16)

    return _pcall(body, name=name, out_shape=_sds((T, D), BF16), grid=(T // tt,),
                  in_specs=[_row(tt, D, 0), _row(tt, D, 1), _const((1, D)), _const((1, D)), _const((G, C, C)),
                            _const((G, C, gd))],
                  out_specs=_row(tt, D), scratch_shapes=[pltpu.VMEM((tt, D), BF16)],
                  compiler_params=_params("parallel"))(uv, uv, v_g, v_b, w_s, bias_b)


def _b_mid_bwd(uv, dgated, v_g, v_b, w_s, bias_b, name):
    T, D2 = uv.shape
    D = D2 // 2
    G, C, _ = w_s.shape
    gd = D // G
    tt = _fit(T, ROW_TILE, C)

    def body(u_ref, v_ref, d_ref, g_ref, b_ref, ws_ref, bias_ref, o_ref, dws_ref, dsb_ref, dvg_ref, dvb_ref,
             vln_ref, dvln_ref):
        @pl.when(pl.program_id(0) == 0)
        def _():
            dws_ref[...] = jnp.zeros_like(dws_ref)
            dsb_ref[...] = jnp.zeros_like(dsb_ref)
            dvg_ref[...] = jnp.zeros_like(dvg_ref)
            dvb_ref[...] = jnp.zeros_like(dvb_ref)

        vpre = v_ref[...].astype(F32)
        vhat = _ln_stats(_gelu(vpre))
        vln_ref[...] = (vhat * g_ref[...] + b_ref[...]).astype(BF16)
        mask = _tril(C)
        lane = lax.broadcasted_iota(jnp.int32, (C, 128), 1)
        for g in range(G):
            wsm = jnp.where(mask, ws_ref[g], 0.0).astype(BF16)
            cs = slice(g * gd, (g + 1) * gd)
            for n in range(tt // C):
                rs = slice(n * C, (n + 1) * C)
                vt = vln_ref[rs, cs]
                sv = jnp.dot(wsm, vt, preferred_element_type=F32) + bias_ref[g]
                dg = d_ref[rs, cs].astype(F32)
                upre = u_ref[rs, cs].astype(F32)
                o_ref[rs, cs] = (dg * sv * _gelu_grad(upre)).astype(BF16)
                dsv = dg * _gelu(upre)
                dsb_ref[...] += jnp.where(lane == g, jnp.sum(dsv, axis=-1, keepdims=True), 0.0)
                dsv16 = dsv.astype(BF16)
                dws_ref[g] += jnp.where(mask, lax.dot_general(dsv16, vt, _DIMS["nt"], preferred_element_type=F32), 0.0)
                dvln_ref[rs, cs] = lax.dot_general(wsm, dsv16, _DIMS["tn"], preferred_element_type=F32)
        dvln = dvln_ref[...]
        dvg_ref[...] += jnp.sum(dvln * vhat, axis=0, keepdims=True)
        dvb_ref[...] += jnp.sum(dvln, axis=0, keepdims=True)
        dvh = dvln * g_ref[...]
        vc = _gelu(vpre)
        vc = vc - jnp.mean(vc, axis=-1, keepdims=True)
        rstd = lax.rsqrt(jnp.mean(vc * vc, axis=-1, keepdims=True) + NORM_EPS)
        dv = rstd * (dvh - jnp.mean(dvh, axis=-1, keepdims=True) - vhat * jnp.mean(dvh * vhat, axis=-1, keepdims=True))
        o_ref[:, D:] = (dv * _gelu_grad(vpre)).astype(BF16)

    return _pcall(body, name=name,
                  out_shape=(_sds((T, 2 * D), BF16), _sds((G, C, C), F32), _sds((C, 128), F32), _sds((1, D), F32),
                             _sds((1, D), F32)),
                  grid=(T // tt,),
                  in_specs=[_row(tt, D, 0), _row(tt, D, 1), _row(tt, D), _const((1, D)), _const((1, D)),
                            _const((G, C, C)), _const((G, C, gd))],
                  out_specs=(_row(tt, 2 * D), _const((G, C, C)), _const((C, 128)), _const((1, D)), _const((1, D))),
                  scratch_shapes=[pltpu.VMEM((tt, D), BF16), pltpu.VMEM((tt, D), F32)],
                  compiler_params=_params("arbitrary"))(uv, uv, dgated, v_g, v_b, w_s, bias_b)


def _c_mid_fwd(ag, conv_w, conv_b, ln_g, ln_b, name):
    T, D2 = ag.shape
    D = D2 // 2
    width = conv_w.shape[0]
    tt = _fit(T, ROW_TILE, HALO)
    chunks, cw = _lanes(D)

    def body(a_ref, g_ref, ah_ref, gh_ref, w_ref, cb_ref, lg_ref, lb_ref, y2_ref, o_ref, win_ref):
        i = pl.program_id(0)
        for sl in chunks:
            yh = ah_ref[:, sl].astype(F32) * jax.nn.sigmoid(gh_ref[:, sl].astype(F32))
            win_ref[0:HALO, :] = jnp.where(i > 0, yh, 0.0)
            win_ref[HALO:, :] = a_ref[:, sl].astype(F32) * jax.nn.sigmoid(g_ref[:, sl].astype(F32))
            acc = jnp.zeros((tt, cw), F32)
            for k, xs in _causal_taps(win_ref[...], width, tt):
                acc = acc + w_ref[k:k + 1, sl] * xs
            y2_ref[:, sl] = acc + cb_ref[:, sl]
        y3 = _ln_stats(y2_ref[...]) * lg_ref[...] + lb_ref[...]
        o_ref[...] = (y3 * jax.nn.sigmoid(y3)).astype(BF16)

    return _pcall(body, name=name, out_shape=(_sds((T, D), F32), _sds((T, D), BF16)), grid=(T // tt,),
                  in_specs=[_row(tt, D, 0), _row(tt, D, 1), _prev_halo(tt, D, 0), _prev_halo(tt, D, 1),
                            _const((width, D)), _const((1, D)), _const((1, D)), _const((1, D))],
                  out_specs=(_row(tt, D), _row(tt, D)), scratch_shapes=[pltpu.VMEM((HALO + tt, cw), F32)],
                  compiler_params=_params("parallel"))(ag, ag, ag, ag, conv_w, conv_b, ln_g, ln_b)


def _c_mid_bwd(ag, y2, dy4, conv_w, ln_g, ln_b, name):
    T, D2 = ag.shape
    D = D2 // 2
    width = conv_w.shape[0]
    tt = _fit(T, ROW_TILE, HALO)
    chunks, cw = _lanes(D)
    n_tiles = T // tt

    def ln_silu_bwd(y2v, dy4v, lg, lb):
        mu = jnp.mean(y2v, axis=-1, keepdims=True)
        yc = y2v - mu
        rstd = lax.rsqrt(jnp.mean(yc * yc, axis=-1, keepdims=True) + NORM_EPS)
        yh = yc * rstd
        y3 = yh * lg + lb
        sg = jax.nn.sigmoid(y3)
        dy3 = dy4v * (sg * (1.0 + y3 * (1.0 - sg)))
        dyh = dy3 * lg
        dy2 = rstd * (dyh - jnp.mean(dyh, axis=-1, keepdims=True) - yh * jnp.mean(dyh * yh, axis=-1, keepdims=True))
        return dy2, dy3, yh

    def body(a_ref, g_ref, ah_ref, gh_ref, y2_ref, y2n_ref, d_ref, dn_ref, w_ref, lg_ref, lb_ref,
             o_ref, dw_ref, dcb_ref, dlg_ref, dlb_ref, win_ref, dwin_ref):
        i = pl.program_id(0)

        @pl.when(i == 0)
        def _():
            dw_ref[...] = jnp.zeros_like(dw_ref)
            dcb_ref[...] = jnp.zeros_like(dcb_ref)
            dlg_ref[...] = jnp.zeros_like(dlg_ref)
            dlb_ref[...] = jnp.zeros_like(dlb_ref)

        lg = lg_ref[...]
        lb = lb_ref[...]
        dy2, dy3, yh = ln_silu_bwd(y2_ref[...], d_ref[...].astype(F32), lg, lb)
        dlg_ref[...] += jnp.sum(dy3 * yh, axis=0, keepdims=True)
        dlb_ref[...] += jnp.sum(dy3, axis=0, keepdims=True)
        dcb_ref[...] += jnp.sum(dy2, axis=0, keepdims=True)
        dwin_ref[0:tt, :] = dy2
        dy2n, _, _ = ln_silu_bwd(y2n_ref[...], dn_ref[...].astype(F32), lg, lb)
        dwin_ref[tt:, :] = jnp.where(i < n_tiles - 1, dy2n, 0.0)
        for ci, sl in enumerate(chunks):
            a = a_ref[:, sl].astype(F32)
            sg = jax.nn.sigmoid(g_ref[:, sl].astype(F32))
            yh1 = ah_ref[:, sl].astype(F32) * jax.nn.sigmoid(gh_ref[:, sl].astype(F32))
            win_ref[0:HALO, :] = jnp.where(i > 0, yh1, 0.0)
            win_ref[HALO:, :] = a * sg
            d2 = dwin_ref[0:tt, sl]
            for k, xs in _causal_taps(win_ref[...], width, tt):
                dw_ref[k:k + 1, sl] += jnp.sum(d2 * xs, axis=0, keepdims=True)
            d1 = jnp.zeros((tt, cw), F32)
            for k, xs in _anticausal_taps(dwin_ref[:, sl], width, tt):
                d1 = d1 + w_ref[k:k + 1, sl] * xs
            o_ref[:, ci * cw:(ci + 1) * cw] = (d1 * sg).astype(BF16)
            o_ref[:, D + ci * cw:D + (ci + 1) * cw] = (d1 * a * sg * (1.0 - sg)).astype(BF16)

    return _pcall(body, name=name,
                  out_shape=(_sds((T, 2 * D), BF16), _sds((width, D), F32), _sds((1, D), F32), _sds((1, D), F32),
                             _sds((1, D), F32)),
                  grid=(n_tiles,),
                  in_specs=[_row(tt, D, 0), _row(tt, D, 1), _prev_halo(tt, D, 0), _prev_halo(tt, D, 1),
                            _row(tt, D), _next_halo(tt, D, T), _row(tt, D), _next_halo(tt, D, T),
                            _const((width, D)), _const((1, D)), _const((1, D))],
                  out_specs=(_row(tt, 2 * D), _const((width, D)), _const((1, D)), _const((1, D)), _const((1, D))),
                  scratch_shapes=[pltpu.VMEM((HALO + tt, cw), F32), pltpu.VMEM((tt + HALO, D), F32)],
                  compiler_params=_params("arbitrary"))(ag, ag, ag, ag, y2, y2, dy4, dy4, conv_w, ln_g, ln_b)


def _place():
    x, y, c = lax.axis_index("x"), lax.axis_index("y"), lax.axis_index("c")
    return x, y, c


def _slot(px, py, pc):
    return 4 * px + 2 * py + pc


def _all_gather(shards, name):
    n = len(shards)

    def body(*refs):
        ins, outs = refs[:n], refs[n:2 * n]
        send_sems, recv_sems, local_sems = refs[2 * n:]
        x, y, c = _place()
        me, sibling = (x, y, c), (x, y, 1 - c)
        chips = [(1 - x, y), (x, 1 - y), (1 - x, 1 - y)]

        def copy(t, k, block, to, src=None):
            dst = outs[t].at[_slot(*block)]
            return pltpu.make_async_remote_copy(
                src_ref=dst if src is None else src, dst_ref=dst, send_sem=send_sems.at[t, k],
                recv_sem=recv_sems.at[t, k], device_id=to, device_id_type=MESH)

        mine = [pltpu.make_async_copy(ins[t], outs[t].at[_slot(*me)], local_sems.at[t]) for t in range(n)]
        for cp in mine:
            cp.start()
        first = []
        for j, chip in enumerate(chips):
            first += [copy(t, 1 + j, me, (*chip, c), src=ins[t]) for t in range(n)]
        first += [copy(t, 0, me, sibling, src=ins[t]) for t in range(n)]
        for cp in first:
            cp.start()
        passed = []
        for j, chip in enumerate(chips):
            for t in range(n):
                copy(t, 1 + j, (*chip, c), me).wait_recv()
                cp = copy(t, 4 + j, (*chip, c), sibling)
                cp.start()
                passed.append(cp)
        for t in range(n):
            copy(t, 0, sibling, me).wait_recv()
            for j, chip in enumerate(chips):
                copy(t, 4 + j, (*chip, 1 - c), me).wait_recv()
        for cp in first + passed:
            cp.wait_send()
        for cp in mine:
            cp.wait()

    outs = _pcall(
        body, name=name, out_shape=tuple(_sds((N_DEV,) + s.shape, s.dtype) for s in shards),
        in_specs=[ANY] * n, out_specs=(ANY,) * n,
        scratch_shapes=[pltpu.SemaphoreType.DMA((n, 7)), pltpu.SemaphoreType.DMA((n, 7)), pltpu.SemaphoreType.DMA((n,))],
    )(*shards)
    return list(outs)


_HBM = pl.BlockSpec(memory_space=pltpu.HBM)
_SEM = pl.BlockSpec(memory_space=pltpu.SEMAPHORE)
_DATAFLOW = pltpu.SideEffectType.DATAFLOW_SIDE_EFFECTING


def _peers(x, y, c):
    out = []
    for j in range(1, N_DEV):
        fx, fy, fc = (j >> 2) & 1, (j >> 1) & 1, j & 1
        out.append((1 - x if fx else x, 1 - y if fy else y, 1 - c if fc else c))
    return out


def _exchange_copies(ins, zones, mode, sems):
    send_sem, recv_sem, local_sem = sems
    x, y, c = _place()
    me = _slot(x, y, c)
    sibling = (x, y, 1 - c)
    chips = [(1 - x, y), (x, 1 - y), (1 - x, 1 - y)]
    local, remote = [], []

    def add(src, dst, to, landed):
        remote.append((pltpu.make_async_remote_copy(src_ref=src, dst_ref=dst, send_sem=send_sem, recv_sem=recv_sem,
                                                    device_id=to, device_id_type=MESH), landed))

    for t, zone in enumerate(zones):
        if mode == "scatter":
            local.append(pltpu.make_async_copy(ins[t].at[me], zone.at[me], local_sem))
            for peer in _peers(x, y, c):
                add(ins[t].at[_slot(*peer)], zone.at[me], peer, zone.at[_slot(*peer)])
        elif mode == "gather_chips":
            local.append(pltpu.make_async_copy(ins[t], zone.at[me], local_sem))
            for peer in [(*chip, c) for chip in chips] + [sibling]:
                add(ins[t], zone.at[me], peer, zone.at[_slot(*peer)])
        else:
            for chip in chips:
                block = zone.at[_slot(*chip, c)]
                add(block, block, sibling, zone.at[_slot(*chip, 1 - c)])
    return local, remote


def _exchange_start(srcs, lands, mode, after, name):
    if lands is None:
        lands = [lax.empty(s.shape if mode == "scatter" else (N_DEV,) + s.shape, s.dtype) for s in srcs]
    ns, na = len(srcs), len(srcs) + len(lands)

    def body(*refs):
        local, remote = _exchange_copies(refs[:ns], refs[ns:na], mode, refs[na + 1:na + 4])
        for cp in local:
            cp.start()
        for cp, _ in remote:
            cp.start()
        refs[-1][...] = jnp.zeros_like(refs[-1])

    hbm = lambda a: pltpu.with_memory_space_constraint(a, pltpu.HBM)
    arrays = list(srcs) + list(lands)
    outs = _pcall(
        body, name=name,
        out_shape=(pltpu.SemaphoreType.DMA(()),) * 3
        + tuple(pltpu.HBM(a.shape, a.dtype) for a in arrays) + (_sds((8, 128), F32),),
        in_specs=[_HBM] * na + [ANY],
        out_specs=(_SEM,) * 3 + (_HBM,) * na + (pl.BlockSpec(memory_space=pltpu.VMEM),),
        input_output_aliases={t: 3 + t for t in range(na)},
        compiler_params=pltpu.CompilerParams(has_side_effects=_DATAFLOW),
    )(*[hbm(a) for a in arrays], after)
    return outs[:3], list(outs[3:3 + ns]), list(outs[3 + ns:3 + na]), outs[-1]


def _exchange_wait(sems, srcs, lands, mode, after, name):
    ns, na = len(srcs), len(srcs) + len(lands)
    afters = list(after) if isinstance(after, (list, tuple)) else [after]

    def body(*refs):
        local, remote = _exchange_copies(refs[:ns], refs[ns:na], mode, refs[na:na + 3])
        for cp in local:
            cp.wait()
        for cp, landed in remote:
            cp.wait_send()
            pltpu.make_async_remote_copy(
                src_ref=landed, dst_ref=landed, send_sem=refs[na], recv_sem=refs[na + 1],
                device_id=_place(), device_id_type=MESH).wait_recv()

    outs = _pcall(
        body, name=name, out_shape=tuple(pltpu.HBM(a.shape, a.dtype) for a in list(srcs) + list(lands)),
        in_specs=[_HBM] * na + [_SEM] * 3 + [ANY] * len(afters), out_specs=(_HBM,) * na,
        input_output_aliases={t: t for t in range(na)},
        compiler_params=pltpu.CompilerParams(has_side_effects=_DATAFLOW),
    )(*srcs, *lands, *sems, *afters)
    return list(outs[ns:])


def _reduce_adam(recvs, w, m, v, name, dep=None):
    L, r, c = w.shape
    tr = _fit(r, max(16, (128 * 1024) // c), 16)
    ni = r // tr

    def recv_spec(l0):
        def index(l, i):
            return 0, jnp.where(l == l0, i, jnp.where(l < l0, 0, ni - 1)), 0
        return pl.BlockSpec((N_DEV, tr, c), index)

    lay = pl.BlockSpec((None, tr, c), lambda l, i: (l, i, 0))

    n_dep = 0 if dep is None else 1

    def body(*refs):
        recv_refs = refs[:L]
        w_ref, m_ref, v_ref = refs[L:L + 3]
        g_out, d_out, m_out, v_out = refs[L + 3 + n_dep:]
        l = pl.program_id(0)
        for l0 in range(L):
            @pl.when(l == l0)
            def _(l0=l0):
                g = recv_refs[l0][0].astype(F32)
                for s in range(1, N_DEV):
                    g = g + recv_refs[l0][s].astype(F32)
                mn = ADAM_B1 * m_ref[...] + (1.0 - ADAM_B1) * g
                vn = ADAM_B2 * v_ref[...] + (1.0 - ADAM_B2) * (g * g)
                m_hat = mn / (1.0 - ADAM_B1 ** ADAM_STEP)
                v_hat = vn / (1.0 - ADAM_B2 ** ADAM_STEP)
                g_out[...] = g
                d_out[...] = -ADAM_LR * (m_hat / (jnp.sqrt(v_hat) + ADAM_EPS) + ADAM_WD * w_ref[...])
                m_out[...] = mn
                v_out[...] = vn

    return _pcall(body, name=name, out_shape=(_sds((L, r, c), F32),) * 4, grid=(L, ni),
                  in_specs=[recv_spec(l0) for l0 in range(L)] + [lay, lay, lay] + [ANY] * n_dep, out_specs=(lay,) * 4,
                  compiler_params=_params("arbitrary", "arbitrary"))(*recvs, w, m, v, *([dep] if n_dep else []))


_SMALL_SHARDED = ["mix_norm", "xa_norm", "ffn_norm", "a_conv_w", "c_conv_w", "c_conv_b", "c_ln_g", "c_ln_b"]
_SMALL_REPLICATED = ["b_v_g", "b_v_b", "b_w_s", "b_s_bias"]
_BIG = ["xa_wq", "xa_wkv", "xa_wo", "ffn_w_gu", "ffn_w_down", "a_w_in", "a_w_out", "b_w_in", "b_w_out", "c_w_in",
        "c_w_out"]
_COL_SHARDED = {"xa_wkv", "ffn_w_gu", "a_w_in", "b_w_in", "c_w_in"}
_WEIGHTS = ["mix_norm", "xa_norm", "xa_wq", "xa_wkv", "xa_wo", "ffn_norm", "ffn_w_gu", "ffn_w_down", "a_w_in",
            "a_conv_w", "a_w_out", "b_w_in", "b_v_g", "b_v_b", "b_w_s", "b_s_bias", "b_w_out", "c_w_in", "c_conv_w",
            "c_conv_b", "c_ln_g", "c_ln_b", "c_w_out"]
_MIXER = "abc"


def _size(shape):
    size = 1
    for s in shape:
        size *= s
    return size


def _row_layout(shapes, width):
    offs, r = [], 0
    for shape in shapes:
        offs.append(r)
        r += -(-(-(-_size(shape) // width)) // 8) * 8
    return offs, r


def _pack_rows(arrays, width, fill):
    offs, total = _row_layout([a.shape for a in arrays], width)
    ends = offs[1:] + [total]
    rows = [jnp.pad(a.reshape(-1), (0, (e - o) * width - a.size), constant_values=fill).reshape(e - o, width)
            for a, o, e in zip(arrays, offs, ends)]
    return jnp.concatenate(rows, axis=0)


def _unpack_rows(packed, like):
    width = packed.shape[-1]
    offs, _ = _row_layout(like, width)
    return [packed[o:o + -(-_size(s) // width)].reshape(-1)[:_size(s)].reshape(s) for o, s in zip(offs, like)]


def _assemble_rows(pieces, rows, width, name):
    n = len(pieces)

    def body(*refs):
        o_ref = refs[n]
        o_ref[...] = jnp.zeros_like(o_ref)
        for r, (a, off) in zip(refs[:n], pieces):
            o_ref[off:off + a.shape[0], :] = r[...]

    return _pcall(body, name=name, out_shape=_sds((rows, width), F32),
                  compiler_params=pltpu.CompilerParams(vmem_limit_bytes=V7X_VMEM_LIMIT))(*[a for a, _ in pieces])


def kernel(x, mem, mix_norm, xa_norm, xa_wq, xa_wkv, xa_wo, ffn_norm, ffn_w_gu, ffn_w_down, a_w_in, a_conv_w, a_w_out, b_w_in, b_v_g, b_v_b, b_w_s, b_s_bias, b_w_out, c_w_in, c_conv_w, c_conv_b, c_ln_g, c_ln_b, c_w_out, loss_target, m_mix_norm, m_xa_norm, m_xa_wq, m_xa_wkv, m_xa_wo, m_ffn_norm, m_ffn_w_gu, m_ffn_w_down, m_a_w_in, m_a_conv_w, m_a_w_out, m_b_w_in, m_b_v_g, m_b_v_b, m_b_w_s, m_b_s_bias, m_b_w_out, m_c_w_in, m_c_conv_w, m_c_conv_b, m_c_ln_g, m_c_ln_b, m_c_w_out, v_mix_norm, v_xa_norm, v_xa_wq, v_xa_wkv, v_xa_wo, v_ffn_norm, v_ffn_w_gu, v_ffn_w_down, v_a_w_in, v_a_conv_w, v_a_w_out, v_b_w_in, v_b_v_g, v_b_v_b, v_b_w_s, v_b_s_bias, v_b_w_out, v_c_w_in, v_c_conv_w, v_c_conv_b, v_c_ln_g, v_c_ln_b, v_c_w_out):
    P = dict(locals())
    T, D = x.shape[1], x.shape[2]
    dl = D // N_DEV
    depth = mix_norm.shape[0]
    x0, mem0, target = x[0], mem[0], loss_target[0]
    my_slot = _slot(*_place())

    sh_shapes = [P[n].shape for n in _SMALL_SHARDED]
    packed = _pack_rows([P[n] for n in _SMALL_SHARDED], dl, 0.0)
    n_sh = packed.shape[0]
    gathered = _all_gather([packed], "ag_small")[0]
    full_rows = jnp.transpose(gathered, (1, 0, 2)).reshape(n_sh, D)
    small = dict(zip(_SMALL_SHARDED, _unpack_rows(full_rows, [s[:-1] + (D,) for s in sh_shapes])))
    G, C = b_w_s.shape[1], b_w_s.shape[2]
    gd = D // G
    bias_b = jnp.broadcast_to(b_s_bias[0][:, :, None], (G, C, gd))
    zero_row = jnp.zeros((1, D), F32)

    w16 = {n: P[n].astype(BF16) for n in _BIG}

    groups = [(i, part) for i in range(depth) for part in range(3)]

    def group_names(i, part):
        mx, slot = _MIXER[i % N_MIXERS], i // N_MIXERS
        if part == 0:
            return [(mx + "_w_in", slot), (mx + "_w_out", slot)]
        if part == 1:
            return [("xa_wq", i), ("xa_wkv", i), ("xa_wo", i)]
        return [("ffn_w_gu", i), ("ffn_w_down", i)]

    no_token = jnp.zeros((8, 128), F32)
    fwd = {"g": 0, "last": full_rows, "token": no_token, "stage1": {}, "stage2": {}}

    def tag(g):
        return "%d_%d" % groups[g]

    def start_stage1(g):
        if g < len(groups):
            names = group_names(*groups[g])
            sems, srcs, lands, token = _exchange_start([w16[n][j] for n, j in names], None, "gather_chips",
                                                       fwd["last"], "ag_start_" + tag(g))
            fwd["stage1"][g] = (sems, srcs, lands)
            fwd["last"] = fwd["token"] = token

    def start_stage2(g, after):
        if g < len(groups):
            sems, srcs, lands = fwd["stage1"].pop(g)
            lands = _exchange_wait(sems, srcs, lands, "gather_chips", after, "ag_wait_" + tag(g))
            sems, _, lands, token = _exchange_start([], lands, "gather_sibling", after, "ag_pass_" + tag(g))
            fwd["stage2"][g] = (sems, lands)
            fwd["last"] = fwd["token"] = token

    def begin_group():
        g, y = fwd["g"], fwd["last"]
        sems, lands = fwd["stage2"].pop(g)
        fulls = _exchange_wait(sems, [], lands, "gather_sibling", y, "ag_done_" + tag(g))
        fwd["last"] = fulls[0]
        start_stage1(g + 3)
        out = {}
        for (n, _), f in zip(group_names(*groups[g]), fulls):
            key = n[2:] if n[1] == "_" and n[0] in _MIXER else n
            out[key] = f if n in _COL_SHARDED else f.reshape(-1, f.shape[-1])
        fwd["g"] += 1
        return out, fwd["token"]

    def mid_group(y):
        start_stage2(fwd["g"], y)
        return fwd["token"]

    def end_group(y):
        fwd["last"] = y

    for g0 in range(3):
        start_stage1(g0)
    start_stage2(0, fwd["last"])

    saved = []
    xin = x0
    h = _rms_fwd(x0, small["mix_norm"][0, 0][None], "rms_first")
    for i in range(depth):
        kind, slot = i % N_MIXERS, i // N_MIXERS
        W, dep = begin_group()
        S = {"W": W, "x0": xin, "h0": h}
        pre = _mm(h, W["w_in"], "nn", BF16, "mm_in_%s" % _MIXER[kind], b_blocked=True, dep=dep)
        S["pre"] = pre
        dep = mid_group(pre)
        if kind == 0:
            mid = _a_mid_fwd(pre, small["a_conv_w"][slot], "a_mid_fwd")
        elif kind == 1:
            mid = _b_mid_fwd(pre, b_v_g, b_v_b, b_w_s[0], bias_b, "b_mid_fwd")
        else:
            y2c, mid = _c_mid_fwd(pre, small["c_conv_w"][slot], small["c_conv_b"], small["c_ln_g"], small["c_ln_b"],
                                  "c_mid_fwd")
            S["y2c"] = y2c
        S["mid"] = mid
        S["y0"] = _mm(mid, W["w_out"], "nn", BF16, "mm_out", dep=dep)
        end_group(S["y0"])
        Wx, dep = begin_group()
        W.update(Wx)
        xin, h = _post_pre_fwd(xin, S["y0"], small["mix_norm"][i, 1][None], small["xa_norm"][i, 0][None], "post_pre")
        S["x1"], S["h1"] = xin, h
        S["q"] = _mm(h, W["xa_wq"], "nn", BF16, "mm_q", dep=dep)
        dep = mid_group(S["q"])
        S["memn"] = _rms_fwd(mem0, small["xa_norm"][i, 2][None], "rms_mem")
        S["kv"] = _mm(S["memn"], W["xa_wkv"], "nn", BF16, "mm_kv", b_blocked=True)
        S["o"] = _attn_fwd(S["q"], S["kv"], "attn_fwd")
        S["y1"] = _mm(S["o"], W["xa_wo"], "nn", BF16, "mm_out", dep=dep)
        end_group(S["y1"])
        Wf, dep = begin_group()
        W.update(Wf)
        xin, h = _post_pre_fwd(xin, S["y1"], small["xa_norm"][i, 1][None], small["ffn_norm"][i, 0][None], "post_pre")
        S["x2"], S["h2"] = xin, h
        S["gate"], S["up"], S["act"] = _ffn_gu_fwd(h, W["ffn_w_gu"], "ffn_gu_fwd", dep=dep)
        S["y2"] = _mm(S["act"], W["ffn_w_down"], "nn", BF16, "mm_down", dep=mid_group(S["act"]))
        end_group(S["y2"])
        if i + 1 < depth:
            xin, h = _post_pre_fwd(xin, S["y2"], small["ffn_norm"][i, 1][None], small["mix_norm"][i + 1, 0][None],
                                   "post_pre")
        saved.append(S)

    last = saved[-1]
    loss_part, dx, dy, dg = _final_fwd_loss(xin, last["y2"], small["ffn_norm"][depth - 1, 1][None], target, "final_loss")
    loss = lax.psum(loss_part[0, 0], ("x", "y", "c"))

    g_mix = [[zero_row, zero_row] for _ in range(depth)]
    g_xa = [[zero_row, zero_row, zero_row] for _ in range(depth)]
    g_ffn = [[zero_row, zero_row] for _ in range(depth)]
    g_small = {}
    recv = {n: [None] * P[n].shape[0] for n in _BIG}
    g_ffn[depth - 1][1] = dg

    bwd = {"pend": None}

    def finish_scatter(after):
        if bwd["pend"] is not None:
            names, tag, sems, srcs, lands, _ = bwd["pend"]
            for (n, j, _), r in zip(names, _exchange_wait(sems, srcs, lands, "scatter", after, "rs_wait_" + tag)):
                recv[n][j] = r
            bwd["pend"] = None

    def scatter_group(names, tag, after):
        finish_scatter(after)
        parts = [g if n in _COL_SHARDED else g.reshape(N_DEV, -1, g.shape[-1]) for n, _, g in names]
        bwd["pend"] = (names, tag) + _exchange_start(parts, None, "scatter", after, "rs_start_" + tag)

    def scatter_token():
        return no_token if bwd["pend"] is None else bwd["pend"][5]

    for i in reversed(range(depth)):
        kind, slot = i % N_MIXERS, i // N_MIXERS
        mx = _MIXER[kind]
        S = saved[i]
        W = S["W"]
        dgu = _ffn_dgu_bwd(dy, W["ffn_w_down"], S["gate"], S["up"], "ffn_dgu_bwd", dep=scatter_token())
        dw_down = _mm(S["act"], dy, "tn", BF16, "mm_dw_down")
        dh = _mm(dgu, W["ffn_w_gu"], "nt", BF16, "mm_dh_gu", a_blocked=True, b_blocked=True)
        dw_gu = _mm(S["h2"], dgu, "tn", BF16, "mm_dw_gu", b_blocked=True, out_blocks=N_DEV)
        dx, g_ffn[i][0], dy, g_xa[i][1] = _pre_post_bwd(dx, dh, S["x2"], small["ffn_norm"][i, 0][None], S["y1"],
                                                         small["xa_norm"][i, 1][None], "pre_post_bwd")
        scatter_group([("ffn_w_gu", i, dw_gu), ("ffn_w_down", i, dw_down)], "%d_2" % i, dx)
        do = _mm(dy, W["xa_wo"], "nt", BF16, "mm_nt_dd16", dep=scatter_token())
        dw_o = _mm(S["o"], dy, "tn", BF16, "mm_dw_dd")
        dq, dkv = _attn_bwd(S["q"], S["kv"], do, "attn_bwd")
        dkv16 = dkv.astype(BF16)
        dh = _mm(dq, W["xa_wq"], "nt", BF16, "mm_nt_dd16")
        dw_q = _mm(S["h1"], dq, "tn", BF16, "mm_dw_dd")
        dw_kv = _mm(S["memn"], dkv16, "tn", BF16, "mm_dw_kv", out_blocks=N_DEV)
        dmemn = _mm(dkv16, W["xa_wkv"], "nt", F32, "mm_dmem", b_blocked=True)
        g_xa[i][2] = _rms_gain_grad(dmemn, mem0, "rms_gain_grad")
        dx, g_xa[i][0], dy, g_mix[i][1] = _pre_post_bwd(dx, dh, S["x1"], small["xa_norm"][i, 0][None], S["y0"],
                                                         small["mix_norm"][i, 1][None], "pre_post_bwd")
        scatter_group([("xa_wq", i, dw_q), ("xa_wkv", i, dw_kv), ("xa_wo", i, dw_o)], "%d_1" % i, dx)
        dmid = _mm(dy, W["w_out"], "nt", BF16, "mm_nt_dd16", dep=scatter_token())
        dw_out = _mm(S["mid"], dy, "tn", BF16, "mm_dw_dd")
        if kind == 0:
            dpre, dcw = _a_mid_bwd(S["pre"], dmid, small["a_conv_w"][slot], "a_mid_bwd")
            g_small.setdefault("a_conv_w", {})[slot] = dcw
        elif kind == 1:
            dpre, dws, dsb, dvg, dvb = _b_mid_bwd(S["pre"], dmid, b_v_g, b_v_b, b_w_s[0], bias_b, "b_mid_bwd")
            dsb_row = jnp.pad(jnp.transpose(dsb[:, :G]).reshape(1, G * C), ((0, 0), (0, (-G * C) % D)))
            g_small.update(b_w_s=dws.reshape(-1, D), b_s_bias=dsb_row.reshape(-1, D), b_v_g=dvg, b_v_b=dvb)
        else:
            dpre, dcw, dcb, dlg, dlb = _c_mid_bwd(S["pre"], S["y2c"], dmid, small["c_conv_w"][slot], small["c_ln_g"],
                                                  small["c_ln_b"], "c_mid_bwd")
            g_small.update(c_conv_w=dcw, c_conv_b=dcb, c_ln_g=dlg, c_ln_b=dlb)
        dh = _mm(dpre, W["w_in"], "nt", BF16, "mm_dh_in_%s" % mx, b_blocked=True)
        dw_in = _mm(S["h0"], dpre, "tn", BF16, "mm_dw_in_%s" % mx, out_blocks=N_DEV)
        if i > 0:
            dx, g_mix[i][0], dy, g_ffn[i - 1][1] = _pre_post_bwd(
                dx, dh, S["x0"], small["mix_norm"][i, 0][None], saved[i - 1]["y2"],
                small["ffn_norm"][i - 1, 1][None], "pre_post_bwd")
        else:
            dx, g_mix[i][0] = _pre_post_bwd(dx, dh, S["x0"], small["mix_norm"][i, 0][None], None, None, "pre_bwd")
        scatter_group([(mx + "_w_in", slot, dw_in), (mx + "_w_out", slot, dw_out)], "%d_0" % i, dx)
        S.clear()
    grad_x = dx[None]

    sh_off = dict(zip(_SMALL_SHARDED, _row_layout(sh_shapes, dl)[0]))
    rep_offs, n_rep = _row_layout([P[n].shape for n in _SMALL_REPLICATED], D)
    rep_off = {n: n_sh + o for n, o in zip(_SMALL_REPLICATED, rep_offs)}
    pieces = []
    for i in range(depth):
        pieces += [(g, sh_off["mix_norm"] + 2 * i + j) for j, g in enumerate(g_mix[i])]
        pieces += [(g, sh_off["xa_norm"] + 3 * i + j) for j, g in enumerate(g_xa[i])]
        pieces += [(g, sh_off["ffn_norm"] + 2 * i + j) for j, g in enumerate(g_ffn[i])]
    pieces += [(g, sh_off["a_conv_w"] + a_conv_w.shape[1] * s) for s, g in g_small["a_conv_w"].items()]
    pieces += [(g_small[n], sh_off[n]) for n in ("c_conv_w", "c_conv_b", "c_ln_g", "c_ln_b")]
    pieces += [(g_small[n], rep_off[n]) for n in _SMALL_REPLICATED]
    part_all = _assemble_rows(pieces, n_sh + n_rep, D, "pack_small_grads")
    parts_all = _all_gather([part_all], "ag_small_grads")[0]
    recv_sh = lax.dynamic_slice_in_dim(parts_all[:, :n_sh], my_slot * dl, dl, axis=2)
    recv_rep = parts_all[:, n_sh:]

    out = {}

    def adam_small(names, recv_s, width, name):
        shapes = [P[n].shape for n in names]
        pw = _pack_rows([P[n] for n in names], width, 0.0)
        pm = _pack_rows([P["m_" + n] for n in names], width, 0.0)
        pv = _pack_rows([P["v_" + n] for n in names], width, 1.0)
        res = _reduce_adam([recv_s], pw[None], pm[None], pv[None], name)
        for kind, r in zip(("grad", "delta", "new_m", "new_v"), res):
            for n, a in zip(names, _unpack_rows(r[0], shapes)):
                out[kind + "_" + n] = a

    adam_small(_SMALL_SHARDED, recv_sh, dl, "adam_small_sharded")
    adam_small(_SMALL_REPLICATED, recv_rep, D, "adam_small_replicated")
    late = [n for n in _BIG if any(r is None for r in recv[n])]
    early_done = [g_xa[i][2] for i in range(depth)]
    for n in [n for n in _BIG if n not in late] + late:
        if n == late[0]:
            finish_scatter(early_done)
        res = _reduce_adam(recv[n], P[n], P["m_" + n], P["v_" + n], "adam_" + n, dep=scatter_token())
        early_done.append(res[0])
        for kind, r in zip(("grad", "delta", "new_m", "new_v"), res):
            out[kind + "_" + n] = r

    return (loss, grad_x, *[out[k + "_" + n] for k in ("grad", "delta", "new_m", "new_v") for n in _WEIGHTS])
```

```python
import functools

import jax
import jax.numpy as jnp
from jax import lax
from jax.experimental import pallas as pl
from jax.experimental.pallas import tpu as pltpu

F32 = jnp.float32
BF16 = jnp.bfloat16
MESH = pl.DeviceIdType.MESH
ANY = pl.BlockSpec(memory_space=pl.ANY)

N_DEV = 8
N_MIXERS = 3
XA_HEADS = 4
GMLP_GROUPS = 8
CHUNK = 128
NORM_EPS = 1e-6
HALO = 32
ROW_TILE = 256
CONV_LANES = 512
V7X_VMEM_LIMIT = 56 * 1024 * 1024

ADAM_LR = 0.001
ADAM_B1 = 0.9
ADAM_B2 = 0.999
ADAM_EPS = 1e-08
ADAM_WD = 0.01
ADAM_STEP = 10


def _pcall(body, **kw):
    return pl.pallas_call(body, **kw)


def _params(*sem):
    return pltpu.CompilerParams(dimension_semantics=sem, vmem_limit_bytes=V7X_VMEM_LIMIT)


def _fit(n, pref, mult=128):
    if n <= pref:
        return n
    t = (pref // mult) * mult
    while t >= mult:
        if n % t == 0:
            return t
        t -= mult
    return n


def _sds(shape, dtype):
    return jax.ShapeDtypeStruct(shape, dtype)


_DIMS = {"nn": (((1,), (0,)), ((), ())), "nt": (((1,), (1,)), ((), ())), "tn": (((0,), (0,)), ((), ()))}
MM_VMEM_BUDGET = 44 * 1024 * 1024


def _gcd(a, b):
    while b:
        a, b = b, a % b
    return a


def _mm(a, b, mode, out_dtype, name, *, a_blocked=False, b_blocked=False, out_blocks=None, dep=None):
    if mode == "tn":
        K, M = a.shape
    elif a_blocked:
        sa, M, ca = a.shape
        K = sa * ca
    else:
        M, K = a.shape
    n_unit = k_unit = None
    if b_blocked:
        _, d1, cb = b.shape
        if mode == "nt":
            N, k_unit = d1, cb
        else:
            N, n_unit = b.shape[0] * cb, cb
    else:
        N = b.shape[0] if mode == "nt" else b.shape[1]
    n_unit = n_unit or N
    k_unit = k_unit or K
    if a_blocked:
        k_unit = _gcd(k_unit, ca)
    if out_blocks:
        n_unit = _gcd(n_unit, N // out_blocks)
    tn = _fit(n_unit, 1536)
    tk = k_unit
    out_bytes = jnp.dtype(out_dtype).itemsize

    def need(tm_, gk_=1):
        nk_ = K // (tk * gk_)
        return (4 * gk_ * (tm_ * tk + tk * tn) + 2 * tm_ * tn * out_bytes
                + 4 * tm_ * tn * ((2 if nk_ > 1 else 1) + (1 if gk_ > 1 else 0)))

    tm = _fit(M, 1024)
    gk = 1
    if b_blocked and mode == "nt" and tk == cb:
        for cand in (8, 4, 2):
            if b.shape[0] % cand == 0 and (not a_blocked or (ca // tk) % cand == 0) and need(tm, cand) <= MM_VMEM_BUDGET:
                gk = cand
                break
    while need(tm, gk) > MM_VMEM_BUDGET and tm % 256 == 0:
        tm //= 2
    nk = K // (tk * gk)

    if mode == "tn":
        a_spec = pl.BlockSpec((tk, tm), lambda i, j, k: (k, i))
    elif a_blocked:
        ka = ca // (tk * gk)
        a_spec = pl.BlockSpec((None, tm, tk * gk), lambda i, j, k: (k // ka, i, k % ka))
    else:
        a_spec = pl.BlockSpec((tm, tk * gk), lambda i, j, k: (i, k))
    if b_blocked and mode == "nt" and gk > 1:
        b_spec = pl.BlockSpec((gk, tn, tk), lambda i, j, k: (k, j, 0))
    elif b_blocked and mode == "nt":
        kb = cb // tk
        b_spec = pl.BlockSpec((None, tn, tk), lambda i, j, k: (k // kb, j, k % kb))
    elif b_blocked:
        nb = cb // tn
        b_spec = pl.BlockSpec((None, tk, tn), lambda i, j, k: (j // nb, k, j % nb))
    elif mode == "nt":
        b_spec = pl.BlockSpec((tn, tk), lambda i, j, k: (j, k))
    else:
        b_spec = pl.BlockSpec((tk, tn), lambda i, j, k: (k, j))
    if out_blocks:
        ob = (N // out_blocks) // tn
        out_shape = _sds((out_blocks, M, N // out_blocks), out_dtype)
        o_spec = pl.BlockSpec((None, tm, tn), lambda i, j, k: (j // ob, i, j % ob))
    else:
        out_shape = _sds((M, N), out_dtype)
        o_spec = pl.BlockSpec((tm, tn), lambda i, j, k: (i, j))
    dims = _DIMS[mode]
    n_in = 2 if dep is None else 3

    def body(*refs):
        a_ref, b_ref = refs[0], refs[1]
        o_ref = refs[n_in]
        if gk == 1:
            p = lax.dot_general(a_ref[...], b_ref[...], dims, preferred_element_type=F32)
        else:
            p = lax.dot_general(a_ref[:, 0:tk], b_ref[0], dims, preferred_element_type=F32)
            for s in range(1, gk):
                p = p + lax.dot_general(a_ref[:, s * tk:(s + 1) * tk], b_ref[s], dims, preferred_element_type=F32)
        if nk == 1:
            o_ref[...] = p.astype(o_ref.dtype)
            return
        acc_ref = refs[n_in + 1]
        k = pl.program_id(2)

        @pl.when(k == 0)
        def _():
            acc_ref[...] = p

        @pl.when(k > 0)
        def _():
            acc_ref[...] += p

        @pl.when(k == nk - 1)
        def _():
            o_ref[...] = acc_ref[...].astype(o_ref.dtype)

    ins, in_specs = [a, b], [a_spec, b_spec]
    if dep is not None:
        ins.append(dep)
        in_specs.append(ANY)
    return _pcall(
        body, name=name, out_shape=out_shape, grid=(M // tm, N // tn, nk),
        in_specs=in_specs, out_specs=o_spec,
        scratch_shapes=[pltpu.VMEM((tm, tn), F32)] if nk > 1 else [],
        compiler_params=_params("parallel", "parallel", "arbitrary"),
    )(*ins)


def _rstd(v):
    return lax.rsqrt(jnp.mean(v * v, axis=-1, keepdims=True) + NORM_EPS)


def _rms_bwd_rows(v, g, dout):
    r = _rstd(v)
    vh = v * r
    dvh = dout * g
    dv = r * (dvh - vh * jnp.mean(dvh * vh, axis=-1, keepdims=True))
    return dv, jnp.sum(dout * vh, axis=0, keepdims=True)


def _row(tt, d, col=0):
    return pl.BlockSpec((tt, d), lambda i: (i, col))


def _const(shape):
    return pl.BlockSpec(shape, lambda i: (0,) * len(shape))


def _prev_halo(tt, d, col=0):
    return pl.BlockSpec((HALO, d), lambda i: (jnp.maximum(i * (tt // HALO) - 1, 0), col))


def _next_halo(tt, d, rows, col=0):
    last = rows // HALO - 1
    return pl.BlockSpec((HALO, d), lambda i: (jnp.minimum((i + 1) * (tt // HALO), last), col))


def _rms_fwd(x, g, name):
    T, D = x.shape
    tt = _fit(T, ROW_TILE, 8)

    def body(x_ref, g_ref, h_ref):
        v = x_ref[...]
        h_ref[...] = (v * _rstd(v) * g_ref[...]).astype(BF16)

    return _pcall(body, name=name, out_shape=_sds((T, D), BF16), grid=(T // tt,),
                  in_specs=[_row(tt, D), _const((1, D))], out_specs=_row(tt, D),
                  compiler_params=_params("parallel"))(x, g)


def _post_pre_fwd(x, y, g_post, g_pre, name):
    T, D = x.shape
    tt = _fit(T, ROW_TILE, 8)

    def body(x_ref, y_ref, gp_ref, gn_ref, xo_ref, h_ref):
        y = y_ref[...].astype(F32)
        xn = x_ref[...] + y * _rstd(y) * gp_ref[...]
        xo_ref[...] = xn
        h_ref[...] = (xn * _rstd(xn) * gn_ref[...]).astype(BF16)

    return _pcall(body, name=name, out_shape=(_sds((T, D), F32), _sds((T, D), BF16)), grid=(T // tt,),
                  in_specs=[_row(tt, D), _row(tt, D), _const((1, D)), _const((1, D))],
                  out_specs=(_row(tt, D), _row(tt, D)),
                  compiler_params=_params("parallel"))(x, y, g_post, g_pre)


def _final_fwd_loss(x, y, g_post, target, name):
    T, D = x.shape
    tt = _fit(T, ROW_TILE, 8)

    def body(x_ref, y_ref, g_ref, t_ref, loss_ref, dx_ref, dy_ref, dg_ref):
        i = pl.program_id(0)
        y = y_ref[...].astype(F32)
        g = g_ref[...]
        err = x_ref[...] + y * _rstd(y) * g - t_ref[...]
        part = 0.5 * jnp.sum(jnp.mean(err * err, axis=-1, keepdims=True))
        dx = err / D
        dx_ref[...] = dx
        dy, dg = _rms_bwd_rows(y, g, dx)
        dy_ref[...] = dy.astype(BF16)

        @pl.when(i == 0)
        def _():
            loss_ref[...] = jnp.zeros_like(loss_ref)
            dg_ref[...] = jnp.zeros_like(dg_ref)

        loss_ref[...] += part
        dg_ref[...] += dg

    return _pcall(body, name=name,
                  out_shape=(_sds((1, 128), F32), _sds((T, D), F32), _sds((T, D), BF16), _sds((1, D), F32)),
                  grid=(T // tt,),
                  in_specs=[_row(tt, D), _row(tt, D), _const((1, D)), _row(tt, D)],
                  out_specs=(_const((1, 128)), _row(tt, D), _row(tt, D), _const((1, D))),
                  compiler_params=_params("arbitrary"))(x, y, g_post, target)


def _pre_post_bwd(dx_out, dh, x_in, g_pre, y_prev, g_post_prev, name):
    T, D = x_in.shape
    tt = _fit(T, ROW_TILE, 8)
    with_prev = y_prev is not None

    def body(*refs):
        if with_prev:
            dxo_ref, dh_ref, x_ref, g_ref, y_ref, gp_ref, dxi_ref, dg_ref, dy_ref, dgp_ref = refs
        else:
            dxo_ref, dh_ref, x_ref, g_ref, dxi_ref, dg_ref = refs
        i = pl.program_id(0)
        dv, dg = _rms_bwd_rows(x_ref[...], g_ref[...], dh_ref[...].astype(F32))
        dxi = dxo_ref[...] + dv
        dxi_ref[...] = dxi

        @pl.when(i == 0)
        def _():
            dg_ref[...] = jnp.zeros_like(dg_ref)

        dg_ref[...] += dg
        if with_prev:
            dy, dgp = _rms_bwd_rows(y_ref[...].astype(F32), gp_ref[...], dxi)
            dy_ref[...] = dy.astype(BF16)

            @pl.when(i == 0)
            def _():
                dgp_ref[...] = jnp.zeros_like(dgp_ref)

            dgp_ref[...] += dgp

    ins = [dx_out, dh, x_in, g_pre]
    in_specs = [_row(tt, D), _row(tt, D), _row(tt, D), _const((1, D))]
    out_shape = [_sds((T, D), F32), _sds((1, D), F32)]
    out_specs = [_row(tt, D), _const((1, D))]
    if with_prev:
        ins += [y_prev, g_post_prev]
        in_specs += [_row(tt, D), _const((1, D))]
        out_shape += [_sds((T, D), BF16), _sds((1, D), F32)]
        out_specs += [_row(tt, D), _const((1, D))]
    return _pcall(body, name=name, out_shape=tuple(out_shape), grid=(T // tt,),
                  in_specs=in_specs, out_specs=tuple(out_specs),
                  compiler_params=_params("arbitrary"))(*ins)


def _rms_gain_grad(dout, v, name):
    T, D = v.shape
    tt = _fit(T, ROW_TILE, 8)

    def body(d_ref, v_ref, dg_ref):
        @pl.when(pl.program_id(0) == 0)
        def _():
            dg_ref[...] = jnp.zeros_like(dg_ref)

        v = v_ref[...]
        dg_ref[...] += jnp.sum(d_ref[...] * (v * _rstd(v)), axis=0, keepdims=True)

    return _pcall(body, name=name, out_shape=_sds((1, D), F32), grid=(T // tt,),
                  in_specs=[_row(tt, D), _row(tt, D)], out_specs=_const((1, D)),
                  compiler_params=_params("arbitrary"))(dout, v)


def _softmax_rows(s):
    e = jnp.exp(s - jnp.max(s, axis=-1, keepdims=True))
    return e / jnp.sum(e, axis=-1, keepdims=True)


def _attn_fwd(q, kv, name):
    T, D = q.shape
    nm = kv.shape[0]
    hd = D // XA_HEADS
    scale = hd ** -0.5
    tq = _fit(T, ROW_TILE, 8)

    def body(q_ref, k_ref, v_ref, o_ref):
        for h in range(XA_HEADS):
            sl = slice(h * hd, (h + 1) * hd)
            s = lax.dot_general(q_ref[:, sl], k_ref[:, sl], _DIMS["nt"], preferred_element_type=F32) * scale
            p = _softmax_rows(s)
            o_ref[:, sl] = jnp.dot(p.astype(BF16), v_ref[:, sl], preferred_element_type=F32).astype(BF16)

    return _pcall(body, name=name, out_shape=_sds((T, D), BF16), grid=(T // tq,),
                  in_specs=[_row(tq, D), pl.BlockSpec((nm, D), lambda i: (0, 0)), pl.BlockSpec((nm, D), lambda i: (0, 1))],
                  out_specs=_row(tq, D), compiler_params=_params("parallel"))(q, kv, kv)


def _attn_bwd(q, kv, do, name):
    T, D = q.shape
    nm = kv.shape[0]
    hd = D // XA_HEADS
    scale = hd ** -0.5
    tq = _fit(T, ROW_TILE, 8)

    def body(q_ref, k_ref, v_ref, do_ref, dq_ref, dkv_ref):
        @pl.when(pl.program_id(0) == 0)
        def _():
            dkv_ref[...] = jnp.zeros_like(dkv_ref)

        for h in range(XA_HEADS):
            sl = slice(h * hd, (h + 1) * hd)
            qh, kh, vh, doh = q_ref[:, sl], k_ref[:, sl], v_ref[:, sl], do_ref[:, sl]
            s = lax.dot_general(qh, kh, _DIMS["nt"], preferred_element_type=F32) * scale
            p = _softmax_rows(s)
            dp = lax.dot_general(doh, vh, _DIMS["nt"], preferred_element_type=F32)
            ds = (p * (dp - jnp.sum(dp * p, axis=-1, keepdims=True)) * scale).astype(BF16)
            dq_ref[:, sl] = jnp.dot(ds, kh, preferred_element_type=F32).astype(BF16)
            dkv_ref[:, sl] += lax.dot_general(ds, qh, _DIMS["tn"], preferred_element_type=F32)
            dkv_ref[:, D + h * hd:D + (h + 1) * hd] += lax.dot_general(
                p.astype(BF16), doh, _DIMS["tn"], preferred_element_type=F32)

    return _pcall(body, name=name, out_shape=(_sds((T, D), BF16), _sds((nm, 2 * D), F32)), grid=(T // tq,),
                  in_specs=[_row(tq, D), pl.BlockSpec((nm, D), lambda i: (0, 0)), pl.BlockSpec((nm, D), lambda i: (0, 1)),
                            _row(tq, D)],
                  out_specs=(_row(tq, D), _const((nm, 2 * D))),
                  compiler_params=_params("arbitrary"))(q, kv, kv, do)


def _ffn_gu_fwd(h, w_gu, name, dep=None, tm=512):
    T, D = h.shape
    S, _, c = w_gu.shape
    F = S * c // 2
    tm = _fit(T, tm)
    tn = _fit(c, 1536)
    nb = c // tn
    nj = F // tn
    n_in = 3 if dep is None else 4

    def w_spec(off):
        return pl.BlockSpec((None, D, tn), lambda i, j: ((j + off) // nb, 0, (j + off) % nb))

    def body(*refs):
        h_ref, wg_ref, wu_ref = refs[:3]
        dg_ref, du_ref, a_ref = refs[n_in:]
        hv = h_ref[...]
        g = jnp.dot(hv, wg_ref[...], preferred_element_type=F32)
        sg = jax.nn.sigmoid(g)
        silu = g * sg
        du_ref[...] = silu.astype(BF16)
        u = jnp.dot(hv, wu_ref[...], preferred_element_type=F32)
        dg_ref[...] = (u * (sg + silu * (1.0 - sg))).astype(BF16)
        a_ref[...] = (silu * u).astype(BF16)

    ins = [h, w_gu, w_gu]
    in_specs = [pl.BlockSpec((tm, D), lambda i, j: (i, 0)), w_spec(0), w_spec(nj)]
    if dep is not None:
        ins.append(dep)
        in_specs.append(ANY)
    o_spec = pl.BlockSpec((tm, tn), lambda i, j: (i, j))
    return _pcall(body, name=name, out_shape=(_sds((T, F), BF16),) * 3, grid=(T // tm, nj),
                  in_specs=in_specs, out_specs=(o_spec,) * 3,
                  compiler_params=_params("parallel", "parallel"))(*ins)


def _ffn_dgu_bwd(dy, w_down, dact_dgate, dact_dup, name, dep=None, tm=1024):
    T, D = dy.shape
    F = w_down.shape[0]
    tm = _fit(T, tm)
    tn = _fit(F, 512)
    cn = _fit(tn, 256)
    n_in = 4 if dep is None else 5

    def body(*refs):
        dy_ref, wd_ref, g_ref, u_ref = refs[:4]
        o_ref = refs[n_in]
        dyv = dy_ref[...]
        for n0 in range(0, tn, cn):
            da = lax.dot_general(dyv, wd_ref[n0:n0 + cn, :], _DIMS["nt"], preferred_element_type=F32)
            o_ref[0, :, n0:n0 + cn] = (da * g_ref[:, n0:n0 + cn].astype(F32)).astype(BF16)
            o_ref[1, :, n0:n0 + cn] = (da * u_ref[:, n0:n0 + cn].astype(F32)).astype(BF16)

    ins = [dy, w_down, dact_dgate, dact_dup]
    gu_spec = pl.BlockSpec((tm, tn), lambda i, j: (i, j))
    in_specs = [pl.BlockSpec((tm, D), lambda i, j: (i, 0)), pl.BlockSpec((tn, D), lambda i, j: (j, 0)), gu_spec, gu_spec]
    if dep is not None:
        ins.append(dep)
        in_specs.append(ANY)
    return _pcall(body, name=name, out_shape=_sds((2, T, F), BF16), grid=(T // tm, F // tn),
                  in_specs=in_specs, out_specs=pl.BlockSpec((2, tm, tn), lambda i, j: (0, i, j)),
                  compiler_params=_params("parallel", "parallel"))(*ins)


def _causal_taps(win, width, tt):
    for b in range(min(8, width)):
        wb = win if b == 0 else pltpu.roll(win, b, 0)
        a = 0
        while 8 * a + b <= width - 1:
            yield width - 1 - (8 * a + b), wb[HALO - 8 * a:HALO - 8 * a + tt]
            a += 1


def _anticausal_taps(win, width, tt):
    rows = tt + HALO
    for b in range(min(8, width)):
        wb = win if b == 0 else pltpu.roll(win, rows - b, 0)
        a = 0
        while 8 * a + b <= width - 1:
            yield width - 1 - (8 * a + b), wb[8 * a:8 * a + tt]
            a += 1


def _lanes(d):
    cw = _fit(d, CONV_LANES)
    return [slice(s, s + cw) for s in range(0, d, cw)], cw


def _a_mid_fwd(bcz, conv_w, name):
    T, D3 = bcz.shape
    D = D3 // 3
    width = conv_w.shape[0]
    tt = _fit(T, ROW_TILE, HALO)
    chunks, cw = _lanes(D)

    def body(b_ref, c_ref, z_ref, ch_ref, zh_ref, w_ref, o_ref, win_ref):
        i = pl.program_id(0)
        for sl in chunks:
            uh = ch_ref[:, sl].astype(F32) * zh_ref[:, sl].astype(F32)
            win_ref[0:HALO, :] = jnp.where(i > 0, uh, 0.0)
            win_ref[HALO:, :] = c_ref[:, sl].astype(F32) * z_ref[:, sl].astype(F32)
            acc = jnp.zeros((tt, cw), F32)
            for k, xs in _causal_taps(win_ref[...], width, tt):
                acc = acc + w_ref[k:k + 1, sl] * xs
            o_ref[:, sl] = (b_ref[:, sl].astype(F32) * acc).astype(BF16)

    return _pcall(body, name=name, out_shape=_sds((T, D), BF16), grid=(T // tt,),
                  in_specs=[_row(tt, D, 0), _row(tt, D, 1), _row(tt, D, 2), _prev_halo(tt, D, 1), _prev_halo(tt, D, 2),
                            _const((width, D))],
                  out_specs=_row(tt, D), scratch_shapes=[pltpu.VMEM((HALO + tt, cw), F32)],
                  compiler_params=_params("parallel"))(bcz, bcz, bcz, bcz, bcz, conv_w)


def _a_mid_bwd(bcz, dy2, conv_w, name):
    T, D3 = bcz.shape
    D = D3 // 3
    width = conv_w.shape[0]
    tt = _fit(T, ROW_TILE, HALO)
    chunks, cw = _lanes(D)
    n_tiles = T // tt

    def body(b_ref, c_ref, z_ref, ch_ref, zh_ref, bn_ref, d_ref, dn_ref, w_ref, o_ref, dw_ref, win_ref, dwin_ref):
        i = pl.program_id(0)

        @pl.when(i == 0)
        def _():
            dw_ref[...] = jnp.zeros_like(dw_ref)

        for ci, sl in enumerate(chunks):
            c = c_ref[:, sl].astype(F32)
            z = z_ref[:, sl].astype(F32)
            b = b_ref[:, sl].astype(F32)
            d2 = d_ref[:, sl].astype(F32)
            uh = ch_ref[:, sl].astype(F32) * zh_ref[:, sl].astype(F32)
            win_ref[0:HALO, :] = jnp.where(i > 0, uh, 0.0)
            win_ref[HALO:, :] = c * z
            d1 = d2 * b
            d1n = dn_ref[:, sl].astype(F32) * bn_ref[:, sl].astype(F32)
            dwin_ref[0:tt, :] = d1
            dwin_ref[tt:, :] = jnp.where(i < n_tiles - 1, d1n, 0.0)
            y1 = jnp.zeros((tt, cw), F32)
            for k, xs in _causal_taps(win_ref[...], width, tt):
                y1 = y1 + w_ref[k:k + 1, sl] * xs
                dw_ref[k:k + 1, sl] += jnp.sum(d1 * xs, axis=0, keepdims=True)
            du = jnp.zeros((tt, cw), F32)
            for k, xs in _anticausal_taps(dwin_ref[...], width, tt):
                du = du + w_ref[k:k + 1, sl] * xs
            o_ref[:, ci * cw:(ci + 1) * cw] = (d2 * y1).astype(BF16)
            o_ref[:, D + ci * cw:D + (ci + 1) * cw] = (du * z).astype(BF16)
            o_ref[:, 2 * D + ci * cw:2 * D + (ci + 1) * cw] = (du * c).astype(BF16)

    return _pcall(body, name=name, out_shape=(_sds((T, 3 * D), BF16), _sds((width, D), F32)), grid=(n_tiles,),
                  in_specs=[_row(tt, D, 0), _row(tt, D, 1), _row(tt, D, 2), _prev_halo(tt, D, 1), _prev_halo(tt, D, 2),
                            _next_halo(tt, D, T, 0), _row(tt, D), _next_halo(tt, D, T), _const((width, D))],
                  out_specs=(_row(tt, 3 * D), _const((width, D))),
                  scratch_shapes=[pltpu.VMEM((HALO + tt, cw), F32), pltpu.VMEM((tt + HALO, cw), F32)],
                  compiler_params=_params("arbitrary"))(bcz, bcz, bcz, bcz, bcz, bcz, dy2, dy2, conv_w)


_GELU_C = 0.7978845608028654
_GELU_A = 0.044715


def _gelu(v):
    return 0.5 * v * (1.0 + jnp.tanh(_GELU_C * (v + _GELU_A * v * v * v)))


def _gelu_grad(v):
    t = jnp.tanh(_GELU_C * (v + _GELU_A * v * v * v))
    return 0.5 * (1.0 + t) + 0.5 * v * (1.0 - t * t) * (_GELU_C * (1.0 + 3.0 * _GELU_A * v * v))


def _ln_stats(v):
    mu = jnp.mean(v, axis=-1, keepdims=True)
    vc = v - mu
    return vc * lax.rsqrt(jnp.mean(vc * vc, axis=-1, keepdims=True) + NORM_EPS)


def _tril(n):
    return lax.broadcasted_iota(jnp.int32, (n, n), 0) >= lax.broadcasted_iota(jnp.int32, (n, n), 1)


def _b_mid_fwd(uv, v_g, v_b, w_s, bias_b, name):
    T, D2 = uv.shape
    D = D2 // 2
    G, C, _ = w_s.shape
    gd = D // G
    tt = _fit(T, ROW_TILE, C)

    def body(u_ref, v_ref, g_ref, b_ref, ws_ref, bias_ref, o_ref, vln_ref):
        vln_ref[...] = (_ln_stats(_gelu(v_ref[...].astype(F32))) * g_ref[...] + b_ref[...]).astype(BF16)
        mask = _tril(C)
        for g in range(G):
            wsm = jnp.where(mask, ws_ref[g], 0.0).astype(BF16)
            cs = slice(g * gd, (g + 1) * gd)
            for n in range(tt // C):
                rs = slice(n * C, (n + 1) * C)
                sv = jnp.dot(wsm, vln_ref[rs, cs], preferred_element_type=F32) + bias_ref[g]
                o_ref[rs, cs] = (_gelu(u_ref[rs, cs].astype(F32)) * sv).astype(BF16)

    return _pcall(body, name=name, out_shape=_sds((T, D), BF16), grid=(T // tt,),
                  in_specs=[_row(tt, D, 0), _row(tt, D, 1), _const((1, D)), _const((1, D)), _const((G, C, C)),
                            _const((G, C, gd))],
                  out_specs=_row(tt, D), scratch_shapes=[pltpu.VMEM((tt, D), BF16)],
                  compiler_params=_params("parallel"))(uv, uv, v_g, v_b, w_s, bias_b)


def _b_mid_bwd(uv, dgated, v_g, v_b, w_s, bias_b, name):
    T, D2 = uv.shape
    D = D2 // 2
    G, C, _ = w_s.shape
    gd = D // G
    tt = _fit(T, ROW_TILE, C)

    def body(u_ref, v_ref, d_ref, g_ref, b_ref, ws_ref, bias_ref, o_ref, dws_ref, dsb_ref, dvg_ref, dvb_ref,
             vln_ref, dvln_ref):
        @pl.when(pl.program_id(0) == 0)
        def _():
            dws_ref[...] = jnp.zeros_like(dws_ref)
            dsb_ref[...] = jnp.zeros_like(dsb_ref)
            dvg_ref[...] = jnp.zeros_like(dvg_ref)
            dvb_ref[...] = jnp.zeros_like(dvb_ref)

        vpre = v_ref[...].astype(F32)
        vhat = _ln_stats(_gelu(vpre))
        vln_ref[...] = (vhat * g_ref[...] + b_ref[...]).astype(BF16)
        mask = _tril(C)
        lane = lax.broadcasted_iota(jnp.int32, (C, 128), 1)
        for g in range(G):
            wsm = jnp.where(mask, ws_ref[g], 0.0).astype(BF16)
            cs = slice(g * gd, (g + 1) * gd)
            for n in range(tt // C):
                rs = slice(n * C, (n + 1) * C)
                vt = vln_ref[rs, cs]
                sv = jnp.dot(wsm, vt, preferred_element_type=F32) + bias_ref[g]
                dg = d_ref[rs, cs].astype(F32)
                upre = u_ref[rs, cs].astype(F32)
                o_ref[rs, cs] = (dg * sv * _gelu_grad(upre)).astype(BF16)
                dsv = dg * _gelu(upre)
                dsb_ref[...] += jnp.where(lane == g, jnp.sum(dsv, axis=-1, keepdims=True), 0.0)
                dsv16 = dsv.astype(BF16)
                dws_ref[g] += jnp.where(mask, lax.dot_general(dsv16, vt, _DIMS["nt"], preferred_element_type=F32), 0.0)
                dvln_ref[rs, cs] = lax.dot_general(wsm, dsv16, _DIMS["tn"], preferred_element_type=F32)
        dvln = dvln_ref[...]
        dvg_ref[...] += jnp.sum(dvln * vhat, axis=0, keepdims=True)
        dvb_ref[...] += jnp.sum(dvln, axis=0, keepdims=True)
        dvh = dvln * g_ref[...]
        vc = _gelu(vpre)
        vc = vc - jnp.mean(vc, axis=-1, keepdims=True)
        rstd = lax.rsqrt(jnp.mean(vc * vc, axis=-1, keepdims=True) + NORM_EPS)
        dv = rstd * (dvh - jnp.mean(dvh, axis=-1, keepdims=True) - vhat * jnp.mean(dvh * vhat, axis=-1, keepdims=True))
        o_ref[:, D:] = (dv * _gelu_grad(vpre)).astype(BF16)

    return _pcall(body, name=name,
                  out_shape=(_sds((T, 2 * D), BF16), _sds((G, C, C), F32), _sds((C, 128), F32), _sds((1, D), F32),
                             _sds((1, D), F32)),
                  grid=(T // tt,),
                  in_specs=[_row(tt, D, 0), _row(tt, D, 1), _row(tt, D), _const((1, D)), _const((1, D)),
                            _const((G, C, C)), _const((G, C, gd))],
                  out_specs=(_row(tt, 2 * D), _const((G, C, C)), _const((C, 128)), _const((1, D)), _const((1, D))),
                  scratch_shapes=[pltpu.VMEM((tt, D), BF16), pltpu.VMEM((tt, D), F32)],
                  compiler_params=_params("arbitrary"))(uv, uv, dgated, v_g, v_b, w_s, bias_b)


def _c_mid_fwd(ag, conv_w, conv_b, ln_g, ln_b, name):
    T, D2 = ag.shape
    D = D2 // 2
    width = conv_w.shape[0]
    tt = _fit(T, ROW_TILE, HALO)
    chunks, cw = _lanes(D)

    def body(a_ref, g_ref, ah_ref, gh_ref, w_ref, cb_ref, lg_ref, lb_ref, y2_ref, o_ref, win_ref):
        i = pl.program_id(0)
        for sl in chunks:
            yh = ah_ref[:, sl].astype(F32) * jax.nn.sigmoid(gh_ref[:, sl].astype(F32))
            win_ref[0:HALO, :] = jnp.where(i > 0, yh, 0.0)
            win_ref[HALO:, :] = a_ref[:, sl].astype(F32) * jax.nn.sigmoid(g_ref[:, sl].astype(F32))
            acc = jnp.zeros((tt, cw), F32)
            for k, xs in _causal_taps(win_ref[...], width, tt):
                acc = acc + w_ref[k:k + 1, sl] * xs
            y2_ref[:, sl] = acc + cb_ref[:, sl]
        y3 = _ln_stats(y2_ref[...]) * lg_ref[...] + lb_ref[...]
        o_ref[...] = (y3 * jax.nn.sigmoid(y3)).astype(BF16)

    return _pcall(body, name=name, out_shape=(_sds((T, D), F32), _sds((T, D), BF16)), grid=(T // tt,),
                  in_specs=[_row(tt, D, 0), _row(tt, D, 1), _prev_halo(tt, D, 0), _prev_halo(tt, D, 1),
                            _const((width, D)), _const((1, D)), _const((1, D)), _const((1, D))],
                  out_specs=(_row(tt, D), _row(tt, D)), scratch_shapes=[pltpu.VMEM((HALO + tt, cw), F32)],
                  compiler_params=_params("parallel"))(ag, ag, ag, ag, conv_w, conv_b, ln_g, ln_b)


def _c_mid_bwd(ag, y2, dy4, conv_w, ln_g, ln_b, name):
    T, D2 = ag.shape
    D = D2 // 2
    width = conv_w.shape[0]
    tt = _fit(T, ROW_TILE, HALO)
    chunks, cw = _lanes(D)
    n_tiles = T // tt

    def ln_silu_bwd(y2v, dy4v, lg, lb):
        mu = jnp.mean(y2v, axis=-1, keepdims=True)
        yc = y2v - mu
        rstd = lax.rsqrt(jnp.mean(yc * yc, axis=-1, keepdims=True) + NORM_EPS)
        yh = yc * rstd
        y3 = yh * lg + lb
        sg = jax.nn.sigmoid(y3)
        dy3 = dy4v * (sg * (1.0 + y3 * (1.0 - sg)))
        dyh = dy3 * lg
        dy2 = rstd * (dyh - jnp.mean(dyh, axis=-1, keepdims=True) - yh * jnp.mean(dyh * yh, axis=-1, keepdims=True))
        return dy2, dy3, yh

    def body(a_ref, g_ref, ah_ref, gh_ref, y2_ref, y2n_ref, d_ref, dn_ref, w_ref, lg_ref, lb_ref,
             o_ref, dw_ref, dcb_ref, dlg_ref, dlb_ref, win_ref, dwin_ref):
        i = pl.program_id(0)

        @pl.when(i == 0)
        def _():
            dw_ref[...] = jnp.zeros_like(dw_ref)
            dcb_ref[...] = jnp.zeros_like(dcb_ref)
            dlg_ref[...] = jnp.zeros_like(dlg_ref)
            dlb_ref[...] = jnp.zeros_like(dlb_ref)

        lg = lg_ref[...]
        lb = lb_ref[...]
        dy2, dy3, yh = ln_silu_bwd(y2_ref[...], d_ref[...].astype(F32), lg, lb)
        dlg_ref[...] += jnp.sum(dy3 * yh, axis=0, keepdims=True)
        dlb_ref[...] += jnp.sum(dy3, axis=0, keepdims=True)
        dcb_ref[...] += jnp.sum(dy2, axis=0, keepdims=True)
        dwin_ref[0:tt, :] = dy2
        dy2n, _, _ = ln_silu_bwd(y2n_ref[...], dn_ref[...].astype(F32), lg, lb)
        dwin_ref[tt:, :] = jnp.where(i < n_tiles - 1, dy2n, 0.0)
        for ci, sl in enumerate(chunks):
            a = a_ref[:, sl].astype(F32)
            sg = jax.nn.sigmoid(g_ref[:, sl].astype(F32))
            yh1 = ah_ref[:, sl].astype(F32) * jax.nn.sigmoid(gh_ref[:, sl].astype(F32))
            win_ref[0:HALO, :] = jnp.where(i > 0, yh1, 0.0)
            win_ref[HALO:, :] = a * sg
            d2 = dwin_ref[0:tt, sl]
            for k, xs in _causal_taps(win_ref[...], width, tt):
                dw_ref[k:k + 1, sl] += jnp.sum(d2 * xs, axis=0, keepdims=True)
            d1 = jnp.zeros((tt, cw), F32)
            for k, xs in _anticausal_taps(dwin_ref[:, sl], width, tt):
                d1 = d1 + w_ref[k:k + 1, sl] * xs
            o_ref[:, ci * cw:(ci + 1) * cw] = (d1 * sg).astype(BF16)
            o_ref[:, D + ci * cw:D + (ci + 1) * cw] = (d1 * a * sg * (1.0 - sg)).astype(BF16)

    return _pcall(body, name=name,
                  out_shape=(_sds((T, 2 * D), BF16), _sds((width, D), F32), _sds((1, D), F32), _sds((1, D), F32),
                             _sds((1, D), F32)),
                  grid=(n_tiles,),
                  in_specs=[_row(tt, D, 0), _row(tt, D, 1), _prev_halo(tt, D, 0), _prev_halo(tt, D, 1),
                            _row(tt, D), _next_halo(tt, D, T), _row(tt, D), _next_halo(tt, D, T),
                            _const((width, D)), _const((1, D)), _const((1, D))],
                  out_specs=(_row(tt, 2 * D), _const((width, D)), _const((1, D)), _const((1, D)), _const((1, D))),
                  scratch_shapes=[pltpu.VMEM((HALO + tt, cw), F32), pltpu.VMEM((tt + HALO, D), F32)],
                  compiler_params=_params("arbitrary"))(ag, ag, ag, ag, y2, y2, dy4, dy4, conv_w, ln_g, ln_b)


def _place():
    x, y, c = lax.axis_index("x"), lax.axis_index("y"), lax.axis_index("c")
    return x, y, c


def _slot(px, py, pc):
    return 4 * px + 2 * py + pc


def _all_gather(shards, name):
    n = len(shards)

    def body(*refs):
        ins, outs = refs[:n], refs[n:2 * n]
        send_sems, recv_sems, local_sems = refs[2 * n:]
        x, y, c = _place()
        me, sibling = (x, y, c), (x, y, 1 - c)
        chips = [(1 - x, y), (x, 1 - y), (1 - x, 1 - y)]

        def copy(t, k, block, to, src=None):
            dst = outs[t].at[_slot(*block)]
            return pltpu.make_async_remote_copy(
                src_ref=dst if src is None else src, dst_ref=dst, send_sem=send_sems.at[t, k],
                recv_sem=recv_sems.at[t, k], device_id=to, device_id_type=MESH)

        mine = [pltpu.make_async_copy(ins[t], outs[t].at[_slot(*me)], local_sems.at[t]) for t in range(n)]
        for cp in mine:
            cp.start()
        first = []
        for j, chip in enumerate(chips):
            first += [copy(t, 1 + j, me, (*chip, c), src=ins[t]) for t in range(n)]
        first += [copy(t, 0, me, sibling, src=ins[t]) for t in range(n)]
        for cp in first:
            cp.start()
        passed = []
        for j, chip in enumerate(chips):
            for t in range(n):
                copy(t, 1 + j, (*chip, c), me).wait_recv()
                cp = copy(t, 4 + j, (*chip, c), sibling)
                cp.start()
                passed.append(cp)
        for t in range(n):
            copy(t, 0, sibling, me).wait_recv()
            for j, chip in enumerate(chips):
                copy(t, 4 + j, (*chip, 1 - c), me).wait_recv()
        for cp in first + passed:
            cp.wait_send()
        for cp in mine:
            cp.wait()

    outs = _pcall(
        body, name=name, out_shape=tuple(_sds((N_DEV,) + s.shape, s.dtype) for s in shards),
        in_specs=[ANY] * n, out_specs=(ANY,) * n,
        scratch_shapes=[pltpu.SemaphoreType.DMA((n, 7)), pltpu.SemaphoreType.DMA((n, 7)), pltpu.SemaphoreType.DMA((n,))],
    )(*shards)
    return list(outs)


_HBM = pl.BlockSpec(memory_space=pltpu.HBM)
_SEM = pl.BlockSpec(memory_space=pltpu.SEMAPHORE)
_DATAFLOW = pltpu.SideEffectType.DATAFLOW_SIDE_EFFECTING


def _peers(x, y, c):
    out = []
    for j in range(1, N_DEV):
        fx, fy, fc = (j >> 2) & 1, (j >> 1) & 1, j & 1
        out.append((1 - x if fx else x, 1 - y if fy else y, 1 - c if fc else c))
    return out


def _exchange_copies(ins, zones, mode, sems):
    send_sem, recv_sem, local_sem = sems
    x, y, c = _place()
    me = _slot(x, y, c)
    sibling = (x, y, 1 - c)
    chips = [(1 - x, y), (x, 1 - y), (1 - x, 1 - y)]
    local, remote = [], []

    def add(src, dst, to, landed):
        remote.append((pltpu.make_async_remote_copy(src_ref=src, dst_ref=dst, send_sem=send_sem, recv_sem=recv_sem,
                                                    device_id=to, device_id_type=MESH), landed))

    for t, zone in enumerate(zones):
        if mode == "scatter":
            local.append(pltpu.make_async_copy(ins[t].at[me], zone.at[me], local_sem))
            for peer in _peers(x, y, c):
                add(ins[t].at[_slot(*peer)], zone.at[me], peer, zone.at[_slot(*peer)])
        elif mode == "gather_chips":
            local.append(pltpu.make_async_copy(ins[t], zone.at[me], local_sem))
            for peer in [(*chip, c) for chip in chips] + [sibling]:
                add(ins[t], zone.at[me], peer, zone.at[_slot(*peer)])
        else:
            for chip in chips:
                block = zone.at[_slot(*chip, c)]
                add(block, block, sibling, zone.at[_slot(*chip, 1 - c)])
    return local, remote


def _exchange_start(srcs, lands, mode, after, name):
    if lands is None:
        lands = [lax.empty(s.shape if mode == "scatter" else (N_DEV,) + s.shape, s.dtype) for s in srcs]
    ns, na = len(srcs), len(srcs) + len(lands)

    def body(*refs):
        local, remote = _exchange_copies(refs[:ns], refs[ns:na], mode, refs[na + 1:na + 4])
        for cp in local:
            cp.start()
        for cp, _ in remote:
            cp.start()
        refs[-1][...] = jnp.zeros_like(refs[-1])

    hbm = lambda a: pltpu.with_memory_space_constraint(a, pltpu.HBM)
    arrays = list(srcs) + list(lands)
    outs = _pcall(
        body, name=name,
        out_shape=(pltpu.SemaphoreType.DMA(()),) * 3
        + tuple(pltpu.HBM(a.shape, a.dtype) for a in arrays) + (_sds((8, 128), F32),),
        in_specs=[_HBM] * na + [ANY],
        out_specs=(_SEM,) * 3 + (_HBM,) * na + (pl.BlockSpec(memory_space=pltpu.VMEM),),
        input_output_aliases={t: 3 + t for t in range(na)},
        compiler_params=pltpu.CompilerParams(has_side_effects=_DATAFLOW),
    )(*[hbm(a) for a in arrays], after)
    return outs[:3], list(outs[3:3 + ns]), list(outs[3 + ns:3 + na]), outs[-1]


def _exchange_wait(sems, srcs, lands, mode, after, name):
    ns, na = len(srcs), len(srcs) + len(lands)
    afters = list(after) if isinstance(after, (list, tuple)) else [after]

    def body(*refs):
        local, remote = _exchange_copies(refs[:ns], refs[ns:na], mode, refs[na:na + 3])
        for cp in local:
            cp.wait()
        for cp, landed in remote:
            cp.wait_send()
            pltpu.make_async_remote_copy(
                src_ref=landed, dst_ref=landed, send_sem=refs[na], recv_sem=refs[na + 1],
                device_id=_place(), device_id_type=MESH).wait_recv()

    outs = _pcall(
        body, name=name, out_shape=tuple(pltpu.HBM(a.shape, a.dtype) for a in list(srcs) + list(lands)),
        in_specs=[_HBM] * na + [_SEM] * 3 + [ANY] * len(afters), out_specs=(_HBM,) * na,
        input_output_aliases={t: t for t in range(na)},
        compiler_params=pltpu.CompilerParams(has_side_effects=_DATAFLOW),
    )(*srcs, *lands, *sems, *afters)
    return list(outs[ns:])


def _reduce_adam(recvs, w, m, v, name, dep=None):
    L, r, c = w.shape
    tr = _fit(r, max(16, (128 * 1024) // c), 16)
    ni = r // tr

    def recv_spec(l0):
        def index(l, i):
            return 0, jnp.where(l == l0, i, jnp.where(l < l0, 0, ni - 1)), 0
        return pl.BlockSpec((N_DEV, tr, c), index)

    lay = pl.BlockSpec((None, tr, c), lambda l, i: (l, i, 0))

    n_dep = 0 if dep is None else 1

    def body(*refs):
        recv_refs = refs[:L]
        w_ref, m_ref, v_ref = refs[L:L + 3]
        g_out, d_out, m_out, v_out = refs[L + 3 + n_dep:]
        l = pl.program_id(0)
        for l0 in range(L):
            @pl.when(l == l0)
            def _(l0=l0):
                g = recv_refs[l0][0].astype(F32)
                for s in range(1, N_DEV):
                    g = g + recv_refs[l0][s].astype(F32)
                mn = ADAM_B1 * m_ref[...] + (1.0 - ADAM_B1) * g
                vn = ADAM_B2 * v_ref[...] + (1.0 - ADAM_B2) * (g * g)
                m_hat = mn / (1.0 - ADAM_B1 ** ADAM_STEP)
                v_hat = vn / (1.0 - ADAM_B2 ** ADAM_STEP)
                g_out[...] = g
                d_out[...] = -ADAM_LR * (m_hat / (jnp.sqrt(v_hat) + ADAM_EPS) + ADAM_WD * w_ref[...])
                m_out[...] = mn
                v_out[...] = vn

    return _pcall(body, name=name, out_shape=(_sds((L, r, c), F32),) * 4, grid=(L, ni),
                  in_specs=[recv_spec(l0) for l0 in range(L)] + [lay, lay, lay] + [ANY] * n_dep, out_specs=(lay,) * 4,
                  compiler_params=_params("arbitrary", "arbitrary"))(*recvs, w, m, v, *([dep] if n_dep else []))


_SMALL_SHARDED = ["mix_norm", "xa_norm", "ffn_norm", "a_conv_w", "c_conv_w", "c_conv_b", "c_ln_g", "c_ln_b"]
_SMALL_REPLICATED = ["b_v_g", "b_v_b", "b_w_s", "b_s_bias"]
_BIG = ["xa_wq", "xa_wkv", "xa_wo", "ffn_w_gu", "ffn_w_down", "a_w_in", "a_w_out", "b_w_in", "b_w_out", "c_w_in",
        "c_w_out"]
_COL_SHARDED = {"xa_wkv", "ffn_w_gu", "a_w_in", "b_w_in", "c_w_in"}
_WEIGHTS = ["mix_norm", "xa_norm", "xa_wq", "xa_wkv", "xa_wo", "ffn_norm", "ffn_w_gu", "ffn_w_down", "a_w_in",
            "a_conv_w", "a_w_out", "b_w_in", "b_v_g", "b_v_b", "b_w_s", "b_s_bias", "b_w_out", "c_w_in", "c_conv_w",
            "c_conv_b", "c_ln_g", "c_ln_b", "c_w_out"]
_MIXER = "abc"


def _size(shape):
    size = 1
    for s in shape:
        size *= s
    return size


def _row_layout(shapes, width):
    offs, r = [], 0
    for shape in shapes:
        offs.append(r)
        r += -(-(-(-_size(shape) // width)) // 8) * 8
    return offs, r


def _pack_rows(arrays, width, fill):
    offs, total = _row_layout([a.shape for a in arrays], width)
    ends = offs[1:] + [total]
    rows = [jnp.pad(a.reshape(-1), (0, (e - o) * width - a.size), constant_values=fill).reshape(e - o, width)
            for a, o, e in zip(arrays, offs, ends)]
    return jnp.concatenate(rows, axis=0)


def _unpack_rows(packed, like):
    width = packed.shape[-1]
    offs, _ = _row_layout(like, width)
    return [packed[o:o + -(-_size(s) // width)].reshape(-1)[:_size(s)].reshape(s) for o, s in zip(offs, like)]


def _assemble_rows(pieces, rows, width, name):
    n = len(pieces)

    def body(*refs):
        o_ref = refs[n]
        o_ref[...] = jnp.zeros_like(o_ref)
        for r, (a, off) in zip(refs[:n], pieces):
            o_ref[off:off + a.shape[0], :] = r[...]

    return _pcall(body, name=name, out_shape=_sds((rows, width), F32),
                  compiler_params=pltpu.CompilerParams(vmem_limit_bytes=V7X_VMEM_LIMIT))(*[a for a, _ in pieces])


def kernel(x, mem, mix_norm, xa_norm, xa_wq, xa_wkv, xa_wo, ffn_norm, ffn_w_gu, ffn_w_down, a_w_in, a_conv_w, a_w_out, b_w_in, b_v_g, b_v_b, b_w_s, b_s_bias, b_w_out, c_w_in, c_conv_w, c_conv_b, c_ln_g, c_ln_b, c_w_out, loss_target, m_mix_norm, m_xa_norm, m_xa_wq, m_xa_wkv, m_xa_wo, m_ffn_norm, m_ffn_w_gu, m_ffn_w_down, m_a_w_in, m_a_conv_w, m_a_w_out, m_b_w_in, m_b_v_g, m_b_v_b, m_b_w_s, m_b_s_bias, m_b_w_out, m_c_w_in, m_c_conv_w, m_c_conv_b, m_c_ln_g, m_c_ln_b, m_c_w_out, v_mix_norm, v_xa_norm, v_xa_wq, v_xa_wkv, v_xa_wo, v_ffn_norm, v_ffn_w_gu, v_ffn_w_down, v_a_w_in, v_a_conv_w, v_a_w_out, v_b_w_in, v_b_v_g, v_b_v_b, v_b_w_s, v_b_s_bias, v_b_w_out, v_c_w_in, v_c_conv_w, v_c_conv_b, v_c_ln_g, v_c_ln_b, v_c_w_out):
    P = dict(locals())
    T, D = x.shape[1], x.shape[2]
    dl = D // N_DEV
    depth = mix_norm.shape[0]
    x0, mem0, target = x[0], mem[0], loss_target[0]
    my_slot = _slot(*_place())

    sh_shapes = [P[n].shape for n in _SMALL_SHARDED]
    packed = _pack_rows([P[n] for n in _SMALL_SHARDED], dl, 0.0)
    n_sh = packed.shape[0]
    gathered = _all_gather([packed], "ag_small")[0]
    full_rows = jnp.transpose(gathered, (1, 0, 2)).reshape(n_sh, D)
    small = dict(zip(_SMALL_SHARDED, _unpack_rows(full_rows, [s[:-1] + (D,) for s in sh_shapes])))
    G, C = b_w_s.shape[1], b_w_s.shape[2]
    gd = D // G
    bias_b = jnp.broadcast_to(b_s_bias[0][:, :, None], (G, C, gd))
    zero_row = jnp.zeros((1, D), F32)

    w16 = {n: P[n].astype(BF16) for n in _BIG}

    groups = [(i, part) for i in range(depth) for part in range(3)]

    def group_names(i, part):
        mx, slot = _MIXER[i % N_MIXERS], i // N_MIXERS
        if part == 0:
            return [(mx + "_w_in", slot), (mx + "_w_out", slot)]
        if part == 1:
            return [("xa_wq", i), ("xa_wkv", i), ("xa_wo", i)]
        return [("ffn_w_gu", i), ("ffn_w_down", i)]

    no_token = jnp.zeros((8, 128), F32)
    fwd = {"g": 0, "last": full_rows, "token": no_token, "stage1": {}, "stage2": {}}

    def tag(g):
        return "%d_%d" % groups[g]

    def start_stage1(g):
        if g < len(groups):
            names = group_names(*groups[g])
            sems, srcs, lands, token = _exchange_start([w16[n][j] for n, j in names], None, "gather_chips",
                                                       fwd["last"], "ag_start_" + tag(g))
            fwd["stage1"][g] = (sems, srcs, lands)
            fwd["last"] = fwd["token"] = token

    def start_stage2(g, after):
        if g < len(groups):
            sems, srcs, lands = fwd["stage1"].pop(g)
            lands = _exchange_wait(sems, srcs, lands, "gather_chips", after, "ag_wait_" + tag(g))
            sems, _, lands, token = _exchange_start([], lands, "gather_sibling", after, "ag_pass_" + tag(g))
            fwd["stage2"][g] = (sems, lands)
            fwd["last"] = fwd["token"] = token

    def begin_group():
        g, y = fwd["g"], fwd["last"]
        sems, lands = fwd["stage2"].pop(g)
        fulls = _exchange_wait(sems, [], lands, "gather_sibling", y, "ag_done_" + tag(g))
        fwd["last"] = fulls[0]
        start_stage1(g + 3)
        out = {}
        for (n, _), f in zip(group_names(*groups[g]), fulls):
            key = n[2:] if n[1] == "_" and n[0] in _MIXER else n
            out[key] = f if n in _COL_SHARDED else f.reshape(-1, f.shape[-1])
        fwd["g"] += 1
        return out, fwd["token"]

    def mid_group(y):
        start_stage2(fwd["g"], y)
        return fwd["token"]

    def end_group(y):
        fwd["last"] = y

    for g0 in range(3):
        start_stage1(g0)
    start_stage2(0, fwd["last"])

    saved = []
    xin = x0
    h = _rms_fwd(x0, small["mix_norm"][0, 0][None], "rms_first")
    for i in range(depth):
        kind, slot = i % N_MIXERS, i // N_MIXERS
        W, dep = begin_group()
        S = {"W": W, "x0": xin, "h0": h}
        pre = _mm(h, W["w_in"], "nn", BF16, "mm_in_%s" % _MIXER[kind], b_blocked=True, dep=dep)
        S["pre"] = pre
        dep = mid_group(pre)
        if kind == 0:
            mid = _a_mid_fwd(pre, small["a_conv_w"][slot], "a_mid_fwd")
        elif kind == 1:
            mid = _b_mid_fwd(pre, b_v_g, b_v_b, b_w_s[0], bias_b, "b_mid_fwd")
        else:
            y2c, mid = _c_mid_fwd(pre, small["c_conv_w"][slot], small["c_conv_b"], small["c_ln_g"], small["c_ln_b"],
                                  "c_mid_fwd")
            S["y2c"] = y2c
        S["mid"] = mid
        S["y0"] = _mm(mid, W["w_out"], "nn", BF16, "mm_out", dep=dep)
        end_group(S["y0"])
        Wx, dep = begin_group()
        W.update(Wx)
        xin, h = _post_pre_fwd(xin, S["y0"], small["mix_norm"][i, 1][None], small["xa_norm"][i, 0][None], "post_pre")
        S["x1"], S["h1"] = xin, h
        S["q"] = _mm(h, W["xa_wq"], "nn", BF16, "mm_q", dep=dep)
        dep = mid_group(S["q"])
        S["memn"] = _rms_fwd(mem0, small["xa_norm"][i, 2][None], "rms_mem")
        S["kv"] = _mm(S["memn"], W["xa_wkv"], "nn", BF16, "mm_kv", b_blocked=True)
        S["o"] = _attn_fwd(S["q"], S["kv"], "attn_fwd")
        S["y1"] = _mm(S["o"], W["xa_wo"], "nn", BF16, "mm_out", dep=dep)
        end_group(S["y1"])
        Wf, dep = begin_group()
        W.update(Wf)
        xin, h = _post_pre_fwd(xin, S["y1"], small["xa_norm"][i, 1][None], small["ffn_norm"][i, 0][None], "post_pre")
        S["x2"], S["h2"] = xin, h
        S["dact_dgate"], S["dact_dup"], S["act"] = _ffn_gu_fwd(h, W["ffn_w_gu"], "ffn_gu_fwd", dep=dep)
        S["y2"] = _mm(S["act"], W["ffn_w_down"], "nn", BF16, "mm_down", dep=mid_group(S["act"]))
        end_group(S["y2"])
        if i + 1 < depth:
            xin, h = _post_pre_fwd(xin, S["y2"], small["ffn_norm"][i, 1][None], small["mix_norm"][i + 1, 0][None],
                                   "post_pre")
        saved.append(S)

    last = saved[-1]
    loss_part, dx, dy, dg = _final_fwd_loss(xin, last["y2"], small["ffn_norm"][depth - 1, 1][None], target, "final_loss")
    loss = lax.psum(loss_part[0, 0], ("x", "y", "c"))

    g_mix = [[zero_row, zero_row] for _ in range(depth)]
    g_xa = [[zero_row, zero_row, zero_row] for _ in range(depth)]
    g_ffn = [[zero_row, zero_row] for _ in range(depth)]
    g_small = {}
    recv = {n: [None] * P[n].shape[0] for n in _BIG}
    g_ffn[depth - 1][1] = dg

    bwd = {"queue": [], "token": no_token}
    scatters_in_flight = 3

    def finish_scatter(after, keep):
        while len(bwd["queue"]) > keep:
            names, tag, sems, srcs, lands = bwd["queue"].pop(0)
            for (n, j, _), r in zip(names, _exchange_wait(sems, srcs, lands, "scatter", after, "rs_wait_" + tag)):
                recv[n][j] = r

    def scatter_group(names, tag, after):
        finish_scatter(after, scatters_in_flight - 1)
        parts = [g if n in _COL_SHARDED else g.reshape(N_DEV, -1, g.shape[-1]) for n, _, g in names]
        sems, srcs, lands, bwd["token"] = _exchange_start(parts, None, "scatter", after, "rs_start_" + tag)
        bwd["queue"].append((names, tag, sems, srcs, lands))

    def scatter_token():
        return bwd["token"]

    for i in reversed(range(depth)):
        kind, slot = i % N_MIXERS, i // N_MIXERS
        mx = _MIXER[kind]
        S = saved[i]
        W = S["W"]
        dgu = _ffn_dgu_bwd(dy, W["ffn_w_down"], S["dact_dgate"], S["dact_dup"], "ffn_dgu_bwd", dep=scatter_token())
        dw_down = _mm(S["act"], dy, "tn", BF16, "mm_dw_down")
        dh = _mm(dgu, W["ffn_w_gu"], "nt", BF16, "mm_dh_gu", a_blocked=True, b_blocked=True)
        dw_gu = _mm(S["h2"], dgu, "tn", BF16, "mm_dw_gu", b_blocked=True, out_blocks=N_DEV)
        dx, g_ffn[i][0], dy, g_xa[i][1] = _pre_post_bwd(dx, dh, S["x2"], small["ffn_norm"][i, 0][None], S["y1"],
                                                         small["xa_norm"][i, 1][None], "pre_post_bwd")
        scatter_group([("ffn_w_gu", i, dw_gu), ("ffn_w_down", i, dw_down)], "%d_2" % i, dx)
        do = _mm(dy, W["xa_wo"], "nt", BF16, "mm_nt_dd16", dep=scatter_token())
        dw_o = _mm(S["o"], dy, "tn", BF16, "mm_dw_dd")
        dq, dkv = _attn_bwd(S["q"], S["kv"], do, "attn_bwd")
        dkv16 = dkv.astype(BF16)
        dh = _mm(dq, W["xa_wq"], "nt", BF16, "mm_nt_dd16")
        dw_q = _mm(S["h1"], dq, "tn", BF16, "mm_dw_dd")
        dw_kv = _mm(S["memn"], dkv16, "tn", BF16, "mm_dw_kv", out_blocks=N_DEV)
        dmemn = _mm(dkv16, W["xa_wkv"], "nt", F32, "mm_dmem", b_blocked=True)
        g_xa[i][2] = _rms_gain_grad(dmemn, mem0, "rms_gain_grad")
        dx, g_xa[i][0], dy, g_mix[i][1] = _pre_post_bwd(dx, dh, S["x1"], small["xa_norm"][i, 0][None], S["y0"],
                                                         small["mix_norm"][i, 1][None], "pre_post_bwd")
        scatter_group([("xa_wq", i, dw_q), ("xa_wkv", i, dw_kv), ("xa_wo", i, dw_o)], "%d_1" % i, dx)
        dmid = _mm(dy, W["w_out"], "nt", BF16, "mm_nt_dd16", dep=scatter_token())
        dw_out = _mm(S["mid"], dy, "tn", BF16, "mm_dw_dd")
        if kind == 0:
            dpre, dcw = _a_mid_bwd(S["pre"], dmid, small["a_conv_w"][slot], "a_mid_bwd")
            g_small.setdefault("a_conv_w", {})[slot] = dcw
        elif kind == 1:
            dpre, dws, dsb, dvg, dvb = _b_mid_bwd(S["pre"], dmid, b_v_g, b_v_b, b_w_s[0], bias_b, "b_mid_bwd")
            dsb_row = jnp.pad(jnp.transpose(dsb[:, :G]).reshape(1, G * C), ((0, 0), (0, (-G * C) % D)))
            g_small.update(b_w_s=dws.reshape(-1, D), b_s_bias=dsb_row.reshape(-1, D), b_v_g=dvg, b_v_b=dvb)
        else:
            dpre, dcw, dcb, dlg, dlb = _c_mid_bwd(S["pre"], S["y2c"], dmid, small["c_conv_w"][slot], small["c_ln_g"],
                                                  small["c_ln_b"], "c_mid_bwd")
            g_small.update(c_conv_w=dcw, c_conv_b=dcb, c_ln_g=dlg, c_ln_b=dlb)
        dh = _mm(dpre, W["w_in"], "nt", BF16, "mm_dh_in_%s" % mx, b_blocked=True)
        dw_in = _mm(S["h0"], dpre, "tn", BF16, "mm_dw_in_%s" % mx, out_blocks=N_DEV)
        if i > 0:
            dx, g_mix[i][0], dy, g_ffn[i - 1][1] = _pre_post_bwd(
                dx, dh, S["x0"], small["mix_norm"][i, 0][None], saved[i - 1]["y2"],
                small["ffn_norm"][i - 1, 1][None], "pre_post_bwd")
        else:
            dx, g_mix[i][0] = _pre_post_bwd(dx, dh, S["x0"], small["mix_norm"][i, 0][None], None, None, "pre_bwd")
        scatter_group([(mx + "_w_in", slot, dw_in), (mx + "_w_out", slot, dw_out)], "%d_0" % i, dx)
        S.clear()
    grad_x = dx[None]

    sh_off = dict(zip(_SMALL_SHARDED, _row_layout(sh_shapes, dl)[0]))
    rep_offs, n_rep = _row_layout([P[n].shape for n in _SMALL_REPLICATED], D)
    rep_off = {n: n_sh + o for n, o in zip(_SMALL_REPLICATED, rep_offs)}
    pieces = []
    for i in range(depth):
        pieces += [(g, sh_off["mix_norm"] + 2 * i + j) for j, g in enumerate(g_mix[i])]
        pieces += [(g, sh_off["xa_norm"] + 3 * i + j) for j, g in enumerate(g_xa[i])]
        pieces += [(g, sh_off["ffn_norm"] + 2 * i + j) for j, g in enumerate(g_ffn[i])]
    pieces += [(g, sh_off["a_conv_w"] + a_conv_w.shape[1] * s) for s, g in g_small["a_conv_w"].items()]
    pieces += [(g_small[n], sh_off[n]) for n in ("c_conv_w", "c_conv_b", "c_ln_g", "c_ln_b")]
    pieces += [(g_small[n], rep_off[n]) for n in _SMALL_REPLICATED]
    part_all = _assemble_rows(pieces, n_sh + n_rep, D, "pack_small_grads")
    parts_all = _all_gather([part_all], "ag_small_grads")[0]
    recv_sh = lax.dynamic_slice_in_dim(parts_all[:, :n_sh], my_slot * dl, dl, axis=2)
    recv_rep = parts_all[:, n_sh:]

    out = {}

    def adam_small(names, recv_s, width, name):
        shapes = [P[n].shape for n in names]
        pw = _pack_rows([P[n] for n in names], width, 0.0)
        pm = _pack_rows([P["m_" + n] for n in names], width, 0.0)
        pv = _pack_rows([P["v_" + n] for n in names], width, 1.0)
        res = _reduce_adam([recv_s], pw[None], pm[None], pv[None], name)
        for kind, r in zip(("grad", "delta", "new_m", "new_v"), res):
            for n, a in zip(names, _unpack_rows(r[0], shapes)):
                out[kind + "_" + n] = a

    adam_small(_SMALL_SHARDED, recv_sh, dl, "adam_small_sharded")
    adam_small(_SMALL_REPLICATED, recv_rep, D, "adam_small_replicated")
    def lands_after(n):
        return max([0] + [k + 1 for k, entry in enumerate(bwd["queue"]) if any(m == n for m, _, _ in entry[0])])

    early_done = [g_xa[i][2] for i in range(depth)]
    for n in sorted(_BIG, key=lands_after):
        while any(r is None for r in recv[n]):
            finish_scatter(early_done, len(bwd["queue"]) - 1)
        res = _reduce_adam(recv[n], P[n], P["m_" + n], P["v_" + n], "adam_" + n, dep=scatter_token())
        early_done.append(res[0])
        for kind, r in zip(("grad", "delta", "new_m", "new_v"), res):
            out[kind + "_" + n] = r

    return (loss, grad_x, *[out[k + "_" + n] for k in ("grad", "delta", "new_m", "new_v") for n in _WEIGHTS])
```

```python
import functools

import jax
import jax.numpy as jnp
from jax import lax
from jax.experimental import pallas as pl
from jax.experimental.pallas import tpu as pltpu

F32 = jnp.float32
BF16 = jnp.bfloat16
MESH = pl.DeviceIdType.MESH
ANY = pl.BlockSpec(memory_space=pl.ANY)

N_DEV = 8
N_MIXERS = 3
XA_HEADS = 4
GMLP_GROUPS = 8
CHUNK = 128
NORM_EPS = 1e-6
HALO = 32
ROW_TILE = 256
CONV_LANES = 512
V7X_VMEM_LIMIT = 56 * 1024 * 1024

ADAM_LR = 0.001
ADAM_B1 = 0.9
ADAM_B2 = 0.999
ADAM_EPS = 1e-08
ADAM_WD = 0.01
ADAM_STEP = 10


def _pcall(body, **kw):
    return pl.pallas_call(body, **kw)


def _params(*sem):
    return pltpu.CompilerParams(dimension_semantics=sem, vmem_limit_bytes=V7X_VMEM_LIMIT)


def _fit(n, pref, mult=128):
    if n <= pref:
        return n
    t = (pref // mult) * mult
    while t >= mult:
        if n % t == 0:
            return t
        t -= mult
    return n


def _sds(shape, dtype):
    return jax.ShapeDtypeStruct(shape, dtype)


_DIMS = {"nn": (((1,), (0,)), ((), ())), "nt": (((1,), (1,)), ((), ())), "tn": (((0,), (0,)), ((), ()))}
MM_VMEM_BUDGET = 44 * 1024 * 1024


def _gcd(a, b):
    while b:
        a, b = b, a % b
    return a


def _mm(a, b, mode, out_dtype, name, *, a_blocked=False, b_blocked=False, out_blocks=None, dep=None):
    if mode == "tn":
        K, M = a.shape
    elif a_blocked:
        sa, M, ca = a.shape
        K = sa * ca
    else:
        M, K = a.shape
    n_unit = k_unit = None
    if b_blocked:
        _, d1, cb = b.shape
        if mode == "nt":
            N, k_unit = d1, cb
        else:
            N, n_unit = b.shape[0] * cb, cb
    else:
        N = b.shape[0] if mode == "nt" else b.shape[1]
    n_unit = n_unit or N
    k_unit = k_unit or K
    if a_blocked:
        k_unit = _gcd(k_unit, ca)
    if out_blocks:
        n_unit = _gcd(n_unit, N // out_blocks)
    tn = _fit(n_unit, 1536)
    tk = k_unit
    out_bytes = jnp.dtype(out_dtype).itemsize

    def need(tm_, gk_=1):
        nk_ = K // (tk * gk_)
        return (4 * gk_ * (tm_ * tk + tk * tn) + 2 * tm_ * tn * out_bytes
                + 4 * tm_ * tn * ((2 if nk_ > 1 else 1) + (1 if gk_ > 1 else 0)))

    tm = _fit(M, 1024)
    gk = 1
    if b_blocked and mode == "nt" and tk == cb:
        for cand in (8, 4, 2):
            if b.shape[0] % cand == 0 and (not a_blocked or (ca // tk) % cand == 0) and need(tm, cand) <= MM_VMEM_BUDGET:
                gk = cand
                break
    while need(tm, gk) > MM_VMEM_BUDGET and tm % 256 == 0:
        tm //= 2
    nk = K // (tk * gk)

    if mode == "tn":
        a_spec = pl.BlockSpec((tk, tm), lambda i, j, k: (k, i))
    elif a_blocked:
        ka = ca // (tk * gk)
        a_spec = pl.BlockSpec((None, tm, tk * gk), lambda i, j, k: (k // ka, i, k % ka))
    else:
        a_spec = pl.BlockSpec((tm, tk * gk), lambda i, j, k: (i, k))
    if b_blocked and mode == "nt" and gk > 1:
        b_spec = pl.BlockSpec((gk, tn, tk), lambda i, j, k: (k, j, 0))
    elif b_blocked and mode == "nt":
        kb = cb // tk
        b_spec = pl.BlockSpec((None, tn, tk), lambda i, j, k: (k // kb, j, k % kb))
    elif b_blocked:
        nb = cb // tn
        b_spec = pl.BlockSpec((None, tk, tn), lambda i, j, k: (j // nb, k, j % nb))
    elif mode == "nt":
        b_spec = pl.BlockSpec((tn, tk), lambda i, j, k: (j, k))
    else:
        b_spec = pl.BlockSpec((tk, tn), lambda i, j, k: (k, j))
    if out_blocks:
        ob = (N // out_blocks) // tn
        out_shape = _sds((out_blocks, M, N // out_blocks), out_dtype)
        o_spec = pl.BlockSpec((None, tm, tn), lambda i, j, k: (j // ob, i, j % ob))
    else:
        out_shape = _sds((M, N), out_dtype)
        o_spec = pl.BlockSpec((tm, tn), lambda i, j, k: (i, j))
    dims = _DIMS[mode]
    n_in = 2 if dep is None else 3

    def body(*refs):
        a_ref, b_ref = refs[0], refs[1]
        o_ref = refs[n_in]
        if gk == 1:
            p = lax.dot_general(a_ref[...], b_ref[...], dims, preferred_element_type=F32)
        else:
            p = lax.dot_general(a_ref[:, 0:tk], b_ref[0], dims, preferred_element_type=F32)
            for s in range(1, gk):
                p = p + lax.dot_general(a_ref[:, s * tk:(s + 1) * tk], b_ref[s], dims, preferred_element_type=F32)
        if nk == 1:
            o_ref[...] = p.astype(o_ref.dtype)
            return
        acc_ref = refs[n_in + 1]
        k = pl.program_id(2)

        @pl.when(k == 0)
        def _():
            acc_ref[...] = p

        @pl.when(k > 0)
        def _():
            acc_ref[...] += p

        @pl.when(k == nk - 1)
        def _():
            o_ref[...] = acc_ref[...].astype(o_ref.dtype)

    ins, in_specs = [a, b], [a_spec, b_spec]
    if dep is not None:
        ins.append(dep)
        in_specs.append(ANY)
    return _pcall(
        body, name=name, out_shape=out_shape, grid=(M // tm, N // tn, nk),
        in_specs=in_specs, out_specs=o_spec,
        scratch_shapes=[pltpu.VMEM((tm, tn), F32)] if nk > 1 else [],
        compiler_params=_params("parallel", "parallel", "arbitrary"),
    )(*ins)


def _rstd(v):
    return lax.rsqrt(jnp.mean(v * v, axis=-1, keepdims=True) + NORM_EPS)


STRIP = 16


def _strip_rows(s):
    return pl.ds(pl.multiple_of(s * STRIP, STRIP), STRIP)


def _rms_bwd_strip(load_v, load_dout, g, acc_ref):
    r = _rstd(load_v())
    m = jnp.mean(load_dout() * g * (load_v() * r), axis=-1, keepdims=True)
    acc_ref[...] += load_dout() * (load_v() * r)
    return r * (load_dout() * g - load_v() * r * m)


def _row(tt, d, col=0):
    return pl.BlockSpec((tt, d), lambda i: (i, col))


def _const(shape):
    return pl.BlockSpec(shape, lambda i: (0,) * len(shape))


def _prev_halo(tt, d, col=0):
    return pl.BlockSpec((HALO, d), lambda i: (jnp.maximum(i * (tt // HALO) - 1, 0), col))


def _next_halo(tt, d, rows, col=0):
    last = rows // HALO - 1
    return pl.BlockSpec((HALO, d), lambda i: (jnp.minimum((i + 1) * (tt // HALO), last), col))


def _rms_fwd(x, g, name):
    T, D = x.shape
    tt = _fit(T, ROW_TILE, 8)

    def body(x_ref, g_ref, h_ref):
        v = x_ref[...]
        h_ref[...] = (v * _rstd(v) * g_ref[...]).astype(BF16)

    return _pcall(body, name=name, out_shape=_sds((T, D), BF16), grid=(T // tt,),
                  in_specs=[_row(tt, D), _const((1, D))], out_specs=_row(tt, D),
                  compiler_params=_params("parallel"))(x, g)


def _post_pre_fwd(x, y, g_post, g_pre, name):
    T, D = x.shape
    tt = _fit(T, ROW_TILE, 8)

    def body(x_ref, y_ref, gp_ref, gn_ref, xo_ref, h_ref):
        def strip(s, carry):
            rows = _strip_rows(s)
            r = _rstd(y_ref[rows, :].astype(F32))
            xn = x_ref[rows, :] + y_ref[rows, :].astype(F32) * r * gp_ref[...]
            xo_ref[rows, :] = xn
            r2 = _rstd(xn)
            h_ref[rows, :] = (xo_ref[rows, :] * r2 * gn_ref[...]).astype(BF16)
            return carry

        lax.fori_loop(0, tt // STRIP, strip, 0)

    return _pcall(body, name=name, out_shape=(_sds((T, D), F32), _sds((T, D), BF16)), grid=(T // tt,),
                  in_specs=[_row(tt, D), _row(tt, D), _const((1, D)), _const((1, D))],
                  out_specs=(_row(tt, D), _row(tt, D)),
                  compiler_params=_params("parallel"))(x, y, g_post, g_pre)


def _final_fwd_loss(x, y, g_post, target, name):
    T, D = x.shape
    tt = _fit(T, ROW_TILE, 8)

    def body(x_ref, y_ref, g_ref, t_ref, loss_ref, dx_ref, dy_ref, dg_ref, acc_ref):
        acc_ref[...] = jnp.zeros_like(acc_ref)

        def strip(s, part):
            rows = _strip_rows(s)
            load_y = lambda: y_ref[rows, :].astype(F32)
            err = x_ref[rows, :] + load_y() * _rstd(load_y()) * g_ref[...] - t_ref[rows, :]
            dx_ref[rows, :] = err / D
            part = part + 0.5 * jnp.sum(jnp.mean(err * err, axis=-1, keepdims=True))
            dy = _rms_bwd_strip(load_y, lambda: dx_ref[rows, :], g_ref[...], acc_ref)
            dy_ref[rows, :] = dy.astype(BF16)
            return part

        part = lax.fori_loop(0, tt // STRIP, strip, jnp.zeros((), F32))

        @pl.when(pl.program_id(0) == 0)
        def _():
            loss_ref[...] = jnp.zeros_like(loss_ref)
            dg_ref[...] = jnp.zeros_like(dg_ref)

        loss_ref[...] += part
        dg_ref[...] += jnp.sum(acc_ref[...], axis=0, keepdims=True)

    return _pcall(body, name=name,
                  out_shape=(_sds((1, 128), F32), _sds((T, D), F32), _sds((T, D), BF16), _sds((1, D), F32)),
                  grid=(T // tt,),
                  in_specs=[_row(tt, D), _row(tt, D), _const((1, D)), _row(tt, D)],
                  out_specs=(_const((1, 128)), _row(tt, D), _row(tt, D), _const((1, D))),
                  scratch_shapes=[pltpu.VMEM((STRIP, D), F32)],
                  compiler_params=_params("arbitrary"))(x, y, g_post, target)


def _pre_post_bwd(dx_out, dh, x_in, g_pre, y_prev, g_post_prev, name):
    T, D = x_in.shape
    tt = _fit(T, ROW_TILE, 8)
    with_prev = y_prev is not None

    def body(*refs):
        if with_prev:
            dxo_ref, dh_ref, x_ref, g_ref, y_ref, gp_ref, dxi_ref, dg_ref, dy_ref, dgp_ref, acc_ref, accp_ref = refs
            accp_ref[...] = jnp.zeros_like(accp_ref)
        else:
            dxo_ref, dh_ref, x_ref, g_ref, dxi_ref, dg_ref, acc_ref = refs
        acc_ref[...] = jnp.zeros_like(acc_ref)

        def strip(s, carry):
            rows = _strip_rows(s)
            dv = _rms_bwd_strip(lambda: x_ref[rows, :], lambda: dh_ref[rows, :].astype(F32), g_ref[...], acc_ref)
            dxi_ref[rows, :] = dxo_ref[rows, :] + dv
            if with_prev:
                dy = _rms_bwd_strip(lambda: y_ref[rows, :].astype(F32), lambda: dxi_ref[rows, :], gp_ref[...], accp_ref)
                dy_ref[rows, :] = dy.astype(BF16)
            return carry

        lax.fori_loop(0, tt // STRIP, strip, 0)

        @pl.when(pl.program_id(0) == 0)
        def _():
            dg_ref[...] = jnp.zeros_like(dg_ref)
            if with_prev:
                dgp_ref[...] = jnp.zeros_like(dgp_ref)

        dg_ref[...] += jnp.sum(acc_ref[...], axis=0, keepdims=True)
        if with_prev:
            dgp_ref[...] += jnp.sum(accp_ref[...], axis=0, keepdims=True)

    ins = [dx_out, dh, x_in, g_pre]
    in_specs = [_row(tt, D), _row(tt, D), _row(tt, D), _const((1, D))]
    out_shape = [_sds((T, D), F32), _sds((1, D), F32)]
    out_specs = [_row(tt, D), _const((1, D))]
    scratch = [pltpu.VMEM((STRIP, D), F32)]
    if with_prev:
        ins += [y_prev, g_post_prev]
        in_specs += [_row(tt, D), _const((1, D))]
        out_shape += [_sds((T, D), BF16), _sds((1, D), F32)]
        out_specs += [_row(tt, D), _const((1, D))]
        scratch += [pltpu.VMEM((STRIP, D), F32)]
    return _pcall(body, name=name, out_shape=tuple(out_shape), grid=(T // tt,),
                  in_specs=in_specs, out_specs=tuple(out_specs), scratch_shapes=scratch,
                  compiler_params=_params("arbitrary"))(*ins)


def _rms_gain_grad(dout, v, name):
    T, D = v.shape
    tt = _fit(T, ROW_TILE, 8)

    def body(d_ref, v_ref, dg_ref):
        @pl.when(pl.program_id(0) == 0)
        def _():
            dg_ref[...] = jnp.zeros_like(dg_ref)

        v = v_ref[...]
        dg_ref[...] += jnp.sum(d_ref[...] * (v * _rstd(v)), axis=0, keepdims=True)

    return _pcall(body, name=name, out_shape=_sds((1, D), F32), grid=(T // tt,),
                  in_specs=[_row(tt, D), _row(tt, D)], out_specs=_const((1, D)),
                  compiler_params=_params("arbitrary"))(dout, v)


def _softmax_rows(s):
    e = jnp.exp(s - jnp.max(s, axis=-1, keepdims=True))
    return e / jnp.sum(e, axis=-1, keepdims=True)


def _attn_fwd(q, kv, name):
    T, D = q.shape
    nm = kv.shape[0]
    hd = D // XA_HEADS
    scale = hd ** -0.5
    tq = _fit(T, ROW_TILE, 8)

    def body(q_ref, k_ref, v_ref, o_ref):
        for h in range(XA_HEADS):
            sl = slice(h * hd, (h + 1) * hd)
            s = lax.dot_general(q_ref[:, sl], k_ref[:, sl], _DIMS["nt"], preferred_element_type=F32) * scale
            p = _softmax_rows(s)
            o_ref[:, sl] = jnp.dot(p.astype(BF16), v_ref[:, sl], preferred_element_type=F32).astype(BF16)

    return _pcall(body, name=name, out_shape=_sds((T, D), BF16), grid=(T // tq,),
                  in_specs=[_row(tq, D), pl.BlockSpec((nm, D), lambda i: (0, 0)), pl.BlockSpec((nm, D), lambda i: (0, 1))],
                  out_specs=_row(tq, D), compiler_params=_params("parallel"))(q, kv, kv)


def _attn_bwd(q, kv, do, name):
    T, D = q.shape
    nm = kv.shape[0]
    hd = D // XA_HEADS
    scale = hd ** -0.5
    tq = _fit(T, ROW_TILE, 8)

    def body(q_ref, k_ref, v_ref, do_ref, dq_ref, dkv_ref):
        @pl.when(pl.program_id(0) == 0)
        def _():
            dkv_ref[...] = jnp.zeros_like(dkv_ref)

        for h in range(XA_HEADS):
            sl = slice(h * hd, (h + 1) * hd)
            qh, kh, vh, doh = q_ref[:, sl], k_ref[:, sl], v_ref[:, sl], do_ref[:, sl]
            s = lax.dot_general(qh, kh, _DIMS["nt"], preferred_element_type=F32) * scale
            p = _softmax_rows(s)
            dp = lax.dot_general(doh, vh, _DIMS["nt"], preferred_element_type=F32)
            ds = (p * (dp - jnp.sum(dp * p, axis=-1, keepdims=True)) * scale).astype(BF16)
            dq_ref[:, sl] = jnp.dot(ds, kh, preferred_element_type=F32).astype(BF16)
            dkv_ref[:, sl] += lax.dot_general(ds, qh, _DIMS["tn"], preferred_element_type=F32)
            dkv_ref[:, D + h * hd:D + (h + 1) * hd] += lax.dot_general(
                p.astype(BF16), doh, _DIMS["tn"], preferred_element_type=F32)

    return _pcall(body, name=name, out_shape=(_sds((T, D), BF16), _sds((nm, 2 * D), F32)), grid=(T // tq,),
                  in_specs=[_row(tq, D), pl.BlockSpec((nm, D), lambda i: (0, 0)), pl.BlockSpec((nm, D), lambda i: (0, 1)),
                            _row(tq, D)],
                  out_specs=(_row(tq, D), _const((nm, 2 * D))),
                  compiler_params=_params("arbitrary"))(q, kv, kv, do)


def _ffn_gu_fwd(h, w_gu, name, dep=None, tm=512):
    T, D = h.shape
    S, _, c = w_gu.shape
    F = S * c // 2
    tm = _fit(T, tm)
    tn = _fit(c, 1536)
    nb = c // tn
    nj = F // tn
    n_in = 3 if dep is None else 4

    def w_spec(off):
        return pl.BlockSpec((None, D, tn), lambda i, j: ((j + off) // nb, 0, (j + off) % nb))

    def body(*refs):
        h_ref, wg_ref, wu_ref = refs[:3]
        dg_ref, du_ref, a_ref = refs[n_in:]
        hv = h_ref[...]
        g = jnp.dot(hv, wg_ref[...], preferred_element_type=F32)
        sg = jax.nn.sigmoid(g)
        silu = g * sg
        du_ref[...] = silu.astype(BF16)
        u = jnp.dot(hv, wu_ref[...], preferred_element_type=F32)
        dg_ref[...] = (u * (sg + silu * (1.0 - sg))).astype(BF16)
        a_ref[...] = (silu * u).astype(BF16)

    ins = [h, w_gu, w_gu]
    in_specs = [pl.BlockSpec((tm, D), lambda i, j: (i, 0)), w_spec(0), w_spec(nj)]
    if dep is not None:
        ins.append(dep)
        in_specs.append(ANY)
    o_spec = pl.BlockSpec((tm, tn), lambda i, j: (i, j))
    return _pcall(body, name=name, out_shape=(_sds((T, F), BF16),) * 3, grid=(T // tm, nj),
                  in_specs=in_specs, out_specs=(o_spec,) * 3,
                  compiler_params=_params("parallel", "parallel"))(*ins)


def _ffn_dgu_bwd(dy, w_down, dact_dgate, dact_dup, name, dep=None, tm=1024):
    T, D = dy.shape
    F = w_down.shape[0]
    tm = _fit(T, tm)
    tn = _fit(F, 512)
    cn = _fit(tn, 256)
    n_in = 4 if dep is None else 5

    def body(*refs):
        dy_ref, wd_ref, g_ref, u_ref = refs[:4]
        o_ref = refs[n_in]
        dyv = dy_ref[...]
        for n0 in range(0, tn, cn):
            da = lax.dot_general(dyv, wd_ref[n0:n0 + cn, :], _DIMS["nt"], preferred_element_type=F32)
            o_ref[0, :, n0:n0 + cn] = (da * g_ref[:, n0:n0 + cn].astype(F32)).astype(BF16)
            o_ref[1, :, n0:n0 + cn] = (da * u_ref[:, n0:n0 + cn].astype(F32)).astype(BF16)

    ins = [dy, w_down, dact_dgate, dact_dup]
    gu_spec = pl.BlockSpec((tm, tn), lambda i, j: (i, j))
    in_specs = [pl.BlockSpec((tm, D), lambda i, j: (i, 0)), pl.BlockSpec((tn, D), lambda i, j: (j, 0)), gu_spec, gu_spec]
    if dep is not None:
        ins.append(dep)
        in_specs.append(ANY)
    return _pcall(body, name=name, out_shape=_sds((2, T, F), BF16), grid=(T // tm, F // tn),
                  in_specs=in_specs, out_specs=pl.BlockSpec((2, tm, tn), lambda i, j: (0, i, j)),
                  compiler_params=_params("parallel", "parallel"))(*ins)


def _causal_taps(win, width, tt):
    for b in range(min(8, width)):
        wb = win if b == 0 else pltpu.roll(win, b, 0)
        a = 0
        while 8 * a + b <= width - 1:
            yield width - 1 - (8 * a + b), wb[HALO - 8 * a:HALO - 8 * a + tt]
            a += 1


def _anticausal_taps(win, width, tt):
    rows = tt + HALO
    for b in range(min(8, width)):
        wb = win if b == 0 else pltpu.roll(win, rows - b, 0)
        a = 0
        while 8 * a + b <= width - 1:
            yield width - 1 - (8 * a + b), wb[8 * a:8 * a + tt]
            a += 1


def _lanes(d):
    cw = _fit(d, CONV_LANES)
    return [slice(s, s + cw) for s in range(0, d, cw)], cw


def _a_mid_fwd(bcz, conv_w, name):
    T, D3 = bcz.shape
    D = D3 // 3
    width = conv_w.shape[0]
    tt = _fit(T, ROW_TILE, HALO)
    chunks, cw = _lanes(D)

    def body(b_ref, c_ref, z_ref, ch_ref, zh_ref, w_ref, o_ref, win_ref):
        i = pl.program_id(0)
        for sl in chunks:
            uh = ch_ref[:, sl].astype(F32) * zh_ref[:, sl].astype(F32)
            win_ref[0:HALO, :] = jnp.where(i > 0, uh, 0.0)
            win_ref[HALO:, :] = c_ref[:, sl].astype(F32) * z_ref[:, sl].astype(F32)
            acc = jnp.zeros((tt, cw), F32)
            for k, xs in _causal_taps(win_ref[...], width, tt):
                acc = acc + w_ref[k:k + 1, sl] * xs
            o_ref[:, sl] = (b_ref[:, sl].astype(F32) * acc).astype(BF16)

    return _pcall(body, name=name, out_shape=_sds((T, D), BF16), grid=(T // tt,),
                  in_specs=[_row(tt, D, 0), _row(tt, D, 1), _row(tt, D, 2), _prev_halo(tt, D, 1), _prev_halo(tt, D, 2),
                            _const((width, D))],
                  out_specs=_row(tt, D), scratch_shapes=[pltpu.VMEM((HALO + tt, cw), F32)],
                  compiler_params=_params("parallel"))(bcz, bcz, bcz, bcz, bcz, conv_w)


def _a_mid_bwd(bcz, dy2, conv_w, name):
    T, D3 = bcz.shape
    D = D3 // 3
    width = conv_w.shape[0]
    tt = _fit(T, ROW_TILE, HALO)
    chunks, cw = _lanes(D)
    n_tiles = T // tt

    def body(b_ref, c_ref, z_ref, ch_ref, zh_ref, bn_ref, d_ref, dn_ref, w_ref, o_ref, dw_ref, win_ref, dwin_ref):
        i = pl.program_id(0)

        @pl.when(i == 0)
        def _():
            dw_ref[...] = jnp.zeros_like(dw_ref)

        for ci, sl in enumerate(chunks):
            c = c_ref[:, sl].astype(F32)
            z = z_ref[:, sl].astype(F32)
            b = b_ref[:, sl].astype(F32)
            d2 = d_ref[:, sl].astype(F32)
            uh = ch_ref[:, sl].astype(F32) * zh_ref[:, sl].astype(F32)
            win_ref[0:HALO, :] = jnp.where(i > 0, uh, 0.0)
            win_ref[HALO:, :] = c * z
            d1 = d2 * b
            d1n = dn_ref[:, sl].astype(F32) * bn_ref[:, sl].astype(F32)
            dwin_ref[0:tt, :] = d1
            dwin_ref[tt:, :] = jnp.where(i < n_tiles - 1, d1n, 0.0)
            y1 = jnp.zeros((tt, cw), F32)
            for k, xs in _causal_taps(win_ref[...], width, tt):
                y1 = y1 + w_ref[k:k + 1, sl] * xs
                dw_ref[k:k + 1, sl] += jnp.sum(d1 * xs, axis=0, keepdims=True)
            du = jnp.zeros((tt, cw), F32)
            for k, xs in _anticausal_taps(dwin_ref[...], width, tt):
                du = du + w_ref[k:k + 1, sl] * xs
            o_ref[:, ci * cw:(ci + 1) * cw] = (d2 * y1).astype(BF16)
            o_ref[:, D + ci * cw:D + (ci + 1) * cw] = (du * z).astype(BF16)
            o_ref[:, 2 * D + ci * cw:2 * D + (ci + 1) * cw] = (du * c).astype(BF16)

    return _pcall(body, name=name, out_shape=(_sds((T, 3 * D), BF16), _sds((width, D), F32)), grid=(n_tiles,),
                  in_specs=[_row(tt, D, 0), _row(tt, D, 1), _row(tt, D, 2), _prev_halo(tt, D, 1), _prev_halo(tt, D, 2),
                            _next_halo(tt, D, T, 0), _row(tt, D), _next_halo(tt, D, T), _const((width, D))],
                  out_specs=(_row(tt, 3 * D), _const((width, D))),
                  scratch_shapes=[pltpu.VMEM((HALO + tt, cw), F32), pltpu.VMEM((tt + HALO, cw), F32)],
                  compiler_params=_params("arbitrary"))(bcz, bcz, bcz, bcz, bcz, bcz, dy2, dy2, conv_w)


_GELU_C = 0.7978845608028654
_GELU_A = 0.044715


def _gelu(v):
    return 0.5 * v * (1.0 + jnp.tanh(_GELU_C * (v + _GELU_A * v * v * v)))


def _gelu_grad(v):
    t = jnp.tanh(_GELU_C * (v + _GELU_A * v * v * v))
    return 0.5 * (1.0 + t) + 0.5 * v * (1.0 - t * t) * (_GELU_C * (1.0 + 3.0 * _GELU_A * v * v))


def _ln_stats(v):
    mu = jnp.mean(v, axis=-1, keepdims=True)
    vc = v - mu
    return vc * lax.rsqrt(jnp.mean(vc * vc, axis=-1, keepdims=True) + NORM_EPS)


def _tril(n):
    return lax.broadcasted_iota(jnp.int32, (n, n), 0) >= lax.broadcasted_iota(jnp.int32, (n, n), 1)


def _b_mid_fwd(uv, v_g, v_b, w_s, bias_b, name):
    T, D2 = uv.shape
    D = D2 // 2
    G, C, _ = w_s.shape
    gd = D // G
    tt = _fit(T, ROW_TILE, C)

    def body(u_ref, v_ref, g_ref, b_ref, ws_ref, bias_ref, o_ref, vln_ref):
        vln_ref[...] = (_ln_stats(_gelu(v_ref[...].astype(F32))) * g_ref[...] + b_ref[...]).astype(BF16)
        mask = _tril(C)
        for g in range(G):
            wsm = jnp.where(mask, ws_ref[g], 0.0).astype(BF16)
            cs = slice(g * gd, (g + 1) * gd)
            for n in range(tt // C):
                rs = slice(n * C, (n + 1) * C)
                sv = jnp.dot(wsm, vln_ref[rs, cs], preferred_element_type=F32) + bias_ref[g]
                o_ref[rs, cs] = (_gelu(u_ref[rs, cs].astype(F32)) * sv).astype(BF16)

    return _pcall(body, name=name, out_shape=_sds((T, D), BF16), grid=(T // tt,),
                  in_specs=[_row(tt, D, 0), _row(tt, D, 1), _const((1, D)), _const((1, D)), _const((G, C, C)),
                            _const((G, C, gd))],
                  out_specs=_row(tt, D), scratch_shapes=[pltpu.VMEM((tt, D), BF16)],
                  compiler_params=_params("parallel"))(uv, uv, v_g, v_b, w_s, bias_b)


def _b_mid_bwd(uv, dgated, v_g, v_b, w_s, bias_b, name):
    T, D2 = uv.shape
    D = D2 // 2
    G, C, _ = w_s.shape
    gd = D // G
    tt = _fit(T, ROW_TILE, C)

    def body(u_ref, v_ref, d_ref, g_ref, b_ref, ws_ref, bias_ref, o_ref, dws_ref, dsb_ref, dvg_ref, dvb_ref,
             vln_ref, dvln_ref):
        @pl.when(pl.program_id(0) == 0)
        def _():
            dws_ref[...] = jnp.zeros_like(dws_ref)
            dsb_ref[...] = jnp.zeros_like(dsb_ref)
            dvg_ref[...] = jnp.zeros_like(dvg_ref)
            dvb_ref[...] = jnp.zeros_like(dvb_ref)

        vpre = v_ref[...].astype(F32)
        vhat = _ln_stats(_gelu(vpre))
        vln_ref[...] = (vhat * g_ref[...] + b_ref[...]).astype(BF16)
        mask = _tril(C)
        lane = lax.broadcasted_iota(jnp.int32, (C, 128), 1)
        for g in range(G):
            wsm = jnp.where(mask, ws_ref[g], 0.0).astype(BF16)
            cs = slice(g * gd, (g + 1) * gd)
            for n in range(tt // C):
                rs = slice(n * C, (n + 1) * C)
                vt = vln_ref[rs, cs]
                sv = jnp.dot(wsm, vt, preferred_element_type=F32) + bias_ref[g]
                dg = d_ref[rs, cs].astype(F32)
                upre = u_ref[rs, cs].astype(F32)
                o_ref[rs, cs] = (dg * sv * _gelu_grad(upre)).astype(BF16)
                dsv = dg * _gelu(upre)
                dsb_ref[...] += jnp.where(lane == g, jnp.sum(dsv, axis=-1, keepdims=True), 0.0)
                dsv16 = dsv.astype(BF16)
                dws_ref[g] += jnp.where(mask, lax.dot_general(dsv16, vt, _DIMS["nt"], preferred_element_type=F32), 0.0)
                dvln_ref[rs, cs] = lax.dot_general(wsm, dsv16, _DIMS["tn"], preferred_element_type=F32)
        dvln = dvln_ref[...]
        dvg_ref[...] += jnp.sum(dvln * vhat, axis=0, keepdims=True)
        dvb_ref[...] += jnp.sum(dvln, axis=0, keepdims=True)
        dvh = dvln * g_ref[...]
        vc = _gelu(vpre)
        vc = vc - jnp.mean(vc, axis=-1, keepdims=True)
        rstd = lax.rsqrt(jnp.mean(vc * vc, axis=-1, keepdims=True) + NORM_EPS)
        dv = rstd * (dvh - jnp.mean(dvh, axis=-1, keepdims=True) - vhat * jnp.mean(dvh * vhat, axis=-1, keepdims=True))
        o_ref[:, D:] = (dv * _gelu_grad(vpre)).astype(BF16)

    return _pcall(body, name=name,
                  out_shape=(_sds((T, 2 * D), BF16), _sds((G, C, C), F32), _sds((C, 128), F32), _sds((1, D), F32),
                             _sds((1, D), F32)),
                  grid=(T // tt,),
                  in_specs=[_row(tt, D, 0), _row(tt, D, 1), _row(tt, D), _const((1, D)), _const((1, D)),
                            _const((G, C, C)), _const((G, C, gd))],
                  out_specs=(_row(tt, 2 * D), _const((G, C, C)), _const((C, 128)), _const((1, D)), _const((1, D))),
                  scratch_shapes=[pltpu.VMEM((tt, D), BF16), pltpu.VMEM((tt, D), F32)],
                  compiler_params=_params("arbitrary"))(uv, uv, dgated, v_g, v_b, w_s, bias_b)


def _c_mid_fwd(ag, conv_w, conv_b, ln_g, ln_b, name):
    T, D2 = ag.shape
    D = D2 // 2
    width = conv_w.shape[0]
    tt = _fit(T, ROW_TILE, HALO)
    chunks, cw = _lanes(D)

    def body(a_ref, g_ref, ah_ref, gh_ref, w_ref, cb_ref, lg_ref, lb_ref, y2_ref, o_ref, win_ref):
        i = pl.program_id(0)
        for sl in chunks:
            yh = ah_ref[:, sl].astype(F32) * jax.nn.sigmoid(gh_ref[:, sl].astype(F32))
            win_ref[0:HALO, :] = jnp.where(i > 0, yh, 0.0)
            win_ref[HALO:, :] = a_ref[:, sl].astype(F32) * jax.nn.sigmoid(g_ref[:, sl].astype(F32))
            acc = jnp.zeros((tt, cw), F32)
            for k, xs in _causal_taps(win_ref[...], width, tt):
                acc = acc + w_ref[k:k + 1, sl] * xs
            y2_ref[:, sl] = acc + cb_ref[:, sl]
        y3 = _ln_stats(y2_ref[...]) * lg_ref[...] + lb_ref[...]
        o_ref[...] = (y3 * jax.nn.sigmoid(y3)).astype(BF16)

    return _pcall(body, name=name, out_shape=(_sds((T, D), F32), _sds((T, D), BF16)), grid=(T // tt,),
                  in_specs=[_row(tt, D, 0), _row(tt, D, 1), _prev_halo(tt, D, 0), _prev_halo(tt, D, 1),
                            _const((width, D)), _const((1, D)), _const((1, D)), _const((1, D))],
                  out_specs=(_row(tt, D), _row(tt, D)), scratch_shapes=[pltpu.VMEM((HALO + tt, cw), F32)],
                  compiler_params=_params("parallel"))(ag, ag, ag, ag, conv_w, conv_b, ln_g, ln_b)


def _c_mid_bwd(ag, y2, dy4, conv_w, ln_g, ln_b, name):
    T, D2 = ag.shape
    D = D2 // 2
    width = conv_w.shape[0]
    tt = _fit(T, ROW_TILE, HALO)
    chunks, cw = _lanes(D)
    n_tiles = T // tt

    def ln_silu_bwd(y2v, dy4v, lg, lb):
        mu = jnp.mean(y2v, axis=-1, keepdims=True)
        yc = y2v - mu
        rstd = lax.rsqrt(jnp.mean(yc * yc, axis=-1, keepdims=True) + NORM_EPS)
        yh = yc * rstd
        y3 = yh * lg + lb
        sg = jax.nn.sigmoid(y3)
        dy3 = dy4v * (sg * (1.0 + y3 * (1.0 - sg)))
        dyh = dy3 * lg
        dy2 = rstd * (dyh - jnp.mean(dyh, axis=-1, keepdims=True) - yh * jnp.mean(dyh * yh, axis=-1, keepdims=True))
        return dy2, dy3, yh

    def body(a_ref, g_ref, ah_ref, gh_ref, y2_ref, y2n_ref, d_ref, dn_ref, w_ref, lg_ref, lb_ref,
             o_ref, dw_ref, dcb_ref, dlg_ref, dlb_ref, win_ref, dwin_ref):
        i = pl.program_id(0)

        @pl.when(i == 0)
        def _():
            dw_ref[...] = jnp.zeros_like(dw_ref)
            dcb_ref[...] = jnp.zeros_like(dcb_ref)
            dlg_ref[...] = jnp.zeros_like(dlg_ref)
            dlb_ref[...] = jnp.zeros_like(dlb_ref)

        lg = lg_ref[...]
        lb = lb_ref[...]
        dy2, dy3, yh = ln_silu_bwd(y2_ref[...], d_ref[...].astype(F32), lg, lb)
        dlg_ref[...] += jnp.sum(dy3 * yh, axis=0, keepdims=True)
        dlb_ref[...] += jnp.sum(dy3, axis=0, keepdims=True)
        dcb_ref[...] += jnp.sum(dy2, axis=0, keepdims=True)
        dwin_ref[0:tt, :] = dy2
        dy2n, _, _ = ln_silu_bwd(y2n_ref[...], dn_ref[...].astype(F32), lg, lb)
        dwin_ref[tt:, :] = jnp.where(i < n_tiles - 1, dy2n, 0.0)
        for ci, sl in enumerate(chunks):
            a = a_ref[:, sl].astype(F32)
            sg = jax.nn.sigmoid(g_ref[:, sl].astype(F32))
            yh1 = ah_ref[:, sl].astype(F32) * jax.nn.sigmoid(gh_ref[:, sl].astype(F32))
            win_ref[0:HALO, :] = jnp.where(i > 0, yh1, 0.0)
            win_ref[HALO:, :] = a * sg
            d2 = dwin_ref[0:tt, sl]
            for k, xs in _causal_taps(win_ref[...], width, tt):
                dw_ref[k:k + 1, sl] += jnp.sum(d2 * xs, axis=0, keepdims=True)
            d1 = jnp.zeros((tt, cw), F32)
            for k, xs in _anticausal_taps(dwin_ref[:, sl], width, tt):
                d1 = d1 + w_ref[k:k + 1, sl] * xs
            o_ref[:, ci * cw:(ci + 1) * cw] = (d1 * sg).astype(BF16)
            o_ref[:, D + ci * cw:D + (ci + 1) * cw] = (d1 * a * sg * (1.0 - sg)).astype(BF16)

    return _pcall(body, name=name,
                  out_shape=(_sds((T, 2 * D), BF16), _sds((width, D), F32), _sds((1, D), F32), _sds((1, D), F32),
                             _sds((1, D), F32)),
                  grid=(n_tiles,),
                  in_specs=[_row(tt, D, 0), _row(tt, D, 1), _prev_halo(tt, D, 0), _prev_halo(tt, D, 1),
                            _row(tt, D), _next_halo(tt, D, T), _row(tt, D), _next_halo(tt, D, T),
                            _const((width, D)), _const((1, D)), _const((1, D))],
                  out_specs=(_row(tt, 2 * D), _const((width, D)), _const((1, D)), _const((1, D)), _const((1, D))),
                  scratch_shapes=[pltpu.VMEM((HALO + tt, cw), F32), pltpu.VMEM((tt + HALO, D), F32)],
                  compiler_params=_params("arbitrary"))(ag, ag, ag, ag, y2, y2, dy4, dy4, conv_w, ln_g, ln_b)


def _place():
    x, y, c = lax.axis_index("x"), lax.axis_index("y"), lax.axis_index("c")
    return x, y, c


def _slot(px, py, pc):
    return 4 * px + 2 * py + pc


def _all_gather(shards, name):
    n = len(shards)

    def body(*refs):
        ins, outs = refs[:n], refs[n:2 * n]
        send_sems, recv_sems, local_sems = refs[2 * n:]
        x, y, c = _place()
        me, sibling = (x, y, c), (x, y, 1 - c)
        chips = [(1 - x, y), (x, 1 - y), (1 - x, 1 - y)]

        def copy(t, k, block, to, src=None):
            dst = outs[t].at[_slot(*block)]
            return pltpu.make_async_remote_copy(
                src_ref=dst if src is None else src, dst_ref=dst, send_sem=send_sems.at[t, k],
                recv_sem=recv_sems.at[t, k], device_id=to, device_id_type=MESH)

        mine = [pltpu.make_async_copy(ins[t], outs[t].at[_slot(*me)], local_sems.at[t]) for t in range(n)]
        for cp in mine:
            cp.start()
        first = []
        for j, chip in enumerate(chips):
            first += [copy(t, 1 + j, me, (*chip, c), src=ins[t]) for t in range(n)]
        first += [copy(t, 0, me, sibling, src=ins[t]) for t in range(n)]
        for cp in first:
            cp.start()
        passed = []
        for j, chip in enumerate(chips):
            for t in range(n):
                copy(t, 1 + j, (*chip, c), me).wait_recv()
                cp = copy(t, 4 + j, (*chip, c), sibling)
                cp.start()
                passed.append(cp)
        for t in range(n):
            copy(t, 0, sibling, me).wait_recv()
            for j, chip in enumerate(chips):
                copy(t, 4 + j, (*chip, 1 - c), me).wait_recv()
        for cp in first + passed:
            cp.wait_send()
        for cp in mine:
            cp.wait()

    outs = _pcall(
        body, name=name, out_shape=tuple(_sds((N_DEV,) + s.shape, s.dtype) for s in shards),
        in_specs=[ANY] * n, out_specs=(ANY,) * n,
        scratch_shapes=[pltpu.SemaphoreType.DMA((n, 7)), pltpu.SemaphoreType.DMA((n, 7)), pltpu.SemaphoreType.DMA((n,))],
    )(*shards)
    return list(outs)


_HBM = pl.BlockSpec(memory_space=pltpu.HBM)
_SEM = pl.BlockSpec(memory_space=pltpu.SEMAPHORE)
_DATAFLOW = pltpu.SideEffectType.DATAFLOW_SIDE_EFFECTING


def _peers(x, y, c):
    out = []
    for j in range(1, N_DEV):
        fx, fy, fc = (j >> 2) & 1, (j >> 1) & 1, j & 1
        out.append((1 - x if fx else x, 1 - y if fy else y, 1 - c if fc else c))
    return out


def _exchange_copies(ins, zones, mode, sems):
    send_sem, recv_sem, local_sem = sems
    x, y, c = _place()
    me = _slot(x, y, c)
    sibling = (x, y, 1 - c)
    chips = [(1 - x, y), (x, 1 - y), (1 - x, 1 - y)]
    local, remote = [], []

    def add(src, dst, to, landed):
        remote.append((pltpu.make_async_remote_copy(src_ref=src, dst_ref=dst, send_sem=send_sem, recv_sem=recv_sem,
                                                    device_id=to, device_id_type=MESH), landed))

    for t, zone in enumerate(zones):
        if mode == "scatter":
            local.append(pltpu.make_async_copy(ins[t].at[me], zone.at[me], local_sem))
            for peer in _peers(x, y, c):
                add(ins[t].at[_slot(*peer)], zone.at[me], peer, zone.at[_slot(*peer)])
        elif mode == "gather_chips":
            local.append(pltpu.make_async_copy(ins[t], zone.at[me], local_sem))
            for peer in [(*chip, c) for chip in chips] + [sibling]:
                add(ins[t], zone.at[me], peer, zone.at[_slot(*peer)])
        else:
            for chip in chips:
                block = zone.at[_slot(*chip, c)]
                add(block, block, sibling, zone.at[_slot(*chip, 1 - c)])
    return local, remote


def _exchange_start(srcs, lands, mode, after, name):
    if lands is None:
        lands = [lax.empty(s.shape if mode == "scatter" else (N_DEV,) + s.shape, s.dtype) for s in srcs]
    ns, na = len(srcs), len(srcs) + len(lands)

    def body(*refs):
        local, remote = _exchange_copies(refs[:ns], refs[ns:na], mode, refs[na + 1:na + 4])
        for cp in local:
            cp.start()
        for cp, _ in remote:
            cp.start()
        refs[-1][...] = jnp.zeros_like(refs[-1])

    hbm = lambda a: pltpu.with_memory_space_constraint(a, pltpu.HBM)
    arrays = list(srcs) + list(lands)
    outs = _pcall(
        body, name=name,
        out_shape=(pltpu.SemaphoreType.DMA(()),) * 3
        + tuple(pltpu.HBM(a.shape, a.dtype) for a in arrays) + (_sds((8, 128), F32),),
        in_specs=[_HBM] * na + [ANY],
        out_specs=(_SEM,) * 3 + (_HBM,) * na + (pl.BlockSpec(memory_space=pltpu.VMEM),),
        input_output_aliases={t: 3 + t for t in range(na)},
        compiler_params=pltpu.CompilerParams(has_side_effects=_DATAFLOW),
    )(*[hbm(a) for a in arrays], after)
    return outs[:3], list(outs[3:3 + ns]), list(outs[3 + ns:3 + na]), outs[-1]


def _exchange_wait(sems, srcs, lands, mode, after, name):
    ns, na = len(srcs), len(srcs) + len(lands)
    afters = list(after) if isinstance(after, (list, tuple)) else [after]

    def body(*refs):
        local, remote = _exchange_copies(refs[:ns], refs[ns:na], mode, refs[na:na + 3])
        for cp in local:
            cp.wait()
        for cp, landed in remote:
            cp.wait_send()
            pltpu.make_async_remote_copy(
                src_ref=landed, dst_ref=landed, send_sem=refs[na], recv_sem=refs[na + 1],
                device_id=_place(), device_id_type=MESH).wait_recv()

    outs = _pcall(
        body, name=name, out_shape=tuple(pltpu.HBM(a.shape, a.dtype) for a in list(srcs) + list(lands)),
        in_specs=[_HBM] * na + [_SEM] * 3 + [ANY] * len(afters), out_specs=(_HBM,) * na,
        input_output_aliases={t: t for t in range(na)},
        compiler_params=pltpu.CompilerParams(has_side_effects=_DATAFLOW),
    )(*srcs, *lands, *sems, *afters)
    return list(outs[ns:])


def _reduce_adam(recvs, w, m, v, name, dep=None):
    L, r, c = w.shape
    tr = _fit(r, max(16, (128 * 1024) // c), 16)
    ni = r // tr

    def recv_spec(l0):
        def index(l, i):
            return 0, jnp.where(l == l0, i, jnp.where(l < l0, 0, ni - 1)), 0
        return pl.BlockSpec((N_DEV, tr, c), index)

    lay = pl.BlockSpec((None, tr, c), lambda l, i: (l, i, 0))

    n_dep = 0 if dep is None else 1

    def body(*refs):
        recv_refs = refs[:L]
        w_ref, m_ref, v_ref = refs[L:L + 3]
        g_out, d_out, m_out, v_out = refs[L + 3 + n_dep:]
        l = pl.program_id(0)
        for l0 in range(L):
            @pl.when(l == l0)
            def _(l0=l0):
                g = recv_refs[l0][0].astype(F32)
                for s in range(1, N_DEV):
                    g = g + recv_refs[l0][s].astype(F32)
                mn = ADAM_B1 * m_ref[...] + (1.0 - ADAM_B1) * g
                vn = ADAM_B2 * v_ref[...] + (1.0 - ADAM_B2) * (g * g)
                m_hat = mn / (1.0 - ADAM_B1 ** ADAM_STEP)
                v_hat = vn / (1.0 - ADAM_B2 ** ADAM_STEP)
                g_out[...] = g
                d_out[...] = -ADAM_LR * (m_hat / (jnp.sqrt(v_hat) + ADAM_EPS) + ADAM_WD * w_ref[...])
                m_out[...] = mn
                v_out[...] = vn

    return _pcall(body, name=name, out_shape=(_sds((L, r, c), F32),) * 4, grid=(L, ni),
                  in_specs=[recv_spec(l0) for l0 in range(L)] + [lay, lay, lay] + [ANY] * n_dep, out_specs=(lay,) * 4,
                  compiler_params=_params("arbitrary", "arbitrary"))(*recvs, w, m, v, *([dep] if n_dep else []))


_SMALL_SHARDED = ["mix_norm", "xa_norm", "ffn_norm", "a_conv_w", "c_conv_w", "c_conv_b", "c_ln_g", "c_ln_b"]
_SMALL_REPLICATED = ["b_v_g", "b_v_b", "b_w_s", "b_s_bias"]
_BIG = ["xa_wq", "xa_wkv", "xa_wo", "ffn_w_gu", "ffn_w_down", "a_w_in", "a_w_out", "b_w_in", "b_w_out", "c_w_in",
        "c_w_out"]
_COL_SHARDED = {"xa_wkv", "ffn_w_gu", "a_w_in", "b_w_in", "c_w_in"}
_WEIGHTS = ["mix_norm", "xa_norm", "xa_wq", "xa_wkv", "xa_wo", "ffn_norm", "ffn_w_gu", "ffn_w_down", "a_w_in",
            "a_conv_w", "a_w_out", "b_w_in", "b_v_g", "b_v_b", "b_w_s", "b_s_bias", "b_w_out", "c_w_in", "c_conv_w",
            "c_conv_b", "c_ln_g", "c_ln_b", "c_w_out"]
_MIXER = "abc"


def _size(shape):
    size = 1
    for s in shape:
        size *= s
    return size


def _row_layout(shapes, width):
    offs, r = [], 0
    for shape in shapes:
        offs.append(r)
        r += -(-(-(-_size(shape) // width)) // 8) * 8
    return offs, r


def _pack_rows(arrays, width, fill):
    offs, total = _row_layout([a.shape for a in arrays], width)
    ends = offs[1:] + [total]
    rows = [jnp.pad(a.reshape(-1), (0, (e - o) * width - a.size), constant_values=fill).reshape(e - o, width)
            for a, o, e in zip(arrays, offs, ends)]
    return jnp.concatenate(rows, axis=0)


def _unpack_rows(packed, like):
    width = packed.shape[-1]
    offs, _ = _row_layout(like, width)
    return [packed[o:o + -(-_size(s) // width)].reshape(-1)[:_size(s)].reshape(s) for o, s in zip(offs, like)]


def _assemble_rows(pieces, rows, width, name):
    n = len(pieces)

    def body(*refs):
        o_ref = refs[n]
        o_ref[...] = jnp.zeros_like(o_ref)
        for r, (a, off) in zip(refs[:n], pieces):
            o_ref[off:off + a.shape[0], :] = r[...]

    return _pcall(body, name=name, out_shape=_sds((rows, width), F32),
                  compiler_params=pltpu.CompilerParams(vmem_limit_bytes=V7X_VMEM_LIMIT))(*[a for a, _ in pieces])


def kernel(x, mem, mix_norm, xa_norm, xa_wq, xa_wkv, xa_wo, ffn_norm, ffn_w_gu, ffn_w_down, a_w_in, a_conv_w, a_w_out, b_w_in, b_v_g, b_v_b, b_w_s, b_s_bias, b_w_out, c_w_in, c_conv_w, c_conv_b, c_ln_g, c_ln_b, c_w_out, loss_target, m_mix_norm, m_xa_norm, m_xa_wq, m_xa_wkv, m_xa_wo, m_ffn_norm, m_ffn_w_gu, m_ffn_w_down, m_a_w_in, m_a_conv_w, m_a_w_out, m_b_w_in, m_b_v_g, m_b_v_b, m_b_w_s, m_b_s_bias, m_b_w_out, m_c_w_in, m_c_conv_w, m_c_conv_b, m_c_ln_g, m_c_ln_b, m_c_w_out, v_mix_norm, v_xa_norm, v_xa_wq, v_xa_wkv, v_xa_wo, v_ffn_norm, v_ffn_w_gu, v_ffn_w_down, v_a_w_in, v_a_conv_w, v_a_w_out, v_b_w_in, v_b_v_g, v_b_v_b, v_b_w_s, v_b_s_bias, v_b_w_out, v_c_w_in, v_c_conv_w, v_c_conv_b, v_c_ln_g, v_c_ln_b, v_c_w_out):
    P = dict(locals())
    T, D = x.shape[1], x.shape[2]
    dl = D // N_DEV
    depth = mix_norm.shape[0]
    x0, mem0, target = x[0], mem[0], loss_target[0]
    my_slot = _slot(*_place())

    sh_shapes = [P[n].shape for n in _SMALL_SHARDED]
    packed = _pack_rows([P[n] for n in _SMALL_SHARDED], dl, 0.0)
    n_sh = packed.shape[0]
    gathered = _all_gather([packed], "ag_small")[0]
    full_rows = jnp.transpose(gathered, (1, 0, 2)).reshape(n_sh, D)
    small = dict(zip(_SMALL_SHARDED, _unpack_rows(full_rows, [s[:-1] + (D,) for s in sh_shapes])))
    G, C = b_w_s.shape[1], b_w_s.shape[2]
    gd = D // G
    bias_b = jnp.broadcast_to(b_s_bias[0][:, :, None], (G, C, gd))
    zero_row = jnp.zeros((1, D), F32)

    w16 = {n: P[n].astype(BF16) for n in _BIG}

    groups = [(i, part) for i in range(depth) for part in range(3)]

    def group_names(i, part):
        mx, slot = _MIXER[i % N_MIXERS], i // N_MIXERS
        if part == 0:
            return [(mx + "_w_in", slot), (mx + "_w_out", slot)]
        if part == 1:
            return [("xa_wq", i), ("xa_wkv", i), ("xa_wo", i)]
        return [("ffn_w_gu", i), ("ffn_w_down", i)]

    no_token = jnp.zeros((8, 128), F32)
    fwd = {"g": 0, "last": full_rows, "token": no_token, "stage1": {}, "stage2": {}}

    def tag(g):
        return "%d_%d" % groups[g]

    def start_stage1(g):
        if g < len(groups):
            names = group_names(*groups[g])
            sems, srcs, lands, token = _exchange_start([w16[n][j] for n, j in names], None, "gather_chips",
                                                       fwd["last"], "ag_start_" + tag(g))
            fwd["stage1"][g] = (sems, srcs, lands)
            fwd["last"] = fwd["token"] = token

    def start_stage2(g, after):
        if g < len(groups):
            sems, srcs, lands = fwd["stage1"].pop(g)
            lands = _exchange_wait(sems, srcs, lands, "gather_chips", after, "ag_wait_" + tag(g))
            sems, _, lands, token = _exchange_start([], lands, "gather_sibling", after, "ag_pass_" + tag(g))
            fwd["stage2"][g] = (sems, lands)
            fwd["last"] = fwd["token"] = token

    def begin_group():
        g, y = fwd["g"], fwd["last"]
        sems, lands = fwd["stage2"].pop(g)
        fulls = _exchange_wait(sems, [], lands, "gather_sibling", y, "ag_done_" + tag(g))
        fwd["last"] = fulls[0]
        start_stage1(g + 3)
        out = {}
        for (n, _), f in zip(group_names(*groups[g]), fulls):
            key = n[2:] if n[1] == "_" and n[0] in _MIXER else n
            out[key] = f if n in _COL_SHARDED else f.reshape(-1, f.shape[-1])
        fwd["g"] += 1
        return out, fwd["token"]

    def mid_group(y):
        start_stage2(fwd["g"], y)
        return fwd["token"]

    def end_group(y):
        fwd["last"] = y

    for g0 in range(3):
        start_stage1(g0)
    start_stage2(0, fwd["last"])

    saved = []
    xin = x0
    h = _rms_fwd(x0, small["mix_norm"][0, 0][None], "rms_first")
    for i in range(depth):
        kind, slot = i % N_MIXERS, i // N_MIXERS
        W, dep = begin_group()
        S = {"W": W, "x0": xin, "h0": h}
        pre = _mm(h, W["w_in"], "nn", BF16, "mm_in_%s" % _MIXER[kind], b_blocked=True, dep=dep)
        S["pre"] = pre
        dep = mid_group(pre)
        if kind == 0:
            mid = _a_mid_fwd(pre, small["a_conv_w"][slot], "a_mid_fwd")
        elif kind == 1:
            mid = _b_mid_fwd(pre, b_v_g, b_v_b, b_w_s[0], bias_b, "b_mid_fwd")
        else:
            y2c, mid = _c_mid_fwd(pre, small["c_conv_w"][slot], small["c_conv_b"], small["c_ln_g"], small["c_ln_b"],
                                  "c_mid_fwd")
            S["y2c"] = y2c
        S["mid"] = mid
        S["y0"] = _mm(mid, W["w_out"], "nn", BF16, "mm_out", dep=dep)
        end_group(S["y0"])
        Wx, dep = begin_group()
        W.update(Wx)
        xin, h = _post_pre_fwd(xin, S["y0"], small["mix_norm"][i, 1][None], small["xa_norm"][i, 0][None], "post_pre")
        S["x1"], S["h1"] = xin, h
        S["q"] = _mm(h, W["xa_wq"], "nn", BF16, "mm_q", dep=dep)
        dep = mid_group(S["q"])
        S["memn"] = _rms_fwd(mem0, small["xa_norm"][i, 2][None], "rms_mem")
        S["kv"] = _mm(S["memn"], W["xa_wkv"], "nn", BF16, "mm_kv", b_blocked=True)
        S["o"] = _attn_fwd(S["q"], S["kv"], "attn_fwd")
        S["y1"] = _mm(S["o"], W["xa_wo"], "nn", BF16, "mm_out", dep=dep)
        end_group(S["y1"])
        Wf, dep = begin_group()
        W.update(Wf)
        xin, h = _post_pre_fwd(xin, S["y1"], small["xa_norm"][i, 1][None], small["ffn_norm"][i, 0][None], "post_pre")
        S["x2"], S["h2"] = xin, h
        S["dact_dgate"], S["dact_dup"], S["act"] = _ffn_gu_fwd(h, W["ffn_w_gu"], "ffn_gu_fwd", dep=dep)
        S["y2"] = _mm(S["act"], W["ffn_w_down"], "nn", BF16, "mm_down", dep=mid_group(S["act"]))
        end_group(S["y2"])
        if i + 1 < depth:
            xin, h = _post_pre_fwd(xin, S["y2"], small["ffn_norm"][i, 1][None], small["mix_norm"][i + 1, 0][None],
                                   "post_pre")
        saved.append(S)

    last = saved[-1]
    loss_part, dx, dy, dg = _final_fwd_loss(xin, last["y2"], small["ffn_norm"][depth - 1, 1][None], target, "final_loss")
    loss = lax.psum(loss_part[0, 0], ("x", "y", "c"))

    g_mix = [[zero_row, zero_row] for _ in range(depth)]
    g_xa = [[zero_row, zero_row, zero_row] for _ in range(depth)]
    g_ffn = [[zero_row, zero_row] for _ in range(depth)]
    g_small = {}
    recv = {n: [None] * P[n].shape[0] for n in _BIG}
    g_ffn[depth - 1][1] = dg

    bwd = {"queue": [], "token": no_token}
    scatters_in_flight = 3

    def finish_scatter(after, keep):
        while len(bwd["queue"]) > keep:
            names, tag, sems, srcs, lands = bwd["queue"].pop(0)
            for (n, j, _), r in zip(names, _exchange_wait(sems, srcs, lands, "scatter", after, "rs_wait_" + tag)):
                recv[n][j] = r

    def scatter_group(names, tag, after):
        finish_scatter(after, scatters_in_flight - 1)
        parts = [g if n in _COL_SHARDED else g.reshape(N_DEV, -1, g.shape[-1]) for n, _, g in names]
        sems, srcs, lands, bwd["token"] = _exchange_start(parts, None, "scatter", after, "rs_start_" + tag)
        bwd["queue"].append((names, tag, sems, srcs, lands))

    def scatter_token():
        return bwd["token"]

    for i in reversed(range(depth)):
        kind, slot = i % N_MIXERS, i // N_MIXERS
        mx = _MIXER[kind]
        S = saved[i]
        W = S["W"]
        dgu = _ffn_dgu_bwd(dy, W["ffn_w_down"], S["dact_dgate"], S["dact_dup"], "ffn_dgu_bwd", dep=scatter_token())
        dw_down = _mm(S["act"], dy, "tn", BF16, "mm_dw_down")
        dh = _mm(dgu, W["ffn_w_gu"], "nt", BF16, "mm_dh_gu", a_blocked=True, b_blocked=True)
        dw_gu = _mm(S["h2"], dgu, "tn", BF16, "mm_dw_gu", b_blocked=True, out_blocks=N_DEV)
        dx, g_ffn[i][0], dy, g_xa[i][1] = _pre_post_bwd(dx, dh, S["x2"], small["ffn_norm"][i, 0][None], S["y1"],
                                                         small["xa_norm"][i, 1][None], "pre_post_bwd")
        scatter_group([("ffn_w_gu", i, dw_gu), ("ffn_w_down", i, dw_down)], "%d_2" % i, dx)
        do = _mm(dy, W["xa_wo"], "nt", BF16, "mm_nt_dd16", dep=scatter_token())
        dw_o = _mm(S["o"], dy, "tn", BF16, "mm_dw_dd")
        dq, dkv = _attn_bwd(S["q"], S["kv"], do, "attn_bwd")
        dkv16 = dkv.astype(BF16)
        dh = _mm(dq, W["xa_wq"], "nt", BF16, "mm_nt_dd16")
        dw_q = _mm(S["h1"], dq, "tn", BF16, "mm_dw_dd")
        dw_kv = _mm(S["memn"], dkv16, "tn", BF16, "mm_dw_kv", out_blocks=N_DEV)
        dmemn = _mm(dkv16, W["xa_wkv"], "nt", F32, "mm_dmem", b_blocked=True)
        g_xa[i][2] = _rms_gain_grad(dmemn, mem0, "rms_gain_grad")
        dx, g_xa[i][0], dy, g_mix[i][1] = _pre_post_bwd(dx, dh, S["x1"], small["xa_norm"][i, 0][None], S["y0"],
                                                         small["mix_norm"][i, 1][None], "pre_post_bwd")
        scatter_group([("xa_wq", i, dw_q), ("xa_wkv", i, dw_kv), ("xa_wo", i, dw_o)], "%d_1" % i, dx)
        dmid = _mm(dy, W["w_out"], "nt", BF16, "mm_nt_dd16", dep=scatter_token())
        dw_out = _mm(S["mid"], dy, "tn", BF16, "mm_dw_dd")
        if kind == 0:
            dpre, dcw = _a_mid_bwd(S["pre"], dmid, small["a_conv_w"][slot], "a_mid_bwd")
            g_small.setdefault("a_conv_w", {})[slot] = dcw
        elif kind == 1:
            dpre, dws, dsb, dvg, dvb = _b_mid_bwd(S["pre"], dmid, b_v_g, b_v_b, b_w_s[0], bias_b, "b_mid_bwd")
            dsb_row = jnp.pad(jnp.transpose(dsb[:, :G]).reshape(1, G * C), ((0, 0), (0, (-G * C) % D)))
            g_small.update(b_w_s=dws.reshape(-1, D), b_s_bias=dsb_row.reshape(-1, D), b_v_g=dvg, b_v_b=dvb)
        else:
            dpre, dcw, dcb, dlg, dlb = _c_mid_bwd(S["pre"], S["y2c"], dmid, small["c_conv_w"][slot], small["c_ln_g"],
                                                  small["c_ln_b"], "c_mid_bwd")
            g_small.update(c_conv_w=dcw, c_conv_b=dcb, c_ln_g=dlg, c_ln_b=dlb)
        dh = _mm(dpre, W["w_in"], "nt", BF16, "mm_dh_in_%s" % mx, b_blocked=True)
        dw_in = _mm(S["h0"], dpre, "tn", BF16, "mm_dw_in_%s" % mx, out_blocks=N_DEV)
        if i > 0:
            dx, g_mix[i][0], dy, g_ffn[i - 1][1] = _pre_post_bwd(
                dx, dh, S["x0"], small["mix_norm"][i, 0][None], saved[i - 1]["y2"],
                small["ffn_norm"][i - 1, 1][None], "pre_post_bwd")
        else:
            dx, g_mix[i][0] = _pre_post_bwd(dx, dh, S["x0"], small["mix_norm"][i, 0][None], None, None, "pre_bwd")
        scatter_group([(mx + "_w_in", slot, dw_in), (mx + "_w_out", slot, dw_out)], "%d_0" % i, dx)
        S.clear()
    grad_x = dx[None]

    sh_off = dict(zip(_SMALL_SHARDED, _row_layout(sh_shapes, dl)[0]))
    rep_offs, n_rep = _row_layout([P[n].shape for n in _SMALL_REPLICATED], D)
    rep_off = {n: n_sh + o for n, o in zip(_SMALL_REPLICATED, rep_offs)}
    pieces = []
    for i in range(depth):
        pieces += [(g, sh_off["mix_norm"] + 2 * i + j) for j, g in enumerate(g_mix[i])]
        pieces += [(g, sh_off["xa_norm"] + 3 * i + j) for j, g in enumerate(g_xa[i])]
        pieces += [(g, sh_off["ffn_norm"] + 2 * i + j) for j, g in enumerate(g_ffn[i])]
    pieces += [(g, sh_off["a_conv_w"] + a_conv_w.shape[1] * s) for s, g in g_small["a_conv_w"].items()]
    pieces += [(g_small[n], sh_off[n]) for n in ("c_conv_w", "c_conv_b", "c_ln_g", "c_ln_b")]
    pieces += [(g_small[n], rep_off[n]) for n in _SMALL_REPLICATED]
    part_all = _assemble_rows(pieces, n_sh + n_rep, D, "pack_small_grads")
    parts_all = _all_gather([part_all], "ag_small_grads")[0]
    recv_sh = lax.dynamic_slice_in_dim(parts_all[:, :n_sh], my_slot * dl, dl, axis=2)
    recv_rep = parts_all[:, n_sh:]

    out = {}

    def adam_small(names, recv_s, width, name):
        shapes = [P[n].shape for n in names]
        pw = _pack_rows([P[n] for n in names], width, 0.0)
        pm = _pack_rows([P["m_" + n] for n in names], width, 0.0)
        pv = _pack_rows([P["v_" + n] for n in names], width, 1.0)
        res = _reduce_adam([recv_s], pw[None], pm[None], pv[None], name)
        for kind, r in zip(("grad", "delta", "new_m", "new_v"), res):
            for n, a in zip(names, _unpack_rows(r[0], shapes)):
                out[kind + "_" + n] = a

    adam_small(_SMALL_SHARDED, recv_sh, dl, "adam_small_sharded")
    adam_small(_SMALL_REPLICATED, recv_rep, D, "adam_small_replicated")
    def lands_after(n):
        return max([0] + [k + 1 for k, entry in enumerate(bwd["queue"]) if any(m == n for m, _, _ in entry[0])])

    early_done = [g_xa[i][2] for i in range(depth)]
    for n in sorted(_BIG, key=lands_after):
        while any(r is None for r in recv[n]):
            finish_scatter(early_done, len(bwd["queue"]) - 1)
        res = _reduce_adam(recv[n], P[n], P["m_" + n], P["v_" + n], "adam_" + n, dep=scatter_token())
        early_done.append(res[0])
        for kind, r in zip(("grad", "delta", "new_m", "new_v"), res):
            out[kind + "_" + n] = r

    return (loss, grad_x, *[out[k + "_" + n] for k in ("grad", "delta", "new_m", "new_v") for n in _WEIGHTS])
```

```python
import functools

import jax
import jax.numpy as jnp
from jax import lax
from jax.experimental import pallas as pl
from jax.experimental.pallas import tpu as pltpu

F32 = jnp.float32
BF16 = jnp.bfloat16
MESH = pl.DeviceIdType.MESH
ANY = pl.BlockSpec(memory_space=pl.ANY)

N_DEV = 8
N_MIXERS = 3
XA_HEADS = 4
GMLP_GROUPS = 8
CHUNK = 128
NORM_EPS = 1e-6
HALO = 32
ROW_TILE = 256
CONV_LANES = 512
V7X_VMEM_LIMIT = 56 * 1024 * 1024

ADAM_LR = 0.001
ADAM_B1 = 0.9
ADAM_B2 = 0.999
ADAM_EPS = 1e-08
ADAM_WD = 0.01
ADAM_STEP = 10


def _pcall(body, **kw):
    return pl.pallas_call(body, **kw)


def _params(*sem):
    return pltpu.CompilerParams(dimension_semantics=sem, vmem_limit_bytes=V7X_VMEM_LIMIT)


def _fit(n, pref, mult=128):
    if n <= pref:
        return n
    t = (pref // mult) * mult
    while t >= mult:
        if n % t == 0:
            return t
        t -= mult
    return n


def _sds(shape, dtype):
    return jax.ShapeDtypeStruct(shape, dtype)


_DIMS = {"nn": (((1,), (0,)), ((), ())), "nt": (((1,), (1,)), ((), ())), "tn": (((0,), (0,)), ((), ()))}
MM_VMEM_BUDGET = 44 * 1024 * 1024


def _gcd(a, b):
    while b:
        a, b = b, a % b
    return a


def _mm(a, b, mode, out_dtype, name, *, a_blocked=False, b_blocked=False, out_blocks=None, dep=None):
    if mode == "tn":
        K, M = a.shape
    elif a_blocked:
        sa, M, ca = a.shape
        K = sa * ca
    else:
        M, K = a.shape
    n_unit = k_unit = None
    if b_blocked:
        _, d1, cb = b.shape
        if mode == "nt":
            N, k_unit = d1, cb
        else:
            N, n_unit = b.shape[0] * cb, cb
    else:
        N = b.shape[0] if mode == "nt" else b.shape[1]
    n_unit = n_unit or N
    k_unit = k_unit or K
    if a_blocked:
        k_unit = _gcd(k_unit, ca)
    if out_blocks:
        n_unit = _gcd(n_unit, N // out_blocks)
    tn = _fit(n_unit, 1536)
    tk = k_unit
    out_bytes = jnp.dtype(out_dtype).itemsize

    def need(tm_, gk_=1):
        nk_ = K // (tk * gk_)
        return (4 * gk_ * (tm_ * tk + tk * tn) + 2 * tm_ * tn * out_bytes
                + 4 * tm_ * tn * ((2 if nk_ > 1 else 1) + (1 if gk_ > 1 else 0)))

    tm = _fit(M, 1024)
    gk = 1
    if b_blocked and mode == "nt" and tk == cb:
        for cand in (8, 4, 2):
            if b.shape[0] % cand == 0 and (not a_blocked or (ca // tk) % cand == 0) and need(tm, cand) <= MM_VMEM_BUDGET:
                gk = cand
                break
    while need(tm, gk) > MM_VMEM_BUDGET and tm % 256 == 0:
        tm //= 2
    nk = K // (tk * gk)

    if mode == "tn":
        a_spec = pl.BlockSpec((tk, tm), lambda i, j, k: (k, i))
    elif a_blocked:
        ka = ca // (tk * gk)
        a_spec = pl.BlockSpec((None, tm, tk * gk), lambda i, j, k: (k // ka, i, k % ka))
    else:
        a_spec = pl.BlockSpec((tm, tk * gk), lambda i, j, k: (i, k))
    if b_blocked and mode == "nt" and gk > 1:
        b_spec = pl.BlockSpec((gk, tn, tk), lambda i, j, k: (k, j, 0))
    elif b_blocked and mode == "nt":
        kb = cb // tk
        b_spec = pl.BlockSpec((None, tn, tk), lambda i, j, k: (k // kb, j, k % kb))
    elif b_blocked:
        nb = cb // tn
        b_spec = pl.BlockSpec((None, tk, tn), lambda i, j, k: (j // nb, k, j % nb))
    elif mode == "nt":
        b_spec = pl.BlockSpec((tn, tk), lambda i, j, k: (j, k))
    else:
        b_spec = pl.BlockSpec((tk, tn), lambda i, j, k: (k, j))
    if out_blocks:
        ob = (N // out_blocks) // tn
        out_shape = _sds((out_blocks, M, N // out_blocks), out_dtype)
        o_spec = pl.BlockSpec((None, tm, tn), lambda i, j, k: (j // ob, i, j % ob))
    else:
        out_shape = _sds((M, N), out_dtype)
        o_spec = pl.BlockSpec((tm, tn), lambda i, j, k: (i, j))
    dims = _DIMS[mode]
    n_in = 2 if dep is None else 3

    def body(*refs):
        a_ref, b_ref = refs[0], refs[1]
        o_ref = refs[n_in]
        if gk == 1:
            p = lax.dot_general(a_ref[...], b_ref[...], dims, preferred_element_type=F32)
        else:
            p = lax.dot_general(a_ref[:, 0:tk], b_ref[0], dims, preferred_element_type=F32)
            for s in range(1, gk):
                p = p + lax.dot_general(a_ref[:, s * tk:(s + 1) * tk], b_ref[s], dims, preferred_element_type=F32)
        if nk == 1:
            o_ref[...] = p.astype(o_ref.dtype)
            return
        acc_ref = refs[n_in + 1]
        k = pl.program_id(2)

        @pl.when(k == 0)
        def _():
            acc_ref[...] = p

        @pl.when(k > 0)
        def _():
            acc_ref[...] += p

        @pl.when(k == nk - 1)
        def _():
            o_ref[...] = acc_ref[...].astype(o_ref.dtype)

    ins, in_specs = [a, b], [a_spec, b_spec]
    if dep is not None:
        ins.append(dep)
        in_specs.append(ANY)
    return _pcall(
        body, name=name, out_shape=out_shape, grid=(M // tm, N // tn, nk),
        in_specs=in_specs, out_specs=o_spec,
        scratch_shapes=[pltpu.VMEM((tm, tn), F32)] if nk > 1 else [],
        compiler_params=_params("parallel", "parallel", "arbitrary"),
    )(*ins)


def _rstd(v):
    return lax.rsqrt(jnp.mean(v * v, axis=-1, keepdims=True) + NORM_EPS)


def _rms_bwd_rows(v, g, dout):
    r = _rstd(v)
    vh = v * r
    dvh = dout * g
    dv = r * (dvh - vh * jnp.mean(dvh * vh, axis=-1, keepdims=True))
    return dv, jnp.sum(dout * vh, axis=0, keepdims=True)


def _row(tt, d, col=0):
    return pl.BlockSpec((tt, d), lambda i: (i, col))


def _const(shape):
    return pl.BlockSpec(shape, lambda i: (0,) * len(shape))


def _prev_halo(tt, d, col=0):
    return pl.BlockSpec((HALO, d), lambda i: (jnp.maximum(i * (tt // HALO) - 1, 0), col))


def _next_halo(tt, d, rows, col=0):
    last = rows // HALO - 1
    return pl.BlockSpec((HALO, d), lambda i: (jnp.minimum((i + 1) * (tt // HALO), last), col))


def _rms_fwd(x, g, name):
    T, D = x.shape
    tt = _fit(T, ROW_TILE, 8)

    def body(x_ref, g_ref, h_ref):
        v = x_ref[...]
        h_ref[...] = (v * _rstd(v) * g_ref[...]).astype(BF16)

    return _pcall(body, name=name, out_shape=_sds((T, D), BF16), grid=(T // tt,),
                  in_specs=[_row(tt, D), _const((1, D))], out_specs=_row(tt, D),
                  compiler_params=_params("parallel"))(x, g)


def _post_pre_fwd(x, y, g_post, g_pre, name):
    T, D = x.shape
    tt = _fit(T, ROW_TILE, 8)

    def body(x_ref, y_ref, gp_ref, gn_ref, xo_ref, h_ref):
        y = y_ref[...].astype(F32)
        xn = x_ref[...] + y * _rstd(y) * gp_ref[...]
        xo_ref[...] = xn
        h_ref[...] = (xn * _rstd(xn) * gn_ref[...]).astype(BF16)

    return _pcall(body, name=name, out_shape=(_sds((T, D), F32), _sds((T, D), BF16)), grid=(T // tt,),
                  in_specs=[_row(tt, D), _row(tt, D), _const((1, D)), _const((1, D))],
                  out_specs=(_row(tt, D), _row(tt, D)),
                  compiler_params=_params("parallel"))(x, y, g_post, g_pre)


def _final_fwd_loss(x, y, g_post, target, name):
    T, D = x.shape
    tt = _fit(T, ROW_TILE, 8)

    def body(x_ref, y_ref, g_ref, t_ref, loss_ref, dx_ref, dy_ref, dg_ref):
        i = pl.program_id(0)
        y = y_ref[...].astype(F32)
        g = g_ref[...]
        err = x_ref[...] + y * _rstd(y) * g - t_ref[...]
        part = 0.5 * jnp.sum(jnp.mean(err * err, axis=-1, keepdims=True))
        dx = err / D
        dx_ref[...] = dx
        dy, dg = _rms_bwd_rows(y, g, dx)
        dy_ref[...] = dy.astype(BF16)

        @pl.when(i == 0)
        def _():
            loss_ref[...] = jnp.zeros_like(loss_ref)
            dg_ref[...] = jnp.zeros_like(dg_ref)

        loss_ref[...] += part
        dg_ref[...] += dg

    return _pcall(body, name=name,
                  out_shape=(_sds((1, 128), F32), _sds((T, D), F32), _sds((T, D), BF16), _sds((1, D), F32)),
                  grid=(T // tt,),
                  in_specs=[_row(tt, D), _row(tt, D), _const((1, D)), _row(tt, D)],
                  out_specs=(_const((1, 128)), _row(tt, D), _row(tt, D), _const((1, D))),
                  compiler_params=_params("arbitrary"))(x, y, g_post, target)


def _pre_post_bwd(dx_out, dh, x_in, g_pre, y_prev, g_post_prev, name):
    T, D = x_in.shape
    tt = _fit(T, ROW_TILE, 8)
    with_prev = y_prev is not None

    def body(*refs):
        if with_prev:
            dxo_ref, dh_ref, x_ref, g_ref, y_ref, gp_ref, dxi_ref, dg_ref, dy_ref, dgp_ref = refs
        else:
            dxo_ref, dh_ref, x_ref, g_ref, dxi_ref, dg_ref = refs
        i = pl.program_id(0)
        dv, dg = _rms_bwd_rows(x_ref[...], g_ref[...], dh_ref[...].astype(F32))
        dxi = dxo_ref[...] + dv
        dxi_ref[...] = dxi

        @pl.when(i == 0)
        def _():
            dg_ref[...] = jnp.zeros_like(dg_ref)

        dg_ref[...] += dg
        if with_prev:
            dy, dgp = _rms_bwd_rows(y_ref[...].astype(F32), gp_ref[...], dxi)
            dy_ref[...] = dy.astype(BF16)

            @pl.when(i == 0)
            def _():
                dgp_ref[...] = jnp.zeros_like(dgp_ref)

            dgp_ref[...] += dgp

    ins = [dx_out, dh, x_in, g_pre]
    in_specs = [_row(tt, D), _row(tt, D), _row(tt, D), _const((1, D))]
    out_shape = [_sds((T, D), F32), _sds((1, D), F32)]
    out_specs = [_row(tt, D), _const((1, D))]
    if with_prev:
        ins += [y_prev, g_post_prev]
        in_specs += [_row(tt, D), _const((1, D))]
        out_shape += [_sds((T, D), BF16), _sds((1, D), F32)]
        out_specs += [_row(tt, D), _const((1, D))]
    return _pcall(body, name=name, out_shape=tuple(out_shape), grid=(T // tt,),
                  in_specs=in_specs, out_specs=tuple(out_specs),
                  compiler_params=_params("arbitrary"))(*ins)


def _rms_gain_grad(dout, v, name):
    T, D = v.shape
    tt = _fit(T, ROW_TILE, 8)

    def body(d_ref, v_ref, dg_ref):
        @pl.when(pl.program_id(0) == 0)
        def _():
            dg_ref[...] = jnp.zeros_like(dg_ref)

        v = v_ref[...]
        dg_ref[...] += jnp.sum(d_ref[...] * (v * _rstd(v)), axis=0, keepdims=True)

    return _pcall(body, name=name, out_shape=_sds((1, D), F32), grid=(T // tt,),
                  in_specs=[_row(tt, D), _row(tt, D)], out_specs=_const((1, D)),
                  compiler_params=_params("arbitrary"))(dout, v)


def _softmax_rows(s):
    e = jnp.exp(s - jnp.max(s, axis=-1, keepdims=True))
    return e / jnp.sum(e, axis=-1, keepdims=True)


def _attn_fwd(q, kv, name):
    T, D = q.shape
    nm = kv.shape[0]
    hd = D // XA_HEADS
    scale = hd ** -0.5
    tq = _fit(T, ROW_TILE, 8)

    def body(q_ref, k_ref, v_ref, o_ref):
        for h in range(XA_HEADS):
            sl = slice(h * hd, (h + 1) * hd)
            s = lax.dot_general(q_ref[:, sl], k_ref[:, sl], _DIMS["nt"], preferred_element_type=F32) * scale
            p = _softmax_rows(s)
            o_ref[:, sl] = jnp.dot(p.astype(BF16), v_ref[:, sl], preferred_element_type=F32).astype(BF16)

    return _pcall(body, name=name, out_shape=_sds((T, D), BF16), grid=(T // tq,),
                  in_specs=[_row(tq, D), pl.BlockSpec((nm, D), lambda i: (0, 0)), pl.BlockSpec((nm, D), lambda i: (0, 1))],
                  out_specs=_row(tq, D), compiler_params=_params("parallel"))(q, kv, kv)


def _attn_bwd(q, kv, do, name):
    T, D = q.shape
    nm = kv.shape[0]
    hd = D // XA_HEADS
    scale = hd ** -0.5
    tq = _fit(T, ROW_TILE, 8)

    def body(q_ref, k_ref, v_ref, do_ref, dq_ref, dkv_ref):
        @pl.when(pl.program_id(0) == 0)
        def _():
            dkv_ref[...] = jnp.zeros_like(dkv_ref)

        for h in range(XA_HEADS):
            sl = slice(h * hd, (h + 1) * hd)
            qh, kh, vh, doh = q_ref[:, sl], k_ref[:, sl], v_ref[:, sl], do_ref[:, sl]
            s = lax.dot_general(qh, kh, _DIMS["nt"], preferred_element_type=F32) * scale
            p = _softmax_rows(s)
            dp = lax.dot_general(doh, vh, _DIMS["nt"], preferred_element_type=F32)
            ds = (p * (dp - jnp.sum(dp * p, axis=-1, keepdims=True)) * scale).astype(BF16)
            dq_ref[:, sl] = jnp.dot(ds, kh, preferred_element_type=F32).astype(BF16)
            dkv_ref[:, sl] += lax.dot_general(ds, qh, _DIMS["tn"], preferred_element_type=F32)
            dkv_ref[:, D + h * hd:D + (h + 1) * hd] += lax.dot_general(
                p.astype(BF16), doh, _DIMS["tn"], preferred_element_type=F32)

    return _pcall(body, name=name, out_shape=(_sds((T, D), BF16), _sds((nm, 2 * D), F32)), grid=(T // tq,),
                  in_specs=[_row(tq, D), pl.BlockSpec((nm, D), lambda i: (0, 0)), pl.BlockSpec((nm, D), lambda i: (0, 1)),
                            _row(tq, D)],
                  out_specs=(_row(tq, D), _const((nm, 2 * D))),
                  compiler_params=_params("arbitrary"))(q, kv, kv, do)


def _ffn_gu_fwd(h, w_gu, name, dep=None, tm=512):
    T, D = h.shape
    S, _, c = w_gu.shape
    F = S * c // 2
    tm = _fit(T, tm)
    tn = _fit(c, 1536)
    nb = c // tn
    nj = F // tn
    n_in = 3 if dep is None else 4

    def w_spec(off):
        return pl.BlockSpec((None, D, tn), lambda i, j: ((j + off) // nb, 0, (j + off) % nb))

    def body(*refs):
        h_ref, wg_ref, wu_ref = refs[:3]
        dg_ref, du_ref, a_ref = refs[n_in:]
        hv = h_ref[...]
        g = jnp.dot(hv, wg_ref[...], preferred_element_type=F32)
        sg = jax.nn.sigmoid(g)
        silu = g * sg
        du_ref[...] = silu.astype(BF16)
        u = jnp.dot(hv, wu_ref[...], preferred_element_type=F32)
        dg_ref[...] = (u * (sg + silu * (1.0 - sg))).astype(BF16)
        a_ref[...] = (silu * u).astype(BF16)

    ins = [h, w_gu, w_gu]
    in_specs = [pl.BlockSpec((tm, D), lambda i, j: (i, 0)), w_spec(0), w_spec(nj)]
    if dep is not None:
        ins.append(dep)
        in_specs.append(ANY)
    o_spec = pl.BlockSpec((tm, tn), lambda i, j: (i, j))
    return _pcall(body, name=name, out_shape=(_sds((T, F), BF16),) * 3, grid=(T // tm, nj),
                  in_specs=in_specs, out_specs=(o_spec,) * 3,
                  compiler_params=_params("parallel", "parallel"))(*ins)


def _ffn_dgu_bwd(dy, w_down, dact_dgate, dact_dup, name, dep=None, tm=1024):
    T, D = dy.shape
    F = w_down.shape[0]
    tm = _fit(T, tm)
    tn = _fit(F, 512)
    cn = _fit(tn, 256)
    n_in = 4 if dep is None else 5

    def body(*refs):
        dy_ref, wd_ref, g_ref, u_ref = refs[:4]
        o_ref = refs[n_in]
        dyv = dy_ref[...]
        for n0 in range(0, tn, cn):
            da = lax.dot_general(dyv, wd_ref[n0:n0 + cn, :], _DIMS["nt"], preferred_element_type=F32)
            o_ref[0, :, n0:n0 + cn] = (da * g_ref[:, n0:n0 + cn].astype(F32)).astype(BF16)
            o_ref[1, :, n0:n0 + cn] = (da * u_ref[:, n0:n0 + cn].astype(F32)).astype(BF16)

    ins = [dy, w_down, dact_dgate, dact_dup]
    gu_spec = pl.BlockSpec((tm, tn), lambda i, j: (i, j))
    in_specs = [pl.BlockSpec((tm, D), lambda i, j: (i, 0)), pl.BlockSpec((tn, D), lambda i, j: (j, 0)), gu_spec, gu_spec]
    if dep is not None:
        ins.append(dep)
        in_specs.append(ANY)
    return _pcall(body, name=name, out_shape=_sds((2, T, F), BF16), grid=(T // tm, F // tn),
                  in_specs=in_specs, out_specs=pl.BlockSpec((2, tm, tn), lambda i, j: (0, i, j)),
                  compiler_params=_params("parallel", "parallel"))(*ins)


def _causal_taps(win, width, tt):
    for b in range(min(8, width)):
        wb = win if b == 0 else pltpu.roll(win, b, 0)
        a = 0
        while 8 * a + b <= width - 1:
            yield width - 1 - (8 * a + b), wb[HALO - 8 * a:HALO - 8 * a + tt]
            a += 1


def _anticausal_taps(win, width, tt):
    rows = tt + HALO
    for b in range(min(8, width)):
        wb = win if b == 0 else pltpu.roll(win, rows - b, 0)
        a = 0
        while 8 * a + b <= width - 1:
            yield width - 1 - (8 * a + b), wb[8 * a:8 * a + tt]
            a += 1


def _lanes(d):
    cw = _fit(d, CONV_LANES)
    return [slice(s, s + cw) for s in range(0, d, cw)], cw


def _a_mid_fwd(bcz, conv_w, name):
    T, D3 = bcz.shape
    D = D3 // 3
    width = conv_w.shape[0]
    tt = _fit(T, ROW_TILE, HALO)
    chunks, cw = _lanes(D)

    def body(b_ref, c_ref, z_ref, ch_ref, zh_ref, w_ref, o_ref, win_ref):
        i = pl.program_id(0)
        for sl in chunks:
            uh = ch_ref[:, sl].astype(F32) * zh_ref[:, sl].astype(F32)
            win_ref[0:HALO, :] = jnp.where(i > 0, uh, 0.0)
            win_ref[HALO:, :] = c_ref[:, sl].astype(F32) * z_ref[:, sl].astype(F32)
            acc = jnp.zeros((tt, cw), F32)
            for k, xs in _causal_taps(win_ref[...], width, tt):
                acc = acc + w_ref[k:k + 1, sl] * xs
            o_ref[:, sl] = (b_ref[:, sl].astype(F32) * acc).astype(BF16)

    return _pcall(body, name=name, out_shape=_sds((T, D), BF16), grid=(T // tt,),
                  in_specs=[_row(tt, D, 0), _row(tt, D, 1), _row(tt, D, 2), _prev_halo(tt, D, 1), _prev_halo(tt, D, 2),
                            _const((width, D))],
                  out_specs=_row(tt, D), scratch_shapes=[pltpu.VMEM((HALO + tt, cw), F32)],
                  compiler_params=_params("parallel"))(bcz, bcz, bcz, bcz, bcz, conv_w)


def _a_mid_bwd(bcz, dy2, conv_w, name):
    T, D3 = bcz.shape
    D = D3 // 3
    width = conv_w.shape[0]
    tt = _fit(T, ROW_TILE, HALO)
    chunks, cw = _lanes(D)
    n_tiles = T // tt

    def body(b_ref, c_ref, z_ref, ch_ref, zh_ref, bn_ref, d_ref, dn_ref, w_ref, o_ref, dw_ref, win_ref, dwin_ref):
        i = pl.program_id(0)

        @pl.when(i == 0)
        def _():
            dw_ref[...] = jnp.zeros_like(dw_ref)

        for ci, sl in enumerate(chunks):
            c = c_ref[:, sl].astype(F32)
            z = z_ref[:, sl].astype(F32)
            b = b_ref[:, sl].astype(F32)
            d2 = d_ref[:, sl].astype(F32)
            uh = ch_ref[:, sl].astype(F32) * zh_ref[:, sl].astype(F32)
            win_ref[0:HALO, :] = jnp.where(i > 0, uh, 0.0)
            win_ref[HALO:, :] = c * z
            d1 = d2 * b
            d1n = dn_ref[:, sl].astype(F32) * bn_ref[:, sl].astype(F32)
            dwin_ref[0:tt, :] = d1
            dwin_ref[tt:, :] = jnp.where(i < n_tiles - 1, d1n, 0.0)
            y1 = jnp.zeros((tt, cw), F32)
            for k, xs in _causal_taps(win_ref[...], width, tt):
                y1 = y1 + w_ref[k:k + 1, sl] * xs
                dw_ref[k:k + 1, sl] += jnp.sum(d1 * xs, axis=0, keepdims=True)
            du = jnp.zeros((tt, cw), F32)
            for k, xs in _anticausal_taps(dwin_ref[...], width, tt):
                du = du + w_ref[k:k + 1, sl] * xs
            o_ref[:, ci * cw:(ci + 1) * cw] = (d2 * y1).astype(BF16)
            o_ref[:, D + ci * cw:D + (ci + 1) * cw] = (du * z).astype(BF16)
            o_ref[:, 2 * D + ci * cw:2 * D + (ci + 1) * cw] = (du * c).astype(BF16)

    return _pcall(body, name=name, out_shape=(_sds((T, 3 * D), BF16), _sds((width, D), F32)), grid=(n_tiles,),
                  in_specs=[_row(tt, D, 0), _row(tt, D, 1), _row(tt, D, 2), _prev_halo(tt, D, 1), _prev_halo(tt, D, 2),
                            _next_halo(tt, D, T, 0), _row(tt, D), _next_halo(tt, D, T), _const((width, D))],
                  out_specs=(_row(tt, 3 * D), _const((width, D))),
                  scratch_shapes=[pltpu.VMEM((HALO + tt, cw), F32), pltpu.VMEM((tt + HALO, cw), F32)],
                  compiler_params=_params("arbitrary"))(bcz, bcz, bcz, bcz, bcz, bcz, dy2, dy2, conv_w)


_GELU_C = 0.7978845608028654
_GELU_A = 0.044715


def _gelu(v):
    return 0.5 * v * (1.0 + jnp.tanh(_GELU_C * (v + _GELU_A * v * v * v)))


def _gelu_grad(v):
    t = jnp.tanh(_GELU_C * (v + _GELU_A * v * v * v))
    return 0.5 * (1.0 + t) + 0.5 * v * (1.0 - t * t) * (_GELU_C * (1.0 + 3.0 * _GELU_A * v * v))


def _ln_stats(v):
    mu = jnp.mean(v, axis=-1, keepdims=True)
    vc = v - mu
    return vc * lax.rsqrt(jnp.mean(vc * vc, axis=-1, keepdims=True) + NORM_EPS)


def _tril(n):
    return lax.broadcasted_iota(jnp.int32, (n, n), 0) >= lax.broadcasted_iota(jnp.int32, (n, n), 1)


def _b_mid_fwd(uv, v_g, v_b, w_s, bias_b, name):
    T, D2 = uv.shape
    D = D2 // 2
    G, C, _ = w_s.shape
    gd = D // G
    tt = _fit(T, ROW_TILE, C)

    def body(u_ref, v_ref, g_ref, b_ref, ws_ref, bias_ref, o_ref, vln_ref):
        vln_ref[...] = (_ln_stats(_gelu(v_ref[...].astype(F32))) * g_ref[...] + b_ref[...]).astype(BF16)
        mask = _tril(C)
        for g in range(G):
            wsm = jnp.where(mask, ws_ref[g], 0.0).astype(BF16)
            cs = slice(g * gd, (g + 1) * gd)
            for n in range(tt // C):
                rs = slice(n * C, (n + 1) * C)
                sv = jnp.dot(wsm, vln_ref[rs, cs], preferred_element_type=F32) + bias_ref[g]
                o_ref[rs, cs] = (_gelu(u_ref[rs, cs].astype(F32)) * sv).astype(BF16)

    return _pcall(body, name=name, out_shape=_sds((T, D), BF16), grid=(T // tt,),
                  in_specs=[_row(tt, D, 0), _row(tt, D, 1), _const((1, D)), _const((1, D)), _const((G, C, C)),
                            _const((G, C, gd))],
                  out_specs=_row(tt, D), scratch_shapes=[pltpu.VMEM((tt, D), BF16)],
                  compiler_params=_params("parallel"))(uv, uv, v_g, v_b, w_s, bias_b)


def _b_mid_bwd(uv, dgated, v_g, v_b, w_s, bias_b, name):
    T, D2 = uv.shape
    D = D2 // 2
    G, C, _ = w_s.shape
    gd = D // G
    tt = _fit(T, ROW_TILE, C)

    def body(u_ref, v_ref, d_ref, g_ref, b_ref, ws_ref, bias_ref, o_ref, dws_ref, dsb_ref, dvg_ref, dvb_ref,
             vln_ref, dvln_ref):
        @pl.when(pl.program_id(0) == 0)
        def _():
            dws_ref[...] = jnp.zeros_like(dws_ref)
            dsb_ref[...] = jnp.zeros_like(dsb_ref)
            dvg_ref[...] = jnp.zeros_like(dvg_ref)
            dvb_ref[...] = jnp.zeros_like(dvb_ref)

        vpre = v_ref[...].astype(F32)
        vhat = _ln_stats(_gelu(vpre))
        vln_ref[...] = (vhat * g_ref[...] + b_ref[...]).astype(BF16)
        mask = _tril(C)
        lane = lax.broadcasted_iota(jnp.int32, (C, 128), 1)
        for g in range(G):
            wsm = jnp.where(mask, ws_ref[g], 0.0).astype(BF16)
            cs = slice(g * gd, (g + 1) * gd)
            for n in range(tt // C):
                rs = slice(n * C, (n + 1) * C)
                vt = vln_ref[rs, cs]
                sv = jnp.dot(wsm, vt, preferred_element_type=F32) + bias_ref[g]
                dg = d_ref[rs, cs].astype(F32)
                upre = u_ref[rs, cs].astype(F32)
                o_ref[rs, cs] = (dg * sv * _gelu_grad(upre)).astype(BF16)
                dsv = dg * _gelu(upre)
                dsb_ref[...] += jnp.where(lane == g, jnp.sum(dsv, axis=-1, keepdims=True), 0.0)
                dsv16 = dsv.astype(BF16)
                dws_ref[g] += jnp.where(mask, lax.dot_general(dsv16, vt, _DIMS["nt"], preferred_element_type=F32), 0.0)
                dvln_ref[rs, cs] = lax.dot_general(wsm, dsv16, _DIMS["tn"], preferred_element_type=F32)
        dvln = dvln_ref[...]
        dvg_ref[...] += jnp.sum(dvln * vhat, axis=0, keepdims=True)
        dvb_ref[...] += jnp.sum(dvln, axis=0, keepdims=True)
        dvh = dvln * g_ref[...]
        vc = _gelu(vpre)
        vc = vc - jnp.mean(vc, axis=-1, keepdims=True)
        rstd = lax.rsqrt(jnp.mean(vc * vc, axis=-1, keepdims=True) + NORM_EPS)
        dv = rstd * (dvh - jnp.mean(dvh, axis=-1, keepdims=True) - vhat * jnp.mean(dvh * vhat, axis=-1, keepdims=True))
        o_ref[:, D:] = (dv * _gelu_grad(vpre)).astype(BF16)

    return _pcall(body, name=name,
                  out_shape=(_sds((T, 2 * D), BF16), _sds((G, C, C), F32), _sds((C, 128), F32), _sds((1, D), F32),
                             _sds((1, D), F32)),
                  grid=(T // tt,),
                  in_specs=[_row(tt, D, 0), _row(tt, D, 1), _row(tt, D), _const((1, D)), _const((1, D)),
                            _const((G, C, C)), _const((G, C, gd))],
                  out_specs=(_row(tt, 2 * D), _const((G, C, C)), _const((C, 128)), _const((1, D)), _const((1, D))),
                  scratch_shapes=[pltpu.VMEM((tt, D), BF16), pltpu.VMEM((tt, D), F32)],
                  compiler_params=_params("arbitrary"))(uv, uv, dgated, v_g, v_b, w_s, bias_b)


def _c_mid_fwd(ag, conv_w, conv_b, ln_g, ln_b, name):
    T, D2 = ag.shape
    D = D2 // 2
    width = conv_w.shape[0]
    tt = _fit(T, ROW_TILE, HALO)
    chunks, cw = _lanes(D)

    def body(a_ref, g_ref, ah_ref, gh_ref, w_ref, cb_ref, lg_ref, lb_ref, y2_ref, o_ref, win_ref):
        i = pl.program_id(0)
        for sl in chunks:
            yh = ah_ref[:, sl].astype(F32) * jax.nn.sigmoid(gh_ref[:, sl].astype(F32))
            win_ref[0:HALO, :] = jnp.where(i > 0, yh, 0.0)
            win_ref[HALO:, :] = a_ref[:, sl].astype(F32) * jax.nn.sigmoid(g_ref[:, sl].astype(F32))
            acc = jnp.zeros((tt, cw), F32)
            for k, xs in _causal_taps(win_ref[...], width, tt):
                acc = acc + w_ref[k:k + 1, sl] * xs
            y2_ref[:, sl] = acc + cb_ref[:, sl]
        y3 = _ln_stats(y2_ref[...]) * lg_ref[...] + lb_ref[...]
        o_ref[...] = (y3 * jax.nn.sigmoid(y3)).astype(BF16)

    return _pcall(body, name=name, out_shape=(_sds((T, D), F32), _sds((T, D), BF16)), grid=(T // tt,),
                  in_specs=[_row(tt, D, 0), _row(tt, D, 1), _prev_halo(tt, D, 0), _prev_halo(tt, D, 1),
                            _const((width, D)), _const((1, D)), _const((1, D)), _const((1, D))],
                  out_specs=(_row(tt, D), _row(tt, D)), scratch_shapes=[pltpu.VMEM((HALO + tt, cw), F32)],
                  compiler_params=_params("parallel"))(ag, ag, ag, ag, conv_w, conv_b, ln_g, ln_b)


def _c_mid_bwd(ag, y2, dy4, conv_w, ln_g, ln_b, name):
    T, D2 = ag.shape
    D = D2 // 2
    width = conv_w.shape[0]
    tt = _fit(T, ROW_TILE, HALO)
    chunks, cw = _lanes(D)
    n_tiles = T // tt

    def ln_silu_bwd(y2v, dy4v, lg, lb):
        mu = jnp.mean(y2v, axis=-1, keepdims=True)
        yc = y2v - mu
        rstd = lax.rsqrt(jnp.mean(yc * yc, axis=-1, keepdims=True) + NORM_EPS)
        yh = yc * rstd
        y3 = yh * lg + lb
        sg = jax.nn.sigmoid(y3)
        dy3 = dy4v * (sg * (1.0 + y3 * (1.0 - sg)))
        dyh = dy3 * lg
        dy2 = rstd * (dyh - jnp.mean(dyh, axis=-1, keepdims=True) - yh * jnp.mean(dyh * yh, axis=-1, keepdims=True))
        return dy2, dy3, yh

    def body(a_ref, g_ref, ah_ref, gh_ref, y2_ref, y2n_ref, d_ref, dn_ref, w_ref, lg_ref, lb_ref,
             o_ref, dw_ref, dcb_ref, dlg_ref, dlb_ref, win_ref, dwin_ref):
        i = pl.program_id(0)

        @pl.when(i == 0)
        def _():
            dw_ref[...] = jnp.zeros_like(dw_ref)
            dcb_ref[...] = jnp.zeros_like(dcb_ref)
            dlg_ref[...] = jnp.zeros_like(dlg_ref)
            dlb_ref[...] = jnp.zeros_like(dlb_ref)

        lg = lg_ref[...]
        lb = lb_ref[...]
        dy2, dy3, yh = ln_silu_bwd(y2_ref[...], d_ref[...].astype(F32), lg, lb)
        dlg_ref[...] += jnp.sum(dy3 * yh, axis=0, keepdims=True)
        dlb_ref[...] += jnp.sum(dy3, axis=0, keepdims=True)
        dcb_ref[...] += jnp.sum(dy2, axis=0, keepdims=True)
        dwin_ref[0:tt, :] = dy2
        dy2n, _, _ = ln_silu_bwd(y2n_ref[...], dn_ref[...].astype(F32), lg, lb)
        dwin_ref[tt:, :] = jnp.where(i < n_tiles - 1, dy2n, 0.0)
        for ci, sl in enumerate(chunks):
            a = a_ref[:, sl].astype(F32)
            sg = jax.nn.sigmoid(g_ref[:, sl].astype(F32))
            yh1 = ah_ref[:, sl].astype(F32) * jax.nn.sigmoid(gh_ref[:, sl].astype(F32))
            win_ref[0:HALO, :] = jnp.where(i > 0, yh1, 0.0)
            win_ref[HALO:, :] = a * sg
            d2 = dwin_ref[0:tt, sl]
            for k, xs in _causal_taps(win_ref[...], width, tt):
                dw_ref[k:k + 1, sl] += jnp.sum(d2 * xs, axis=0, keepdims=True)
            d1 = jnp.zeros((tt, cw), F32)
            for k, xs in _anticausal_taps(dwin_ref[:, sl], width, tt):
                d1 = d1 + w_ref[k:k + 1, sl] * xs
            o_ref[:, ci * cw:(ci + 1) * cw] = (d1 * sg).astype(BF16)
            o_ref[:, D + ci * cw:D + (ci + 1) * cw] = (d1 * a * sg * (1.0 - sg)).astype(BF16)

    return _pcall(body, name=name,
                  out_shape=(_sds((T, 2 * D), BF16), _sds((width, D), F32), _sds((1, D), F32), _sds((1, D), F32),
                             _sds((1, D), F32)),
                  grid=(n_tiles,),
                  in_specs=[_row(tt, D, 0), _row(tt, D, 1), _prev_halo(tt, D, 0), _prev_halo(tt, D, 1),
                            _row(tt, D), _next_halo(tt, D, T), _row(tt, D), _next_halo(tt, D, T),
                            _const((width, D)), _const((1, D)), _const((1, D))],
                  out_specs=(_row(tt, 2 * D), _const((width, D)), _const((1, D)), _const((1, D)), _const((1, D))),
                  scratch_shapes=[pltpu.VMEM((HALO + tt, cw), F32), pltpu.VMEM((tt + HALO, D), F32)],
                  compiler_params=_params("arbitrary"))(ag, ag, ag, ag, y2, y2, dy4, dy4, conv_w, ln_g, ln_b)


def _place():
    x, y, c = lax.axis_index("x"), lax.axis_index("y"), lax.axis_index("c")
    return x, y, c


def _slot(px, py, pc):
    return 4 * px + 2 * py + pc


def _all_gather(shards, name):
    n = len(shards)

    def body(*refs):
        ins, outs = refs[:n], refs[n:2 * n]
        send_sems, recv_sems, local_sems = refs[2 * n:]
        x, y, c = _place()
        me, sibling = (x, y, c), (x, y, 1 - c)
        chips = [(1 - x, y), (x, 1 - y), (1 - x, 1 - y)]

        def copy(t, k, block, to, src=None):
            dst = outs[t].at[_slot(*block)]
            return pltpu.make_async_remote_copy(
                src_ref=dst if src is None else src, dst_ref=dst, send_sem=send_sems.at[t, k],
                recv_sem=recv_sems.at[t, k], device_id=to, device_id_type=MESH)

        mine = [pltpu.make_async_copy(ins[t], outs[t].at[_slot(*me)], local_sems.at[t]) for t in range(n)]
        for cp in mine:
            cp.start()
        first = []
        for j, chip in enumerate(chips):
            first += [copy(t, 1 + j, me, (*chip, c), src=ins[t]) for t in range(n)]
        first += [copy(t, 0, me, sibling, src=ins[t]) for t in range(n)]
        for cp in first:
            cp.start()
        passed = []
        for j, chip in enumerate(chips):
            for t in range(n):
                copy(t, 1 + j, (*chip, c), me).wait_recv()
                cp = copy(t, 4 + j, (*chip, c), sibling)
                cp.start()
                passed.append(cp)
        for t in range(n):
            copy(t, 0, sibling, me).wait_recv()
            for j, chip in enumerate(chips):
                copy(t, 4 + j, (*chip, 1 - c), me).wait_recv()
        for cp in first + passed:
            cp.wait_send()
        for cp in mine:
            cp.wait()

    outs = _pcall(
        body, name=name, out_shape=tuple(_sds((N_DEV,) + s.shape, s.dtype) for s in shards),
        in_specs=[ANY] * n, out_specs=(ANY,) * n,
        scratch_shapes=[pltpu.SemaphoreType.DMA((n, 7)), pltpu.SemaphoreType.DMA((n, 7)), pltpu.SemaphoreType.DMA((n,))],
    )(*shards)
    return list(outs)


_HBM = pl.BlockSpec(memory_space=pltpu.HBM)
_SEM = pl.BlockSpec(memory_space=pltpu.SEMAPHORE)
_DATAFLOW = pltpu.SideEffectType.DATAFLOW_SIDE_EFFECTING


def _peers(x, y, c):
    out = []
    for j in range(1, N_DEV):
        fx, fy, fc = (j >> 2) & 1, (j >> 1) & 1, j & 1
        out.append((1 - x if fx else x, 1 - y if fy else y, 1 - c if fc else c))
    return out


def _exchange_copies(ins, zones, mode, sems):
    send_sem, recv_sem, local_sem = sems
    x, y, c = _place()
    me = _slot(x, y, c)
    sibling = (x, y, 1 - c)
    chips = [(1 - x, y), (x, 1 - y), (1 - x, 1 - y)]
    local, remote = [], []

    def add(src, dst, to, landed):
        remote.append((pltpu.make_async_remote_copy(src_ref=src, dst_ref=dst, send_sem=send_sem, recv_sem=recv_sem,
                                                    device_id=to, device_id_type=MESH), landed))

    for t, zone in enumerate(zones):
        if mode == "scatter":
            local.append(pltpu.make_async_copy(ins[t].at[me], zone.at[me], local_sem))
            for peer in _peers(x, y, c):
                add(ins[t].at[_slot(*peer)], zone.at[me], peer, zone.at[_slot(*peer)])
        elif mode in ("gather_chips", "gather_all"):
            local.append(pltpu.make_async_copy(ins[t], zone.at[me], local_sem))
            for peer in ([(*chip, c) for chip in chips] + [sibling] if mode == "gather_chips" else _peers(x, y, c)):
                add(ins[t], zone.at[me], peer, zone.at[_slot(*peer)])
        else:
            for chip in chips:
                block = zone.at[_slot(*chip, c)]
                add(block, block, sibling, zone.at[_slot(*chip, 1 - c)])
    return local, remote


def _exchange_start(srcs, lands, mode, after, name):
    if lands is None:
        lands = [lax.empty(s.shape if mode == "scatter" else (N_DEV,) + s.shape, s.dtype) for s in srcs]
    ns, na = len(srcs), len(srcs) + len(lands)

    def body(*refs):
        local, remote = _exchange_copies(refs[:ns], refs[ns:na], mode, refs[na + 1:na + 4])
        for cp in local:
            cp.start()
        for cp, _ in remote:
            cp.start()
        refs[-1][...] = jnp.zeros_like(refs[-1])

    hbm = lambda a: pltpu.with_memory_space_constraint(a, pltpu.HBM)
    arrays = list(srcs) + list(lands)
    outs = _pcall(
        body, name=name,
        out_shape=(pltpu.SemaphoreType.DMA(()),) * 3
        + tuple(pltpu.HBM(a.shape, a.dtype) for a in arrays) + (_sds((8, 128), F32),),
        in_specs=[_HBM] * na + [ANY],
        out_specs=(_SEM,) * 3 + (_HBM,) * na + (pl.BlockSpec(memory_space=pltpu.VMEM),),
        input_output_aliases={t: 3 + t for t in range(na)},
        compiler_params=pltpu.CompilerParams(has_side_effects=_DATAFLOW),
    )(*[hbm(a) for a in arrays], after)
    return outs[:3], list(outs[3:3 + ns]), list(outs[3 + ns:3 + na]), outs[-1]


def _exchange_wait(sems, srcs, lands, mode, after, name):
    ns, na = len(srcs), len(srcs) + len(lands)
    afters = list(after) if isinstance(after, (list, tuple)) else [after]

    def body(*refs):
        local, remote = _exchange_copies(refs[:ns], refs[ns:na], mode, refs[na:na + 3])
        for cp in local:
            cp.wait()
        for cp, landed in remote:
            cp.wait_send()
            pltpu.make_async_remote_copy(
                src_ref=landed, dst_ref=landed, send_sem=refs[na], recv_sem=refs[na + 1],
                device_id=_place(), device_id_type=MESH).wait_recv()

    outs = _pcall(
        body, name=name, out_shape=tuple(pltpu.HBM(a.shape, a.dtype) for a in list(srcs) + list(lands)),
        in_specs=[_HBM] * na + [_SEM] * 3 + [ANY] * len(afters), out_specs=(_HBM,) * na,
        input_output_aliases={t: t for t in range(na)},
        compiler_params=pltpu.CompilerParams(has_side_effects=_DATAFLOW),
    )(*srcs, *lands, *sems, *afters)
    return list(outs[ns:])


def _reduce_adam(recvs, w, m, v, name, dep=None):
    L, r, c = w.shape
    tr = _fit(r, max(16, (256 * 1024) // c), 16)
    ni = r // tr

    def recv_spec(l0):
        def index(l, i):
            return 0, jnp.where(l == l0, i, jnp.where(l < l0, 0, ni - 1)), 0
        return pl.BlockSpec((N_DEV, tr, c), index)

    lay = pl.BlockSpec((None, tr, c), lambda l, i: (l, i, 0))

    n_dep = 0 if dep is None else 1

    def body(*refs):
        recv_refs = refs[:L]
        w_ref, m_ref, v_ref = refs[L:L + 3]
        g_out, d_out, m_out, v_out = refs[L + 3 + n_dep:]
        l = pl.program_id(0)
        for l0 in range(L):
            @pl.when(l == l0)
            def _(l0=l0):
                g = recv_refs[l0][0].astype(F32)
                for s in range(1, N_DEV):
                    g = g + recv_refs[l0][s].astype(F32)
                mn = ADAM_B1 * m_ref[...] + (1.0 - ADAM_B1) * g
                vn = ADAM_B2 * v_ref[...] + (1.0 - ADAM_B2) * (g * g)
                m_hat = mn / (1.0 - ADAM_B1 ** ADAM_STEP)
                v_hat = vn / (1.0 - ADAM_B2 ** ADAM_STEP)
                g_out[...] = g
                d_out[...] = -ADAM_LR * (m_hat / (jnp.sqrt(v_hat) + ADAM_EPS) + ADAM_WD * w_ref[...])
                m_out[...] = mn
                v_out[...] = vn

    return _pcall(body, name=name, out_shape=(_sds((L, r, c), F32),) * 4, grid=(L, ni),
                  in_specs=[recv_spec(l0) for l0 in range(L)] + [lay, lay, lay] + [ANY] * n_dep, out_specs=(lay,) * 4,
                  compiler_params=_params("arbitrary", "arbitrary"))(*recvs, w, m, v, *([dep] if n_dep else []))


_SMALL_SHARDED = ["mix_norm", "xa_norm", "ffn_norm", "a_conv_w", "c_conv_w", "c_conv_b", "c_ln_g", "c_ln_b"]
_SMALL_REPLICATED = ["b_v_g", "b_v_b", "b_w_s", "b_s_bias"]
_BIG = ["xa_wq", "xa_wkv", "xa_wo", "ffn_w_gu", "ffn_w_down", "a_w_in", "a_w_out", "b_w_in", "b_w_out", "c_w_in",
        "c_w_out"]
_COL_SHARDED = {"xa_wkv", "ffn_w_gu", "a_w_in", "b_w_in", "c_w_in"}
_WEIGHTS = ["mix_norm", "xa_norm", "xa_wq", "xa_wkv", "xa_wo", "ffn_norm", "ffn_w_gu", "ffn_w_down", "a_w_in",
            "a_conv_w", "a_w_out", "b_w_in", "b_v_g", "b_v_b", "b_w_s", "b_s_bias", "b_w_out", "c_w_in", "c_conv_w",
            "c_conv_b", "c_ln_g", "c_ln_b", "c_w_out"]
_MIXER = "abc"


def _size(shape):
    size = 1
    for s in shape:
        size *= s
    return size


def _row_layout(shapes, width):
    offs, r = [], 0
    for shape in shapes:
        offs.append(r)
        r += -(-(-(-_size(shape) // width)) // 8) * 8
    return offs, r


def _pack_rows(arrays, width, fill):
    offs, total = _row_layout([a.shape for a in arrays], width)
    ends = offs[1:] + [total]
    rows = [jnp.pad(a.reshape(-1), (0, (e - o) * width - a.size), constant_values=fill).reshape(e - o, width)
            for a, o, e in zip(arrays, offs, ends)]
    return jnp.concatenate(rows, axis=0)


def _unpack_rows(packed, like):
    width = packed.shape[-1]
    offs, _ = _row_layout(like, width)
    return [packed[o:o + -(-_size(s) // width)].reshape(-1)[:_size(s)].reshape(s) for o, s in zip(offs, like)]


def _assemble_rows(pieces, rows, width, name):
    n = len(pieces)

    def body(*refs):
        o_ref = refs[n]
        o_ref[...] = jnp.zeros_like(o_ref)
        for r, (a, off) in zip(refs[:n], pieces):
            o_ref[off:off + a.shape[0], :] = r[...]

    return _pcall(body, name=name, out_shape=_sds((rows, width), F32),
                  compiler_params=pltpu.CompilerParams(vmem_limit_bytes=V7X_VMEM_LIMIT))(*[a for a, _ in pieces])


def kernel(x, mem, mix_norm, xa_norm, xa_wq, xa_wkv, xa_wo, ffn_norm, ffn_w_gu, ffn_w_down, a_w_in, a_conv_w, a_w_out, b_w_in, b_v_g, b_v_b, b_w_s, b_s_bias, b_w_out, c_w_in, c_conv_w, c_conv_b, c_ln_g, c_ln_b, c_w_out, loss_target, m_mix_norm, m_xa_norm, m_xa_wq, m_xa_wkv, m_xa_wo, m_ffn_norm, m_ffn_w_gu, m_ffn_w_down, m_a_w_in, m_a_conv_w, m_a_w_out, m_b_w_in, m_b_v_g, m_b_v_b, m_b_w_s, m_b_s_bias, m_b_w_out, m_c_w_in, m_c_conv_w, m_c_conv_b, m_c_ln_g, m_c_ln_b, m_c_w_out, v_mix_norm, v_xa_norm, v_xa_wq, v_xa_wkv, v_xa_wo, v_ffn_norm, v_ffn_w_gu, v_ffn_w_down, v_a_w_in, v_a_conv_w, v_a_w_out, v_b_w_in, v_b_v_g, v_b_v_b, v_b_w_s, v_b_s_bias, v_b_w_out, v_c_w_in, v_c_conv_w, v_c_conv_b, v_c_ln_g, v_c_ln_b, v_c_w_out):
    P = dict(locals())
    T, D = x.shape[1], x.shape[2]
    dl = D // N_DEV
    depth = mix_norm.shape[0]
    x0, mem0, target = x[0], mem[0], loss_target[0]
    my_slot = _slot(*_place())

    sh_shapes = [P[n].shape for n in _SMALL_SHARDED]
    packed = _pack_rows([P[n] for n in _SMALL_SHARDED], dl, 0.0)
    n_sh = packed.shape[0]
    gathered = _all_gather([packed], "ag_small")[0]
    full_rows = jnp.transpose(gathered, (1, 0, 2)).reshape(n_sh, D)
    small = dict(zip(_SMALL_SHARDED, _unpack_rows(full_rows, [s[:-1] + (D,) for s in sh_shapes])))
    G, C = b_w_s.shape[1], b_w_s.shape[2]
    gd = D // G
    bias_b = jnp.broadcast_to(b_s_bias[0][:, :, None], (G, C, gd))
    zero_row = jnp.zeros((1, D), F32)

    groups = [(i, part) for i in range(depth) for part in (("in", "out", "xa", "gu", "down") if i == 0 else
                                                           ("mix", "xa", "ffn"))]

    def group_names(i, part):
        mx, slot = _MIXER[i % N_MIXERS], i // N_MIXERS
        names = {"in": [(mx + "_w_in", slot)], "out": [(mx + "_w_out", slot)],
                 "xa": [("xa_wq", i), ("xa_wkv", i), ("xa_wo", i)],
                 "gu": [("ffn_w_gu", i)], "down": [("ffn_w_down", i)]}
        names["mix"] = names["in"] + names["out"]
        names["ffn"] = names["gu"] + names["down"]
        return names[part]

    no_token = jnp.zeros((8, 128), F32)
    fwd = {"g": 0, "last": full_rows, "token": no_token, "stage1": {}, "stage2": {}}

    def tag(g):
        return "%d_%s" % groups[g]

    def start_stage1(g):
        if g < len(groups):
            names = group_names(*groups[g])
            sems, srcs, lands, token = _exchange_start([P[n][j].astype(BF16) for n, j in names], None, "gather_chips",
                                                       fwd["last"], "ag_start_" + tag(g))
            fwd["stage1"][g] = (sems, srcs, lands)
            fwd["last"] = fwd["token"] = token

    def start_stage2(g, after):
        if g in fwd["stage1"]:
            sems, srcs, lands = fwd["stage1"].pop(g)
            lands = _exchange_wait(sems, srcs, lands, "gather_chips", after, "ag_wait_" + tag(g))
            sems, _, lands, token = _exchange_start([], lands, "gather_sibling", after, "ag_pass_" + tag(g))
            fwd["stage2"][g] = (sems, lands)
            fwd["last"] = fwd["token"] = token

    def begin_group():
        g, y = fwd["g"], fwd["last"]
        start_stage2(g, y)
        sems, lands = fwd["stage2"].pop(g)
        fulls = _exchange_wait(sems, [], lands, "gather_sibling", y, "ag_done_" + tag(g))
        fwd["last"] = fulls[0]
        start_stage1(g + 3)
        out = {}
        for (n, _), f in zip(group_names(*groups[g]), fulls):
            key = n[2:] if n[1] == "_" and n[0] in _MIXER else n
            out[key] = f if n in _COL_SHARDED else f.reshape(-1, f.shape[-1])
        fwd["g"] += 1
        return out, fwd["token"]

    def mid_group(y):
        start_stage2(fwd["g"], y)
        return fwd["token"]

    def end_group(y):
        fwd["last"] = y

    for g0 in range(3):
        start_stage1(g0)
    start_stage2(0, fwd["last"])

    saved = []
    xin = x0
    h = _rms_fwd(x0, small["mix_norm"][0, 0][None], "rms_first")
    for i in range(depth):
        kind, slot = i % N_MIXERS, i // N_MIXERS
        W, dep = begin_group()
        S = {"W": W, "x0": xin, "h0": h}
        pre = _mm(h, W["w_in"], "nn", BF16, "mm_in_%s" % _MIXER[kind], b_blocked=True, dep=dep)
        S["pre"] = pre
        dep = mid_group(pre)
        if kind == 0:
            mid = _a_mid_fwd(pre, small["a_conv_w"][slot], "a_mid_fwd")
        elif kind == 1:
            mid = _b_mid_fwd(pre, b_v_g, b_v_b, b_w_s[0], bias_b, "b_mid_fwd")
        else:
            y2c, mid = _c_mid_fwd(pre, small["c_conv_w"][slot], small["c_conv_b"], small["c_ln_g"], small["c_ln_b"],
                                  "c_mid_fwd")
            S["y2c"] = y2c
        S["mid"] = mid
        if "w_out" not in W:
            end_group(mid)
            Wo, dep = begin_group()
            W.update(Wo)
        S["y0"] = _mm(mid, W["w_out"], "nn", BF16, "mm_out", dep=dep)
        end_group(S["y0"])
        Wx, dep = begin_group()
        W.update(Wx)
        xin, h = _post_pre_fwd(xin, S["y0"], small["mix_norm"][i, 1][None], small["xa_norm"][i, 0][None], "post_pre")
        S["x1"], S["h1"] = xin, h
        S["q"] = _mm(h, W["xa_wq"], "nn", BF16, "mm_q", dep=dep)
        dep = mid_group(S["q"])
        S["memn"] = _rms_fwd(mem0, small["xa_norm"][i, 2][None], "rms_mem")
        S["kv"] = _mm(S["memn"], W["xa_wkv"], "nn", BF16, "mm_kv", b_blocked=True)
        S["o"] = _attn_fwd(S["q"], S["kv"], "attn_fwd")
        S["y1"] = _mm(S["o"], W["xa_wo"], "nn", BF16, "mm_out", dep=dep)
        end_group(S["y1"])
        Wf, dep = begin_group()
        W.update(Wf)
        xin, h = _post_pre_fwd(xin, S["y1"], small["xa_norm"][i, 1][None], small["ffn_norm"][i, 0][None], "post_pre")
        S["x2"], S["h2"] = xin, h
        S["dact_dgate"], S["dact_dup"], S["act"] = _ffn_gu_fwd(h, W["ffn_w_gu"], "ffn_gu_fwd", dep=dep)
        dep = mid_group(S["act"])
        if "ffn_w_down" not in W:
            end_group(S["act"])
            Wd, dep = begin_group()
            W.update(Wd)
        S["y2"] = _mm(S["act"], W["ffn_w_down"], "nn", BF16, "mm_down", dep=dep)
        end_group(S["y2"])
        if i + 1 < depth:
            xin, h = _post_pre_fwd(xin, S["y2"], small["ffn_norm"][i, 1][None], small["mix_norm"][i + 1, 0][None],
                                   "post_pre")
        saved.append(S)

    last = saved[-1]
    loss_part, dx, dy, dg = _final_fwd_loss(xin, last["y2"], small["ffn_norm"][depth - 1, 1][None], target, "final_loss")
    loss = lax.psum(loss_part[0, 0], ("x", "y", "c"))

    g_mix = [[zero_row, zero_row] for _ in range(depth)]
    g_xa = [[zero_row, zero_row, zero_row] for _ in range(depth)]
    g_ffn = [[zero_row, zero_row] for _ in range(depth)]
    g_small = {}
    recv = {n: [None] * P[n].shape[0] for n in _BIG}
    g_ffn[depth - 1][1] = dg

    bwd = {"queue": [], "token": no_token}
    scatters_in_flight = 3

    def finish_scatter(after, keep):
        while len(bwd["queue"]) > keep:
            names, tag, sems, srcs, lands = bwd["queue"].pop(0)
            for (n, j, _), r in zip(names, _exchange_wait(sems, srcs, lands, "scatter", after, "rs_wait_" + tag)):
                recv[n][j] = r

    def scatter_group(names, tag, after):
        finish_scatter(after, scatters_in_flight - 1)
        parts = [g if n in _COL_SHARDED else g.reshape(N_DEV, -1, g.shape[-1]) for n, _, g in names]
        sems, srcs, lands, bwd["token"] = _exchange_start(parts, None, "scatter", after, "rs_start_" + tag)
        bwd["queue"].append((names, tag, sems, srcs, lands))

    def scatter_token():
        return bwd["token"]

    for i in reversed(range(depth)):
        kind, slot = i % N_MIXERS, i // N_MIXERS
        mx = _MIXER[kind]
        S = saved[i]
        W = S["W"]
        dgu = _ffn_dgu_bwd(dy, W["ffn_w_down"], S["dact_dgate"], S["dact_dup"], "ffn_dgu_bwd", dep=scatter_token())
        dw_down = _mm(S["act"], dy, "tn", BF16, "mm_dw_down")
        dh = _mm(dgu, W["ffn_w_gu"], "nt", BF16, "mm_dh_gu", a_blocked=True, b_blocked=True)
        dw_gu = _mm(S["h2"], dgu, "tn", BF16, "mm_dw_gu", b_blocked=True, out_blocks=N_DEV)
        dx, g_ffn[i][0], dy, g_xa[i][1] = _pre_post_bwd(dx, dh, S["x2"], small["ffn_norm"][i, 0][None], S["y1"],
                                                         small["xa_norm"][i, 1][None], "pre_post_bwd")
        scatter_group([("ffn_w_gu", i, dw_gu), ("ffn_w_down", i, dw_down)], "%d_2" % i, dx)
        do = _mm(dy, W["xa_wo"], "nt", BF16, "mm_nt_dd16", dep=scatter_token())
        dw_o = _mm(S["o"], dy, "tn", BF16, "mm_dw_dd")
        dq, dkv = _attn_bwd(S["q"], S["kv"], do, "attn_bwd")
        dkv16 = dkv.astype(BF16)
        dh = _mm(dq, W["xa_wq"], "nt", BF16, "mm_nt_dd16")
        dw_q = _mm(S["h1"], dq, "tn", BF16, "mm_dw_dd")
        dw_kv = _mm(S["memn"], dkv16, "tn", BF16, "mm_dw_kv", out_blocks=N_DEV)
        dmemn = _mm(dkv16, W["xa_wkv"], "nt", F32, "mm_dmem", b_blocked=True)
        g_xa[i][2] = _rms_gain_grad(dmemn, mem0, "rms_gain_grad")
        dx, g_xa[i][0], dy, g_mix[i][1] = _pre_post_bwd(dx, dh, S["x1"], small["xa_norm"][i, 0][None], S["y0"],
                                                         small["mix_norm"][i, 1][None], "pre_post_bwd")
        scatter_group([("xa_wq", i, dw_q), ("xa_wkv", i, dw_kv), ("xa_wo", i, dw_o)], "%d_1" % i, dx)
        dmid = _mm(dy, W["w_out"], "nt", BF16, "mm_nt_dd16", dep=scatter_token())
        dw_out = _mm(S["mid"], dy, "tn", BF16, "mm_dw_dd")
        if i == 0:
            scatter_group([(mx + "_w_out", slot, dw_out)], "0_0_out", dy)
        if kind == 0:
            dpre, dcw = _a_mid_bwd(S["pre"], dmid, small["a_conv_w"][slot], "a_mid_bwd")
            g_small.setdefault("a_conv_w", {})[slot] = dcw
        elif kind == 1:
            dpre, dws, dsb, dvg, dvb = _b_mid_bwd(S["pre"], dmid, b_v_g, b_v_b, b_w_s[0], bias_b, "b_mid_bwd")
            dsb_row = jnp.pad(jnp.transpose(dsb[:, :G]).reshape(1, G * C), ((0, 0), (0, (-G * C) % D)))
            g_small.update(b_w_s=dws.reshape(-1, D), b_s_bias=dsb_row.reshape(-1, D), b_v_g=dvg, b_v_b=dvb)
        else:
            dpre, dcw, dcb, dlg, dlb = _c_mid_bwd(S["pre"], S["y2c"], dmid, small["c_conv_w"][slot], small["c_ln_g"],
                                                  small["c_ln_b"], "c_mid_bwd")
            g_small.update(c_conv_w=dcw, c_conv_b=dcb, c_ln_g=dlg, c_ln_b=dlb)
        dh = _mm(dpre, W["w_in"], "nt", BF16, "mm_dh_in_%s" % mx, b_blocked=True, dep=scatter_token())
        dw_in = _mm(S["h0"], dpre, "tn", BF16, "mm_dw_in_%s" % mx, out_blocks=N_DEV)
        if i > 0:
            dx, g_mix[i][0], dy, g_ffn[i - 1][1] = _pre_post_bwd(
                dx, dh, S["x0"], small["mix_norm"][i, 0][None], saved[i - 1]["y2"],
                small["ffn_norm"][i - 1, 1][None], "pre_post_bwd")
        else:
            dx, g_mix[i][0] = _pre_post_bwd(dx, dh, S["x0"], small["mix_norm"][i, 0][None], None, None, "pre_bwd")
        scatter_group([(mx + "_w_in", slot, dw_in)] + ([(mx + "_w_out", slot, dw_out)] if i > 0 else []),
                      "%d_0" % i, dx)
        S.clear()
    grad_x = dx[None]

    sh_off = dict(zip(_SMALL_SHARDED, _row_layout(sh_shapes, dl)[0]))
    rep_offs, n_rep = _row_layout([P[n].shape for n in _SMALL_REPLICATED], D)
    rep_off = {n: n_sh + o for n, o in zip(_SMALL_REPLICATED, rep_offs)}
    pieces = []
    for i in range(depth):
        pieces += [(g, sh_off["mix_norm"] + 2 * i + j) for j, g in enumerate(g_mix[i])]
        pieces += [(g, sh_off["xa_norm"] + 3 * i + j) for j, g in enumerate(g_xa[i])]
        pieces += [(g, sh_off["ffn_norm"] + 2 * i + j) for j, g in enumerate(g_ffn[i])]
    pieces += [(g, sh_off["a_conv_w"] + a_conv_w.shape[1] * s) for s, g in g_small["a_conv_w"].items()]
    pieces += [(g_small[n], sh_off[n]) for n in ("c_conv_w", "c_conv_b", "c_ln_g", "c_ln_b")]
    pieces += [(g_small[n], rep_off[n]) for n in _SMALL_REPLICATED]
    part_all = _assemble_rows(pieces, n_sh + n_rep, D, "pack_small_grads")
    small_sems, small_srcs, small_lands, started = _exchange_start([part_all], None, "gather_all", scatter_token(),
                                                                   "ag_small_grads_start")

    out = {}

    def adam_small(names, recv_s, width, name):
        shapes = [P[n].shape for n in names]
        pw = _pack_rows([P[n] for n in names], width, 0.0)
        pm = _pack_rows([P["m_" + n] for n in names], width, 0.0)
        pv = _pack_rows([P["v_" + n] for n in names], width, 1.0)
        res = _reduce_adam([recv_s], pw[None], pm[None], pv[None], name)
        for kind, r in zip(("grad", "delta", "new_m", "new_v"), res):
            for n, a in zip(names, _unpack_rows(r[0], shapes)):
                out[kind + "_" + n] = a

    def lands_after(n):
        return max([0] + [k + 1 for k, entry in enumerate(bwd["queue"]) if any(m == n for m, _, _ in entry[0])])

    early_done = [g_xa[i][2] for i in range(depth)]
    for n in sorted(_BIG, key=lands_after):
        while any(r is None for r in recv[n]):
            finish_scatter(early_done, len(bwd["queue"]) - 1)
        res = _reduce_adam(recv[n], P[n], P["m_" + n], P["v_" + n], "adam_" + n, dep=started)
        early_done.append(res[0])
        for kind, r in zip(("grad", "delta", "new_m", "new_v"), res):
            out[kind + "_" + n] = r

    parts_all = _exchange_wait(small_sems, small_srcs, small_lands, "gather_all", early_done,
                               "ag_small_grads_wait")[0]
    recv_sh = lax.dynamic_slice_in_dim(parts_all[:, :n_sh], my_slot * dl, dl, axis=2)
    adam_small(_SMALL_SHARDED, recv_sh, dl, "adam_small_sharded")
    adam_small(_SMALL_REPLICATED, parts_all[:, n_sh:], D, "adam_small_replicated")

    return (loss, grad_x, *[out[k + "_" + n] for k in ("grad", "delta", "new_m", "new_v") for n in _WEIGHTS])
```

```python
import functools

import jax
import jax.numpy as jnp
from jax import lax
from jax.experimental import pallas as pl
from jax.experimental.pallas import tpu as pltpu

F32 = jnp.float32
BF16 = jnp.bfloat16
MESH = pl.DeviceIdType.MESH
ANY = pl.BlockSpec(memory_space=pl.ANY)

N_DEV = 8
N_MIXERS = 3
XA_HEADS = 4
GMLP_GROUPS = 8
CHUNK = 128
NORM_EPS = 1e-6
HALO = 32
ROW_TILE = 256
CONV_LANES = 512
V7X_VMEM_LIMIT = 56 * 1024 * 1024

ADAM_LR = 0.001
ADAM_B1 = 0.9
ADAM_B2 = 0.999
ADAM_EPS = 1e-08
ADAM_WD = 0.01
ADAM_STEP = 10


def _pcall(body, **kw):
    return pl.pallas_call(body, **kw)


def _params(*sem):
    return pltpu.CompilerParams(dimension_semantics=sem, vmem_limit_bytes=V7X_VMEM_LIMIT)


def _fit(n, pref, mult=128):
    if n <= pref:
        return n
    t = (pref // mult) * mult
    while t >= mult:
        if n % t == 0:
            return t
        t -= mult
    return n


def _sds(shape, dtype):
    return jax.ShapeDtypeStruct(shape, dtype)


_DIMS = {"nn": (((1,), (0,)), ((), ())), "nt": (((1,), (1,)), ((), ())), "tn": (((0,), (0,)), ((), ()))}
MM_VMEM_BUDGET = 44 * 1024 * 1024


def _gcd(a, b):
    while b:
        a, b = b, a % b
    return a


def _mm(a, b, mode, out_dtype, name, *, a_blocked=False, b_blocked=False, out_blocks=None, dep=None):
    if mode == "tn":
        K, M = a.shape
    elif a_blocked:
        sa, M, ca = a.shape
        K = sa * ca
    else:
        M, K = a.shape
    n_unit = k_unit = None
    if b_blocked:
        _, d1, cb = b.shape
        if mode == "nt":
            N, k_unit = d1, cb
        else:
            N, n_unit = b.shape[0] * cb, cb
    else:
        N = b.shape[0] if mode == "nt" else b.shape[1]
    n_unit = n_unit or N
    k_unit = k_unit or K
    if a_blocked:
        k_unit = _gcd(k_unit, ca)
    if out_blocks:
        n_unit = _gcd(n_unit, N // out_blocks)
    tn = _fit(n_unit, 1536)
    tk = k_unit
    out_bytes = jnp.dtype(out_dtype).itemsize

    def need(tm_, gk_=1):
        nk_ = K // (tk * gk_)
        return (4 * gk_ * (tm_ * tk + tk * tn) + 2 * tm_ * tn * out_bytes
                + 4 * tm_ * tn * ((2 if nk_ > 1 else 1) + (1 if gk_ > 1 else 0)))

    tm = _fit(M, 1024)
    gk = 1
    if b_blocked and mode == "nt" and tk == cb:
        for cand in (8, 4, 2):
            if b.shape[0] % cand == 0 and (not a_blocked or (ca // tk) % cand == 0) and need(tm, cand) <= MM_VMEM_BUDGET:
                gk = cand
                break
    while need(tm, gk) > MM_VMEM_BUDGET and tm % 256 == 0:
        tm //= 2
    nk = K // (tk * gk)

    if mode == "tn":
        a_spec = pl.BlockSpec((tk, tm), lambda i, j, k: (k, i))
    elif a_blocked:
        ka = ca // (tk * gk)
        a_spec = pl.BlockSpec((None, tm, tk * gk), lambda i, j, k: (k // ka, i, k % ka))
    else:
        a_spec = pl.BlockSpec((tm, tk * gk), lambda i, j, k: (i, k))
    if b_blocked and mode == "nt" and gk > 1:
        b_spec = pl.BlockSpec((gk, tn, tk), lambda i, j, k: (k, j, 0))
    elif b_blocked and mode == "nt":
        kb = cb // tk
        b_spec = pl.BlockSpec((None, tn, tk), lambda i, j, k: (k // kb, j, k % kb))
    elif b_blocked:
        nb = cb // tn
        b_spec = pl.BlockSpec((None, tk, tn), lambda i, j, k: (j // nb, k, j % nb))
    elif mode == "nt":
        b_spec = pl.BlockSpec((tn, tk), lambda i, j, k: (j, k))
    else:
        b_spec = pl.BlockSpec((tk, tn), lambda i, j, k: (k, j))
    if out_blocks:
        ob = (N // out_blocks) // tn
        out_shape = _sds((out_blocks, M, N // out_blocks), out_dtype)
        o_spec = pl.BlockSpec((None, tm, tn), lambda i, j, k: (j // ob, i, j % ob))
    else:
        out_shape = _sds((M, N), out_dtype)
        o_spec = pl.BlockSpec((tm, tn), lambda i, j, k: (i, j))
    dims = _DIMS[mode]
    n_in = 2 if dep is None else 3

    def body(*refs):
        a_ref, b_ref = refs[0], refs[1]
        o_ref = refs[n_in]
        if gk == 1:
            p = lax.dot_general(a_ref[...], b_ref[...], dims, preferred_element_type=F32)
        else:
            p = lax.dot_general(a_ref[:, 0:tk], b_ref[0], dims, preferred_element_type=F32)
            for s in range(1, gk):
                p = p + lax.dot_general(a_ref[:, s * tk:(s + 1) * tk], b_ref[s], dims, preferred_element_type=F32)
        if nk == 1:
            o_ref[...] = p.astype(o_ref.dtype)
            return
        acc_ref = refs[n_in + 1]
        k = pl.program_id(2)

        @pl.when(k == 0)
        def _():
            acc_ref[...] = p

        @pl.when(k > 0)
        def _():
            acc_ref[...] += p

        @pl.when(k == nk - 1)
        def _():
            o_ref[...] = acc_ref[...].astype(o_ref.dtype)

    ins, in_specs = [a, b], [a_spec, b_spec]
    if dep is not None:
        ins.append(dep)
        in_specs.append(ANY)
    return _pcall(
        body, name=name, out_shape=out_shape, grid=(M // tm, N // tn, nk),
        in_specs=in_specs, out_specs=o_spec,
        scratch_shapes=[pltpu.VMEM((tm, tn), F32)] if nk > 1 else [],
        compiler_params=_params("parallel", "parallel", "arbitrary"),
    )(*ins)


def _rstd(v):
    return lax.rsqrt(jnp.mean(v * v, axis=-1, keepdims=True) + NORM_EPS)


def _rms_bwd_rows(v, g, dout):
    r = _rstd(v)
    vh = v * r
    dvh = dout * g
    dv = r * (dvh - vh * jnp.mean(dvh * vh, axis=-1, keepdims=True))
    return dv, jnp.sum(dout * vh, axis=0, keepdims=True)


def _row(tt, d, col=0):
    return pl.BlockSpec((tt, d), lambda i: (i, col))


def _const(shape):
    return pl.BlockSpec(shape, lambda i: (0,) * len(shape))


def _prev_halo(tt, d, col=0):
    return pl.BlockSpec((HALO, d), lambda i: (jnp.maximum(i * (tt // HALO) - 1, 0), col))


def _next_halo(tt, d, rows, col=0):
    last = rows // HALO - 1
    return pl.BlockSpec((HALO, d), lambda i: (jnp.minimum((i + 1) * (tt // HALO), last), col))


def _rms_fwd(x, g, name):
    T, D = x.shape
    tt = _fit(T, ROW_TILE, 8)

    def body(x_ref, g_ref, h_ref):
        v = x_ref[...]
        h_ref[...] = (v * _rstd(v) * g_ref[...]).astype(BF16)

    return _pcall(body, name=name, out_shape=_sds((T, D), BF16), grid=(T // tt,),
                  in_specs=[_row(tt, D), _const((1, D))], out_specs=_row(tt, D),
                  compiler_params=_params("parallel"))(x, g)


def _post_pre_fwd(x, y, g_post, g_pre, name):
    T, D = x.shape
    tt = _fit(T, ROW_TILE, 8)

    def body(x_ref, y_ref, gp_ref, gn_ref, xo_ref, h_ref):
        y = y_ref[...].astype(F32)
        xn = x_ref[...] + y * _rstd(y) * gp_ref[...]
        xo_ref[...] = xn
        h_ref[...] = (xn * _rstd(xn) * gn_ref[...]).astype(BF16)

    return _pcall(body, name=name, out_shape=(_sds((T, D), F32), _sds((T, D), BF16)), grid=(T // tt,),
                  in_specs=[_row(tt, D), _row(tt, D), _const((1, D)), _const((1, D))],
                  out_specs=(_row(tt, D), _row(tt, D)),
                  compiler_params=_params("parallel"))(x, y, g_post, g_pre)


def _final_fwd_loss(x, y, g_post, target, name):
    T, D = x.shape
    tt = _fit(T, ROW_TILE, 8)

    def body(x_ref, y_ref, g_ref, t_ref, loss_ref, dx_ref, dy_ref, dg_ref):
        i = pl.program_id(0)
        y = y_ref[...].astype(F32)
        g = g_ref[...]
        err = x_ref[...] + y * _rstd(y) * g - t_ref[...]
        part = 0.5 * jnp.sum(jnp.mean(err * err, axis=-1, keepdims=True))
        dx = err / D
        dx_ref[...] = dx
        dy, dg = _rms_bwd_rows(y, g, dx)
        dy_ref[...] = dy.astype(BF16)

        @pl.when(i == 0)
        def _():
            loss_ref[...] = jnp.zeros_like(loss_ref)
            dg_ref[...] = jnp.zeros_like(dg_ref)

        loss_ref[...] += part
        dg_ref[...] += dg

    return _pcall(body, name=name,
                  out_shape=(_sds((1, 128), F32), _sds((T, D), F32), _sds((T, D), BF16), _sds((1, D), F32)),
                  grid=(T // tt,),
                  in_specs=[_row(tt, D), _row(tt, D), _const((1, D)), _row(tt, D)],
                  out_specs=(_const((1, 128)), _row(tt, D), _row(tt, D), _const((1, D))),
                  compiler_params=_params("arbitrary"))(x, y, g_post, target)


def _pre_post_bwd(dx_out, dh, x_in, g_pre, y_prev, g_post_prev, name):
    T, D = x_in.shape
    tt = _fit(T, ROW_TILE, 8)
    with_prev = y_prev is not None

    def body(*refs):
        if with_prev:
            dxo_ref, dh_ref, x_ref, g_ref, y_ref, gp_ref, dxi_ref, dg_ref, dy_ref, dgp_ref = refs
        else:
            dxo_ref, dh_ref, x_ref, g_ref, dxi_ref, dg_ref = refs
        i = pl.program_id(0)
        dv, dg = _rms_bwd_rows(x_ref[...], g_ref[...], dh_ref[...].astype(F32))
        dxi = dxo_ref[...] + dv
        dxi_ref[...] = dxi

        @pl.when(i == 0)
        def _():
            dg_ref[...] = jnp.zeros_like(dg_ref)

        dg_ref[...] += dg
        if with_prev:
            dy, dgp = _rms_bwd_rows(y_ref[...].astype(F32), gp_ref[...], dxi)
            dy_ref[...] = dy.astype(BF16)

            @pl.when(i == 0)
            def _():
                dgp_ref[...] = jnp.zeros_like(dgp_ref)

            dgp_ref[...] += dgp

    ins = [dx_out, dh, x_in, g_pre]
    in_specs = [_row(tt, D), _row(tt, D), _row(tt, D), _const((1, D))]
    out_shape = [_sds((T, D), F32), _sds((1, D), F32)]
    out_specs = [_row(tt, D), _const((1, D))]
    if with_prev:
        ins += [y_prev, g_post_prev]
        in_specs += [_row(tt, D), _const((1, D))]
        out_shape += [_sds((T, D), BF16), _sds((1, D), F32)]
        out_specs += [_row(tt, D), _const((1, D))]
    return _pcall(body, name=name, out_shape=tuple(out_shape), grid=(T // tt,),
                  in_specs=in_specs, out_specs=tuple(out_specs),
                  compiler_params=_params("arbitrary"))(*ins)


def _rms_gain_grad(dout, v, name):
    T, D = v.shape
    tt = _fit(T, ROW_TILE, 8)

    def body(d_ref, v_ref, dg_ref):
        @pl.when(pl.program_id(0) == 0)
        def _():
            dg_ref[...] = jnp.zeros_like(dg_ref)

        v = v_ref[...]
        dg_ref[...] += jnp.sum(d_ref[...] * (v * _rstd(v)), axis=0, keepdims=True)

    return _pcall(body, name=name, out_shape=_sds((1, D), F32), grid=(T // tt,),
                  in_specs=[_row(tt, D), _row(tt, D)], out_specs=_const((1, D)),
                  compiler_params=_params("arbitrary"))(dout, v)


def _softmax_rows(s):
    e = jnp.exp(s - jnp.max(s, axis=-1, keepdims=True))
    return e / jnp.sum(e, axis=-1, keepdims=True)


def _attn_fwd(q, kv, name):
    T, D = q.shape
    nm = kv.shape[0]
    hd = D // XA_HEADS
    scale = hd ** -0.5
    tq = _fit(T, ROW_TILE, 8)

    def body(q_ref, k_ref, v_ref, o_ref):
        for h in range(XA_HEADS):
            sl = slice(h * hd, (h + 1) * hd)
            s = lax.dot_general(q_ref[:, sl], k_ref[:, sl], _DIMS["nt"], preferred_element_type=F32) * scale
            p = _softmax_rows(s)
            o_ref[:, sl] = jnp.dot(p.astype(BF16), v_ref[:, sl], preferred_element_type=F32).astype(BF16)

    return _pcall(body, name=name, out_shape=_sds((T, D), BF16), grid=(T // tq,),
                  in_specs=[_row(tq, D), pl.BlockSpec((nm, D), lambda i: (0, 0)), pl.BlockSpec((nm, D), lambda i: (0, 1))],
                  out_specs=_row(tq, D), compiler_params=_params("parallel"))(q, kv, kv)


def _attn_bwd(q, kv, do, name):
    T, D = q.shape
    nm = kv.shape[0]
    hd = D // XA_HEADS
    scale = hd ** -0.5
    tq = _fit(T, ROW_TILE, 8)

    def body(q_ref, k_ref, v_ref, do_ref, dq_ref, dkv_ref):
        @pl.when(pl.program_id(0) == 0)
        def _():
            dkv_ref[...] = jnp.zeros_like(dkv_ref)

        for h in range(XA_HEADS):
            sl = slice(h * hd, (h + 1) * hd)
            qh, kh, vh, doh = q_ref[:, sl], k_ref[:, sl], v_ref[:, sl], do_ref[:, sl]
            s = lax.dot_general(qh, kh, _DIMS["nt"], preferred_element_type=F32) * scale
            p = _softmax_rows(s)
            dp = lax.dot_general(doh, vh, _DIMS["nt"], preferred_element_type=F32)
            ds = (p * (dp - jnp.sum(dp * p, axis=-1, keepdims=True)) * scale).astype(BF16)
            dq_ref[:, sl] = jnp.dot(ds, kh, preferred_element_type=F32).astype(BF16)
            dkv_ref[:, sl] += lax.dot_general(ds, qh, _DIMS["tn"], preferred_element_type=F32)
            dkv_ref[:, D + h * hd:D + (h + 1) * hd] += lax.dot_general(
                p.astype(BF16), doh, _DIMS["tn"], preferred_element_type=F32)

    return _pcall(body, name=name, out_shape=(_sds((T, D), BF16), _sds((nm, 2 * D), F32)), grid=(T // tq,),
                  in_specs=[_row(tq, D), pl.BlockSpec((nm, D), lambda i: (0, 0)), pl.BlockSpec((nm, D), lambda i: (0, 1)),
                            _row(tq, D)],
                  out_specs=(_row(tq, D), _const((nm, 2 * D))),
                  compiler_params=_params("arbitrary"))(q, kv, kv, do)


def _ffn_gu_fwd(h, w_gu, name, dep=None, tm=512):
    T, D = h.shape
    S, _, c = w_gu.shape
    F = S * c // 2
    tm = _fit(T, tm)
    tn = _fit(c, 1536)
    nb = c // tn
    nj = F // tn
    n_in = 3 if dep is None else 4

    def w_spec(off):
        return pl.BlockSpec((None, D, tn), lambda i, j: ((j + off) // nb, 0, (j + off) % nb))

    def body(*refs):
        h_ref, wg_ref, wu_ref = refs[:3]
        dg_ref, du_ref, a_ref = refs[n_in:]
        hv = h_ref[...]
        g = jnp.dot(hv, wg_ref[...], preferred_element_type=F32)
        sg = jax.nn.sigmoid(g)
        silu = g * sg
        du_ref[...] = silu.astype(BF16)
        u = jnp.dot(hv, wu_ref[...], preferred_element_type=F32)
        dg_ref[...] = (u * (sg + silu * (1.0 - sg))).astype(BF16)
        a_ref[...] = (silu * u).astype(BF16)

    ins = [h, w_gu, w_gu]
    in_specs = [pl.BlockSpec((tm, D), lambda i, j: (i, 0)), w_spec(0), w_spec(nj)]
    if dep is not None:
        ins.append(dep)
        in_specs.append(ANY)
    o_spec = pl.BlockSpec((tm, tn), lambda i, j: (i, j))
    return _pcall(body, name=name, out_shape=(_sds((T, F), BF16),) * 3, grid=(T // tm, nj),
                  in_specs=in_specs, out_specs=(o_spec,) * 3,
                  compiler_params=_params("parallel", "parallel"))(*ins)


def _ffn_dgu_bwd(dy, w_down, dact_dgate, dact_dup, name, dep=None, tm=1024):
    T, D = dy.shape
    F = w_down.shape[0]
    tm = _fit(T, tm)
    tn = _fit(F, 512)
    cn = _fit(tn, 256)
    n_in = 4 if dep is None else 5

    def body(*refs):
        dy_ref, wd_ref, g_ref, u_ref = refs[:4]
        o_ref = refs[n_in]
        dyv = dy_ref[...]
        for n0 in range(0, tn, cn):
            da = lax.dot_general(dyv, wd_ref[n0:n0 + cn, :], _DIMS["nt"], preferred_element_type=F32)
            o_ref[0, :, n0:n0 + cn] = (da * g_ref[:, n0:n0 + cn].astype(F32)).astype(BF16)
            o_ref[1, :, n0:n0 + cn] = (da * u_ref[:, n0:n0 + cn].astype(F32)).astype(BF16)

    ins = [dy, w_down, dact_dgate, dact_dup]
    gu_spec = pl.BlockSpec((tm, tn), lambda i, j: (i, j))
    in_specs = [pl.BlockSpec((tm, D), lambda i, j: (i, 0)), pl.BlockSpec((tn, D), lambda i, j: (j, 0)), gu_spec, gu_spec]
    if dep is not None:
        ins.append(dep)
        in_specs.append(ANY)
    return _pcall(body, name=name, out_shape=_sds((2, T, F), BF16), grid=(T // tm, F // tn),
                  in_specs=in_specs, out_specs=pl.BlockSpec((2, tm, tn), lambda i, j: (0, i, j)),
                  compiler_params=_params("parallel", "parallel"))(*ins)


def _causal_taps(win, width, tt):
    for b in range(min(8, width)):
        wb = win if b == 0 else pltpu.roll(win, b, 0)
        a = 0
        while 8 * a + b <= width - 1:
            yield width - 1 - (8 * a + b), wb[HALO - 8 * a:HALO - 8 * a + tt]
            a += 1


def _anticausal_taps(win, width, tt):
    rows = tt + HALO
    for b in range(min(8, width)):
        wb = win if b == 0 else pltpu.roll(win, rows - b, 0)
        a = 0
        while 8 * a + b <= width - 1:
            yield width - 1 - (8 * a + b), wb[8 * a:8 * a + tt]
            a += 1


WIDE_CONV_LANES = 256
WIDE_CONV_ROWS = 64


def _tap_list(width):
    return [(width - 1 - (8 * a + b), b, a) for b in range(min(8, width)) for a in range((width - 1 - b) // 8 + 1)]


def _lanes(d):
    cw = _fit(d, CONV_LANES)
    return [slice(s, s + cw) for s in range(0, d, cw)], cw


def _a_mid_fwd(bcz, conv_w, name):
    T, D3 = bcz.shape
    D = D3 // 3
    width = conv_w.shape[0]
    tt = _fit(T, ROW_TILE, HALO)
    chunks, cw = _lanes(D)

    def body(b_ref, c_ref, z_ref, ch_ref, zh_ref, w_ref, o_ref, win_ref):
        i = pl.program_id(0)
        for sl in chunks:
            uh = ch_ref[:, sl].astype(F32) * zh_ref[:, sl].astype(F32)
            win_ref[0:HALO, :] = jnp.where(i > 0, uh, 0.0)
            win_ref[HALO:, :] = c_ref[:, sl].astype(F32) * z_ref[:, sl].astype(F32)
            acc = jnp.zeros((tt, cw), F32)
            for k, xs in _causal_taps(win_ref[...], width, tt):
                acc = acc + w_ref[k:k + 1, sl] * xs
            o_ref[:, sl] = (b_ref[:, sl].astype(F32) * acc).astype(BF16)

    return _pcall(body, name=name, out_shape=_sds((T, D), BF16), grid=(T // tt,),
                  in_specs=[_row(tt, D, 0), _row(tt, D, 1), _row(tt, D, 2), _prev_halo(tt, D, 1), _prev_halo(tt, D, 2),
                            _const((width, D))],
                  out_specs=_row(tt, D), scratch_shapes=[pltpu.VMEM((HALO + tt, cw), F32)],
                  compiler_params=_params("parallel"))(bcz, bcz, bcz, bcz, bcz, conv_w)


def _a_mid_bwd(bcz, dy2, conv_w, name):
    T, D3 = bcz.shape
    D = D3 // 3
    width = conv_w.shape[0]
    tt = _fit(T, ROW_TILE, HALO)
    chunks, cw = _lanes(D)
    n_tiles = T // tt

    def body(b_ref, c_ref, z_ref, ch_ref, zh_ref, bn_ref, d_ref, dn_ref, w_ref, o_ref, dw_ref, win_ref, dwin_ref):
        i = pl.program_id(0)

        @pl.when(i == 0)
        def _():
            dw_ref[...] = jnp.zeros_like(dw_ref)

        for ci, sl in enumerate(chunks):
            c = c_ref[:, sl].astype(F32)
            z = z_ref[:, sl].astype(F32)
            b = b_ref[:, sl].astype(F32)
            d2 = d_ref[:, sl].astype(F32)
            uh = ch_ref[:, sl].astype(F32) * zh_ref[:, sl].astype(F32)
            win_ref[0:HALO, :] = jnp.where(i > 0, uh, 0.0)
            win_ref[HALO:, :] = c * z
            d1 = d2 * b
            d1n = dn_ref[:, sl].astype(F32) * bn_ref[:, sl].astype(F32)
            dwin_ref[0:tt, :] = d1
            dwin_ref[tt:, :] = jnp.where(i < n_tiles - 1, d1n, 0.0)
            y1 = jnp.zeros((tt, cw), F32)
            for k, xs in _causal_taps(win_ref[...], width, tt):
                y1 = y1 + w_ref[k:k + 1, sl] * xs
                dw_ref[k:k + 1, sl] += jnp.sum(d1 * xs, axis=0, keepdims=True)
            du = jnp.zeros((tt, cw), F32)
            for k, xs in _anticausal_taps(dwin_ref[...], width, tt):
                du = du + w_ref[k:k + 1, sl] * xs
            o_ref[:, ci * cw:(ci + 1) * cw] = (d2 * y1).astype(BF16)
            o_ref[:, D + ci * cw:D + (ci + 1) * cw] = (du * z).astype(BF16)
            o_ref[:, 2 * D + ci * cw:2 * D + (ci + 1) * cw] = (du * c).astype(BF16)

    return _pcall(body, name=name, out_shape=(_sds((T, 3 * D), BF16), _sds((width, D), F32)), grid=(n_tiles,),
                  in_specs=[_row(tt, D, 0), _row(tt, D, 1), _row(tt, D, 2), _prev_halo(tt, D, 1), _prev_halo(tt, D, 2),
                            _next_halo(tt, D, T, 0), _row(tt, D), _next_halo(tt, D, T), _const((width, D))],
                  out_specs=(_row(tt, 3 * D), _const((width, D))),
                  scratch_shapes=[pltpu.VMEM((HALO + tt, cw), F32), pltpu.VMEM((tt + HALO, cw), F32)],
                  compiler_params=_params("arbitrary"))(bcz, bcz, bcz, bcz, bcz, bcz, dy2, dy2, conv_w)


_GELU_C = 0.7978845608028654
_GELU_A = 0.044715


def _gelu(v):
    return 0.5 * v * (1.0 + jnp.tanh(_GELU_C * (v + _GELU_A * v * v * v)))


def _gelu_grad(v):
    t = jnp.tanh(_GELU_C * (v + _GELU_A * v * v * v))
    return 0.5 * (1.0 + t) + 0.5 * v * (1.0 - t * t) * (_GELU_C * (1.0 + 3.0 * _GELU_A * v * v))


def _ln_stats(v):
    mu = jnp.mean(v, axis=-1, keepdims=True)
    vc = v - mu
    return vc * lax.rsqrt(jnp.mean(vc * vc, axis=-1, keepdims=True) + NORM_EPS)


def _tril(n):
    return lax.broadcasted_iota(jnp.int32, (n, n), 0) >= lax.broadcasted_iota(jnp.int32, (n, n), 1)


def _b_mid_fwd(uv, v_g, v_b, w_s, bias_b, name):
    T, D2 = uv.shape
    D = D2 // 2
    G, C, _ = w_s.shape
    gd = D // G
    tt = _fit(T, ROW_TILE, C)

    def body(u_ref, v_ref, g_ref, b_ref, ws_ref, bias_ref, o_ref, vln_ref):
        vln_ref[...] = (_ln_stats(_gelu(v_ref[...].astype(F32))) * g_ref[...] + b_ref[...]).astype(BF16)
        mask = _tril(C)
        for g in range(G):
            wsm = jnp.where(mask, ws_ref[g], 0.0).astype(BF16)
            cs = slice(g * gd, (g + 1) * gd)
            for n in range(tt // C):
                rs = slice(n * C, (n + 1) * C)
                sv = jnp.dot(wsm, vln_ref[rs, cs], preferred_element_type=F32) + bias_ref[g]
                o_ref[rs, cs] = (_gelu(u_ref[rs, cs].astype(F32)) * sv).astype(BF16)

    return _pcall(body, name=name, out_shape=_sds((T, D), BF16), grid=(T // tt,),
                  in_specs=[_row(tt, D, 0), _row(tt, D, 1), _const((1, D)), _const((1, D)), _const((G, C, C)),
                            _const((G, C, gd))],
                  out_specs=_row(tt, D), scratch_shapes=[pltpu.VMEM((tt, D), BF16)],
                  compiler_params=_params("parallel"))(uv, uv, v_g, v_b, w_s, bias_b)


def _b_mid_bwd(uv, dgated, v_g, v_b, w_s, bias_b, name):
    T, D2 = uv.shape
    D = D2 // 2
    G, C, _ = w_s.shape
    gd = D // G
    tt = _fit(T, ROW_TILE, C)

    def body(u_ref, v_ref, d_ref, g_ref, b_ref, ws_ref, bias_ref, o_ref, dws_ref, dsb_ref, dvg_ref, dvb_ref,
             vln_ref, dvln_ref):
        @pl.when(pl.program_id(0) == 0)
        def _():
            dws_ref[...] = jnp.zeros_like(dws_ref)
            dsb_ref[...] = jnp.zeros_like(dsb_ref)
            dvg_ref[...] = jnp.zeros_like(dvg_ref)
            dvb_ref[...] = jnp.zeros_like(dvb_ref)

        vpre = v_ref[...].astype(F32)
        vhat = _ln_stats(_gelu(vpre))
        vln_ref[...] = (vhat * g_ref[...] + b_ref[...]).astype(BF16)
        mask = _tril(C)
        lane = lax.broadcasted_iota(jnp.int32, (C, 128), 1)
        for g in range(G):
            wsm = jnp.where(mask, ws_ref[g], 0.0).astype(BF16)
            cs = slice(g * gd, (g + 1) * gd)
            for n in range(tt // C):
                rs = slice(n * C, (n + 1) * C)
                vt = vln_ref[rs, cs]
                sv = jnp.dot(wsm, vt, preferred_element_type=F32) + bias_ref[g]
                dg = d_ref[rs, cs].astype(F32)
                upre = u_ref[rs, cs].astype(F32)
                o_ref[rs, cs] = (dg * sv * _gelu_grad(upre)).astype(BF16)
                dsv = dg * _gelu(upre)
                dsb_ref[...] += jnp.where(lane == g, jnp.sum(dsv, axis=-1, keepdims=True), 0.0)
                dsv16 = dsv.astype(BF16)
                dws_ref[g] += jnp.where(mask, lax.dot_general(dsv16, vt, _DIMS["nt"], preferred_element_type=F32), 0.0)
                dvln_ref[rs, cs] = lax.dot_general(wsm, dsv16, _DIMS["tn"], preferred_element_type=F32)
        dvln = dvln_ref[...]
        dvg_ref[...] += jnp.sum(dvln * vhat, axis=0, keepdims=True)
        dvb_ref[...] += jnp.sum(dvln, axis=0, keepdims=True)
        dvh = dvln * g_ref[...]
        vc = _gelu(vpre)
        vc = vc - jnp.mean(vc, axis=-1, keepdims=True)
        rstd = lax.rsqrt(jnp.mean(vc * vc, axis=-1, keepdims=True) + NORM_EPS)
        dv = rstd * (dvh - jnp.mean(dvh, axis=-1, keepdims=True) - vhat * jnp.mean(dvh * vhat, axis=-1, keepdims=True))
        o_ref[:, D:] = (dv * _gelu_grad(vpre)).astype(BF16)

    return _pcall(body, name=name,
                  out_shape=(_sds((T, 2 * D), BF16), _sds((G, C, C), F32), _sds((C, 128), F32), _sds((1, D), F32),
                             _sds((1, D), F32)),
                  grid=(T // tt,),
                  in_specs=[_row(tt, D, 0), _row(tt, D, 1), _row(tt, D), _const((1, D)), _const((1, D)),
                            _const((G, C, C)), _const((G, C, gd))],
                  out_specs=(_row(tt, 2 * D), _const((G, C, C)), _const((C, 128)), _const((1, D)), _const((1, D))),
                  scratch_shapes=[pltpu.VMEM((tt, D), BF16), pltpu.VMEM((tt, D), F32)],
                  compiler_params=_params("arbitrary"))(uv, uv, dgated, v_g, v_b, w_s, bias_b)


def _c_mid_fwd(ag, conv_w, conv_b, ln_g, ln_b, name):
    T, D2 = ag.shape
    D = D2 // 2
    width = conv_w.shape[0]
    tt = _fit(T, ROW_TILE, HALO)
    cw = _fit(D, WIDE_CONV_LANES)
    rb = _fit(tt, WIDE_CONV_ROWS, 8)
    taps = _tap_list(width)

    def body(a_ref, g_ref, ah_ref, gh_ref, w_ref, cb_ref, lg_ref, lb_ref, y2_ref, o_ref, win_ref, roll_ref):
        i = pl.program_id(0)
        yh = ah_ref[...].astype(F32) * jax.nn.sigmoid(gh_ref[...].astype(F32))
        win_ref[0:HALO, :] = jnp.where(i > 0, yh, 0.0)
        win_ref[HALO:, :] = a_ref[...].astype(F32) * jax.nn.sigmoid(g_ref[...].astype(F32))
        for c0 in range(0, D, cw):
            sl = slice(c0, c0 + cw)
            wv = win_ref[:, sl]
            for b in range(1, min(8, width)):
                roll_ref[b - 1] = pltpu.roll(wv, b, 0)
            for r0 in range(0, tt, rb):
                acc = jnp.zeros((rb, cw), F32)
                for k, b, a in taps:
                    lo = HALO - 8 * a + r0
                    xs = win_ref[lo:lo + rb, sl] if b == 0 else roll_ref[b - 1, lo:lo + rb, :]
                    acc = acc + w_ref[k:k + 1, sl] * xs
                y2_ref[r0:r0 + rb, sl] = acc + cb_ref[:, sl]
        y3 = _ln_stats(y2_ref[...]) * lg_ref[...] + lb_ref[...]
        o_ref[...] = (y3 * jax.nn.sigmoid(y3)).astype(BF16)

    return _pcall(body, name=name, out_shape=(_sds((T, D), F32), _sds((T, D), BF16)), grid=(T // tt,),
                  in_specs=[_row(tt, D, 0), _row(tt, D, 1), _prev_halo(tt, D, 0), _prev_halo(tt, D, 1),
                            _const((width, D)), _const((1, D)), _const((1, D)), _const((1, D))],
                  out_specs=(_row(tt, D), _row(tt, D)),
                  scratch_shapes=[pltpu.VMEM((HALO + tt, D), F32), pltpu.VMEM((7, HALO + tt, cw), F32)],
                  compiler_params=_params("parallel"))(ag, ag, ag, ag, conv_w, conv_b, ln_g, ln_b)


def _c_mid_bwd(ag, y2, dy4, conv_w, ln_g, ln_b, name):
    T, D2 = ag.shape
    D = D2 // 2
    width = conv_w.shape[0]
    tt = _fit(T, ROW_TILE, HALO)
    cw = _fit(D, WIDE_CONV_LANES)
    rb = _fit(tt, WIDE_CONV_ROWS, 8)
    taps = _tap_list(width)
    n_tiles = T // tt

    def ln_silu_bwd(y2v, dy4v, lg, lb):
        mu = jnp.mean(y2v, axis=-1, keepdims=True)
        yc = y2v - mu
        rstd = lax.rsqrt(jnp.mean(yc * yc, axis=-1, keepdims=True) + NORM_EPS)
        yh = yc * rstd
        y3 = yh * lg + lb
        sg = jax.nn.sigmoid(y3)
        dy3 = dy4v * (sg * (1.0 + y3 * (1.0 - sg)))
        dyh = dy3 * lg
        dy2 = rstd * (dyh - jnp.mean(dyh, axis=-1, keepdims=True) - yh * jnp.mean(dyh * yh, axis=-1, keepdims=True))
        return dy2, dy3, yh

    def body(a_ref, g_ref, ah_ref, gh_ref, y2_ref, y2n_ref, d_ref, dn_ref, w_ref, lg_ref, lb_ref,
             o_ref, dw_ref, dcb_ref, dlg_ref, dlb_ref, win_ref, dwin_ref, roll_ref, droll_ref):
        i = pl.program_id(0)

        @pl.when(i == 0)
        def _():
            dw_ref[...] = jnp.zeros_like(dw_ref)
            dcb_ref[...] = jnp.zeros_like(dcb_ref)
            dlg_ref[...] = jnp.zeros_like(dlg_ref)
            dlb_ref[...] = jnp.zeros_like(dlb_ref)

        lg = lg_ref[...]
        lb = lb_ref[...]
        dy2, dy3, yh = ln_silu_bwd(y2_ref[...], d_ref[...].astype(F32), lg, lb)
        dlg_ref[...] += jnp.sum(dy3 * yh, axis=0, keepdims=True)
        dlb_ref[...] += jnp.sum(dy3, axis=0, keepdims=True)
        dcb_ref[...] += jnp.sum(dy2, axis=0, keepdims=True)
        dwin_ref[0:tt, :] = dy2
        dy2n, _, _ = ln_silu_bwd(y2n_ref[...], dn_ref[...].astype(F32), lg, lb)
        dwin_ref[tt:, :] = jnp.where(i < n_tiles - 1, dy2n, 0.0)
        yh1 = ah_ref[...].astype(F32) * jax.nn.sigmoid(gh_ref[...].astype(F32))
        win_ref[0:HALO, :] = jnp.where(i > 0, yh1, 0.0)
        win_ref[HALO:, :] = a_ref[...].astype(F32) * jax.nn.sigmoid(g_ref[...].astype(F32))
        rows = tt + HALO
        for c0 in range(0, D, cw):
            sl = slice(c0, c0 + cw)
            wv = win_ref[:, sl]
            dv = dwin_ref[:, sl]
            for b in range(1, min(8, width)):
                roll_ref[b - 1] = pltpu.roll(wv, b, 0)
                droll_ref[b - 1] = pltpu.roll(dv, rows - b, 0)
            for k, b, a in taps:
                acc = jnp.zeros((8, cw), F32)
                for r0 in range(0, tt, rb):
                    lo = HALO - 8 * a + r0
                    xs = win_ref[lo:lo + rb, sl] if b == 0 else roll_ref[b - 1, lo:lo + rb, :]
                    prod = dwin_ref[r0:r0 + rb, sl] * xs
                    for q in range(0, rb, 8):
                        acc = acc + prod[q:q + 8]
                dw_ref[k:k + 1, sl] += jnp.sum(acc, axis=0, keepdims=True)
            for r0 in range(0, tt, rb):
                d1 = jnp.zeros((rb, cw), F32)
                for k, b, a in taps:
                    lo = 8 * a + r0
                    xs = dwin_ref[lo:lo + rb, sl] if b == 0 else droll_ref[b - 1, lo:lo + rb, :]
                    d1 = d1 + w_ref[k:k + 1, sl] * xs
                av = a_ref[r0:r0 + rb, sl].astype(F32)
                sg = jax.nn.sigmoid(g_ref[r0:r0 + rb, sl].astype(F32))
                o_ref[r0:r0 + rb, c0:c0 + cw] = (d1 * sg).astype(BF16)
                o_ref[r0:r0 + rb, D + c0:D + c0 + cw] = (d1 * av * sg * (1.0 - sg)).astype(BF16)

    return _pcall(body, name=name,
                  out_shape=(_sds((T, 2 * D), BF16), _sds((width, D), F32), _sds((1, D), F32), _sds((1, D), F32),
                             _sds((1, D), F32)),
                  grid=(n_tiles,),
                  in_specs=[_row(tt, D, 0), _row(tt, D, 1), _prev_halo(tt, D, 0), _prev_halo(tt, D, 1),
                            _row(tt, D), _next_halo(tt, D, T), _row(tt, D), _next_halo(tt, D, T),
                            _const((width, D)), _const((1, D)), _const((1, D))],
                  out_specs=(_row(tt, 2 * D), _const((width, D)), _const((1, D)), _const((1, D)), _const((1, D))),
                  scratch_shapes=[pltpu.VMEM((HALO + tt, D), F32), pltpu.VMEM((tt + HALO, D), F32),
                                  pltpu.VMEM((7, HALO + tt, cw), F32), pltpu.VMEM((7, tt + HALO, cw), F32)],
                  compiler_params=_params("arbitrary"))(ag, ag, ag, ag, y2, y2, dy4, dy4, conv_w, ln_g, ln_b)


def _place():
    x, y, c = lax.axis_index("x"), lax.axis_index("y"), lax.axis_index("c")
    return x, y, c


def _slot(px, py, pc):
    return 4 * px + 2 * py + pc


def _all_gather(shards, name):
    n = len(shards)

    def body(*refs):
        ins, outs = refs[:n], refs[n:2 * n]
        send_sems, recv_sems, local_sems = refs[2 * n:]
        x, y, c = _place()
        me, sibling = (x, y, c), (x, y, 1 - c)
        chips = [(1 - x, y), (x, 1 - y), (1 - x, 1 - y)]

        def copy(t, k, block, to, src=None):
            dst = outs[t].at[_slot(*block)]
            return pltpu.make_async_remote_copy(
                src_ref=dst if src is None else src, dst_ref=dst, send_sem=send_sems.at[t, k],
                recv_sem=recv_sems.at[t, k], device_id=to, device_id_type=MESH)

        mine = [pltpu.make_async_copy(ins[t], outs[t].at[_slot(*me)], local_sems.at[t]) for t in range(n)]
        for cp in mine:
            cp.start()
        first = []
        for j, chip in enumerate(chips):
            first += [copy(t, 1 + j, me, (*chip, c), src=ins[t]) for t in range(n)]
        first += [copy(t, 0, me, sibling, src=ins[t]) for t in range(n)]
        for cp in first:
            cp.start()
        passed = []
        for j, chip in enumerate(chips):
            for t in range(n):
                copy(t, 1 + j, (*chip, c), me).wait_recv()
                cp = copy(t, 4 + j, (*chip, c), sibling)
                cp.start()
                passed.append(cp)
        for t in range(n):
            copy(t, 0, sibling, me).wait_recv()
            for j, chip in enumerate(chips):
                copy(t, 4 + j, (*chip, 1 - c), me).wait_recv()
        for cp in first + passed:
            cp.wait_send()
        for cp in mine:
            cp.wait()

    outs = _pcall(
        body, name=name, out_shape=tuple(_sds((N_DEV,) + s.shape, s.dtype) for s in shards),
        in_specs=[ANY] * n, out_specs=(ANY,) * n,
        scratch_shapes=[pltpu.SemaphoreType.DMA((n, 7)), pltpu.SemaphoreType.DMA((n, 7)), pltpu.SemaphoreType.DMA((n,))],
    )(*shards)
    return list(outs)


_HBM = pl.BlockSpec(memory_space=pltpu.HBM)
_SEM = pl.BlockSpec(memory_space=pltpu.SEMAPHORE)
_DATAFLOW = pltpu.SideEffectType.DATAFLOW_SIDE_EFFECTING


def _peers(x, y, c):
    out = []
    for j in range(1, N_DEV):
        fx, fy, fc = (j >> 2) & 1, (j >> 1) & 1, j & 1
        out.append((1 - x if fx else x, 1 - y if fy else y, 1 - c if fc else c))
    return out


def _exchange_copies(ins, zones, mode, sems):
    send_sem, recv_sem, local_sem = sems
    x, y, c = _place()
    me = _slot(x, y, c)
    sibling = (x, y, 1 - c)
    chips = [(1 - x, y), (x, 1 - y), (1 - x, 1 - y)]
    local, remote = [], []

    def add(src, dst, to, landed):
        remote.append((pltpu.make_async_remote_copy(src_ref=src, dst_ref=dst, send_sem=send_sem, recv_sem=recv_sem,
                                                    device_id=to, device_id_type=MESH), landed))

    for t, zone in enumerate(zones):
        if mode == "scatter":
            local.append(pltpu.make_async_copy(ins[t].at[me], zone.at[me], local_sem))
            for peer in _peers(x, y, c):
                add(ins[t].at[_slot(*peer)], zone.at[me], peer, zone.at[_slot(*peer)])
        elif mode in ("gather_chips", "gather_all"):
            local.append(pltpu.make_async_copy(ins[t], zone.at[me], local_sem))
            for peer in ([(*chip, c) for chip in chips] + [sibling] if mode == "gather_chips" else _peers(x, y, c)):
                add(ins[t], zone.at[me], peer, zone.at[_slot(*peer)])
        else:
            for chip in chips:
                block = zone.at[_slot(*chip, c)]
                add(block, block, sibling, zone.at[_slot(*chip, 1 - c)])
    return local, remote


def _exchange_start(srcs, lands, mode, after, name):
    if lands is None:
        lands = [lax.empty(s.shape if mode == "scatter" else (N_DEV,) + s.shape, s.dtype) for s in srcs]
    ns, na = len(srcs), len(srcs) + len(lands)

    def body(*refs):
        local, remote = _exchange_copies(refs[:ns], refs[ns:na], mode, refs[na + 1:na + 4])
        for cp in local:
            cp.start()
        for cp, _ in remote:
            cp.start()
        refs[-1][...] = jnp.zeros_like(refs[-1])

    hbm = lambda a: pltpu.with_memory_space_constraint(a, pltpu.HBM)
    arrays = list(srcs) + list(lands)
    outs = _pcall(
        body, name=name,
        out_shape=(pltpu.SemaphoreType.DMA(()),) * 3
        + tuple(pltpu.HBM(a.shape, a.dtype) for a in arrays) + (_sds((8, 128), F32),),
        in_specs=[_HBM] * na + [ANY],
        out_specs=(_SEM,) * 3 + (_HBM,) * na + (pl.BlockSpec(memory_space=pltpu.VMEM),),
        input_output_aliases={t: 3 + t for t in range(na)},
        compiler_params=pltpu.CompilerParams(has_side_effects=_DATAFLOW),
    )(*[hbm(a) for a in arrays], after)
    return outs[:3], list(outs[3:3 + ns]), list(outs[3 + ns:3 + na]), outs[-1]


def _exchange_wait(sems, srcs, lands, mode, after, name):
    ns, na = len(srcs), len(srcs) + len(lands)
    afters = list(after) if isinstance(after, (list, tuple)) else [after]

    def body(*refs):
        local, remote = _exchange_copies(refs[:ns], refs[ns:na], mode, refs[na:na + 3])
        for cp in local:
            cp.wait()
        for cp, landed in remote:
            cp.wait_send()
            pltpu.make_async_remote_copy(
                src_ref=landed, dst_ref=landed, send_sem=refs[na], recv_sem=refs[na + 1],
                device_id=_place(), device_id_type=MESH).wait_recv()

    outs = _pcall(
        body, name=name, out_shape=tuple(pltpu.HBM(a.shape, a.dtype) for a in list(srcs) + list(lands)),
        in_specs=[_HBM] * na + [_SEM] * 3 + [ANY] * len(afters), out_specs=(_HBM,) * na,
        input_output_aliases={t: t for t in range(na)},
        compiler_params=pltpu.CompilerParams(has_side_effects=_DATAFLOW),
    )(*srcs, *lands, *sems, *afters)
    return list(outs[ns:])


def _reduce_adam(recvs, w, m, v, name, dep=None):
    L, r, c = w.shape
    tr = _fit(r, max(16, (256 * 1024) // c), 16)
    ni = r // tr

    def recv_spec(l0):
        def index(l, i):
            return 0, jnp.where(l == l0, i, jnp.where(l < l0, 0, ni - 1)), 0
        return pl.BlockSpec((N_DEV, tr, c), index)

    lay = pl.BlockSpec((None, tr, c), lambda l, i: (l, i, 0))

    n_dep = 0 if dep is None else 1

    def body(*refs):
        recv_refs = refs[:L]
        w_ref, m_ref, v_ref = refs[L:L + 3]
        g_out, d_out, m_out, v_out = refs[L + 3 + n_dep:]
        l = pl.program_id(0)
        for l0 in range(L):
            @pl.when(l == l0)
            def _(l0=l0):
                g = recv_refs[l0][0].astype(F32)
                for s in range(1, N_DEV):
                    g = g + recv_refs[l0][s].astype(F32)
                mn = ADAM_B1 * m_ref[...] + (1.0 - ADAM_B1) * g
                vn = ADAM_B2 * v_ref[...] + (1.0 - ADAM_B2) * (g * g)
                m_hat = mn / (1.0 - ADAM_B1 ** ADAM_STEP)
                v_hat = vn / (1.0 - ADAM_B2 ** ADAM_STEP)
                g_out[...] = g
                d_out[...] = -ADAM_LR * (m_hat / (jnp.sqrt(v_hat) + ADAM_EPS) + ADAM_WD * w_ref[...])
                m_out[...] = mn
                v_out[...] = vn

    return _pcall(body, name=name, out_shape=(_sds((L, r, c), F32),) * 4, grid=(L, ni),
                  in_specs=[recv_spec(l0) for l0 in range(L)] + [lay, lay, lay] + [ANY] * n_dep, out_specs=(lay,) * 4,
                  compiler_params=_params("arbitrary", "arbitrary"))(*recvs, w, m, v, *([dep] if n_dep else []))


_SMALL_SHARDED = ["mix_norm", "xa_norm", "ffn_norm", "a_conv_w", "c_conv_w", "c_conv_b", "c_ln_g", "c_ln_b"]
_SMALL_REPLICATED = ["b_v_g", "b_v_b", "b_w_s", "b_s_bias"]
_BIG = ["xa_wq", "xa_wkv", "xa_wo", "ffn_w_gu", "ffn_w_down", "a_w_in", "a_w_out", "b_w_in", "b_w_out", "c_w_in",
        "c_w_out"]
_COL_SHARDED = {"xa_wkv", "ffn_w_gu", "a_w_in", "b_w_in", "c_w_in"}
_WEIGHTS = ["mix_norm", "xa_norm", "xa_wq", "xa_wkv", "xa_wo", "ffn_norm", "ffn_w_gu", "ffn_w_down", "a_w_in",
            "a_conv_w", "a_w_out", "b_w_in", "b_v_g", "b_v_b", "b_w_s", "b_s_bias", "b_w_out", "c_w_in", "c_conv_w",
            "c_conv_b", "c_ln_g", "c_ln_b", "c_w_out"]
_MIXER = "abc"


def _size(shape):
    size = 1
    for s in shape:
        size *= s
    return size


def _row_layout(shapes, width):
    offs, r = [], 0
    for shape in shapes:
        offs.append(r)
        r += -(-(-(-_size(shape) // width)) // 8) * 8
    return offs, r


def _pack_rows(arrays, width, fill):
    offs, total = _row_layout([a.shape for a in arrays], width)
    ends = offs[1:] + [total]
    rows = [jnp.pad(a.reshape(-1), (0, (e - o) * width - a.size), constant_values=fill).reshape(e - o, width)
            for a, o, e in zip(arrays, offs, ends)]
    return jnp.concatenate(rows, axis=0)


def _unpack_rows(packed, like):
    width = packed.shape[-1]
    offs, _ = _row_layout(like, width)
    return [packed[o:o + -(-_size(s) // width)].reshape(-1)[:_size(s)].reshape(s) for o, s in zip(offs, like)]


def _assemble_rows(pieces, rows, width, name):
    n = len(pieces)

    def body(*refs):
        o_ref = refs[n]
        o_ref[...] = jnp.zeros_like(o_ref)
        for r, (a, off) in zip(refs[:n], pieces):
            o_ref[off:off + a.shape[0], :] = r[...]

    return _pcall(body, name=name, out_shape=_sds((rows, width), F32),
                  compiler_params=pltpu.CompilerParams(vmem_limit_bytes=V7X_VMEM_LIMIT))(*[a for a, _ in pieces])


def kernel(x, mem, mix_norm, xa_norm, xa_wq, xa_wkv, xa_wo, ffn_norm, ffn_w_gu, ffn_w_down, a_w_in, a_conv_w, a_w_out, b_w_in, b_v_g, b_v_b, b_w_s, b_s_bias, b_w_out, c_w_in, c_conv_w, c_conv_b, c_ln_g, c_ln_b, c_w_out, loss_target, m_mix_norm, m_xa_norm, m_xa_wq, m_xa_wkv, m_xa_wo, m_ffn_norm, m_ffn_w_gu, m_ffn_w_down, m_a_w_in, m_a_conv_w, m_a_w_out, m_b_w_in, m_b_v_g, m_b_v_b, m_b_w_s, m_b_s_bias, m_b_w_out, m_c_w_in, m_c_conv_w, m_c_conv_b, m_c_ln_g, m_c_ln_b, m_c_w_out, v_mix_norm, v_xa_norm, v_xa_wq, v_xa_wkv, v_xa_wo, v_ffn_norm, v_ffn_w_gu, v_ffn_w_down, v_a_w_in, v_a_conv_w, v_a_w_out, v_b_w_in, v_b_v_g, v_b_v_b, v_b_w_s, v_b_s_bias, v_b_w_out, v_c_w_in, v_c_conv_w, v_c_conv_b, v_c_ln_g, v_c_ln_b, v_c_w_out):
    P = dict(locals())
    T, D = x.shape[1], x.shape[2]
    dl = D // N_DEV
    depth = mix_norm.shape[0]
    x0, mem0, target = x[0], mem[0], loss_target[0]
    my_slot = _slot(*_place())

    sh_shapes = [P[n].shape for n in _SMALL_SHARDED]
    packed = _pack_rows([P[n] for n in _SMALL_SHARDED], dl, 0.0)
    n_sh = packed.shape[0]
    gathered = _all_gather([packed], "ag_small")[0]
    full_rows = jnp.transpose(gathered, (1, 0, 2)).reshape(n_sh, D)
    small = dict(zip(_SMALL_SHARDED, _unpack_rows(full_rows, [s[:-1] + (D,) for s in sh_shapes])))
    G, C = b_w_s.shape[1], b_w_s.shape[2]
    gd = D // G
    bias_b = jnp.broadcast_to(b_s_bias[0][:, :, None], (G, C, gd))
    zero_row = jnp.zeros((1, D), F32)

    groups = [(i, part) for i in range(depth) for part in (("in", "out", "xa", "gu", "down") if i == 0 else
                                                           ("mix", "xa", "ffn"))]

    def group_names(i, part):
        mx, slot = _MIXER[i % N_MIXERS], i // N_MIXERS
        names = {"in": [(mx + "_w_in", slot)], "out": [(mx + "_w_out", slot)],
                 "xa": [("xa_wq", i), ("xa_wkv", i), ("xa_wo", i)],
                 "gu": [("ffn_w_gu", i)], "down": [("ffn_w_down", i)]}
        names["mix"] = names["in"] + names["out"]
        names["ffn"] = names["gu"] + names["down"]
        return names[part]

    no_token = jnp.zeros((8, 128), F32)
    fwd = {"g": 0, "last": full_rows, "token": no_token, "stage1": {}, "stage2": {}}

    def tag(g):
        return "%d_%s" % groups[g]

    def start_stage1(g):
        if g < len(groups):
            names = group_names(*groups[g])
            sems, srcs, lands, token = _exchange_start([P[n][j].astype(BF16) for n, j in names], None, "gather_chips",
                                                       fwd["last"], "ag_start_" + tag(g))
            fwd["stage1"][g] = (sems, srcs, lands)
            fwd["last"] = fwd["token"] = token

    def start_stage2(g, after):
        if g in fwd["stage1"]:
            sems, srcs, lands = fwd["stage1"].pop(g)
            lands = _exchange_wait(sems, srcs, lands, "gather_chips", after, "ag_wait_" + tag(g))
            sems, _, lands, token = _exchange_start([], lands, "gather_sibling", after, "ag_pass_" + tag(g))
            fwd["stage2"][g] = (sems, lands)
            fwd["last"] = fwd["token"] = token

    def begin_group():
        g, y = fwd["g"], fwd["last"]
        start_stage2(g, y)
        sems, lands = fwd["stage2"].pop(g)
        fulls = _exchange_wait(sems, [], lands, "gather_sibling", y, "ag_done_" + tag(g))
        fwd["last"] = fulls[0]
        start_stage1(g + 3)
        out = {}
        for (n, _), f in zip(group_names(*groups[g]), fulls):
            key = n[2:] if n[1] == "_" and n[0] in _MIXER else n
            out[key] = f if n in _COL_SHARDED else f.reshape(-1, f.shape[-1])
        fwd["g"] += 1
        return out, fwd["token"]

    def mid_group(y):
        start_stage2(fwd["g"], y)
        return fwd["token"]

    def end_group(y):
        fwd["last"] = y

    for g0 in range(3):
        start_stage1(g0)
    start_stage2(0, fwd["last"])

    saved = []
    xin = x0
    h = _rms_fwd(x0, small["mix_norm"][0, 0][None], "rms_first")
    for i in range(depth):
        kind, slot = i % N_MIXERS, i // N_MIXERS
        W, dep = begin_group()
        S = {"W": W, "x0": xin, "h0": h}
        pre = _mm(h, W["w_in"], "nn", BF16, "mm_in_%s" % _MIXER[kind], b_blocked=True, dep=dep)
        S["pre"] = pre
        dep = mid_group(pre)
        if kind == 0:
            mid = _a_mid_fwd(pre, small["a_conv_w"][slot], "a_mid_fwd")
        elif kind == 1:
            mid = _b_mid_fwd(pre, b_v_g, b_v_b, b_w_s[0], bias_b, "b_mid_fwd")
        else:
            y2c, mid = _c_mid_fwd(pre, small["c_conv_w"][slot], small["c_conv_b"], small["c_ln_g"], small["c_ln_b"],
                                  "c_mid_fwd")
            S["y2c"] = y2c
        S["mid"] = mid
        if "w_out" not in W:
            end_group(mid)
            Wo, dep = begin_group()
            W.update(Wo)
        S["y0"] = _mm(mid, W["w_out"], "nn", BF16, "mm_out", dep=dep)
        end_group(S["y0"])
        Wx, dep = begin_group()
        W.update(Wx)
        xin, h = _post_pre_fwd(xin, S["y0"], small["mix_norm"][i, 1][None], small["xa_norm"][i, 0][None], "post_pre")
        S["x1"], S["h1"] = xin, h
        S["q"] = _mm(h, W["xa_wq"], "nn", BF16, "mm_q", dep=dep)
        dep = mid_group(S["q"])
        S["memn"] = _rms_fwd(mem0, small["xa_norm"][i, 2][None], "rms_mem")
        S["kv"] = _mm(S["memn"], W["xa_wkv"], "nn", BF16, "mm_kv", b_blocked=True)
        S["o"] = _attn_fwd(S["q"], S["kv"], "attn_fwd")
        S["y1"] = _mm(S["o"], W["xa_wo"], "nn", BF16, "mm_out", dep=dep)
        end_group(S["y1"])
        Wf, dep = begin_group()
        W.update(Wf)
        xin, h = _post_pre_fwd(xin, S["y1"], small["xa_norm"][i, 1][None], small["ffn_norm"][i, 0][None], "post_pre")
        S["x2"], S["h2"] = xin, h
        S["dact_dgate"], S["dact_dup"], S["act"] = _ffn_gu_fwd(h, W["ffn_w_gu"], "ffn_gu_fwd", dep=dep)
        dep = mid_group(S["act"])
        if "ffn_w_down" not in W:
            end_group(S["act"])
            Wd, dep = begin_group()
            W.update(Wd)
        S["y2"] = _mm(S["act"], W["ffn_w_down"], "nn", BF16, "mm_down", dep=dep)
        end_group(S["y2"])
        if i + 1 < depth:
            xin, h = _post_pre_fwd(xin, S["y2"], small["ffn_norm"][i, 1][None], small["mix_norm"][i + 1, 0][None],
                                   "post_pre")
        saved.append(S)

    last = saved[-1]
    loss_part, dx, dy, dg = _final_fwd_loss(xin, last["y2"], small["ffn_norm"][depth - 1, 1][None], target, "final_loss")
    loss = lax.psum(loss_part[0, 0], ("x", "y", "c"))

    g_mix = [[zero_row, zero_row] for _ in range(depth)]
    g_xa = [[zero_row, zero_row, zero_row] for _ in range(depth)]
    g_ffn = [[zero_row, zero_row] for _ in range(depth)]
    g_small = {}
    recv = {n: [None] * P[n].shape[0] for n in _BIG}
    g_ffn[depth - 1][1] = dg

    bwd = {"queue": [], "token": no_token}
    scatters_in_flight = 3

    def finish_scatter(after, keep):
        while len(bwd["queue"]) > keep:
            names, tag, sems, srcs, lands = bwd["queue"].pop(0)
            for (n, j, _), r in zip(names, _exchange_wait(sems, srcs, lands, "scatter", after, "rs_wait_" + tag)):
                recv[n][j] = r

    def scatter_group(names, tag, after):
        finish_scatter(after, scatters_in_flight - 1)
        parts = [g if n in _COL_SHARDED else g.reshape(N_DEV, -1, g.shape[-1]) for n, _, g in names]
        sems, srcs, lands, bwd["token"] = _exchange_start(parts, None, "scatter", after, "rs_start_" + tag)
        bwd["queue"].append((names, tag, sems, srcs, lands))

    def scatter_token():
        return bwd["token"]

    for i in reversed(range(depth)):
        kind, slot = i % N_MIXERS, i // N_MIXERS
        mx = _MIXER[kind]
        S = saved[i]
        W = S["W"]
        dgu = _ffn_dgu_bwd(dy, W["ffn_w_down"], S["dact_dgate"], S["dact_dup"], "ffn_dgu_bwd", dep=scatter_token())
        dw_down = _mm(S["act"], dy, "tn", BF16, "mm_dw_down")
        dh = _mm(dgu, W["ffn_w_gu"], "nt", BF16, "mm_dh_gu", a_blocked=True, b_blocked=True)
        dw_gu = _mm(S["h2"], dgu, "tn", BF16, "mm_dw_gu", b_blocked=True, out_blocks=N_DEV)
        dx, g_ffn[i][0], dy, g_xa[i][1] = _pre_post_bwd(dx, dh, S["x2"], small["ffn_norm"][i, 0][None], S["y1"],
                                                         small["xa_norm"][i, 1][None], "pre_post_bwd")
        scatter_group([("ffn_w_gu", i, dw_gu), ("ffn_w_down", i, dw_down)], "%d_2" % i, dx)
        do = _mm(dy, W["xa_wo"], "nt", BF16, "mm_nt_dd16", dep=scatter_token())
        dw_o = _mm(S["o"], dy, "tn", BF16, "mm_dw_dd")
        dq, dkv = _attn_bwd(S["q"], S["kv"], do, "attn_bwd")
        dkv16 = dkv.astype(BF16)
        dh = _mm(dq, W["xa_wq"], "nt", BF16, "mm_nt_dd16")
        dw_q = _mm(S["h1"], dq, "tn", BF16, "mm_dw_dd")
        dw_kv = _mm(S["memn"], dkv16, "tn", BF16, "mm_dw_kv", out_blocks=N_DEV)
        dmemn = _mm(dkv16, W["xa_wkv"], "nt", F32, "mm_dmem", b_blocked=True)
        g_xa[i][2] = _rms_gain_grad(dmemn, mem0, "rms_gain_grad")
        dx, g_xa[i][0], dy, g_mix[i][1] = _pre_post_bwd(dx, dh, S["x1"], small["xa_norm"][i, 0][None], S["y0"],
                                                         small["mix_norm"][i, 1][None], "pre_post_bwd")
        scatter_group([("xa_wq", i, dw_q), ("xa_wkv", i, dw_kv), ("xa_wo", i, dw_o)], "%d_1" % i, dx)
        dmid = _mm(dy, W["w_out"], "nt", BF16, "mm_nt_dd16", dep=scatter_token())
        dw_out = _mm(S["mid"], dy, "tn", BF16, "mm_dw_dd")
        if i == 0:
            scatter_group([(mx + "_w_out", slot, dw_out)], "0_0_out", dy)
        if kind == 0:
            dpre, dcw = _a_mid_bwd(S["pre"], dmid, small["a_conv_w"][slot], "a_mid_bwd")
            g_small.setdefault("a_conv_w", {})[slot] = dcw
        elif kind == 1:
            dpre, dws, dsb, dvg, dvb = _b_mid_bwd(S["pre"], dmid, b_v_g, b_v_b, b_w_s[0], bias_b, "b_mid_bwd")
            dsb_row = jnp.pad(jnp.transpose(dsb[:, :G]).reshape(1, G * C), ((0, 0), (0, (-G * C) % D)))
            g_small.update(b_w_s=dws.reshape(-1, D), b_s_bias=dsb_row.reshape(-1, D), b_v_g=dvg, b_v_b=dvb)
        else:
            dpre, dcw, dcb, dlg, dlb = _c_mid_bwd(S["pre"], S["y2c"], dmid, small["c_conv_w"][slot], small["c_ln_g"],
                                                  small["c_ln_b"], "c_mid_bwd")
            g_small.update(c_conv_w=dcw, c_conv_b=dcb, c_ln_g=dlg, c_ln_b=dlb)
        dh = _mm(dpre, W["w_in"], "nt", BF16, "mm_dh_in_%s" % mx, b_blocked=True, dep=scatter_token())
        dw_in = _mm(S["h0"], dpre, "tn", BF16, "mm_dw_in_%s" % mx, out_blocks=N_DEV)
        if i > 0:
            dx, g_mix[i][0], dy, g_ffn[i - 1][1] = _pre_post_bwd(
                dx, dh, S["x0"], small["mix_norm"][i, 0][None], saved[i - 1]["y2"],
                small["ffn_norm"][i - 1, 1][None], "pre_post_bwd")
        else:
            dx, g_mix[i][0] = _pre_post_bwd(dx, dh, S["x0"], small["mix_norm"][i, 0][None], None, None, "pre_bwd")
        scatter_group([(mx + "_w_in", slot, dw_in)] + ([(mx + "_w_out", slot, dw_out)] if i > 0 else []),
                      "%d_0" % i, dx)
        S.clear()
    grad_x = dx[None]

    sh_off = dict(zip(_SMALL_SHARDED, _row_layout(sh_shapes, dl)[0]))
    rep_offs, n_rep = _row_layout([P[n].shape for n in _SMALL_REPLICATED], D)
    rep_off = {n: n_sh + o for n, o in zip(_SMALL_REPLICATED, rep_offs)}
    pieces = []
    for i in range(depth):
        pieces += [(g, sh_off["mix_norm"] + 2 * i + j) for j, g in enumerate(g_mix[i])]
        pieces += [(g, sh_off["xa_norm"] + 3 * i + j) for j, g in enumerate(g_xa[i])]
        pieces += [(g, sh_off["ffn_norm"] + 2 * i + j) for j, g in enumerate(g_ffn[i])]
    pieces += [(g, sh_off["a_conv_w"] + a_conv_w.shape[1] * s) for s, g in g_small["a_conv_w"].items()]
    pieces += [(g_small[n], sh_off[n]) for n in ("c_conv_w", "c_conv_b", "c_ln_g", "c_ln_b")]
    pieces += [(g_small[n], rep_off[n]) for n in _SMALL_REPLICATED]
    part_all = _assemble_rows(pieces, n_sh + n_rep, D, "pack_small_grads")
    small_sems, small_srcs, small_lands, started = _exchange_start([part_all], None, "gather_all", scatter_token(),
                                                                   "ag_small_grads_start")

    out = {}

    def adam_small(names, recv_s, width, name):
        shapes = [P[n].shape for n in names]
        pw = _pack_rows([P[n] for n in names], width, 0.0)
        pm = _pack_rows([P["m_" + n] for n in names], width, 0.0)
        pv = _pack_rows([P["v_" + n] for n in names], width, 1.0)
        res = _reduce_adam([recv_s], pw[None], pm[None], pv[None], name)
        for kind, r in zip(("grad", "delta", "new_m", "new_v"), res):
            for n, a in zip(names, _unpack_rows(r[0], shapes)):
                out[kind + "_" + n] = a

    def lands_after(n):
        return max([0] + [k + 1 for k, entry in enumerate(bwd["queue"]) if any(m == n for m, _, _ in entry[0])])

    early_done = [g_xa[i][2] for i in range(depth)]
    for n in sorted(_BIG, key=lands_after):
        while any(r is None for r in recv[n]):
            finish_scatter(early_done, len(bwd["queue"]) - 1)
        res = _reduce_adam(recv[n], P[n], P["m_" + n], P["v_" + n], "adam_" + n, dep=started)
        early_done.append(res[0])
        for kind, r in zip(("grad", "delta", "new_m", "new_v"), res):
            out[kind + "_" + n] = r

    parts_all = _exchange_wait(small_sems, small_srcs, small_lands, "gather_all", early_done,
                               "ag_small_grads_wait")[0]
    recv_sh = lax.dynamic_slice_in_dim(parts_all[:, :n_sh], my_slot * dl, dl, axis=2)
    adam_small(_SMALL_SHARDED, recv_sh, dl, "adam_small_sharded")
    adam_small(_SMALL_REPLICATED, parts_all[:, n_sh:], D, "adam_small_replicated")

    return (loss, grad_x, *[out[k + "_" + n] for k in ("grad", "delta", "new_m", "new_v") for n in _WEIGHTS])
```

```python
import functools

import jax
import jax.numpy as jnp
from jax import lax
from jax.experimental import pallas as pl
from jax.experimental.pallas import tpu as pltpu

F32 = jnp.float32
BF16 = jnp.bfloat16
MESH = pl.DeviceIdType.MESH
ANY = pl.BlockSpec(memory_space=pl.ANY)

N_DEV = 8
N_MIXERS = 3
XA_HEADS = 4
GMLP_GROUPS = 8
CHUNK = 128
NORM_EPS = 1e-6
HALO = 32
ROW_TILE = 256
NORM_ROW_TILE = 512
CONV_LANES = 512
V7X_VMEM_LIMIT = 56 * 1024 * 1024

ADAM_LR = 0.001
ADAM_B1 = 0.9
ADAM_B2 = 0.999
ADAM_EPS = 1e-08
ADAM_WD = 0.01
ADAM_STEP = 10


def _pcall(body, **kw):
    return pl.pallas_call(body, **kw)


def _params(*sem):
    return pltpu.CompilerParams(dimension_semantics=sem, vmem_limit_bytes=V7X_VMEM_LIMIT)


def _fit(n, pref, mult=128):
    if n <= pref:
        return n
    t = (pref // mult) * mult
    while t >= mult:
        if n % t == 0:
            return t
        t -= mult
    return n


def _sds(shape, dtype):
    return jax.ShapeDtypeStruct(shape, dtype)


_DIMS = {"nn": (((1,), (0,)), ((), ())), "nt": (((1,), (1,)), ((), ())), "tn": (((0,), (0,)), ((), ()))}
MM_VMEM_BUDGET = 44 * 1024 * 1024


def _gcd(a, b):
    while b:
        a, b = b, a % b
    return a


def _mm(a, b, mode, out_dtype, name, *, a_blocked=False, b_blocked=False, out_blocks=None, dep=None):
    if mode == "tn":
        K, M = a.shape
    elif a_blocked:
        sa, M, ca = a.shape
        K = sa * ca
    else:
        M, K = a.shape
    n_unit = k_unit = None
    if b_blocked:
        _, d1, cb = b.shape
        if mode == "nt":
            N, k_unit = d1, cb
        else:
            N, n_unit = b.shape[0] * cb, cb
    else:
        N = b.shape[0] if mode == "nt" else b.shape[1]
    n_unit = n_unit or N
    k_unit = k_unit or K
    if a_blocked:
        k_unit = _gcd(k_unit, ca)
    if out_blocks:
        n_unit = _gcd(n_unit, N // out_blocks)
    tn = _fit(n_unit, 1536)
    tk = k_unit
    out_bytes = jnp.dtype(out_dtype).itemsize

    def need(tm_, gk_=1):
        nk_ = K // (tk * gk_)
        return (4 * gk_ * (tm_ * tk + tk * tn) + 2 * tm_ * tn * out_bytes
                + 4 * tm_ * tn * ((2 if nk_ > 1 else 1) + (1 if gk_ > 1 else 0)))

    tm = _fit(M, 1024)
    gk = 1
    if b_blocked and mode == "nt" and tk == cb:
        for cand in (8, 4, 2):
            if b.shape[0] % cand == 0 and (not a_blocked or (ca // tk) % cand == 0) and need(tm, cand) <= MM_VMEM_BUDGET:
                gk = cand
                break
    while need(tm, gk) > MM_VMEM_BUDGET and tm % 256 == 0:
        tm //= 2
    nk = K // (tk * gk)

    if mode == "tn":
        a_spec = pl.BlockSpec((tk, tm), lambda i, j, k: (k, i))
    elif a_blocked:
        ka = ca // (tk * gk)
        a_spec = pl.BlockSpec((None, tm, tk * gk), lambda i, j, k: (k // ka, i, k % ka))
    else:
        a_spec = pl.BlockSpec((tm, tk * gk), lambda i, j, k: (i, k))
    if b_blocked and mode == "nt" and gk > 1:
        b_spec = pl.BlockSpec((gk, tn, tk), lambda i, j, k: (k, j, 0))
    elif b_blocked and mode == "nt":
        kb = cb // tk
        b_spec = pl.BlockSpec((None, tn, tk), lambda i, j, k: (k // kb, j, k % kb))
    elif b_blocked:
        nb = cb // tn
        b_spec = pl.BlockSpec((None, tk, tn), lambda i, j, k: (j // nb, k, j % nb))
    elif mode == "nt":
        b_spec = pl.BlockSpec((tn, tk), lambda i, j, k: (j, k))
    else:
        b_spec = pl.BlockSpec((tk, tn), lambda i, j, k: (k, j))
    if out_blocks:
        ob = (N // out_blocks) // tn
        out_shape = _sds((out_blocks, M, N // out_blocks), out_dtype)
        o_spec = pl.BlockSpec((None, tm, tn), lambda i, j, k: (j // ob, i, j % ob))
    else:
        out_shape = _sds((M, N), out_dtype)
        o_spec = pl.BlockSpec((tm, tn), lambda i, j, k: (i, j))
    dims = _DIMS[mode]
    n_in = 2 if dep is None else 3

    def body(*refs):
        a_ref, b_ref = refs[0], refs[1]
        o_ref = refs[n_in]
        if gk == 1:
            p = lax.dot_general(a_ref[...], b_ref[...], dims, preferred_element_type=F32)
        else:
            p = lax.dot_general(a_ref[:, 0:tk], b_ref[0], dims, preferred_element_type=F32)
            for s in range(1, gk):
                p = p + lax.dot_general(a_ref[:, s * tk:(s + 1) * tk], b_ref[s], dims, preferred_element_type=F32)
        if nk == 1:
            o_ref[...] = p.astype(o_ref.dtype)
            return
        acc_ref = refs[n_in + 1]
        k = pl.program_id(2)

        @pl.when(k == 0)
        def _():
            acc_ref[...] = p

        @pl.when(k > 0)
        def _():
            acc_ref[...] += p

        @pl.when(k == nk - 1)
        def _():
            o_ref[...] = acc_ref[...].astype(o_ref.dtype)

    ins, in_specs = [a, b], [a_spec, b_spec]
    if dep is not None:
        ins.append(dep)
        in_specs.append(ANY)
    return _pcall(
        body, name=name, out_shape=out_shape, grid=(M // tm, N // tn, nk),
        in_specs=in_specs, out_specs=o_spec,
        scratch_shapes=[pltpu.VMEM((tm, tn), F32)] if nk > 1 else [],
        compiler_params=_params("parallel", "parallel", "arbitrary"),
    )(*ins)


def _rstd(v):
    return lax.rsqrt(jnp.mean(v * v, axis=-1, keepdims=True) + NORM_EPS)


def _rms_bwd_rows(v, g, dout):
    r = _rstd(v)
    vh = v * r
    dvh = dout * g
    dv = r * (dvh - vh * jnp.mean(dvh * vh, axis=-1, keepdims=True))
    return dv, jnp.sum(dout * vh, axis=0, keepdims=True)


def _row(tt, d, col=0):
    return pl.BlockSpec((tt, d), lambda i: (i, col))


def _const(shape):
    return pl.BlockSpec(shape, lambda i: (0,) * len(shape))


def _prev_halo(tt, d, col=0):
    return pl.BlockSpec((HALO, d), lambda i: (jnp.maximum(i * (tt // HALO) - 1, 0), col))


def _next_halo(tt, d, rows, col=0):
    last = rows // HALO - 1
    return pl.BlockSpec((HALO, d), lambda i: (jnp.minimum((i + 1) * (tt // HALO), last), col))


def _rms_fwd(x, g, name):
    T, D = x.shape
    tt = _fit(T, ROW_TILE, 8)

    def body(x_ref, g_ref, h_ref):
        v = x_ref[...]
        h_ref[...] = (v * _rstd(v) * g_ref[...]).astype(BF16)

    return _pcall(body, name=name, out_shape=_sds((T, D), BF16), grid=(T // tt,),
                  in_specs=[_row(tt, D), _const((1, D))], out_specs=_row(tt, D),
                  compiler_params=_params("parallel"))(x, g)


def _post_pre_fwd(x, y, g_post, g_pre, name):
    T, D = x.shape
    tt = _fit(T, NORM_ROW_TILE, 16)

    def body(x_ref, y_ref, gp_ref, gn_ref, xo_ref, h_ref):
        y = y_ref[...].astype(F32)
        xn = x_ref[...] + y * _rstd(y) * gp_ref[...]
        xo_ref[...] = xn
        h_ref[...] = (xn * _rstd(xn) * gn_ref[...]).astype(BF16)

    return _pcall(body, name=name, out_shape=(_sds((T, D), F32), _sds((T, D), BF16)), grid=(T // tt,),
                  in_specs=[_row(tt, D), _row(tt, D), _const((1, D)), _const((1, D))],
                  out_specs=(_row(tt, D), _row(tt, D)),
                  compiler_params=_params("parallel"))(x, y, g_post, g_pre)


def _final_fwd_loss(x, y, g_post, target, name):
    T, D = x.shape
    tt = _fit(T, ROW_TILE, 8)

    def body(x_ref, y_ref, g_ref, t_ref, loss_ref, dx_ref, dy_ref, dg_ref):
        i = pl.program_id(0)
        y = y_ref[...].astype(F32)
        g = g_ref[...]
        err = x_ref[...] + y * _rstd(y) * g - t_ref[...]
        part = 0.5 * jnp.sum(jnp.mean(err * err, axis=-1, keepdims=True))
        dx = err / D
        dx_ref[...] = dx
        dy, dg = _rms_bwd_rows(y, g, dx)
        dy_ref[...] = dy.astype(BF16)

        @pl.when(i == 0)
        def _():
            loss_ref[...] = jnp.zeros_like(loss_ref)
            dg_ref[...] = jnp.zeros_like(dg_ref)

        loss_ref[...] += part
        dg_ref[...] += dg

    return _pcall(body, name=name,
                  out_shape=(_sds((1, 128), F32), _sds((T, D), F32), _sds((T, D), BF16), _sds((1, D), F32)),
                  grid=(T // tt,),
                  in_specs=[_row(tt, D), _row(tt, D), _const((1, D)), _row(tt, D)],
                  out_specs=(_const((1, 128)), _row(tt, D), _row(tt, D), _const((1, D))),
                  compiler_params=_params("arbitrary"))(x, y, g_post, target)


def _pre_post_bwd(dx_out, dh, x_in, g_pre, y_prev, g_post_prev, name):
    T, D = x_in.shape
    tt = _fit(T, NORM_ROW_TILE, 16)
    with_prev = y_prev is not None

    def body(*refs):
        if with_prev:
            dxo_ref, dh_ref, x_ref, g_ref, y_ref, gp_ref, dxi_ref, dg_ref, dy_ref, dgp_ref = refs
        else:
            dxo_ref, dh_ref, x_ref, g_ref, dxi_ref, dg_ref = refs
        i = pl.program_id(0)
        dv, dg = _rms_bwd_rows(x_ref[...], g_ref[...], dh_ref[...].astype(F32))
        dxi = dxo_ref[...] + dv
        dxi_ref[...] = dxi

        @pl.when(i == 0)
        def _():
            dg_ref[...] = jnp.zeros_like(dg_ref)

        dg_ref[...] += dg
        if with_prev:
            dy, dgp = _rms_bwd_rows(y_ref[...].astype(F32), gp_ref[...], dxi)
            dy_ref[...] = dy.astype(BF16)

            @pl.when(i == 0)
            def _():
                dgp_ref[...] = jnp.zeros_like(dgp_ref)

            dgp_ref[...] += dgp

    ins = [dx_out, dh, x_in, g_pre]
    in_specs = [_row(tt, D), _row(tt, D), _row(tt, D), _const((1, D))]
    out_shape = [_sds((T, D), F32), _sds((1, D), F32)]
    out_specs = [_row(tt, D), _const((1, D))]
    if with_prev:
        ins += [y_prev, g_post_prev]
        in_specs += [_row(tt, D), _const((1, D))]
        out_shape += [_sds((T, D), BF16), _sds((1, D), F32)]
        out_specs += [_row(tt, D), _const((1, D))]
    return _pcall(body, name=name, out_shape=tuple(out_shape), grid=(T // tt,),
                  in_specs=in_specs, out_specs=tuple(out_specs),
                  compiler_params=_params("arbitrary"))(*ins)


def _rms_gain_grad(dout, v, name):
    T, D = v.shape
    tt = _fit(T, ROW_TILE, 8)

    def body(d_ref, v_ref, dg_ref):
        @pl.when(pl.program_id(0) == 0)
        def _():
            dg_ref[...] = jnp.zeros_like(dg_ref)

        v = v_ref[...]
        dg_ref[...] += jnp.sum(d_ref[...] * (v * _rstd(v)), axis=0, keepdims=True)

    return _pcall(body, name=name, out_shape=_sds((1, D), F32), grid=(T // tt,),
                  in_specs=[_row(tt, D), _row(tt, D)], out_specs=_const((1, D)),
                  compiler_params=_params("arbitrary"))(dout, v)


def _softmax_rows(s):
    e = jnp.exp(s - jnp.max(s, axis=-1, keepdims=True))
    return e / jnp.sum(e, axis=-1, keepdims=True)


def _attn_fwd(q, kv, name):
    T, D = q.shape
    nm = kv.shape[0]
    hd = D // XA_HEADS
    scale = hd ** -0.5
    tq = _fit(T, ROW_TILE, 8)

    def body(q_ref, k_ref, v_ref, o_ref):
        for h in range(XA_HEADS):
            sl = slice(h * hd, (h + 1) * hd)
            s = lax.dot_general(q_ref[:, sl], k_ref[:, sl], _DIMS["nt"], preferred_element_type=F32) * scale
            p = _softmax_rows(s)
            o_ref[:, sl] = jnp.dot(p.astype(BF16), v_ref[:, sl], preferred_element_type=F32).astype(BF16)

    return _pcall(body, name=name, out_shape=_sds((T, D), BF16), grid=(T // tq,),
                  in_specs=[_row(tq, D), pl.BlockSpec((nm, D), lambda i: (0, 0)), pl.BlockSpec((nm, D), lambda i: (0, 1))],
                  out_specs=_row(tq, D), compiler_params=_params("parallel"))(q, kv, kv)


def _attn_bwd(q, kv, do, name):
    T, D = q.shape
    nm = kv.shape[0]
    hd = D // XA_HEADS
    scale = hd ** -0.5
    tq = _fit(T, ROW_TILE, 8)

    def body(q_ref, k_ref, v_ref, do_ref, dq_ref, dkv_ref):
        @pl.when(pl.program_id(0) == 0)
        def _():
            dkv_ref[...] = jnp.zeros_like(dkv_ref)

        for h in range(XA_HEADS):
            sl = slice(h * hd, (h + 1) * hd)
            qh, kh, vh, doh = q_ref[:, sl], k_ref[:, sl], v_ref[:, sl], do_ref[:, sl]
            s = lax.dot_general(qh, kh, _DIMS["nt"], preferred_element_type=F32) * scale
            p = _softmax_rows(s)
            dp = lax.dot_general(doh, vh, _DIMS["nt"], preferred_element_type=F32)
            ds = (p * (dp - jnp.sum(dp * p, axis=-1, keepdims=True)) * scale).astype(BF16)
            dq_ref[:, sl] = jnp.dot(ds, kh, preferred_element_type=F32).astype(BF16)
            dkv_ref[:, sl] += lax.dot_general(ds, qh, _DIMS["tn"], preferred_element_type=F32)
            dkv_ref[:, D + h * hd:D + (h + 1) * hd] += lax.dot_general(
                p.astype(BF16), doh, _DIMS["tn"], preferred_element_type=F32)

    return _pcall(body, name=name, out_shape=(_sds((T, D), BF16), _sds((nm, 2 * D), F32)), grid=(T // tq,),
                  in_specs=[_row(tq, D), pl.BlockSpec((nm, D), lambda i: (0, 0)), pl.BlockSpec((nm, D), lambda i: (0, 1)),
                            _row(tq, D)],
                  out_specs=(_row(tq, D), _const((nm, 2 * D))),
                  compiler_params=_params("arbitrary"))(q, kv, kv, do)


def _ffn_gu_fwd(h, w_gu, name, dep=None, tm=512):
    T, D = h.shape
    S, _, c = w_gu.shape
    F = S * c // 2
    tm = _fit(T, tm)
    tn = _fit(c, 1536)
    nb = c // tn
    nj = F // tn
    n_in = 3 if dep is None else 4

    def w_spec(off):
        return pl.BlockSpec((None, D, tn), lambda i, j: ((j + off) // nb, 0, (j + off) % nb))

    def body(*refs):
        h_ref, wg_ref, wu_ref = refs[:3]
        dg_ref, du_ref, a_ref = refs[n_in:]
        hv = h_ref[...]
        g = jnp.dot(hv, wg_ref[...], preferred_element_type=F32)
        sg = jax.nn.sigmoid(g)
        silu = g * sg
        du_ref[...] = silu.astype(BF16)
        u = jnp.dot(hv, wu_ref[...], preferred_element_type=F32)
        dg_ref[...] = (u * (sg + silu * (1.0 - sg))).astype(BF16)
        a_ref[...] = (silu * u).astype(BF16)

    ins = [h, w_gu, w_gu]
    in_specs = [pl.BlockSpec((tm, D), lambda i, j: (i, 0)), w_spec(0), w_spec(nj)]
    if dep is not None:
        ins.append(dep)
        in_specs.append(ANY)
    o_spec = pl.BlockSpec((tm, tn), lambda i, j: (i, j))
    return _pcall(body, name=name, out_shape=(_sds((T, F), BF16),) * 3, grid=(T // tm, nj),
                  in_specs=in_specs, out_specs=(o_spec,) * 3,
                  compiler_params=_params("parallel", "parallel"))(*ins)


def _ffn_dgu_bwd(dy, w_down, dact_dgate, dact_dup, name, dep=None, tm=1024):
    T, D = dy.shape
    F = w_down.shape[0]
    tm = _fit(T, tm)
    tn = _fit(F, 512)
    cn = _fit(tn, 256)
    n_in = 4 if dep is None else 5

    def body(*refs):
        dy_ref, wd_ref, g_ref, u_ref = refs[:4]
        o_ref = refs[n_in]
        dyv = dy_ref[...]
        for n0 in range(0, tn, cn):
            da = lax.dot_general(dyv, wd_ref[n0:n0 + cn, :], _DIMS["nt"], preferred_element_type=F32)
            o_ref[0, :, n0:n0 + cn] = (da * g_ref[:, n0:n0 + cn].astype(F32)).astype(BF16)
            o_ref[1, :, n0:n0 + cn] = (da * u_ref[:, n0:n0 + cn].astype(F32)).astype(BF16)

    ins = [dy, w_down, dact_dgate, dact_dup]
    gu_spec = pl.BlockSpec((tm, tn), lambda i, j: (i, j))
    in_specs = [pl.BlockSpec((tm, D), lambda i, j: (i, 0)), pl.BlockSpec((tn, D), lambda i, j: (j, 0)), gu_spec, gu_spec]
    if dep is not None:
        ins.append(dep)
        in_specs.append(ANY)
    return _pcall(body, name=name, out_shape=_sds((2, T, F), BF16), grid=(T // tm, F // tn),
                  in_specs=in_specs, out_specs=pl.BlockSpec((2, tm, tn), lambda i, j: (0, i, j)),
                  compiler_params=_params("parallel", "parallel"))(*ins)


def _causal_taps(win, width, tt):
    for b in range(min(8, width)):
        wb = win if b == 0 else pltpu.roll(win, b, 0)
        a = 0
        while 8 * a + b <= width - 1:
            yield width - 1 - (8 * a + b), wb[HALO - 8 * a:HALO - 8 * a + tt]
            a += 1


def _anticausal_taps(win, width, tt):
    rows = tt + HALO
    for b in range(min(8, width)):
        wb = win if b == 0 else pltpu.roll(win, rows - b, 0)
        a = 0
        while 8 * a + b <= width - 1:
            yield width - 1 - (8 * a + b), wb[8 * a:8 * a + tt]
            a += 1


WIDE_CONV_LANES = 256
WIDE_CONV_ROWS = 64


def _tap_list(width):
    return [(width - 1 - (8 * a + b), b, a) for b in range(min(8, width)) for a in range((width - 1 - b) // 8 + 1)]


def _lanes(d):
    cw = _fit(d, CONV_LANES)
    return [slice(s, s + cw) for s in range(0, d, cw)], cw


def _a_mid_fwd(bcz, conv_w, name):
    T, D3 = bcz.shape
    D = D3 // 3
    width = conv_w.shape[0]
    tt = _fit(T, ROW_TILE, HALO)
    chunks, cw = _lanes(D)

    def body(b_ref, c_ref, z_ref, ch_ref, zh_ref, w_ref, o_ref, win_ref):
        i = pl.program_id(0)
        for sl in chunks:
            uh = ch_ref[:, sl].astype(F32) * zh_ref[:, sl].astype(F32)
            win_ref[0:HALO, :] = jnp.where(i > 0, uh, 0.0)
            win_ref[HALO:, :] = c_ref[:, sl].astype(F32) * z_ref[:, sl].astype(F32)
            acc = jnp.zeros((tt, cw), F32)
            for k, xs in _causal_taps(win_ref[...], width, tt):
                acc = acc + w_ref[k:k + 1, sl] * xs
            o_ref[:, sl] = (b_ref[:, sl].astype(F32) * acc).astype(BF16)

    return _pcall(body, name=name, out_shape=_sds((T, D), BF16), grid=(T // tt,),
                  in_specs=[_row(tt, D, 0), _row(tt, D, 1), _row(tt, D, 2), _prev_halo(tt, D, 1), _prev_halo(tt, D, 2),
                            _const((width, D))],
                  out_specs=_row(tt, D), scratch_shapes=[pltpu.VMEM((HALO + tt, cw), F32)],
                  compiler_params=_params("parallel"))(bcz, bcz, bcz, bcz, bcz, conv_w)


def _a_mid_bwd(bcz, dy2, conv_w, name):
    T, D3 = bcz.shape
    D = D3 // 3
    width = conv_w.shape[0]
    tt = _fit(T, ROW_TILE, HALO)
    chunks, cw = _lanes(D)
    n_tiles = T // tt

    def body(b_ref, c_ref, z_ref, ch_ref, zh_ref, bn_ref, d_ref, dn_ref, w_ref, o_ref, dw_ref, win_ref, dwin_ref):
        i = pl.program_id(0)

        @pl.when(i == 0)
        def _():
            dw_ref[...] = jnp.zeros_like(dw_ref)

        for ci, sl in enumerate(chunks):
            c = c_ref[:, sl].astype(F32)
            z = z_ref[:, sl].astype(F32)
            b = b_ref[:, sl].astype(F32)
            d2 = d_ref[:, sl].astype(F32)
            uh = ch_ref[:, sl].astype(F32) * zh_ref[:, sl].astype(F32)
            win_ref[0:HALO, :] = jnp.where(i > 0, uh, 0.0)
            win_ref[HALO:, :] = c * z
            d1 = d2 * b
            d1n = dn_ref[:, sl].astype(F32) * bn_ref[:, sl].astype(F32)
            dwin_ref[0:tt, :] = d1
            dwin_ref[tt:, :] = jnp.where(i < n_tiles - 1, d1n, 0.0)
            y1 = jnp.zeros((tt, cw), F32)
            for k, xs in _causal_taps(win_ref[...], width, tt):
                y1 = y1 + w_ref[k:k + 1, sl] * xs
                dw_ref[k:k + 1, sl] += jnp.sum(d1 * xs, axis=0, keepdims=True)
            du = jnp.zeros((tt, cw), F32)
            for k, xs in _anticausal_taps(dwin_ref[...], width, tt):
                du = du + w_ref[k:k + 1, sl] * xs
            o_ref[:, ci * cw:(ci + 1) * cw] = (d2 * y1).astype(BF16)
            o_ref[:, D + ci * cw:D + (ci + 1) * cw] = (du * z).astype(BF16)
            o_ref[:, 2 * D + ci * cw:2 * D + (ci + 1) * cw] = (du * c).astype(BF16)

    return _pcall(body, name=name, out_shape=(_sds((T, 3 * D), BF16), _sds((width, D), F32)), grid=(n_tiles,),
                  in_specs=[_row(tt, D, 0), _row(tt, D, 1), _row(tt, D, 2), _prev_halo(tt, D, 1), _prev_halo(tt, D, 2),
                            _next_halo(tt, D, T, 0), _row(tt, D), _next_halo(tt, D, T), _const((width, D))],
                  out_specs=(_row(tt, 3 * D), _const((width, D))),
                  scratch_shapes=[pltpu.VMEM((HALO + tt, cw), F32), pltpu.VMEM((tt + HALO, cw), F32)],
                  compiler_params=_params("arbitrary"))(bcz, bcz, bcz, bcz, bcz, bcz, dy2, dy2, conv_w)


_GELU_C = 0.7978845608028654
_GELU_A = 0.044715


def _gelu(v):
    return 0.5 * v * (1.0 + jnp.tanh(_GELU_C * (v + _GELU_A * v * v * v)))


def _gelu_grad(v):
    t = jnp.tanh(_GELU_C * (v + _GELU_A * v * v * v))
    return 0.5 * (1.0 + t) + 0.5 * v * (1.0 - t * t) * (_GELU_C * (1.0 + 3.0 * _GELU_A * v * v))


def _ln_stats(v):
    mu = jnp.mean(v, axis=-1, keepdims=True)
    vc = v - mu
    return vc * lax.rsqrt(jnp.mean(vc * vc, axis=-1, keepdims=True) + NORM_EPS)


def _tril(n):
    return lax.broadcasted_iota(jnp.int32, (n, n), 0) >= lax.broadcasted_iota(jnp.int32, (n, n), 1)


def _b_mid_fwd(uv, v_g, v_b, w_s, bias_b, name):
    T, D2 = uv.shape
    D = D2 // 2
    G, C, _ = w_s.shape
    gd = D // G
    tt = _fit(T, ROW_TILE, C)

    def body(u_ref, v_ref, g_ref, b_ref, ws_ref, bias_ref, o_ref, vln_ref):
        vln_ref[...] = (_ln_stats(_gelu(v_ref[...].astype(F32))) * g_ref[...] + b_ref[...]).astype(BF16)
        mask = _tril(C)
        for g in range(G):
            wsm = jnp.where(mask, ws_ref[g], 0.0).astype(BF16)
            cs = slice(g * gd, (g + 1) * gd)
            for n in range(tt // C):
                rs = slice(n * C, (n + 1) * C)
                sv = jnp.dot(wsm, vln_ref[rs, cs], preferred_element_type=F32) + bias_ref[g]
                o_ref[rs, cs] = (_gelu(u_ref[rs, cs].astype(F32)) * sv).astype(BF16)

    return _pcall(body, name=name, out_shape=_sds((T, D), BF16), grid=(T // tt,),
                  in_specs=[_row(tt, D, 0), _row(tt, D, 1), _const((1, D)), _const((1, D)), _const((G, C, C)),
                            _const((G, C, gd))],
                  out_specs=_row(tt, D), scratch_shapes=[pltpu.VMEM((tt, D), BF16)],
                  compiler_params=_params("parallel"))(uv, uv, v_g, v_b, w_s, bias_b)


def _b_mid_bwd(uv, dgated, v_g, v_b, w_s, bias_b, name):
    T, D2 = uv.shape
    D = D2 // 2
    G, C, _ = w_s.shape
    gd = D // G
    tt = _fit(T, ROW_TILE, C)

    def body(u_ref, v_ref, d_ref, g_ref, b_ref, ws_ref, bias_ref, o_ref, dws_ref, dsb_ref, dvg_ref, dvb_ref,
             vln_ref, dvln_ref):
        @pl.when(pl.program_id(0) == 0)
        def _():
            dws_ref[...] = jnp.zeros_like(dws_ref)
            dsb_ref[...] = jnp.zeros_like(dsb_ref)
            dvg_ref[...] = jnp.zeros_like(dvg_ref)
            dvb_ref[...] = jnp.zeros_like(dvb_ref)

        vpre = v_ref[...].astype(F32)
        vhat = _ln_stats(_gelu(vpre))
        vln_ref[...] = (vhat * g_ref[...] + b_ref[...]).astype(BF16)
        mask = _tril(C)
        lane = lax.broadcasted_iota(jnp.int32, (C, 128), 1)
        for g in range(G):
            wsm = jnp.where(mask, ws_ref[g], 0.0).astype(BF16)
            cs = slice(g * gd, (g + 1) * gd)
            for n in range(tt // C):
                rs = slice(n * C, (n + 1) * C)
                vt = vln_ref[rs, cs]
                sv = jnp.dot(wsm, vt, preferred_element_type=F32) + bias_ref[g]
                dg = d_ref[rs, cs].astype(F32)
                upre = u_ref[rs, cs].astype(F32)
                o_ref[rs, cs] = (dg * sv * _gelu_grad(upre)).astype(BF16)
                dsv = dg * _gelu(upre)
                dsb_ref[...] += jnp.where(lane == g, jnp.sum(dsv, axis=-1, keepdims=True), 0.0)
                dsv16 = dsv.astype(BF16)
                dws_ref[g] += jnp.where(mask, lax.dot_general(dsv16, vt, _DIMS["nt"], preferred_element_type=F32), 0.0)
                dvln_ref[rs, cs] = lax.dot_general(wsm, dsv16, _DIMS["tn"], preferred_element_type=F32)
        dvln = dvln_ref[...]
        dvg_ref[...] += jnp.sum(dvln * vhat, axis=0, keepdims=True)
        dvb_ref[...] += jnp.sum(dvln, axis=0, keepdims=True)
        dvh = dvln * g_ref[...]
        vc = _gelu(vpre)
        vc = vc - jnp.mean(vc, axis=-1, keepdims=True)
        rstd = lax.rsqrt(jnp.mean(vc * vc, axis=-1, keepdims=True) + NORM_EPS)
        dv = rstd * (dvh - jnp.mean(dvh, axis=-1, keepdims=True) - vhat * jnp.mean(dvh * vhat, axis=-1, keepdims=True))
        o_ref[:, D:] = (dv * _gelu_grad(vpre)).astype(BF16)

    return _pcall(body, name=name,
                  out_shape=(_sds((T, 2 * D), BF16), _sds((G, C, C), F32), _sds((C, 128), F32), _sds((1, D), F32),
                             _sds((1, D), F32)),
                  grid=(T // tt,),
                  in_specs=[_row(tt, D, 0), _row(tt, D, 1), _row(tt, D), _const((1, D)), _const((1, D)),
                            _const((G, C, C)), _const((G, C, gd))],
                  out_specs=(_row(tt, 2 * D), _const((G, C, C)), _const((C, 128)), _const((1, D)), _const((1, D))),
                  scratch_shapes=[pltpu.VMEM((tt, D), BF16), pltpu.VMEM((tt, D), F32)],
                  compiler_params=_params("arbitrary"))(uv, uv, dgated, v_g, v_b, w_s, bias_b)


def _c_mid_fwd(ag, conv_w, conv_b, ln_g, ln_b, name):
    T, D2 = ag.shape
    D = D2 // 2
    width = conv_w.shape[0]
    tt = _fit(T, ROW_TILE, HALO)
    cw = _fit(D, WIDE_CONV_LANES)
    rb = _fit(tt, WIDE_CONV_ROWS, 8)
    taps = _tap_list(width)

    def body(a_ref, g_ref, ah_ref, gh_ref, w_ref, cb_ref, lg_ref, lb_ref, y2_ref, o_ref, win_ref, roll_ref):
        i = pl.program_id(0)
        yh = ah_ref[...].astype(F32) * jax.nn.sigmoid(gh_ref[...].astype(F32))
        win_ref[0:HALO, :] = jnp.where(i > 0, yh, 0.0)
        win_ref[HALO:, :] = a_ref[...].astype(F32) * jax.nn.sigmoid(g_ref[...].astype(F32))
        for c0 in range(0, D, cw):
            sl = slice(c0, c0 + cw)
            wv = win_ref[:, sl]
            for b in range(1, min(8, width)):
                roll_ref[b - 1] = pltpu.roll(wv, b, 0)
            for r0 in range(0, tt, rb):
                acc = jnp.zeros((rb, cw), F32)
                for k, b, a in taps:
                    lo = HALO - 8 * a + r0
                    xs = win_ref[lo:lo + rb, sl] if b == 0 else roll_ref[b - 1, lo:lo + rb, :]
                    acc = acc + w_ref[k:k + 1, sl] * xs
                y2_ref[r0:r0 + rb, sl] = acc + cb_ref[:, sl]
        y3 = _ln_stats(y2_ref[...]) * lg_ref[...] + lb_ref[...]
        o_ref[...] = (y3 * jax.nn.sigmoid(y3)).astype(BF16)

    return _pcall(body, name=name, out_shape=(_sds((T, D), F32), _sds((T, D), BF16)), grid=(T // tt,),
                  in_specs=[_row(tt, D, 0), _row(tt, D, 1), _prev_halo(tt, D, 0), _prev_halo(tt, D, 1),
                            _const((width, D)), _const((1, D)), _const((1, D)), _const((1, D))],
                  out_specs=(_row(tt, D), _row(tt, D)),
                  scratch_shapes=[pltpu.VMEM((HALO + tt, D), F32), pltpu.VMEM((7, HALO + tt, cw), F32)],
                  compiler_params=_params("parallel"))(ag, ag, ag, ag, conv_w, conv_b, ln_g, ln_b)


def _c_mid_bwd(ag, y2, dy4, conv_w, ln_g, ln_b, name):
    T, D2 = ag.shape
    D = D2 // 2
    width = conv_w.shape[0]
    tt = _fit(T, ROW_TILE, HALO)
    cw = _fit(D, WIDE_CONV_LANES)
    rb = _fit(tt, WIDE_CONV_ROWS, 8)
    taps = _tap_list(width)
    n_tiles = T // tt

    def ln_silu_bwd(y2v, dy4v, lg, lb):
        mu = jnp.mean(y2v, axis=-1, keepdims=True)
        yc = y2v - mu
        rstd = lax.rsqrt(jnp.mean(yc * yc, axis=-1, keepdims=True) + NORM_EPS)
        yh = yc * rstd
        y3 = yh * lg + lb
        sg = jax.nn.sigmoid(y3)
        dy3 = dy4v * (sg * (1.0 + y3 * (1.0 - sg)))
        dyh = dy3 * lg
        dy2 = rstd * (dyh - jnp.mean(dyh, axis=-1, keepdims=True) - yh * jnp.mean(dyh * yh, axis=-1, keepdims=True))
        return dy2, dy3, yh

    def body(a_ref, g_ref, ah_ref, gh_ref, y2_ref, y2n_ref, d_ref, dn_ref, w_ref, lg_ref, lb_ref,
             o_ref, dw_ref, dcb_ref, dlg_ref, dlb_ref, win_ref, dwin_ref, roll_ref, droll_ref):
        i = pl.program_id(0)

        @pl.when(i == 0)
        def _():
            dw_ref[...] = jnp.zeros_like(dw_ref)
            dcb_ref[...] = jnp.zeros_like(dcb_ref)
            dlg_ref[...] = jnp.zeros_like(dlg_ref)
            dlb_ref[...] = jnp.zeros_like(dlb_ref)

        lg = lg_ref[...]
        lb = lb_ref[...]
        dy2, dy3, yh = ln_silu_bwd(y2_ref[...], d_ref[...].astype(F32), lg, lb)
        dlg_ref[...] += jnp.sum(dy3 * yh, axis=0, keepdims=True)
        dlb_ref[...] += jnp.sum(dy3, axis=0, keepdims=True)
        dcb_ref[...] += jnp.sum(dy2, axis=0, keepdims=True)
        dwin_ref[0:tt, :] = dy2
        dy2n, _, _ = ln_silu_bwd(y2n_ref[...], dn_ref[...].astype(F32), lg, lb)
        dwin_ref[tt:, :] = jnp.where(i < n_tiles - 1, dy2n, 0.0)
        yh1 = ah_ref[...].astype(F32) * jax.nn.sigmoid(gh_ref[...].astype(F32))
        win_ref[0:HALO, :] = jnp.where(i > 0, yh1, 0.0)
        win_ref[HALO:, :] = a_ref[...].astype(F32) * jax.nn.sigmoid(g_ref[...].astype(F32))
        rows = tt + HALO
        for c0 in range(0, D, cw):
            sl = slice(c0, c0 + cw)
            wv = win_ref[:, sl]
            dv = dwin_ref[:, sl]
            for b in range(1, min(8, width)):
                roll_ref[b - 1] = pltpu.roll(wv, b, 0)
                droll_ref[b - 1] = pltpu.roll(dv, rows - b, 0)
            for k, b, a in taps:
                acc = jnp.zeros((8, cw), F32)
                for r0 in range(0, tt, rb):
                    lo = HALO - 8 * a + r0
                    xs = win_ref[lo:lo + rb, sl] if b == 0 else roll_ref[b - 1, lo:lo + rb, :]
                    prod = dwin_ref[r0:r0 + rb, sl] * xs
                    for q in range(0, rb, 8):
                        acc = acc + prod[q:q + 8]
                dw_ref[k:k + 1, sl] += jnp.sum(acc, axis=0, keepdims=True)
            for r0 in range(0, tt, rb):
                d1 = jnp.zeros((rb, cw), F32)
                for k, b, a in taps:
                    lo = 8 * a + r0
                    xs = dwin_ref[lo:lo + rb, sl] if b == 0 else droll_ref[b - 1, lo:lo + rb, :]
                    d1 = d1 + w_ref[k:k + 1, sl] * xs
                av = a_ref[r0:r0 + rb, sl].astype(F32)
                sg = jax.nn.sigmoid(g_ref[r0:r0 + rb, sl].astype(F32))
                o_ref[r0:r0 + rb, c0:c0 + cw] = (d1 * sg).astype(BF16)
                o_ref[r0:r0 + rb, D + c0:D + c0 + cw] = (d1 * av * sg * (1.0 - sg)).astype(BF16)

    return _pcall(body, name=name,
                  out_shape=(_sds((T, 2 * D), BF16), _sds((width, D), F32), _sds((1, D), F32), _sds((1, D), F32),
                             _sds((1, D), F32)),
                  grid=(n_tiles,),
                  in_specs=[_row(tt, D, 0), _row(tt, D, 1), _prev_halo(tt, D, 0), _prev_halo(tt, D, 1),
                            _row(tt, D), _next_halo(tt, D, T), _row(tt, D), _next_halo(tt, D, T),
                            _const((width, D)), _const((1, D)), _const((1, D))],
                  out_specs=(_row(tt, 2 * D), _const((width, D)), _const((1, D)), _const((1, D)), _const((1, D))),
                  scratch_shapes=[pltpu.VMEM((HALO + tt, D), F32), pltpu.VMEM((tt + HALO, D), F32),
                                  pltpu.VMEM((7, HALO + tt, cw), F32), pltpu.VMEM((7, tt + HALO, cw), F32)],
                  compiler_params=_params("arbitrary"))(ag, ag, ag, ag, y2, y2, dy4, dy4, conv_w, ln_g, ln_b)


def _place():
    x, y, c = lax.axis_index("x"), lax.axis_index("y"), lax.axis_index("c")
    return x, y, c


def _slot(px, py, pc):
    return 4 * px + 2 * py + pc


def _all_gather(shards, name):
    n = len(shards)

    def body(*refs):
        ins, outs = refs[:n], refs[n:2 * n]
        send_sems, recv_sems, local_sems = refs[2 * n:]
        x, y, c = _place()
        me, sibling = (x, y, c), (x, y, 1 - c)
        chips = [(1 - x, y), (x, 1 - y), (1 - x, 1 - y)]

        def copy(t, k, block, to, src=None):
            dst = outs[t].at[_slot(*block)]
            return pltpu.make_async_remote_copy(
                src_ref=dst if src is None else src, dst_ref=dst, send_sem=send_sems.at[t, k],
                recv_sem=recv_sems.at[t, k], device_id=to, device_id_type=MESH)

        mine = [pltpu.make_async_copy(ins[t], outs[t].at[_slot(*me)], local_sems.at[t]) for t in range(n)]
        for cp in mine:
            cp.start()
        first = []
        for j, chip in enumerate(chips):
            first += [copy(t, 1 + j, me, (*chip, c), src=ins[t]) for t in range(n)]
        first += [copy(t, 0, me, sibling, src=ins[t]) for t in range(n)]
        for cp in first:
            cp.start()
        passed = []
        for j, chip in enumerate(chips):
            for t in range(n):
                copy(t, 1 + j, (*chip, c), me).wait_recv()
                cp = copy(t, 4 + j, (*chip, c), sibling)
                cp.start()
                passed.append(cp)
        for t in range(n):
            copy(t, 0, sibling, me).wait_recv()
            for j, chip in enumerate(chips):
                copy(t, 4 + j, (*chip, 1 - c), me).wait_recv()
        for cp in first + passed:
            cp.wait_send()
        for cp in mine:
            cp.wait()

    outs = _pcall(
        body, name=name, out_shape=tuple(_sds((N_DEV,) + s.shape, s.dtype) for s in shards),
        in_specs=[ANY] * n, out_specs=(ANY,) * n,
        scratch_shapes=[pltpu.SemaphoreType.DMA((n, 7)), pltpu.SemaphoreType.DMA((n, 7)), pltpu.SemaphoreType.DMA((n,))],
    )(*shards)
    return list(outs)


_HBM = pl.BlockSpec(memory_space=pltpu.HBM)
_SEM = pl.BlockSpec(memory_space=pltpu.SEMAPHORE)
_DATAFLOW = pltpu.SideEffectType.DATAFLOW_SIDE_EFFECTING


def _peers(x, y, c):
    out = []
    for j in range(1, N_DEV):
        fx, fy, fc = (j >> 2) & 1, (j >> 1) & 1, j & 1
        out.append((1 - x if fx else x, 1 - y if fy else y, 1 - c if fc else c))
    return out


def _exchange_copies(ins, zones, mode, sems):
    send_sem, recv_sem, local_sem = sems
    x, y, c = _place()
    me = _slot(x, y, c)
    sibling = (x, y, 1 - c)
    chips = [(1 - x, y), (x, 1 - y), (1 - x, 1 - y)]
    local, remote = [], []

    def add(src, dst, to, landed):
        remote.append((pltpu.make_async_remote_copy(src_ref=src, dst_ref=dst, send_sem=send_sem, recv_sem=recv_sem,
                                                    device_id=to, device_id_type=MESH), landed))

    for t, zone in enumerate(zones):
        if mode == "scatter":
            local.append(pltpu.make_async_copy(ins[t].at[me], zone.at[me], local_sem))
            for peer in _peers(x, y, c):
                add(ins[t].at[_slot(*peer)], zone.at[me], peer, zone.at[_slot(*peer)])
        elif mode in ("gather_chips", "gather_all"):
            local.append(pltpu.make_async_copy(ins[t], zone.at[me], local_sem))
            for peer in ([(*chip, c) for chip in chips] + [sibling] if mode == "gather_chips" else _peers(x, y, c)):
                add(ins[t], zone.at[me], peer, zone.at[_slot(*peer)])
        else:
            for chip in chips:
                block = zone.at[_slot(*chip, c)]
                add(block, block, sibling, zone.at[_slot(*chip, 1 - c)])
    return local, remote


def _exchange_start(srcs, lands, mode, after, name):
    if lands is None:
        lands = [lax.empty(s.shape if mode == "scatter" else (N_DEV,) + s.shape, s.dtype) for s in srcs]
    ns, na = len(srcs), len(srcs) + len(lands)

    def body(*refs):
        local, remote = _exchange_copies(refs[:ns], refs[ns:na], mode, refs[na + 1:na + 4])
        for cp in local:
            cp.start()
        for cp, _ in remote:
            cp.start()
        refs[-1][...] = jnp.zeros_like(refs[-1])

    hbm = lambda a: pltpu.with_memory_space_constraint(a, pltpu.HBM)
    arrays = list(srcs) + list(lands)
    outs = _pcall(
        body, name=name,
        out_shape=(pltpu.SemaphoreType.DMA(()),) * 3
        + tuple(pltpu.HBM(a.shape, a.dtype) for a in arrays) + (_sds((8, 128), F32),),
        in_specs=[_HBM] * na + [ANY],
        out_specs=(_SEM,) * 3 + (_HBM,) * na + (pl.BlockSpec(memory_space=pltpu.VMEM),),
        input_output_aliases={t: 3 + t for t in range(na)},
        compiler_params=pltpu.CompilerParams(has_side_effects=_DATAFLOW),
    )(*[hbm(a) for a in arrays], after)
    return outs[:3], list(outs[3:3 + ns]), list(outs[3 + ns:3 + na]), outs[-1]


def _exchange_wait(sems, srcs, lands, mode, after, name):
    ns, na = len(srcs), len(srcs) + len(lands)
    afters = list(after) if isinstance(after, (list, tuple)) else [after]

    def body(*refs):
        local, remote = _exchange_copies(refs[:ns], refs[ns:na], mode, refs[na:na + 3])
        for cp in local:
            cp.wait()
        for cp, landed in remote:
            cp.wait_send()
            pltpu.make_async_remote_copy(
                src_ref=landed, dst_ref=landed, send_sem=refs[na], recv_sem=refs[na + 1],
                device_id=_place(), device_id_type=MESH).wait_recv()

    outs = _pcall(
        body, name=name, out_shape=tuple(pltpu.HBM(a.shape, a.dtype) for a in list(srcs) + list(lands)),
        in_specs=[_HBM] * na + [_SEM] * 3 + [ANY] * len(afters), out_specs=(_HBM,) * na,
        input_output_aliases={t: t for t in range(na)},
        compiler_params=pltpu.CompilerParams(has_side_effects=_DATAFLOW),
    )(*srcs, *lands, *sems, *afters)
    return list(outs[ns:])


def _reduce_adam(recvs, w, m, v, name, dep=None):
    L, r, c = w.shape
    tr = _fit(r, max(16, (256 * 1024) // c), 16)
    ni = r // tr

    def recv_spec(l0):
        def index(l, i):
            return 0, jnp.where(l == l0, i, jnp.where(l < l0, 0, ni - 1)), 0
        return pl.BlockSpec((N_DEV, tr, c), index)

    lay = pl.BlockSpec((None, tr, c), lambda l, i: (l, i, 0))

    n_dep = 0 if dep is None else 1

    def body(*refs):
        recv_refs = refs[:L]
        w_ref, m_ref, v_ref = refs[L:L + 3]
        g_out, d_out, m_out, v_out = refs[L + 3 + n_dep:]
        l = pl.program_id(0)
        for l0 in range(L):
            @pl.when(l == l0)
            def _(l0=l0):
                g = recv_refs[l0][0].astype(F32)
                for s in range(1, N_DEV):
                    g = g + recv_refs[l0][s].astype(F32)
                mn = ADAM_B1 * m_ref[...] + (1.0 - ADAM_B1) * g
                vn = ADAM_B2 * v_ref[...] + (1.0 - ADAM_B2) * (g * g)
                m_hat = mn / (1.0 - ADAM_B1 ** ADAM_STEP)
                v_hat = vn / (1.0 - ADAM_B2 ** ADAM_STEP)
                g_out[...] = g
                d_out[...] = -ADAM_LR * (m_hat / (jnp.sqrt(v_hat) + ADAM_EPS) + ADAM_WD * w_ref[...])
                m_out[...] = mn
                v_out[...] = vn

    return _pcall(body, name=name, out_shape=(_sds((L, r, c), F32),) * 4, grid=(L, ni),
                  in_specs=[recv_spec(l0) for l0 in range(L)] + [lay, lay, lay] + [ANY] * n_dep, out_specs=(lay,) * 4,
                  compiler_params=_params("arbitrary", "arbitrary"))(*recvs, w, m, v, *([dep] if n_dep else []))


_SMALL_SHARDED = ["mix_norm", "xa_norm", "ffn_norm", "a_conv_w", "c_conv_w", "c_conv_b", "c_ln_g", "c_ln_b"]
_SMALL_REPLICATED = ["b_v_g", "b_v_b", "b_w_s", "b_s_bias"]
_BIG = ["xa_wq", "xa_wkv", "xa_wo", "ffn_w_gu", "ffn_w_down", "a_w_in", "a_w_out", "b_w_in", "b_w_out", "c_w_in",
        "c_w_out"]
_COL_SHARDED = {"xa_wkv", "ffn_w_gu", "a_w_in", "b_w_in", "c_w_in"}
_WEIGHTS = ["mix_norm", "xa_norm", "xa_wq", "xa_wkv", "xa_wo", "ffn_norm", "ffn_w_gu", "ffn_w_down", "a_w_in",
            "a_conv_w", "a_w_out", "b_w_in", "b_v_g", "b_v_b", "b_w_s", "b_s_bias", "b_w_out", "c_w_in", "c_conv_w",
            "c_conv_b", "c_ln_g", "c_ln_b", "c_w_out"]
_MIXER = "abc"


def _size(shape):
    size = 1
    for s in shape:
        size *= s
    return size


def _row_layout(shapes, width):
    offs, r = [], 0
    for shape in shapes:
        offs.append(r)
        r += -(-(-(-_size(shape) // width)) // 8) * 8
    return offs, r


def _pack_rows(arrays, width, fill):
    offs, total = _row_layout([a.shape for a in arrays], width)
    ends = offs[1:] + [total]
    rows = [jnp.pad(a.reshape(-1), (0, (e - o) * width - a.size), constant_values=fill).reshape(e - o, width)
            for a, o, e in zip(arrays, offs, ends)]
    return jnp.concatenate(rows, axis=0)


def _unpack_rows(packed, like):
    width = packed.shape[-1]
    offs, _ = _row_layout(like, width)
    return [packed[o:o + -(-_size(s) // width)].reshape(-1)[:_size(s)].reshape(s) for o, s in zip(offs, like)]


def _assemble_rows(pieces, rows, width, name):
    n = len(pieces)

    def body(*refs):
        o_ref = refs[n]
        o_ref[...] = jnp.zeros_like(o_ref)
        for r, (a, off) in zip(refs[:n], pieces):
            o_ref[off:off + a.shape[0], :] = r[...]

    return _pcall(body, name=name, out_shape=_sds((rows, width), F32),
                  compiler_params=pltpu.CompilerParams(vmem_limit_bytes=V7X_VMEM_LIMIT))(*[a for a, _ in pieces])


def kernel(x, mem, mix_norm, xa_norm, xa_wq, xa_wkv, xa_wo, ffn_norm, ffn_w_gu, ffn_w_down, a_w_in, a_conv_w, a_w_out, b_w_in, b_v_g, b_v_b, b_w_s, b_s_bias, b_w_out, c_w_in, c_conv_w, c_conv_b, c_ln_g, c_ln_b, c_w_out, loss_target, m_mix_norm, m_xa_norm, m_xa_wq, m_xa_wkv, m_xa_wo, m_ffn_norm, m_ffn_w_gu, m_ffn_w_down, m_a_w_in, m_a_conv_w, m_a_w_out, m_b_w_in, m_b_v_g, m_b_v_b, m_b_w_s, m_b_s_bias, m_b_w_out, m_c_w_in, m_c_conv_w, m_c_conv_b, m_c_ln_g, m_c_ln_b, m_c_w_out, v_mix_norm, v_xa_norm, v_xa_wq, v_xa_wkv, v_xa_wo, v_ffn_norm, v_ffn_w_gu, v_ffn_w_down, v_a_w_in, v_a_conv_w, v_a_w_out, v_b_w_in, v_b_v_g, v_b_v_b, v_b_w_s, v_b_s_bias, v_b_w_out, v_c_w_in, v_c_conv_w, v_c_conv_b, v_c_ln_g, v_c_ln_b, v_c_w_out):
    P = dict(locals())
    T, D = x.shape[1], x.shape[2]
    dl = D // N_DEV
    depth = mix_norm.shape[0]
    x0, mem0, target = x[0], mem[0], loss_target[0]
    my_slot = _slot(*_place())

    sh_shapes = [P[n].shape for n in _SMALL_SHARDED]
    packed = _pack_rows([P[n] for n in _SMALL_SHARDED], dl, 0.0)
    n_sh = packed.shape[0]
    gathered = _all_gather([packed], "ag_small")[0]
    full_rows = jnp.transpose(gathered, (1, 0, 2)).reshape(n_sh, D)
    small = dict(zip(_SMALL_SHARDED, _unpack_rows(full_rows, [s[:-1] + (D,) for s in sh_shapes])))
    G, C = b_w_s.shape[1], b_w_s.shape[2]
    gd = D // G
    bias_b = jnp.broadcast_to(b_s_bias[0][:, :, None], (G, C, gd))
    zero_row = jnp.zeros((1, D), F32)

    groups = [(i, part) for i in range(depth) for part in (("in", "out", "xa", "gu", "down") if i == 0 else
                                                           ("mix", "xa", "ffn"))]

    def group_names(i, part):
        mx, slot = _MIXER[i % N_MIXERS], i // N_MIXERS
        names = {"in": [(mx + "_w_in", slot)], "out": [(mx + "_w_out", slot)],
                 "xa": [("xa_wq", i), ("xa_wkv", i), ("xa_wo", i)],
                 "gu": [("ffn_w_gu", i)], "down": [("ffn_w_down", i)]}
        names["mix"] = names["in"] + names["out"]
        names["ffn"] = names["gu"] + names["down"]
        return names[part]

    no_token = jnp.zeros((8, 128), F32)
    fwd = {"g": 0, "last": full_rows, "token": no_token, "stage1": {}, "stage2": {}}

    def tag(g):
        return "%d_%s" % groups[g]

    def start_stage1(g):
        if g < len(groups):
            names = group_names(*groups[g])
            sems, srcs, lands, token = _exchange_start([P[n][j].astype(BF16) for n, j in names], None, "gather_chips",
                                                       fwd["last"], "ag_start_" + tag(g))
            fwd["stage1"][g] = (sems, srcs, lands)
            fwd["last"] = fwd["token"] = token

    def start_stage2(g, after):
        if g in fwd["stage1"]:
            sems, srcs, lands = fwd["stage1"].pop(g)
            lands = _exchange_wait(sems, srcs, lands, "gather_chips", after, "ag_wait_" + tag(g))
            sems, _, lands, token = _exchange_start([], lands, "gather_sibling", after, "ag_pass_" + tag(g))
            fwd["stage2"][g] = (sems, lands)
            fwd["last"] = fwd["token"] = token

    def begin_group():
        g, y = fwd["g"], fwd["last"]
        start_stage2(g, y)
        sems, lands = fwd["stage2"].pop(g)
        fulls = _exchange_wait(sems, [], lands, "gather_sibling", y, "ag_done_" + tag(g))
        fwd["last"] = fulls[0]
        start_stage1(g + 3)
        out = {}
        for (n, _), f in zip(group_names(*groups[g]), fulls):
            key = n[2:] if n[1] == "_" and n[0] in _MIXER else n
            out[key] = f if n in _COL_SHARDED else f.reshape(-1, f.shape[-1])
        fwd["g"] += 1
        return out, fwd["token"]

    def mid_group(y):
        start_stage2(fwd["g"], y)
        return fwd["token"]

    def end_group(y):
        fwd["last"] = y

    for g0 in range(3):
        start_stage1(g0)
    start_stage2(0, fwd["last"])

    saved = []
    xin = x0
    h = _rms_fwd(x0, small["mix_norm"][0, 0][None], "rms_first")
    for i in range(depth):
        kind, slot = i % N_MIXERS, i // N_MIXERS
        W, dep = begin_group()
        S = {"W": W, "x0": xin, "h0": h}
        pre = _mm(h, W["w_in"], "nn", BF16, "mm_in_%s" % _MIXER[kind], b_blocked=True, dep=dep)
        S["pre"] = pre
        dep = mid_group(pre)
        if kind == 0:
            mid = _a_mid_fwd(pre, small["a_conv_w"][slot], "a_mid_fwd")
        elif kind == 1:
            mid = _b_mid_fwd(pre, b_v_g, b_v_b, b_w_s[0], bias_b, "b_mid_fwd")
        else:
            y2c, mid = _c_mid_fwd(pre, small["c_conv_w"][slot], small["c_conv_b"], small["c_ln_g"], small["c_ln_b"],
                                  "c_mid_fwd")
            S["y2c"] = y2c
        S["mid"] = mid
        if "w_out" not in W:
            end_group(mid)
            Wo, dep = begin_group()
            W.update(Wo)
        S["y0"] = _mm(mid, W["w_out"], "nn", BF16, "mm_out", dep=dep)
        end_group(S["y0"])
        Wx, dep = begin_group()
        W.update(Wx)
        xin, h = _post_pre_fwd(xin, S["y0"], small["mix_norm"][i, 1][None], small["xa_norm"][i, 0][None], "post_pre")
        S["x1"], S["h1"] = xin, h
        S["q"] = _mm(h, W["xa_wq"], "nn", BF16, "mm_q", dep=dep)
        dep = mid_group(S["q"])
        S["memn"] = _rms_fwd(mem0, small["xa_norm"][i, 2][None], "rms_mem")
        S["kv"] = _mm(S["memn"], W["xa_wkv"], "nn", BF16, "mm_kv", b_blocked=True)
        S["o"] = _attn_fwd(S["q"], S["kv"], "attn_fwd")
        S["y1"] = _mm(S["o"], W["xa_wo"], "nn", BF16, "mm_out", dep=dep)
        end_group(S["y1"])
        Wf, dep = begin_group()
        W.update(Wf)
        xin, h = _post_pre_fwd(xin, S["y1"], small["xa_norm"][i, 1][None], small["ffn_norm"][i, 0][None], "post_pre")
        S["x2"], S["h2"] = xin, h
        S["dact_dgate"], S["dact_dup"], S["act"] = _ffn_gu_fwd(h, W["ffn_w_gu"], "ffn_gu_fwd", dep=dep)
        dep = mid_group(S["act"])
        if "ffn_w_down" not in W:
            end_group(S["act"])
            Wd, dep = begin_group()
            W.update(Wd)
        S["y2"] = _mm(S["act"], W["ffn_w_down"], "nn", BF16, "mm_down", dep=dep)
        end_group(S["y2"])
        if i + 1 < depth:
            xin, h = _post_pre_fwd(xin, S["y2"], small["ffn_norm"][i, 1][None], small["mix_norm"][i + 1, 0][None],
                                   "post_pre")
        saved.append(S)

    last = saved[-1]
    loss_part, dx, dy, dg = _final_fwd_loss(xin, last["y2"], small["ffn_norm"][depth - 1, 1][None], target, "final_loss")
    loss = lax.psum(loss_part[0, 0], ("x", "y", "c"))

    g_mix = [[zero_row, zero_row] for _ in range(depth)]
    g_xa = [[zero_row, zero_row, zero_row] for _ in range(depth)]
    g_ffn = [[zero_row, zero_row] for _ in range(depth)]
    g_small = {}
    recv = {n: [None] * P[n].shape[0] for n in _BIG}
    g_ffn[depth - 1][1] = dg

    bwd = {"queue": [], "token": no_token}
    scatters_in_flight = 3

    def finish_scatter(after, keep):
        while len(bwd["queue"]) > keep:
            names, tag, sems, srcs, lands = bwd["queue"].pop(0)
            for (n, j, _), r in zip(names, _exchange_wait(sems, srcs, lands, "scatter", after, "rs_wait_" + tag)):
                recv[n][j] = r

    def scatter_group(names, tag, after):
        finish_scatter(after, scatters_in_flight - 1)
        parts = [g if n in _COL_SHARDED else g.reshape(N_DEV, -1, g.shape[-1]) for n, _, g in names]
        sems, srcs, lands, bwd["token"] = _exchange_start(parts, None, "scatter", after, "rs_start_" + tag)
        bwd["queue"].append((names, tag, sems, srcs, lands))

    def scatter_token():
        return bwd["token"]

    for i in reversed(range(depth)):
        kind, slot = i % N_MIXERS, i // N_MIXERS
        mx = _MIXER[kind]
        S = saved[i]
        W = S["W"]
        dgu = _ffn_dgu_bwd(dy, W["ffn_w_down"], S["dact_dgate"], S["dact_dup"], "ffn_dgu_bwd", dep=scatter_token())
        dw_down = _mm(S["act"], dy, "tn", BF16, "mm_dw_down")
        dh = _mm(dgu, W["ffn_w_gu"], "nt", BF16, "mm_dh_gu", a_blocked=True, b_blocked=True)
        dw_gu = _mm(S["h2"], dgu, "tn", BF16, "mm_dw_gu", b_blocked=True, out_blocks=N_DEV)
        dx, g_ffn[i][0], dy, g_xa[i][1] = _pre_post_bwd(dx, dh, S["x2"], small["ffn_norm"][i, 0][None], S["y1"],
                                                         small["xa_norm"][i, 1][None], "pre_post_bwd")
        scatter_group([("ffn_w_gu", i, dw_gu), ("ffn_w_down", i, dw_down)], "%d_2" % i, dx)
        do = _mm(dy, W["xa_wo"], "nt", BF16, "mm_nt_dd16", dep=scatter_token())
        dw_o = _mm(S["o"], dy, "tn", BF16, "mm_dw_dd")
        dq, dkv = _attn_bwd(S["q"], S["kv"], do, "attn_bwd")
        dkv16 = dkv.astype(BF16)
        dh = _mm(dq, W["xa_wq"], "nt", BF16, "mm_nt_dd16")
        dw_q = _mm(S["h1"], dq, "tn", BF16, "mm_dw_dd")
        dw_kv = _mm(S["memn"], dkv16, "tn", BF16, "mm_dw_kv", out_blocks=N_DEV)
        dmemn = _mm(dkv16, W["xa_wkv"], "nt", F32, "mm_dmem", b_blocked=True)
        g_xa[i][2] = _rms_gain_grad(dmemn, mem0, "rms_gain_grad")
        dx, g_xa[i][0], dy, g_mix[i][1] = _pre_post_bwd(dx, dh, S["x1"], small["xa_norm"][i, 0][None], S["y0"],
                                                         small["mix_norm"][i, 1][None], "pre_post_bwd")
        scatter_group([("xa_wq", i, dw_q), ("xa_wkv", i, dw_kv), ("xa_wo", i, dw_o)], "%d_1" % i, dx)
        dmid = _mm(dy, W["w_out"], "nt", BF16, "mm_nt_dd16", dep=scatter_token())
        dw_out = _mm(S["mid"], dy, "tn", BF16, "mm_dw_dd")
        if i == 0:
            scatter_group([(mx + "_w_out", slot, dw_out)], "0_0_out", dy)
        if kind == 0:
            dpre, dcw = _a_mid_bwd(S["pre"], dmid, small["a_conv_w"][slot], "a_mid_bwd")
            g_small.setdefault("a_conv_w", {})[slot] = dcw
        elif kind == 1:
            dpre, dws, dsb, dvg, dvb = _b_mid_bwd(S["pre"], dmid, b_v_g, b_v_b, b_w_s[0], bias_b, "b_mid_bwd")
            dsb_row = jnp.pad(jnp.transpose(dsb[:, :G]).reshape(1, G * C), ((0, 0), (0, (-G * C) % D)))
            g_small.update(b_w_s=dws.reshape(-1, D), b_s_bias=dsb_row.reshape(-1, D), b_v_g=dvg, b_v_b=dvb)
        else:
            dpre, dcw, dcb, dlg, dlb = _c_mid_bwd(S["pre"], S["y2c"], dmid, small["c_conv_w"][slot], small["c_ln_g"],
                                                  small["c_ln_b"], "c_mid_bwd")
            g_small.update(c_conv_w=dcw, c_conv_b=dcb, c_ln_g=dlg, c_ln_b=dlb)
        dh = _mm(dpre, W["w_in"], "nt", BF16, "mm_dh_in_%s" % mx, b_blocked=True, dep=scatter_token())
        dw_in = _mm(S["h0"], dpre, "tn", BF16, "mm_dw_in_%s" % mx, out_blocks=N_DEV)
        if i > 0:
            dx, g_mix[i][0], dy, g_ffn[i - 1][1] = _pre_post_bwd(
                dx, dh, S["x0"], small["mix_norm"][i, 0][None], saved[i - 1]["y2"],
                small["ffn_norm"][i - 1, 1][None], "pre_post_bwd")
        else:
            dx, g_mix[i][0] = _pre_post_bwd(dx, dh, S["x0"], small["mix_norm"][i, 0][None], None, None, "pre_bwd")
        scatter_group([(mx + "_w_in", slot, dw_in)] + ([(mx + "_w_out", slot, dw_out)] if i > 0 else []),
                      "%d_0" % i, dx)
        S.clear()
    grad_x = dx[None]

    sh_off = dict(zip(_SMALL_SHARDED, _row_layout(sh_shapes, dl)[0]))
    rep_offs, n_rep = _row_layout([P[n].shape for n in _SMALL_REPLICATED], D)
    rep_off = {n: n_sh + o for n, o in zip(_SMALL_REPLICATED, rep_offs)}
    pieces = []
    for i in range(depth):
        pieces += [(g, sh_off["mix_norm"] + 2 * i + j) for j, g in enumerate(g_mix[i])]
        pieces += [(g, sh_off["xa_norm"] + 3 * i + j) for j, g in enumerate(g_xa[i])]
        pieces += [(g, sh_off["ffn_norm"] + 2 * i + j) for j, g in enumerate(g_ffn[i])]
    pieces += [(g, sh_off["a_conv_w"] + a_conv_w.shape[1] * s) for s, g in g_small["a_conv_w"].items()]
    pieces += [(g_small[n], sh_off[n]) for n in ("c_conv_w", "c_conv_b", "c_ln_g", "c_ln_b")]
    pieces += [(g_small[n], rep_off[n]) for n in _SMALL_REPLICATED]
    part_all = _assemble_rows(pieces, n_sh + n_rep, D, "pack_small_grads")
    small_sems, small_srcs, small_lands, started = _exchange_start([part_all], None, "gather_all", scatter_token(),
                                                                   "ag_small_grads_start")

    out = {}

    def adam_small(names, recv_s, width, name):
        shapes = [P[n].shape for n in names]
        pw = _pack_rows([P[n] for n in names], width, 0.0)
        pm = _pack_rows([P["m_" + n] for n in names], width, 0.0)
        pv = _pack_rows([P["v_" + n] for n in names], width, 1.0)
        res = _reduce_adam([recv_s], pw[None], pm[None], pv[None], name)
        for kind, r in zip(("grad", "delta", "new_m", "new_v"), res):
            for n, a in zip(names, _unpack_rows(r[0], shapes)):
                out[kind + "_" + n] = a

    def lands_after(n):
        return max([0] + [k + 1 for k, entry in enumerate(bwd["queue"]) if any(m == n for m, _, _ in entry[0])])

    early_done = [g_xa[i][2] for i in range(depth)]
    for n in sorted(_BIG, key=lands_after):
        while any(r is None for r in recv[n]):
            finish_scatter(early_done, len(bwd["queue"]) - 1)
        res = _reduce_adam(recv[n], P[n], P["m_" + n], P["v_" + n], "adam_" + n, dep=started)
        early_done.append(res[0])
        for kind, r in zip(("grad", "delta", "new_m", "new_v"), res):
            out[kind + "_" + n] = r

    parts_all = _exchange_wait(small_sems, small_srcs, small_lands, "gather_all", early_done,
                               "ag_small_grads_wait")[0]
    recv_sh = lax.dynamic_slice_in_dim(parts_all[:, :n_sh], my_slot * dl, dl, axis=2)
    adam_small(_SMALL_SHARDED, recv_sh, dl, "adam_small_sharded")
    adam_small(_SMALL_REPLICATED, parts_all[:, n_sh:], D, "adam_small_replicated")

    return (loss, grad_x, *[out[k + "_" + n] for k in ("grad", "delta", "new_m", "new_v") for n in _WEIGHTS])
```

```python
import functools

import jax
import jax.numpy as jnp
from jax import lax
from jax.experimental import pallas as pl
from jax.experimental.pallas import tpu as pltpu

F32 = jnp.float32
BF16 = jnp.bfloat16
MESH = pl.DeviceIdType.MESH
ANY = pl.BlockSpec(memory_space=pl.ANY)

N_DEV = 8
N_MIXERS = 3
XA_HEADS = 4
GMLP_GROUPS = 8
CHUNK = 128
NORM_EPS = 1e-6
HALO = 32
ROW_TILE = 256
NORM_ROW_TILE = 512
CONV_LANES = 512
V7X_VMEM_LIMIT = 56 * 1024 * 1024

ADAM_LR = 0.001
ADAM_B1 = 0.9
ADAM_B2 = 0.999
ADAM_EPS = 1e-08
ADAM_WD = 0.01
ADAM_STEP = 10


def _pcall(body, **kw):
    return pl.pallas_call(body, **kw)


def _params(*sem):
    return pltpu.CompilerParams(dimension_semantics=sem, vmem_limit_bytes=V7X_VMEM_LIMIT)


def _fit(n, pref, mult=128):
    if n <= pref:
        return n
    t = (pref // mult) * mult
    while t >= mult:
        if n % t == 0:
            return t
        t -= mult
    return n


def _sds(shape, dtype):
    return jax.ShapeDtypeStruct(shape, dtype)


_DIMS = {"nn": (((1,), (0,)), ((), ())), "nt": (((1,), (1,)), ((), ())), "tn": (((0,), (0,)), ((), ()))}
MM_VMEM_BUDGET = 44 * 1024 * 1024


def _gcd(a, b):
    while b:
        a, b = b, a % b
    return a


def _mm(a, b, mode, out_dtype, name, *, a_blocked=False, b_blocked=False, out_blocks=None, dep=None):
    if mode == "tn":
        K, M = a.shape
    elif a_blocked:
        sa, M, ca = a.shape
        K = sa * ca
    else:
        M, K = a.shape
    n_unit = k_unit = None
    if b_blocked:
        _, d1, cb = b.shape
        if mode == "nt":
            N, k_unit = d1, cb
        else:
            N, n_unit = b.shape[0] * cb, cb
    else:
        N = b.shape[0] if mode == "nt" else b.shape[1]
    n_unit = n_unit or N
    k_unit = k_unit or K
    if a_blocked:
        k_unit = _gcd(k_unit, ca)
    if out_blocks:
        n_unit = _gcd(n_unit, N // out_blocks)
    tn = _fit(n_unit, 1536)
    tk = k_unit
    out_bytes = jnp.dtype(out_dtype).itemsize

    def need(tm_, gk_=1):
        nk_ = K // (tk * gk_)
        return (4 * gk_ * (tm_ * tk + tk * tn) + 2 * tm_ * tn * out_bytes
                + 4 * tm_ * tn * ((2 if nk_ > 1 else 1) + (1 if gk_ > 1 else 0)))

    tm = _fit(M, 1024)
    gk = 1
    if b_blocked and mode == "nt" and tk == cb:
        for cand in (8, 4, 2):
            fits = [t for t in (tm, tm // 2) if t % 256 == 0 and need(t, cand) <= MM_VMEM_BUDGET]
            if b.shape[0] % cand == 0 and (not a_blocked or (ca // tk) % cand == 0) and fits:
                gk, tm = cand, fits[0]
                break
    while need(tm, gk) > MM_VMEM_BUDGET and tm % 256 == 0:
        tm //= 2
    nk = K // (tk * gk)

    if mode == "tn":
        a_spec = pl.BlockSpec((tk, tm), lambda i, j, k: (k, i))
    elif a_blocked:
        ka = ca // (tk * gk)
        a_spec = pl.BlockSpec((None, tm, tk * gk), lambda i, j, k: (k // ka, i, k % ka))
    else:
        a_spec = pl.BlockSpec((tm, tk * gk), lambda i, j, k: (i, k))
    if b_blocked and mode == "nt" and gk > 1:
        b_spec = pl.BlockSpec((gk, tn, tk), lambda i, j, k: (k, j, 0))
    elif b_blocked and mode == "nt":
        kb = cb // tk
        b_spec = pl.BlockSpec((None, tn, tk), lambda i, j, k: (k // kb, j, k % kb))
    elif b_blocked:
        nb = cb // tn
        b_spec = pl.BlockSpec((None, tk, tn), lambda i, j, k: (j // nb, k, j % nb))
    elif mode == "nt":
        b_spec = pl.BlockSpec((tn, tk), lambda i, j, k: (j, k))
    else:
        b_spec = pl.BlockSpec((tk, tn), lambda i, j, k: (k, j))
    if out_blocks:
        ob = (N // out_blocks) // tn
        out_shape = _sds((out_blocks, M, N // out_blocks), out_dtype)
        o_spec = pl.BlockSpec((None, tm, tn), lambda i, j, k: (j // ob, i, j % ob))
    else:
        out_shape = _sds((M, N), out_dtype)
        o_spec = pl.BlockSpec((tm, tn), lambda i, j, k: (i, j))
    dims = _DIMS[mode]
    n_in = 2 if dep is None else 3

    def body(*refs):
        a_ref, b_ref = refs[0], refs[1]
        o_ref = refs[n_in]
        if gk == 1:
            p = lax.dot_general(a_ref[...], b_ref[...], dims, preferred_element_type=F32)
        else:
            p = lax.dot_general(a_ref[:, 0:tk], b_ref[0], dims, preferred_element_type=F32)
            for s in range(1, gk):
                p = p + lax.dot_general(a_ref[:, s * tk:(s + 1) * tk], b_ref[s], dims, preferred_element_type=F32)
        if nk == 1:
            o_ref[...] = p.astype(o_ref.dtype)
            return
        acc_ref = refs[n_in + 1]
        k = pl.program_id(2)

        @pl.when(k == 0)
        def _():
            acc_ref[...] = p

        @pl.when(k > 0)
        def _():
            acc_ref[...] += p

        @pl.when(k == nk - 1)
        def _():
            o_ref[...] = acc_ref[...].astype(o_ref.dtype)

    ins, in_specs = [a, b], [a_spec, b_spec]
    if dep is not None:
        ins.append(dep)
        in_specs.append(ANY)
    return _pcall(
        body, name=name, out_shape=out_shape, grid=(M // tm, N // tn, nk),
        in_specs=in_specs, out_specs=o_spec,
        scratch_shapes=[pltpu.VMEM((tm, tn), F32)] if nk > 1 else [],
        compiler_params=_params("parallel", "parallel", "arbitrary"),
    )(*ins)


def _rstd(v):
    return lax.rsqrt(jnp.mean(v * v, axis=-1, keepdims=True) + NORM_EPS)


def _rms_bwd_rows(v, g, dout):
    r = _rstd(v)
    vh = v * r
    dvh = dout * g
    dv = r * (dvh - vh * jnp.mean(dvh * vh, axis=-1, keepdims=True))
    return dv, jnp.sum(dout * vh, axis=0, keepdims=True)


def _row(tt, d, col=0):
    return pl.BlockSpec((tt, d), lambda i: (i, col))


def _const(shape):
    return pl.BlockSpec(shape, lambda i: (0,) * len(shape))


def _prev_halo(tt, d, col=0):
    return pl.BlockSpec((HALO, d), lambda i: (jnp.maximum(i * (tt // HALO) - 1, 0), col))


def _next_halo(tt, d, rows, col=0):
    last = rows // HALO - 1
    return pl.BlockSpec((HALO, d), lambda i: (jnp.minimum((i + 1) * (tt // HALO), last), col))


def _rms_fwd(x, g, name):
    T, D = x.shape
    tt = _fit(T, ROW_TILE, 8)

    def body(x_ref, g_ref, h_ref):
        v = x_ref[...]
        h_ref[...] = (v * _rstd(v) * g_ref[...]).astype(BF16)

    return _pcall(body, name=name, out_shape=_sds((T, D), BF16), grid=(T // tt,),
                  in_specs=[_row(tt, D), _const((1, D))], out_specs=_row(tt, D),
                  compiler_params=_params("parallel"))(x, g)


def _post_pre_fwd(x, y, g_post, g_pre, name):
    T, D = x.shape
    tt = _fit(T, NORM_ROW_TILE, 16)

    def body(x_ref, y_ref, gp_ref, gn_ref, xo_ref, h_ref):
        y = y_ref[...].astype(F32)
        xn = x_ref[...] + y * _rstd(y) * gp_ref[...]
        xo_ref[...] = xn
        h_ref[...] = (xn * _rstd(xn) * gn_ref[...]).astype(BF16)

    return _pcall(body, name=name, out_shape=(_sds((T, D), F32), _sds((T, D), BF16)), grid=(T // tt,),
                  in_specs=[_row(tt, D), _row(tt, D), _const((1, D)), _const((1, D))],
                  out_specs=(_row(tt, D), _row(tt, D)),
                  compiler_params=_params("parallel"))(x, y, g_post, g_pre)


def _final_fwd_loss(x, y, g_post, target, name):
    T, D = x.shape
    tt = _fit(T, ROW_TILE, 8)

    def body(x_ref, y_ref, g_ref, t_ref, loss_ref, dx_ref, dy_ref, dg_ref):
        i = pl.program_id(0)
        y = y_ref[...].astype(F32)
        g = g_ref[...]
        err = x_ref[...] + y * _rstd(y) * g - t_ref[...]
        part = 0.5 * jnp.sum(jnp.mean(err * err, axis=-1, keepdims=True))
        dx = err / D
        dx_ref[...] = dx
        dy, dg = _rms_bwd_rows(y, g, dx)
        dy_ref[...] = dy.astype(BF16)

        @pl.when(i == 0)
        def _():
            loss_ref[...] = jnp.zeros_like(loss_ref)
            dg_ref[...] = jnp.zeros_like(dg_ref)

        loss_ref[...] += part
        dg_ref[...] += dg

    return _pcall(body, name=name,
                  out_shape=(_sds((1, 128), F32), _sds((T, D), F32), _sds((T, D), BF16), _sds((1, D), F32)),
                  grid=(T // tt,),
                  in_specs=[_row(tt, D), _row(tt, D), _const((1, D)), _row(tt, D)],
                  out_specs=(_const((1, 128)), _row(tt, D), _row(tt, D), _const((1, D))),
                  compiler_params=_params("arbitrary"))(x, y, g_post, target)


def _pre_post_bwd(dx_out, dh, x_in, g_pre, y_prev, g_post_prev, name):
    T, D = x_in.shape
    tt = _fit(T, NORM_ROW_TILE, 16)
    with_prev = y_prev is not None

    def body(*refs):
        if with_prev:
            dxo_ref, dh_ref, x_ref, g_ref, y_ref, gp_ref, dxi_ref, dg_ref, dy_ref, dgp_ref = refs
        else:
            dxo_ref, dh_ref, x_ref, g_ref, dxi_ref, dg_ref = refs
        i = pl.program_id(0)
        dv, dg = _rms_bwd_rows(x_ref[...], g_ref[...], dh_ref[...].astype(F32))
        dxi = dxo_ref[...] + dv
        dxi_ref[...] = dxi

        @pl.when(i == 0)
        def _():
            dg_ref[...] = jnp.zeros_like(dg_ref)

        dg_ref[...] += dg
        if with_prev:
            dy, dgp = _rms_bwd_rows(y_ref[...].astype(F32), gp_ref[...], dxi)
            dy_ref[...] = dy.astype(BF16)

            @pl.when(i == 0)
            def _():
                dgp_ref[...] = jnp.zeros_like(dgp_ref)

            dgp_ref[...] += dgp

    ins = [dx_out, dh, x_in, g_pre]
    in_specs = [_row(tt, D), _row(tt, D), _row(tt, D), _const((1, D))]
    out_shape = [_sds((T, D), F32), _sds((1, D), F32)]
    out_specs = [_row(tt, D), _const((1, D))]
    if with_prev:
        ins += [y_prev, g_post_prev]
        in_specs += [_row(tt, D), _const((1, D))]
        out_shape += [_sds((T, D), BF16), _sds((1, D), F32)]
        out_specs += [_row(tt, D), _const((1, D))]
    return _pcall(body, name=name, out_shape=tuple(out_shape), grid=(T // tt,),
                  in_specs=in_specs, out_specs=tuple(out_specs),
                  compiler_params=_params("arbitrary"))(*ins)


def _rms_gain_grad(dout, v, name):
    T, D = v.shape
    tt = _fit(T, ROW_TILE, 8)

    def body(d_ref, v_ref, dg_ref):
        @pl.when(pl.program_id(0) == 0)
        def _():
            dg_ref[...] = jnp.zeros_like(dg_ref)

        v = v_ref[...]
        dg_ref[...] += jnp.sum(d_ref[...] * (v * _rstd(v)), axis=0, keepdims=True)

    return _pcall(body, name=name, out_shape=_sds((1, D), F32), grid=(T // tt,),
                  in_specs=[_row(tt, D), _row(tt, D)], out_specs=_const((1, D)),
                  compiler_params=_params("arbitrary"))(dout, v)


def _softmax_rows(s):
    e = jnp.exp(s - jnp.max(s, axis=-1, keepdims=True))
    return e / jnp.sum(e, axis=-1, keepdims=True)


def _attn_fwd(q, kv, name):
    T, D = q.shape
    nm = kv.shape[0]
    hd = D // XA_HEADS
    scale = hd ** -0.5
    tq = _fit(T, ROW_TILE, 8)

    def body(q_ref, k_ref, v_ref, o_ref):
        for h in range(XA_HEADS):
            sl = slice(h * hd, (h + 1) * hd)
            s = lax.dot_general(q_ref[:, sl], k_ref[:, sl], _DIMS["nt"], preferred_element_type=F32) * scale
            p = _softmax_rows(s)
            o_ref[:, sl] = jnp.dot(p.astype(BF16), v_ref[:, sl], preferred_element_type=F32).astype(BF16)

    return _pcall(body, name=name, out_shape=_sds((T, D), BF16), grid=(T // tq,),
                  in_specs=[_row(tq, D), pl.BlockSpec((nm, D), lambda i: (0, 0)), pl.BlockSpec((nm, D), lambda i: (0, 1))],
                  out_specs=_row(tq, D), compiler_params=_params("parallel"))(q, kv, kv)


def _attn_bwd(q, kv, do, name):
    T, D = q.shape
    nm = kv.shape[0]
    hd = D // XA_HEADS
    scale = hd ** -0.5
    tq = _fit(T, ROW_TILE, 8)

    def body(q_ref, k_ref, v_ref, do_ref, dq_ref, dkv_ref):
        @pl.when(pl.program_id(0) == 0)
        def _():
            dkv_ref[...] = jnp.zeros_like(dkv_ref)

        for h in range(XA_HEADS):
            sl = slice(h * hd, (h + 1) * hd)
            qh, kh, vh, doh = q_ref[:, sl], k_ref[:, sl], v_ref[:, sl], do_ref[:, sl]
            s = lax.dot_general(qh, kh, _DIMS["nt"], preferred_element_type=F32) * scale
            p = _softmax_rows(s)
            dp = lax.dot_general(doh, vh, _DIMS["nt"], preferred_element_type=F32)
            ds = (p * (dp - jnp.sum(dp * p, axis=-1, keepdims=True)) * scale).astype(BF16)
            dq_ref[:, sl] = jnp.dot(ds, kh, preferred_element_type=F32).astype(BF16)
            dkv_ref[:, sl] += lax.dot_general(ds, qh, _DIMS["tn"], preferred_element_type=F32)
            dkv_ref[:, D + h * hd:D + (h + 1) * hd] += lax.dot_general(
                p.astype(BF16), doh, _DIMS["tn"], preferred_element_type=F32)

    return _pcall(body, name=name, out_shape=(_sds((T, D), BF16), _sds((nm, 2 * D), F32)), grid=(T // tq,),
                  in_specs=[_row(tq, D), pl.BlockSpec((nm, D), lambda i: (0, 0)), pl.BlockSpec((nm, D), lambda i: (0, 1)),
                            _row(tq, D)],
                  out_specs=(_row(tq, D), _const((nm, 2 * D))),
                  compiler_params=_params("arbitrary"))(q, kv, kv, do)


def _ffn_gu_fwd(h, w_gu, name, dep=None, tm=512):
    T, D = h.shape
    S, _, c = w_gu.shape
    F = S * c // 2
    tm = _fit(T, tm)
    tn = _fit(c, 1536)
    nb = c // tn
    nj = F // tn
    n_in = 3 if dep is None else 4

    def w_spec(off):
        return pl.BlockSpec((None, D, tn), lambda i, j: ((j + off) // nb, 0, (j + off) % nb))

    def body(*refs):
        h_ref, wg_ref, wu_ref = refs[:3]
        dg_ref, du_ref, a_ref = refs[n_in:]
        hv = h_ref[...]
        g = jnp.dot(hv, wg_ref[...], preferred_element_type=F32)
        sg = jax.nn.sigmoid(g)
        silu = g * sg
        du_ref[...] = silu.astype(BF16)
        u = jnp.dot(hv, wu_ref[...], preferred_element_type=F32)
        dg_ref[...] = (u * (sg + silu * (1.0 - sg))).astype(BF16)
        a_ref[...] = (silu * u).astype(BF16)

    ins = [h, w_gu, w_gu]
    in_specs = [pl.BlockSpec((tm, D), lambda i, j: (i, 0)), w_spec(0), w_spec(nj)]
    if dep is not None:
        ins.append(dep)
        in_specs.append(ANY)
    o_spec = pl.BlockSpec((tm, tn), lambda i, j: (i, j))
    return _pcall(body, name=name, out_shape=(_sds((T, F), BF16),) * 3, grid=(T // tm, nj),
                  in_specs=in_specs, out_specs=(o_spec,) * 3,
                  compiler_params=_params("parallel", "parallel"))(*ins)


def _ffn_dgu_bwd(dy, w_down, dact_dgate, dact_dup, name, dep=None, tm=1024):
    T, D = dy.shape
    F = w_down.shape[0]
    tm = _fit(T, tm)
    tn = _fit(F, 512)
    cn = _fit(tn, 256)
    n_in = 4 if dep is None else 5

    def body(*refs):
        dy_ref, wd_ref, g_ref, u_ref = refs[:4]
        o_ref = refs[n_in]
        dyv = dy_ref[...]
        for n0 in range(0, tn, cn):
            da = lax.dot_general(dyv, wd_ref[n0:n0 + cn, :], _DIMS["nt"], preferred_element_type=F32)
            o_ref[0, :, n0:n0 + cn] = (da * g_ref[:, n0:n0 + cn].astype(F32)).astype(BF16)
            o_ref[1, :, n0:n0 + cn] = (da * u_ref[:, n0:n0 + cn].astype(F32)).astype(BF16)

    ins = [dy, w_down, dact_dgate, dact_dup]
    gu_spec = pl.BlockSpec((tm, tn), lambda i, j: (i, j))
    in_specs = [pl.BlockSpec((tm, D), lambda i, j: (i, 0)), pl.BlockSpec((tn, D), lambda i, j: (j, 0)), gu_spec, gu_spec]
    if dep is not None:
        ins.append(dep)
        in_specs.append(ANY)
    return _pcall(body, name=name, out_shape=_sds((2, T, F), BF16), grid=(T // tm, F // tn),
                  in_specs=in_specs, out_specs=pl.BlockSpec((2, tm, tn), lambda i, j: (0, i, j)),
                  compiler_params=_params("parallel", "parallel"))(*ins)


def _causal_taps(win, width, tt):
    for b in range(min(8, width)):
        wb = win if b == 0 else pltpu.roll(win, b, 0)
        a = 0
        while 8 * a + b <= width - 1:
            yield width - 1 - (8 * a + b), wb[HALO - 8 * a:HALO - 8 * a + tt]
            a += 1


def _anticausal_taps(win, width, tt):
    rows = tt + HALO
    for b in range(min(8, width)):
        wb = win if b == 0 else pltpu.roll(win, rows - b, 0)
        a = 0
        while 8 * a + b <= width - 1:
            yield width - 1 - (8 * a + b), wb[8 * a:8 * a + tt]
            a += 1


WIDE_CONV_LANES = 256
WIDE_CONV_ROWS = 64


def _tap_list(width):
    return [(width - 1 - (8 * a + b), b, a) for b in range(min(8, width)) for a in range((width - 1 - b) // 8 + 1)]


def _lanes(d):
    cw = _fit(d, CONV_LANES)
    return [slice(s, s + cw) for s in range(0, d, cw)], cw


def _a_mid_fwd(bcz, conv_w, name):
    T, D3 = bcz.shape
    D = D3 // 3
    width = conv_w.shape[0]
    tt = _fit(T, ROW_TILE, HALO)
    chunks, cw = _lanes(D)

    def body(b_ref, c_ref, z_ref, ch_ref, zh_ref, w_ref, o_ref, win_ref):
        i = pl.program_id(0)
        for sl in chunks:
            uh = ch_ref[:, sl].astype(F32) * zh_ref[:, sl].astype(F32)
            win_ref[0:HALO, :] = jnp.where(i > 0, uh, 0.0)
            win_ref[HALO:, :] = c_ref[:, sl].astype(F32) * z_ref[:, sl].astype(F32)
            acc = jnp.zeros((tt, cw), F32)
            for k, xs in _causal_taps(win_ref[...], width, tt):
                acc = acc + w_ref[k:k + 1, sl] * xs
            o_ref[:, sl] = (b_ref[:, sl].astype(F32) * acc).astype(BF16)

    return _pcall(body, name=name, out_shape=_sds((T, D), BF16), grid=(T // tt,),
                  in_specs=[_row(tt, D, 0), _row(tt, D, 1), _row(tt, D, 2), _prev_halo(tt, D, 1), _prev_halo(tt, D, 2),
                            _const((width, D))],
                  out_specs=_row(tt, D), scratch_shapes=[pltpu.VMEM((HALO + tt, cw), F32)],
                  compiler_params=_params("parallel"))(bcz, bcz, bcz, bcz, bcz, conv_w)


def _a_mid_bwd(bcz, dy2, conv_w, name):
    T, D3 = bcz.shape
    D = D3 // 3
    width = conv_w.shape[0]
    tt = _fit(T, ROW_TILE, HALO)
    chunks, cw = _lanes(D)
    n_tiles = T // tt

    def body(b_ref, c_ref, z_ref, ch_ref, zh_ref, bn_ref, d_ref, dn_ref, w_ref, o_ref, dw_ref, win_ref, dwin_ref):
        i = pl.program_id(0)

        @pl.when(i == 0)
        def _():
            dw_ref[...] = jnp.zeros_like(dw_ref)

        for ci, sl in enumerate(chunks):
            c = c_ref[:, sl].astype(F32)
            z = z_ref[:, sl].astype(F32)
            b = b_ref[:, sl].astype(F32)
            d2 = d_ref[:, sl].astype(F32)
            uh = ch_ref[:, sl].astype(F32) * zh_ref[:, sl].astype(F32)
            win_ref[0:HALO, :] = jnp.where(i > 0, uh, 0.0)
            win_ref[HALO:, :] = c * z
            d1 = d2 * b
            d1n = dn_ref[:, sl].astype(F32) * bn_ref[:, sl].astype(F32)
            dwin_ref[0:tt, :] = d1
            dwin_ref[tt:, :] = jnp.where(i < n_tiles - 1, d1n, 0.0)
            y1 = jnp.zeros((tt, cw), F32)
            for k, xs in _causal_taps(win_ref[...], width, tt):
                y1 = y1 + w_ref[k:k + 1, sl] * xs
                dw_ref[k:k + 1, sl] += jnp.sum(d1 * xs, axis=0, keepdims=True)
            du = jnp.zeros((tt, cw), F32)
            for k, xs in _anticausal_taps(dwin_ref[...], width, tt):
                du = du + w_ref[k:k + 1, sl] * xs
            o_ref[:, ci * cw:(ci + 1) * cw] = (d2 * y1).astype(BF16)
            o_ref[:, D + ci * cw:D + (ci + 1) * cw] = (du * z).astype(BF16)
            o_ref[:, 2 * D + ci * cw:2 * D + (ci + 1) * cw] = (du * c).astype(BF16)

    return _pcall(body, name=name, out_shape=(_sds((T, 3 * D), BF16), _sds((width, D), F32)), grid=(n_tiles,),
                  in_specs=[_row(tt, D, 0), _row(tt, D, 1), _row(tt, D, 2), _prev_halo(tt, D, 1), _prev_halo(tt, D, 2),
                            _next_halo(tt, D, T, 0), _row(tt, D), _next_halo(tt, D, T), _const((width, D))],
                  out_specs=(_row(tt, 3 * D), _const((width, D))),
                  scratch_shapes=[pltpu.VMEM((HALO + tt, cw), F32), pltpu.VMEM((tt + HALO, cw), F32)],
                  compiler_params=_params("arbitrary"))(bcz, bcz, bcz, bcz, bcz, bcz, dy2, dy2, conv_w)


_GELU_C = 0.7978845608028654
_GELU_A = 0.044715


def _gelu(v):
    return 0.5 * v * (1.0 + jnp.tanh(_GELU_C * (v + _GELU_A * v * v * v)))


def _gelu_grad(v):
    t = jnp.tanh(_GELU_C * (v + _GELU_A * v * v * v))
    return 0.5 * (1.0 + t) + 0.5 * v * (1.0 - t * t) * (_GELU_C * (1.0 + 3.0 * _GELU_A * v * v))


def _ln_stats(v):
    mu = jnp.mean(v, axis=-1, keepdims=True)
    vc = v - mu
    return vc * lax.rsqrt(jnp.mean(vc * vc, axis=-1, keepdims=True) + NORM_EPS)


def _tril(n):
    return lax.broadcasted_iota(jnp.int32, (n, n), 0) >= lax.broadcasted_iota(jnp.int32, (n, n), 1)


def _b_mid_fwd(uv, v_g, v_b, w_s, bias_b, name):
    T, D2 = uv.shape
    D = D2 // 2
    G, C, _ = w_s.shape
    gd = D // G
    tt = _fit(T, ROW_TILE, C)

    def body(u_ref, v_ref, g_ref, b_ref, ws_ref, bias_ref, o_ref, vln_ref):
        vln_ref[...] = (_ln_stats(_gelu(v_ref[...].astype(F32))) * g_ref[...] + b_ref[...]).astype(BF16)
        mask = _tril(C)
        for g in range(G):
            wsm = jnp.where(mask, ws_ref[g], 0.0).astype(BF16)
            cs = slice(g * gd, (g + 1) * gd)
            for n in range(tt // C):
                rs = slice(n * C, (n + 1) * C)
                sv = jnp.dot(wsm, vln_ref[rs, cs], preferred_element_type=F32) + bias_ref[g]
                o_ref[rs, cs] = (_gelu(u_ref[rs, cs].astype(F32)) * sv).astype(BF16)

    return _pcall(body, name=name, out_shape=_sds((T, D), BF16), grid=(T // tt,),
                  in_specs=[_row(tt, D, 0), _row(tt, D, 1), _const((1, D)), _const((1, D)), _const((G, C, C)),
                            _const((G, C, gd))],
                  out_specs=_row(tt, D), scratch_shapes=[pltpu.VMEM((tt, D), BF16)],
                  compiler_params=_params("parallel"))(uv, uv, v_g, v_b, w_s, bias_b)


def _b_mid_bwd(uv, dgated, v_g, v_b, w_s, bias_b, name):
    T, D2 = uv.shape
    D = D2 // 2
    G, C, _ = w_s.shape
    gd = D // G
    tt = _fit(T, ROW_TILE, C)

    def body(u_ref, v_ref, d_ref, g_ref, b_ref, ws_ref, bias_ref, o_ref, dws_ref, dsb_ref, dvg_ref, dvb_ref,
             vln_ref, dvln_ref):
        @pl.when(pl.program_id(0) == 0)
        def _():
            dws_ref[...] = jnp.zeros_like(dws_ref)
            dsb_ref[...] = jnp.zeros_like(dsb_ref)
            dvg_ref[...] = jnp.zeros_like(dvg_ref)
            dvb_ref[...] = jnp.zeros_like(dvb_ref)

        vpre = v_ref[...].astype(F32)
        vhat = _ln_stats(_gelu(vpre))
        vln_ref[...] = (vhat * g_ref[...] + b_ref[...]).astype(BF16)
        mask = _tril(C)
        lane = lax.broadcasted_iota(jnp.int32, (C, 128), 1)
        for g in range(G):
            wsm = jnp.where(mask, ws_ref[g], 0.0).astype(BF16)
            cs = slice(g * gd, (g + 1) * gd)
            for n in range(tt // C):
                rs = slice(n * C, (n + 1) * C)
                vt = vln_ref[rs, cs]
                sv = jnp.dot(wsm, vt, preferred_element_type=F32) + bias_ref[g]
                dg = d_ref[rs, cs].astype(F32)
                upre = u_ref[rs, cs].astype(F32)
                o_ref[rs, cs] = (dg * sv * _gelu_grad(upre)).astype(BF16)
                dsv = dg * _gelu(upre)
                dsb_ref[...] += jnp.where(lane == g, jnp.sum(dsv, axis=-1, keepdims=True), 0.0)
                dsv16 = dsv.astype(BF16)
                dws_ref[g] += jnp.where(mask, lax.dot_general(dsv16, vt, _DIMS["nt"], preferred_element_type=F32), 0.0)
                dvln_ref[rs, cs] = lax.dot_general(wsm, dsv16, _DIMS["tn"], preferred_element_type=F32)
        dvln = dvln_ref[...]
        dvg_ref[...] += jnp.sum(dvln * vhat, axis=0, keepdims=True)
        dvb_ref[...] += jnp.sum(dvln, axis=0, keepdims=True)
        dvh = dvln * g_ref[...]
        vc = _gelu(vpre)
        vc = vc - jnp.mean(vc, axis=-1, keepdims=True)
        rstd = lax.rsqrt(jnp.mean(vc * vc, axis=-1, keepdims=True) + NORM_EPS)
        dv = rstd * (dvh - jnp.mean(dvh, axis=-1, keepdims=True) - vhat * jnp.mean(dvh * vhat, axis=-1, keepdims=True))
        o_ref[:, D:] = (dv * _gelu_grad(vpre)).astype(BF16)

    return _pcall(body, name=name,
                  out_shape=(_sds((T, 2 * D), BF16), _sds((G, C, C), F32), _sds((C, 128), F32), _sds((1, D), F32),
                             _sds((1, D), F32)),
                  grid=(T // tt,),
                  in_specs=[_row(tt, D, 0), _row(tt, D, 1), _row(tt, D), _const((1, D)), _const((1, D)),
                            _const((G, C, C)), _const((G, C, gd))],
                  out_specs=(_row(tt, 2 * D), _const((G, C, C)), _const((C, 128)), _const((1, D)), _const((1, D))),
                  scratch_shapes=[pltpu.VMEM((tt, D), BF16), pltpu.VMEM((tt, D), F32)],
                  compiler_params=_params("arbitrary"))(uv, uv, dgated, v_g, v_b, w_s, bias_b)


def _c_mid_fwd(ag, conv_w, conv_b, ln_g, ln_b, name):
    T, D2 = ag.shape
    D = D2 // 2
    width = conv_w.shape[0]
    tt = _fit(T, ROW_TILE, HALO)
    cw = _fit(D, WIDE_CONV_LANES)
    rb = _fit(tt, WIDE_CONV_ROWS, 8)
    taps = _tap_list(width)

    def body(a_ref, g_ref, ah_ref, gh_ref, w_ref, cb_ref, lg_ref, lb_ref, y2_ref, o_ref, win_ref, roll_ref):
        i = pl.program_id(0)
        yh = ah_ref[...].astype(F32) * jax.nn.sigmoid(gh_ref[...].astype(F32))
        win_ref[0:HALO, :] = jnp.where(i > 0, yh, 0.0)
        win_ref[HALO:, :] = a_ref[...].astype(F32) * jax.nn.sigmoid(g_ref[...].astype(F32))
        for c0 in range(0, D, cw):
            sl = slice(c0, c0 + cw)
            wv = win_ref[:, sl]
            for b in range(1, min(8, width)):
                roll_ref[b - 1] = pltpu.roll(wv, b, 0)
            for r0 in range(0, tt, rb):
                acc = jnp.zeros((rb, cw), F32)
                for k, b, a in taps:
                    lo = HALO - 8 * a + r0
                    xs = win_ref[lo:lo + rb, sl] if b == 0 else roll_ref[b - 1, lo:lo + rb, :]
                    acc = acc + w_ref[k:k + 1, sl] * xs
                y2_ref[r0:r0 + rb, sl] = acc + cb_ref[:, sl]
        y3 = _ln_stats(y2_ref[...]) * lg_ref[...] + lb_ref[...]
        o_ref[...] = (y3 * jax.nn.sigmoid(y3)).astype(BF16)

    return _pcall(body, name=name, out_shape=(_sds((T, D), F32), _sds((T, D), BF16)), grid=(T // tt,),
                  in_specs=[_row(tt, D, 0), _row(tt, D, 1), _prev_halo(tt, D, 0), _prev_halo(tt, D, 1),
                            _const((width, D)), _const((1, D)), _const((1, D)), _const((1, D))],
                  out_specs=(_row(tt, D), _row(tt, D)),
                  scratch_shapes=[pltpu.VMEM((HALO + tt, D), F32), pltpu.VMEM((7, HALO + tt, cw), F32)],
                  compiler_params=_params("parallel"))(ag, ag, ag, ag, conv_w, conv_b, ln_g, ln_b)


def _c_mid_bwd(ag, y2, dy4, conv_w, ln_g, ln_b, name):
    T, D2 = ag.shape
    D = D2 // 2
    width = conv_w.shape[0]
    tt = _fit(T, ROW_TILE, HALO)
    cw = _fit(D, WIDE_CONV_LANES)
    rb = _fit(tt, WIDE_CONV_ROWS, 8)
    taps = _tap_list(width)
    n_tiles = T // tt

    def ln_silu_bwd(y2v, dy4v, lg, lb):
        mu = jnp.mean(y2v, axis=-1, keepdims=True)
        yc = y2v - mu
        rstd = lax.rsqrt(jnp.mean(yc * yc, axis=-1, keepdims=True) + NORM_EPS)
        yh = yc * rstd
        y3 = yh * lg + lb
        sg = jax.nn.sigmoid(y3)
        dy3 = dy4v * (sg * (1.0 + y3 * (1.0 - sg)))
        dyh = dy3 * lg
        dy2 = rstd * (dyh - jnp.mean(dyh, axis=-1, keepdims=True) - yh * jnp.mean(dyh * yh, axis=-1, keepdims=True))
        return dy2, dy3, yh

    def body(a_ref, g_ref, ah_ref, gh_ref, y2_ref, y2n_ref, d_ref, dn_ref, w_ref, lg_ref, lb_ref,
             o_ref, dw_ref, dcb_ref, dlg_ref, dlb_ref, win_ref, dwin_ref, roll_ref, droll_ref):
        i = pl.program_id(0)

        @pl.when(i == 0)
        def _():
            dw_ref[...] = jnp.zeros_like(dw_ref)
            dcb_ref[...] = jnp.zeros_like(dcb_ref)
            dlg_ref[...] = jnp.zeros_like(dlg_ref)
            dlb_ref[...] = jnp.zeros_like(dlb_ref)

        lg = lg_ref[...]
        lb = lb_ref[...]
        dy2, dy3, yh = ln_silu_bwd(y2_ref[...], d_ref[...].astype(F32), lg, lb)
        dlg_ref[...] += jnp.sum(dy3 * yh, axis=0, keepdims=True)
        dlb_ref[...] += jnp.sum(dy3, axis=0, keepdims=True)
        dcb_ref[...] += jnp.sum(dy2, axis=0, keepdims=True)
        dwin_ref[0:tt, :] = dy2
        dy2n, _, _ = ln_silu_bwd(y2n_ref[...], dn_ref[...].astype(F32), lg, lb)
        dwin_ref[tt:, :] = jnp.where(i < n_tiles - 1, dy2n, 0.0)
        yh1 = ah_ref[...].astype(F32) * jax.nn.sigmoid(gh_ref[...].astype(F32))
        win_ref[0:HALO, :] = jnp.where(i > 0, yh1, 0.0)
        win_ref[HALO:, :] = a_ref[...].astype(F32) * jax.nn.sigmoid(g_ref[...].astype(F32))
        rows = tt + HALO
        for c0 in range(0, D, cw):
            sl = slice(c0, c0 + cw)
            wv = win_ref[:, sl]
            dv = dwin_ref[:, sl]
            for b in range(1, min(8, width)):
                roll_ref[b - 1] = pltpu.roll(wv, b, 0)
                droll_ref[b - 1] = pltpu.roll(dv, rows - b, 0)
            for k, b, a in taps:
                acc = jnp.zeros((8, cw), F32)
                for r0 in range(0, tt, rb):
                    lo = HALO - 8 * a + r0
                    xs = win_ref[lo:lo + rb, sl] if b == 0 else roll_ref[b - 1, lo:lo + rb, :]
                    prod = dwin_ref[r0:r0 + rb, sl] * xs
                    for q in range(0, rb, 8):
                        acc = acc + prod[q:q + 8]
                dw_ref[k:k + 1, sl] += jnp.sum(acc, axis=0, keepdims=True)
            for r0 in range(0, tt, rb):
                d1 = jnp.zeros((rb, cw), F32)
                for k, b, a in taps:
                    lo = 8 * a + r0
                    xs = dwin_ref[lo:lo + rb, sl] if b == 0 else droll_ref[b - 1, lo:lo + rb, :]
                    d1 = d1 + w_ref[k:k + 1, sl] * xs
                av = a_ref[r0:r0 + rb, sl].astype(F32)
                sg = jax.nn.sigmoid(g_ref[r0:r0 + rb, sl].astype(F32))
                o_ref[r0:r0 + rb, c0:c0 + cw] = (d1 * sg).astype(BF16)
                o_ref[r0:r0 + rb, D + c0:D + c0 + cw] = (d1 * av * sg * (1.0 - sg)).astype(BF16)

    return _pcall(body, name=name,
                  out_shape=(_sds((T, 2 * D), BF16), _sds((width, D), F32), _sds((1, D), F32), _sds((1, D), F32),
                             _sds((1, D), F32)),
                  grid=(n_tiles,),
                  in_specs=[_row(tt, D, 0), _row(tt, D, 1), _prev_halo(tt, D, 0), _prev_halo(tt, D, 1),
                            _row(tt, D), _next_halo(tt, D, T), _row(tt, D), _next_halo(tt, D, T),
                            _const((width, D)), _const((1, D)), _const((1, D))],
                  out_specs=(_row(tt, 2 * D), _const((width, D)), _const((1, D)), _const((1, D)), _const((1, D))),
                  scratch_shapes=[pltpu.VMEM((HALO + tt, D), F32), pltpu.VMEM((tt + HALO, D), F32),
                                  pltpu.VMEM((7, HALO + tt, cw), F32), pltpu.VMEM((7, tt + HALO, cw), F32)],
                  compiler_params=_params("arbitrary"))(ag, ag, ag, ag, y2, y2, dy4, dy4, conv_w, ln_g, ln_b)


def _place():
    x, y, c = lax.axis_index("x"), lax.axis_index("y"), lax.axis_index("c")
    return x, y, c


def _slot(px, py, pc):
    return 4 * px + 2 * py + pc


def _all_gather(shards, name):
    n = len(shards)

    def body(*refs):
        ins, outs = refs[:n], refs[n:2 * n]
        send_sems, recv_sems, local_sems = refs[2 * n:]
        x, y, c = _place()
        me, sibling = (x, y, c), (x, y, 1 - c)
        chips = [(1 - x, y), (x, 1 - y), (1 - x, 1 - y)]

        def copy(t, k, block, to, src=None):
            dst = outs[t].at[_slot(*block)]
            return pltpu.make_async_remote_copy(
                src_ref=dst if src is None else src, dst_ref=dst, send_sem=send_sems.at[t, k],
                recv_sem=recv_sems.at[t, k], device_id=to, device_id_type=MESH)

        mine = [pltpu.make_async_copy(ins[t], outs[t].at[_slot(*me)], local_sems.at[t]) for t in range(n)]
        for cp in mine:
            cp.start()
        first = []
        for j, chip in enumerate(chips):
            first += [copy(t, 1 + j, me, (*chip, c), src=ins[t]) for t in range(n)]
        first += [copy(t, 0, me, sibling, src=ins[t]) for t in range(n)]
        for cp in first:
            cp.start()
        passed = []
        for j, chip in enumerate(chips):
            for t in range(n):
                copy(t, 1 + j, (*chip, c), me).wait_recv()
                cp = copy(t, 4 + j, (*chip, c), sibling)
                cp.start()
                passed.append(cp)
        for t in range(n):
            copy(t, 0, sibling, me).wait_recv()
            for j, chip in enumerate(chips):
                copy(t, 4 + j, (*chip, 1 - c), me).wait_recv()
        for cp in first + passed:
            cp.wait_send()
        for cp in mine:
            cp.wait()

    outs = _pcall(
        body, name=name, out_shape=tuple(_sds((N_DEV,) + s.shape, s.dtype) for s in shards),
        in_specs=[ANY] * n, out_specs=(ANY,) * n,
        scratch_shapes=[pltpu.SemaphoreType.DMA((n, 7)), pltpu.SemaphoreType.DMA((n, 7)), pltpu.SemaphoreType.DMA((n,))],
    )(*shards)
    return list(outs)


_HBM = pl.BlockSpec(memory_space=pltpu.HBM)
_SEM = pl.BlockSpec(memory_space=pltpu.SEMAPHORE)
_DATAFLOW = pltpu.SideEffectType.DATAFLOW_SIDE_EFFECTING


def _peers(x, y, c):
    out = []
    for j in range(1, N_DEV):
        fx, fy, fc = (j >> 2) & 1, (j >> 1) & 1, j & 1
        out.append((1 - x if fx else x, 1 - y if fy else y, 1 - c if fc else c))
    return out


def _exchange_copies(ins, zones, mode, sems):
    send_sem, recv_sem, local_sem = sems
    x, y, c = _place()
    me = _slot(x, y, c)
    sibling = (x, y, 1 - c)
    chips = [(1 - x, y), (x, 1 - y), (1 - x, 1 - y)]
    local, remote = [], []

    def add(src, dst, to, landed):
        remote.append((pltpu.make_async_remote_copy(src_ref=src, dst_ref=dst, send_sem=send_sem, recv_sem=recv_sem,
                                                    device_id=to, device_id_type=MESH), landed))

    for t, zone in enumerate(zones):
        if mode == "scatter":
            local.append(pltpu.make_async_copy(ins[t].at[me], zone.at[me], local_sem))
            for peer in _peers(x, y, c):
                add(ins[t].at[_slot(*peer)], zone.at[me], peer, zone.at[_slot(*peer)])
        elif mode in ("gather_chips", "gather_all"):
            local.append(pltpu.make_async_copy(ins[t], zone.at[me], local_sem))
            for peer in ([(*chip, c) for chip in chips] + [sibling] if mode == "gather_chips" else _peers(x, y, c)):
                add(ins[t], zone.at[me], peer, zone.at[_slot(*peer)])
        else:
            for chip in chips:
                block = zone.at[_slot(*chip, c)]
                add(block, block, sibling, zone.at[_slot(*chip, 1 - c)])
    return local, remote


def _exchange_start(srcs, lands, mode, after, name):
    if lands is None:
        lands = [lax.empty(s.shape if mode == "scatter" else (N_DEV,) + s.shape, s.dtype) for s in srcs]
    ns, na = len(srcs), len(srcs) + len(lands)

    def body(*refs):
        local, remote = _exchange_copies(refs[:ns], refs[ns:na], mode, refs[na + 1:na + 4])
        for cp in local:
            cp.start()
        for cp, _ in remote:
            cp.start()
        refs[-1][...] = jnp.zeros_like(refs[-1])

    hbm = lambda a: pltpu.with_memory_space_constraint(a, pltpu.HBM)
    arrays = list(srcs) + list(lands)
    outs = _pcall(
        body, name=name,
        out_shape=(pltpu.SemaphoreType.DMA(()),) * 3
        + tuple(pltpu.HBM(a.shape, a.dtype) for a in arrays) + (_sds((8, 128), F32),),
        in_specs=[_HBM] * na + [ANY],
        out_specs=(_SEM,) * 3 + (_HBM,) * na + (pl.BlockSpec(memory_space=pltpu.VMEM),),
        input_output_aliases={t: 3 + t for t in range(na)},
        compiler_params=pltpu.CompilerParams(has_side_effects=_DATAFLOW),
    )(*[hbm(a) for a in arrays], after)
    return outs[:3], list(outs[3:3 + ns]), list(outs[3 + ns:3 + na]), outs[-1]


def _exchange_wait(sems, srcs, lands, mode, after, name):
    ns, na = len(srcs), len(srcs) + len(lands)
    afters = list(after) if isinstance(after, (list, tuple)) else [after]

    def body(*refs):
        local, remote = _exchange_copies(refs[:ns], refs[ns:na], mode, refs[na:na + 3])
        for cp in local:
            cp.wait()
        for cp, landed in remote:
            cp.wait_send()
            pltpu.make_async_remote_copy(
                src_ref=landed, dst_ref=landed, send_sem=refs[na], recv_sem=refs[na + 1],
                device_id=_place(), device_id_type=MESH).wait_recv()

    outs = _pcall(
        body, name=name, out_shape=tuple(pltpu.HBM(a.shape, a.dtype) for a in list(srcs) + list(lands)),
        in_specs=[_HBM] * na + [_SEM] * 3 + [ANY] * len(afters), out_specs=(_HBM,) * na,
        input_output_aliases={t: t for t in range(na)},
        compiler_params=pltpu.CompilerParams(has_side_effects=_DATAFLOW),
    )(*srcs, *lands, *sems, *afters)
    return list(outs[ns:])


def _reduce_adam(recvs, w, m, v, name, dep=None):
    L, r, c = w.shape
    tr = _fit(r, max(16, (256 * 1024) // c), 16)
    ni = r // tr

    def recv_spec(l0):
        def index(l, i):
            return 0, jnp.where(l == l0, i, jnp.where(l < l0, 0, ni - 1)), 0
        return pl.BlockSpec((N_DEV, tr, c), index)

    lay = pl.BlockSpec((None, tr, c), lambda l, i: (l, i, 0))

    n_dep = 0 if dep is None else 1

    def body(*refs):
        recv_refs = refs[:L]
        w_ref, m_ref, v_ref = refs[L:L + 3]
        g_out, d_out, m_out, v_out = refs[L + 3 + n_dep:]
        l = pl.program_id(0)
        for l0 in range(L):
            @pl.when(l == l0)
            def _(l0=l0):
                g = recv_refs[l0][0].astype(F32)
                for s in range(1, N_DEV):
                    g = g + recv_refs[l0][s].astype(F32)
                mn = ADAM_B1 * m_ref[...] + (1.0 - ADAM_B1) * g
                vn = ADAM_B2 * v_ref[...] + (1.0 - ADAM_B2) * (g * g)
                m_hat = mn / (1.0 - ADAM_B1 ** ADAM_STEP)
                v_hat = vn / (1.0 - ADAM_B2 ** ADAM_STEP)
                g_out[...] = g
                d_out[...] = -ADAM_LR * (m_hat / (jnp.sqrt(v_hat) + ADAM_EPS) + ADAM_WD * w_ref[...])
                m_out[...] = mn
                v_out[...] = vn

    return _pcall(body, name=name, out_shape=(_sds((L, r, c), F32),) * 4, grid=(L, ni),
                  in_specs=[recv_spec(l0) for l0 in range(L)] + [lay, lay, lay] + [ANY] * n_dep, out_specs=(lay,) * 4,
                  compiler_params=_params("arbitrary", "arbitrary"))(*recvs, w, m, v, *([dep] if n_dep else []))


_SMALL_SHARDED = ["mix_norm", "xa_norm", "ffn_norm", "a_conv_w", "c_conv_w", "c_conv_b", "c_ln_g", "c_ln_b"]
_SMALL_REPLICATED = ["b_v_g", "b_v_b", "b_w_s", "b_s_bias"]
_BIG = ["xa_wq", "xa_wkv", "xa_wo", "ffn_w_gu", "ffn_w_down", "a_w_in", "a_w_out", "b_w_in", "b_w_out", "c_w_in",
        "c_w_out"]
_COL_SHARDED = {"xa_wkv", "ffn_w_gu", "a_w_in", "b_w_in", "c_w_in"}
_WEIGHTS = ["mix_norm", "xa_norm", "xa_wq", "xa_wkv", "xa_wo", "ffn_norm", "ffn_w_gu", "ffn_w_down", "a_w_in",
            "a_conv_w", "a_w_out", "b_w_in", "b_v_g", "b_v_b", "b_w_s", "b_s_bias", "b_w_out", "c_w_in", "c_conv_w",
            "c_conv_b", "c_ln_g", "c_ln_b", "c_w_out"]
_MIXER = "abc"


def _size(shape):
    size = 1
    for s in shape:
        size *= s
    return size


def _row_layout(shapes, width):
    offs, r = [], 0
    for shape in shapes:
        offs.append(r)
        r += -(-(-(-_size(shape) // width)) // 8) * 8
    return offs, r


def _pack_rows(arrays, width, fill):
    offs, total = _row_layout([a.shape for a in arrays], width)
    ends = offs[1:] + [total]
    rows = [jnp.pad(a.reshape(-1), (0, (e - o) * width - a.size), constant_values=fill).reshape(e - o, width)
            for a, o, e in zip(arrays, offs, ends)]
    return jnp.concatenate(rows, axis=0)


def _unpack_rows(packed, like):
    width = packed.shape[-1]
    offs, _ = _row_layout(like, width)
    return [packed[o:o + -(-_size(s) // width)].reshape(-1)[:_size(s)].reshape(s) for o, s in zip(offs, like)]


def _assemble_rows(pieces, rows, width, name):
    n = len(pieces)

    def body(*refs):
        o_ref = refs[n]
        o_ref[...] = jnp.zeros_like(o_ref)
        for r, (a, off) in zip(refs[:n], pieces):
            o_ref[off:off + a.shape[0], :] = r[...]

    return _pcall(body, name=name, out_shape=_sds((rows, width), F32),
                  compiler_params=pltpu.CompilerParams(vmem_limit_bytes=V7X_VMEM_LIMIT))(*[a for a, _ in pieces])


def kernel(x, mem, mix_norm, xa_norm, xa_wq, xa_wkv, xa_wo, ffn_norm, ffn_w_gu, ffn_w_down, a_w_in, a_conv_w, a_w_out, b_w_in, b_v_g, b_v_b, b_w_s, b_s_bias, b_w_out, c_w_in, c_conv_w, c_conv_b, c_ln_g, c_ln_b, c_w_out, loss_target, m_mix_norm, m_xa_norm, m_xa_wq, m_xa_wkv, m_xa_wo, m_ffn_norm, m_ffn_w_gu, m_ffn_w_down, m_a_w_in, m_a_conv_w, m_a_w_out, m_b_w_in, m_b_v_g, m_b_v_b, m_b_w_s, m_b_s_bias, m_b_w_out, m_c_w_in, m_c_conv_w, m_c_conv_b, m_c_ln_g, m_c_ln_b, m_c_w_out, v_mix_norm, v_xa_norm, v_xa_wq, v_xa_wkv, v_xa_wo, v_ffn_norm, v_ffn_w_gu, v_ffn_w_down, v_a_w_in, v_a_conv_w, v_a_w_out, v_b_w_in, v_b_v_g, v_b_v_b, v_b_w_s, v_b_s_bias, v_b_w_out, v_c_w_in, v_c_conv_w, v_c_conv_b, v_c_ln_g, v_c_ln_b, v_c_w_out):
    P = dict(locals())
    T, D = x.shape[1], x.shape[2]
    dl = D // N_DEV
    depth = mix_norm.shape[0]
    x0, mem0, target = x[0], mem[0], loss_target[0]
    my_slot = _slot(*_place())

    sh_shapes = [P[n].shape for n in _SMALL_SHARDED]
    packed = _pack_rows([P[n] for n in _SMALL_SHARDED], dl, 0.0)
    n_sh = packed.shape[0]
    gathered = _all_gather([packed], "ag_small")[0]
    full_rows = jnp.transpose(gathered, (1, 0, 2)).reshape(n_sh, D)
    small = dict(zip(_SMALL_SHARDED, _unpack_rows(full_rows, [s[:-1] + (D,) for s in sh_shapes])))
    G, C = b_w_s.shape[1], b_w_s.shape[2]
    gd = D // G
    bias_b = jnp.broadcast_to(b_s_bias[0][:, :, None], (G, C, gd))
    zero_row = jnp.zeros((1, D), F32)

    groups = [(i, part) for i in range(depth) for part in (("in", "out", "xa", "gu", "down") if i == 0 else
                                                           ("mix", "xa", "ffn"))]

    def group_names(i, part):
        mx, slot = _MIXER[i % N_MIXERS], i // N_MIXERS
        names = {"in": [(mx + "_w_in", slot)], "out": [(mx + "_w_out", slot)],
                 "xa": [("xa_wq", i), ("xa_wkv", i), ("xa_wo", i)],
                 "gu": [("ffn_w_gu", i)], "down": [("ffn_w_down", i)]}
        names["mix"] = names["in"] + names["out"]
        names["ffn"] = names["gu"] + names["down"]
        return names[part]

    no_token = jnp.zeros((8, 128), F32)
    fwd = {"g": 0, "last": full_rows, "token": no_token, "stage1": {}, "stage2": {}}

    def tag(g):
        return "%d_%s" % groups[g]

    def start_stage1(g):
        if g < len(groups):
            names = group_names(*groups[g])
            sems, srcs, lands, token = _exchange_start([P[n][j].astype(BF16) for n, j in names], None, "gather_chips",
                                                       fwd["last"], "ag_start_" + tag(g))
            fwd["stage1"][g] = (sems, srcs, lands)
            fwd["last"] = fwd["token"] = token

    def start_stage2(g, after):
        if g in fwd["stage1"]:
            sems, srcs, lands = fwd["stage1"].pop(g)
            lands = _exchange_wait(sems, srcs, lands, "gather_chips", after, "ag_wait_" + tag(g))
            sems, _, lands, token = _exchange_start([], lands, "gather_sibling", after, "ag_pass_" + tag(g))
            fwd["stage2"][g] = (sems, lands)
            fwd["last"] = fwd["token"] = token

    def begin_group():
        g, y = fwd["g"], fwd["last"]
        start_stage2(g, y)
        sems, lands = fwd["stage2"].pop(g)
        fulls = _exchange_wait(sems, [], lands, "gather_sibling", y, "ag_done_" + tag(g))
        fwd["last"] = fulls[0]
        start_stage1(g + 3)
        out = {}
        for (n, _), f in zip(group_names(*groups[g]), fulls):
            key = n[2:] if n[1] == "_" and n[0] in _MIXER else n
            out[key] = f if n in _COL_SHARDED else f.reshape(-1, f.shape[-1])
        fwd["g"] += 1
        return out, fwd["token"]

    def mid_group(y):
        start_stage2(fwd["g"], y)
        return fwd["token"]

    def end_group(y):
        fwd["last"] = y

    for g0 in range(3):
        start_stage1(g0)
    start_stage2(0, fwd["last"])

    saved = []
    xin = x0
    h = _rms_fwd(x0, small["mix_norm"][0, 0][None], "rms_first")
    for i in range(depth):
        kind, slot = i % N_MIXERS, i // N_MIXERS
        W, dep = begin_group()
        S = {"W": W, "x0": xin, "h0": h}
        pre = _mm(h, W["w_in"], "nn", BF16, "mm_in_%s" % _MIXER[kind], b_blocked=True, dep=dep)
        S["pre"] = pre
        dep = mid_group(pre)
        if kind == 0:
            mid = _a_mid_fwd(pre, small["a_conv_w"][slot], "a_mid_fwd")
        elif kind == 1:
            mid = _b_mid_fwd(pre, b_v_g, b_v_b, b_w_s[0], bias_b, "b_mid_fwd")
        else:
            y2c, mid = _c_mid_fwd(pre, small["c_conv_w"][slot], small["c_conv_b"], small["c_ln_g"], small["c_ln_b"],
                                  "c_mid_fwd")
            S["y2c"] = y2c
        S["mid"] = mid
        if "w_out" not in W:
            end_group(mid)
            Wo, dep = begin_group()
            W.update(Wo)
        S["y0"] = _mm(mid, W["w_out"], "nn", BF16, "mm_out", dep=dep)
        end_group(S["y0"])
        Wx, dep = begin_group()
        W.update(Wx)
        xin, h = _post_pre_fwd(xin, S["y0"], small["mix_norm"][i, 1][None], small["xa_norm"][i, 0][None], "post_pre")
        S["x1"], S["h1"] = xin, h
        S["q"] = _mm(h, W["xa_wq"], "nn", BF16, "mm_q", dep=dep)
        dep = mid_group(S["q"])
        S["memn"] = _rms_fwd(mem0, small["xa_norm"][i, 2][None], "rms_mem")
        S["kv"] = _mm(S["memn"], W["xa_wkv"], "nn", BF16, "mm_kv", b_blocked=True)
        S["o"] = _attn_fwd(S["q"], S["kv"], "attn_fwd")
        S["y1"] = _mm(S["o"], W["xa_wo"], "nn", BF16, "mm_out", dep=dep)
        end_group(S["y1"])
        Wf, dep = begin_group()
        W.update(Wf)
        xin, h = _post_pre_fwd(xin, S["y1"], small["xa_norm"][i, 1][None], small["ffn_norm"][i, 0][None], "post_pre")
        S["x2"], S["h2"] = xin, h
        S["dact_dgate"], S["dact_dup"], S["act"] = _ffn_gu_fwd(h, W["ffn_w_gu"], "ffn_gu_fwd", dep=dep)
        dep = mid_group(S["act"])
        if "ffn_w_down" not in W:
            end_group(S["act"])
            Wd, dep = begin_group()
            W.update(Wd)
        S["y2"] = _mm(S["act"], W["ffn_w_down"], "nn", BF16, "mm_down", dep=dep)
        end_group(S["y2"])
        if i + 1 < depth:
            xin, h = _post_pre_fwd(xin, S["y2"], small["ffn_norm"][i, 1][None], small["mix_norm"][i + 1, 0][None],
                                   "post_pre")
        saved.append(S)

    last = saved[-1]
    loss_part, dx, dy, dg = _final_fwd_loss(xin, last["y2"], small["ffn_norm"][depth - 1, 1][None], target, "final_loss")
    loss = lax.psum(loss_part[0, 0], ("x", "y", "c"))

    g_mix = [[zero_row, zero_row] for _ in range(depth)]
    g_xa = [[zero_row, zero_row, zero_row] for _ in range(depth)]
    g_ffn = [[zero_row, zero_row] for _ in range(depth)]
    g_small = {}
    recv = {n: [None] * P[n].shape[0] for n in _BIG}
    g_ffn[depth - 1][1] = dg

    bwd = {"queue": [], "token": no_token}
    scatters_in_flight = 3

    def finish_scatter(after, keep):
        while len(bwd["queue"]) > keep:
            names, tag, sems, srcs, lands = bwd["queue"].pop(0)
            for (n, j, _), r in zip(names, _exchange_wait(sems, srcs, lands, "scatter", after, "rs_wait_" + tag)):
                recv[n][j] = r

    def scatter_group(names, tag, after):
        finish_scatter(after, scatters_in_flight - 1)
        parts = [g if n in _COL_SHARDED else g.reshape(N_DEV, -1, g.shape[-1]) for n, _, g in names]
        sems, srcs, lands, bwd["token"] = _exchange_start(parts, None, "scatter", after, "rs_start_" + tag)
        bwd["queue"].append((names, tag, sems, srcs, lands))

    def scatter_token():
        return bwd["token"]

    for i in reversed(range(depth)):
        kind, slot = i % N_MIXERS, i // N_MIXERS
        mx = _MIXER[kind]
        S = saved[i]
        W = S["W"]
        dgu = _ffn_dgu_bwd(dy, W["ffn_w_down"], S["dact_dgate"], S["dact_dup"], "ffn_dgu_bwd", dep=scatter_token())
        dw_down = _mm(S["act"], dy, "tn", BF16, "mm_dw_down")
        dh = _mm(dgu, W["ffn_w_gu"], "nt", BF16, "mm_dh_gu", a_blocked=True, b_blocked=True)
        dw_gu = _mm(S["h2"], dgu, "tn", BF16, "mm_dw_gu", b_blocked=True, out_blocks=N_DEV)
        dx, g_ffn[i][0], dy, g_xa[i][1] = _pre_post_bwd(dx, dh, S["x2"], small["ffn_norm"][i, 0][None], S["y1"],
                                                         small["xa_norm"][i, 1][None], "pre_post_bwd")
        scatter_group([("ffn_w_gu", i, dw_gu), ("ffn_w_down", i, dw_down)], "%d_2" % i, dx)
        do = _mm(dy, W["xa_wo"], "nt", BF16, "mm_nt_dd16", dep=scatter_token())
        dw_o = _mm(S["o"], dy, "tn", BF16, "mm_dw_dd")
        dq, dkv = _attn_bwd(S["q"], S["kv"], do, "attn_bwd")
        dkv16 = dkv.astype(BF16)
        dh = _mm(dq, W["xa_wq"], "nt", BF16, "mm_nt_dd16")
        dw_q = _mm(S["h1"], dq, "tn", BF16, "mm_dw_dd")
        dw_kv = _mm(S["memn"], dkv16, "tn", BF16, "mm_dw_kv", out_blocks=N_DEV)
        dmemn = _mm(dkv16, W["xa_wkv"], "nt", F32, "mm_dmem", b_blocked=True)
        g_xa[i][2] = _rms_gain_grad(dmemn, mem0, "rms_gain_grad")
        dx, g_xa[i][0], dy, g_mix[i][1] = _pre_post_bwd(dx, dh, S["x1"], small["xa_norm"][i, 0][None], S["y0"],
                                                         small["mix_norm"][i, 1][None], "pre_post_bwd")
        scatter_group([("xa_wq", i, dw_q), ("xa_wkv", i, dw_kv), ("xa_wo", i, dw_o)], "%d_1" % i, dx)
        dmid = _mm(dy, W["w_out"], "nt", BF16, "mm_nt_dd16", dep=scatter_token())
        dw_out = _mm(S["mid"], dy, "tn", BF16, "mm_dw_dd")
        if i == 0:
            scatter_group([(mx + "_w_out", slot, dw_out)], "0_0_out", dy)
        if kind == 0:
            dpre, dcw = _a_mid_bwd(S["pre"], dmid, small["a_conv_w"][slot], "a_mid_bwd")
            g_small.setdefault("a_conv_w", {})[slot] = dcw
        elif kind == 1:
            dpre, dws, dsb, dvg, dvb = _b_mid_bwd(S["pre"], dmid, b_v_g, b_v_b, b_w_s[0], bias_b, "b_mid_bwd")
            dsb_row = jnp.pad(jnp.transpose(dsb[:, :G]).reshape(1, G * C), ((0, 0), (0, (-G * C) % D)))
            g_small.update(b_w_s=dws.reshape(-1, D), b_s_bias=dsb_row.reshape(-1, D), b_v_g=dvg, b_v_b=dvb)
        else:
            dpre, dcw, dcb, dlg, dlb = _c_mid_bwd(S["pre"], S["y2c"], dmid, small["c_conv_w"][slot], small["c_ln_g"],
                                                  small["c_ln_b"], "c_mid_bwd")
            g_small.update(c_conv_w=dcw, c_conv_b=dcb, c_ln_g=dlg, c_ln_b=dlb)
        dh = _mm(dpre, W["w_in"], "nt", BF16, "mm_dh_in_%s" % mx, b_blocked=True, dep=scatter_token())
        dw_in = _mm(S["h0"], dpre, "tn", BF16, "mm_dw_in_%s" % mx, out_blocks=N_DEV)
        if i > 0:
            dx, g_mix[i][0], dy, g_ffn[i - 1][1] = _pre_post_bwd(
                dx, dh, S["x0"], small["mix_norm"][i, 0][None], saved[i - 1]["y2"],
                small["ffn_norm"][i - 1, 1][None], "pre_post_bwd")
        else:
            dx, g_mix[i][0] = _pre_post_bwd(dx, dh, S["x0"], small["mix_norm"][i, 0][None], None, None, "pre_bwd")
        scatter_group([(mx + "_w_in", slot, dw_in)] + ([(mx + "_w_out", slot, dw_out)] if i > 0 else []),
                      "%d_0" % i, dx)
        S.clear()
    grad_x = dx[None]

    sh_off = dict(zip(_SMALL_SHARDED, _row_layout(sh_shapes, dl)[0]))
    rep_offs, n_rep = _row_layout([P[n].shape for n in _SMALL_REPLICATED], D)
    rep_off = {n: n_sh + o for n, o in zip(_SMALL_REPLICATED, rep_offs)}
    pieces = []
    for i in range(depth):
        pieces += [(g, sh_off["mix_norm"] + 2 * i + j) for j, g in enumerate(g_mix[i])]
        pieces += [(g, sh_off["xa_norm"] + 3 * i + j) for j, g in enumerate(g_xa[i])]
        pieces += [(g, sh_off["ffn_norm"] + 2 * i + j) for j, g in enumerate(g_ffn[i])]
    pieces += [(g, sh_off["a_conv_w"] + a_conv_w.shape[1] * s) for s, g in g_small["a_conv_w"].items()]
    pieces += [(g_small[n], sh_off[n]) for n in ("c_conv_w", "c_conv_b", "c_ln_g", "c_ln_b")]
    pieces += [(g_small[n], rep_off[n]) for n in _SMALL_REPLICATED]
    part_all = _assemble_rows(pieces, n_sh + n_rep, D, "pack_small_grads")
    small_sems, small_srcs, small_lands, started = _exchange_start([part_all], None, "gather_all", scatter_token(),
                                                                   "ag_small_grads_start")

    out = {}

    def adam_small(names, recv_s, width, name):
        shapes = [P[n].shape for n in names]
        pw = _pack_rows([P[n] for n in names], width, 0.0)
        pm = _pack_rows([P["m_" + n] for n in names], width, 0.0)
        pv = _pack_rows([P["v_" + n] for n in names], width, 1.0)
        res = _reduce_adam([recv_s], pw[None], pm[None], pv[None], name)
        for kind, r in zip(("grad", "delta", "new_m", "new_v"), res):
            for n, a in zip(names, _unpack_rows(r[0], shapes)):
                out[kind + "_" + n] = a

    def lands_after(n):
        return max([0] + [k + 1 for k, entry in enumerate(bwd["queue"]) if any(m == n for m, _, _ in entry[0])])

    early_done = [g_xa[i][2] for i in range(depth)]
    for n in sorted(_BIG, key=lands_after):
        while any(r is None for r in recv[n]):
            finish_scatter(early_done, len(bwd["queue"]) - 1)
        res = _reduce_adam(recv[n], P[n], P["m_" + n], P["v_" + n], "adam_" + n, dep=started)
        early_done.append(res[0])
        for kind, r in zip(("grad", "delta", "new_m", "new_v"), res):
            out[kind + "_" + n] = r

    parts_all = _exchange_wait(small_sems, small_srcs, small_lands, "gather_all", early_done,
                               "ag_small_grads_wait")[0]
    recv_sh = lax.dynamic_slice_in_dim(parts_all[:, :n_sh], my_slot * dl, dl, axis=2)
    adam_small(_SMALL_SHARDED, recv_sh, dl, "adam_small_sharded")
    adam_small(_SMALL_REPLICATED, parts_all[:, n_sh:], D, "adam_small_replicated")

    return (loss, grad_x, *[out[k + "_" + n] for k in ("grad", "delta", "new_m", "new_v") for n in _WEIGHTS])
```
